```python
import jax, jax.numpy as jnp
from jax import lax
import numpy as np

D_MODEL = 1024
BATCH = 4
SEQ = 8192
DEPTH = 4
DEC_BATCH = 16
DEC_SEQ = 16
PAST_LEN = 2048

CHUNK = 64
N_EVEN = (DEPTH + 1) // 2
N_ODD = DEPTH // 2
EPS = 1e-6
F32 = jnp.float32

FOX_HEADS = 8
FOX_HEAD_DIM = 64
FOX_WIDTH = FOX_HEADS * FOX_HEAD_DIM
FOX_QBLOCK = 128
FOX_F_BIAS = 4.0
ML_HEADS = 4
ML_HEAD_DIM = 128
ML_WIDTH = ML_HEADS * ML_HEAD_DIM
ML_CONV = 4
ML_F_BIAS = 3.0
GLA_HEADS = 4
GLA_DK = 128
GLA_DV = 256
GLA_KW = GLA_HEADS * GLA_DK
GLA_VW = GLA_HEADS * GLA_DV
GLA_RANK = 16
GLA_TAU = 16.0
D_FF = 4 * D_MODEL

EVEN_SPLITS = (FOX_WIDTH, FOX_WIDTH, FOX_WIDTH, FOX_HEADS,
               2 * ML_WIDTH, ML_WIDTH, ML_WIDTH, ML_HEADS, ML_HEADS)
EVEN_IN = 3 * FOX_WIDTH + FOX_HEADS + 4 * ML_WIDTH + 2 * ML_HEADS
ODD_SPLITS = (GLA_KW, GLA_KW, GLA_VW, GLA_VW, GLA_RANK)
ODD_IN = 2 * GLA_KW + 2 * GLA_VW + GLA_RANK

kernel_name = "fox_mlstm_gla_streaming_step"


def _rmsnorm(x, g):
    xf = x.astype(F32)
    y = xf * lax.rsqrt(jnp.mean(xf * xf, axis=-1, keepdims=True) + EPS)
    return (y * g.astype(F32)).astype(x.dtype)


def _head_rmsnorm(h, g):
    B, T, H, Dh = h.shape
    hf = h.astype(F32)
    y = hf * lax.rsqrt(jnp.mean(hf * hf, axis=-1, keepdims=True) + EPS)
    return (y * g.reshape(H, Dh).astype(F32)).reshape(B, T, H * Dh)


def _split(z, sizes):
    return jnp.split(z, np.cumsum(sizes)[:-1].tolist(), axis=-1)


def _heads(a, n):
    return a.reshape(a.shape[0], a.shape[1], n, -1)


def _to_chunks(a, L):
    B, T = a.shape[:2]
    return jnp.moveaxis(a.reshape(B, T // L, L, *a.shape[2:]), 1, 0)


def _from_chunks(a):
    nc, B, L = a.shape[:3]
    return jnp.moveaxis(a, 0, 1).reshape(B, nc * L, *a.shape[3:])


def _causal_dwconv(u, buf, w, b):
    full = jnp.concatenate([buf.astype(u.dtype), u], axis=1)
    y = lax.conv_general_dilated(full, w[:, None, :].astype(u.dtype), window_strides=(1,),
                                 padding='VALID', dimension_numbers=('NWC', 'WIO', 'NWC'),
                                 feature_group_count=u.shape[-1])
    return y + b.astype(u.dtype), full[:, -(ML_CONV - 1):]


def _fox(q, k, v, logf, past_k, past_v, past_logf):
    B, T, H, Dh = q.shape
    P = past_k.shape[1]
    k_all = jnp.concatenate([past_k.astype(k.dtype), k], axis=1)
    v_all = jnp.concatenate([past_v.astype(v.dtype), v], axis=1)
    lf_all = jnp.concatenate([past_logf.astype(F32), logf], axis=1)
    Fk = jnp.cumsum(lf_all, axis=1).transpose(0, 2, 1)
    kpos = jnp.arange(P + T)
    qb = min(FOX_QBLOCK, T)
    nb = T // qb
    q_blocks = q.reshape(B, nb, qb, H, Dh).swapaxes(0, 1)
    Fq_blocks = Fk[:, :, P:].reshape(B, H, nb, qb).transpose(2, 0, 1, 3)
    qpos_blocks = (P + jnp.arange(T)).reshape(nb, qb)
    scale = FOX_HEAD_DIM ** -0.5

    def attend(args):
        qblk, Fq, qpos = args
        s = jnp.einsum('bqhd,bkhd->bhqk', qblk, k_all).astype(F32) * scale
        s = s + Fq[..., :, None] - Fk[..., None, :]
        s = jnp.where(kpos[None, :] <= qpos[:, None], s, -jnp.inf)
        p = jax.nn.softmax(s, axis=-1).astype(v_all.dtype)
        return jnp.einsum('bhqk,bkhd->bqhd', p, v_all)

    o = lax.map(attend, (q_blocks, Fq_blocks, qpos_blocks))
    return o.swapaxes(0, 1).reshape(B, T, H * Dh)


def _mlstm(q, k, v, i_pre, f_pre, C0, n0, m0):
    B, T, H, Dh = q.shape
    L = min(CHUNK, T)
    q, k, v = q.astype(F32), k.astype(F32), v.astype(F32)
    log_f = jax.nn.log_sigmoid(f_pre.astype(F32))
    i_pre = i_pre.astype(F32)
    causal = jnp.tril(jnp.ones((L, L), dtype=bool))

    def step(carry, inp):
        C, n, m = carry
        qc, kc, vc, ic, lfc = inp
        b = jnp.cumsum(lfc, axis=1).transpose(0, 2, 1)
        ic = ic.transpose(0, 2, 1)
        dmat = jnp.where(causal, b[..., :, None] - b[..., None, :] + ic[..., None, :], -jnp.inf)
        inter = b + m[..., None]
        m_t = jnp.maximum(inter, jnp.max(dmat, axis=-1))
        w = jnp.exp(dmat - m_t[..., None])
        g = jnp.exp(inter - m_t)
        a = w * jnp.einsum('bthd,bshd->bhts', qc, kc)
        num = g[..., None] * jnp.einsum('bhed,bthd->bhte', C, qc) + jnp.einsum('bhts,bshe->bhte', a, vc)
        den = g * jnp.einsum('bhd,bthd->bht', n, qc) + jnp.sum(a, axis=-1)
        h = num / jnp.maximum(jnp.abs(den), jnp.exp(-m_t))[..., None]
        w_end, g_end = w[..., -1, :], g[..., -1]
        C_new = g_end[..., None, None] * C + jnp.einsum('bhs,bshe,bshd->bhed', w_end, vc, kc)
        n_new = g_end[..., None] * n + jnp.einsum('bhs,bshd->bhd', w_end, kc)
        return (C_new, n_new, m_t[..., -1]), h.transpose(0, 2, 1, 3)

    xs = tuple(_to_chunks(a, L) for a in (q, k, v, i_pre, log_f))
    (C, n, m), h = lax.scan(step, (C0.astype(F32), n0.astype(F32), m0.astype(F32)), xs)
    return _from_chunks(h), C, n, m


def _gla(q, k, v, log_a, S0):
    B, T, H, DK = q.shape
    L = min(CHUNK, T)
    causal = jnp.tril(jnp.ones((L, L), dtype=bool))[None, :, :, None, None]

    def step(S, inp):
        qc, kc, vc, lac = inp
        cb = jnp.cumsum(lac, axis=1)
        decay = jnp.exp(jnp.where(causal, cb[:, :, None] - cb[:, None, :], -jnp.inf))
        a = jnp.einsum('btshk,bshk->bhts', qc[:, :, None] * decay, kc)
        o = jnp.einsum('bthk,bhkv->bthv', qc * jnp.exp(cb), S) + jnp.einsum('bhts,bshv->bthv', a, vc)
        cl = cb[:, -1]
        S_new = jnp.exp(cl)[..., None] * S + jnp.einsum('bshk,bshv->bhkv', kc * jnp.exp(cl[:, None] - cb), vc)
        return S_new, o

    xs = tuple(_to_chunks(a, L) for a in (q, k, v, log_a))
    S, o = lax.scan(step, S0.astype(F32), xs)
    return _from_chunks(o), S


def _even_mixer(h, w_in, b_fox_f, conv_w, conv_b, b_i, b_f, g_ml,
                fk_past, fv_past, flf_past, C0, n0, m0, buf):
    z = h @ w_in
    fq, fk, fv, ff, mqk, mv, mo, mi, mf = _split(z, EVEN_SPLITS)
    f_logf = jax.nn.log_sigmoid(ff.astype(F32) + b_fox_f.astype(F32))
    k_f, v_f = _heads(fk, FOX_HEADS), _heads(fv, FOX_HEADS)
    attn = _fox(_heads(fq, FOX_HEADS), k_f, v_f, f_logf, fk_past, fv_past, flf_past)
    qk, new_buf = _causal_dwconv(mqk, buf, conv_w, conv_b)
    mq, mk = jnp.split(jax.nn.silu(qk), 2, axis=-1)
    h_ml, C, n, m = _mlstm(_heads(mq, ML_HEADS), _heads(mk, ML_HEADS) * (ML_HEAD_DIM ** -0.5),
                           _heads(mv, ML_HEADS), mi + b_i, mf + b_f, C0, n0, m0)
    h_ml = _head_rmsnorm(h_ml, g_ml) * jax.nn.sigmoid(mo.astype(F32))
    mix = jnp.concatenate([attn, h_ml.astype(h.dtype)], axis=-1)
    return mix, (k_f, v_f, f_logf, C, n, m, new_buf)


def _odd_mixer(h, w_in, w_a2, b_a, g_gla, S0):
    z = h @ w_in
    gq, gk, gv, gg, ga = _split(z, ODD_SPLITS)
    log_a = jax.nn.log_sigmoid((ga @ w_a2 + b_a).astype(F32)) / GLA_TAU
    q = _heads(gq, GLA_HEADS).astype(F32) * (GLA_DK ** -0.5)
    o, S = _gla(q, _heads(gk, GLA_HEADS).astype(F32), _heads(gv, GLA_HEADS).astype(F32),
                _heads(log_a, GLA_HEADS), S0)
    o = _head_rmsnorm(o, g_gla) * jax.nn.silu(gg.astype(F32))
    return o.astype(h.dtype), S


def _trunk(x, fox_k, fox_v, fox_lf, ml_c, ml_n, ml_m, ml_buf, gla_s,
           norm_mix, norm_ffn, norm_final, w_in_even, b_fox_f, conv_w_ml, conv_b_ml,
           b_ml_i, b_ml_f, g_ml, w_in_odd, w_gla_a2, b_gla_a, g_gla, w_out, w_ff1, w_ff2):
    even_states, odd_states = [], []
    for layer in range(DEPTH):
        h = _rmsnorm(x, norm_mix[layer])
        j = layer // 2
        if layer % 2 == 0:
            mix, st = _even_mixer(h, w_in_even[j], b_fox_f[j], conv_w_ml[j], conv_b_ml[j],
                                  b_ml_i[j], b_ml_f[j], g_ml[j], fox_k[j], fox_v[j], fox_lf[j],
                                  ml_c[j], ml_n[j], ml_m[j], ml_buf[j])
            even_states.append(st)
        else:
            mix, S = _odd_mixer(h, w_in_odd[j], w_gla_a2[j], b_gla_a[j], g_gla[j], gla_s[j])
            odd_states.append(S)
        x = x + mix @ w_out[layer]
        h = _rmsnorm(x, norm_ffn[layer])
        x = x + jnp.square(jax.nn.relu(h @ w_ff1[layer])) @ w_ff2[layer]
    y = _rmsnorm(x, norm_final)
    dt = x.dtype
    ev = [jnp.stack([s[i] for s in even_states]).astype(dt) for i in range(7)]
    return y, ev, jnp.stack(odd_states).astype(dt)


def setup_inputs(seed: int = 0) -> dict:
    key = jax.random.key(seed)
    ks = iter(jax.random.split(key, 32))

    def nrm(shape, scale=1.0):
        return jax.random.normal(next(ks), shape, F32) * scale

    D = D_MODEL
    return {
        "x_prompt": nrm((BATCH, SEQ, D)),
        "x_sample": nrm((DEC_BATCH, DEC_SEQ, D)),
        "cache_fox_k": nrm((N_EVEN, DEC_BATCH, PAST_LEN, FOX_HEADS, FOX_HEAD_DIM)),
        "cache_fox_v": nrm((N_EVEN, DEC_BATCH, PAST_LEN, FOX_HEADS, FOX_HEAD_DIM)),
        "cache_fox_logf": jax.nn.log_sigmoid(FOX_F_BIAS + nrm((N_EVEN, DEC_BATCH, PAST_LEN, FOX_HEADS))),
        "state_mlstm_c": nrm((N_EVEN, DEC_BATCH, ML_HEADS, ML_HEAD_DIM, ML_HEAD_DIM), 0.5),
        "state_mlstm_n": nrm((N_EVEN, DEC_BATCH, ML_HEADS, ML_HEAD_DIM), 0.5),
        "state_mlstm_m": nrm((N_EVEN, DEC_BATCH, ML_HEADS)),
        "state_mlstm_conv": nrm((N_EVEN, DEC_BATCH, ML_CONV - 1, 2 * ML_WIDTH)),
        "state_gla_s": nrm((N_ODD, DEC_BATCH, GLA_HEADS, GLA_DK, GLA_DV)),
        "norm_mix": 1.0 + nrm((DEPTH, D), 0.05),
        "norm_ffn": 1.0 + nrm((DEPTH, D), 0.05),
        "norm_final": 1.0 + nrm((D,), 0.05),
        "w_in_even": nrm((N_EVEN, D, EVEN_IN), D ** -0.5),
        "b_fox_f": FOX_F_BIAS + nrm((N_EVEN, FOX_HEADS), 0.5),
        "conv_w_ml": nrm((N_EVEN, ML_CONV, 2 * ML_WIDTH), ML_CONV ** -0.5),
        "conv_b_ml": nrm((N_EVEN, 2 * ML_WIDTH), 0.02),
        "b_ml_i": nrm((N_EVEN, ML_HEADS), 0.1),
        "b_ml_f": ML_F_BIAS + nrm((N_EVEN, ML_HEADS), 0.5),
        "g_ml": 1.0 + nrm((N_EVEN, ML_WIDTH), 0.05),
        "w_in_odd": nrm((N_ODD, D, ODD_IN), D ** -0.5),
        "w_gla_a2": nrm((N_ODD, GLA_RANK, GLA_KW), GLA_RANK ** -0.5),
        "b_gla_a": nrm((N_ODD, GLA_KW), 0.02),
        "g_gla": 1.0 + nrm((N_ODD, GLA_VW), 0.05),
        "w_out": nrm((DEPTH, D, D), D ** -0.5),
        "w_ff1": nrm((DEPTH, D, D_FF), D ** -0.5),
        "w_ff2": nrm((DEPTH, D_FF, D), D_FF ** -0.5),
    }


def reference(x_prompt, x_sample, cache_fox_k, cache_fox_v, cache_fox_logf, state_mlstm_c,
              state_mlstm_n, state_mlstm_m, state_mlstm_conv, state_gla_s,
              norm_mix, norm_ffn, norm_final, w_in_even, b_fox_f, conv_w_ml, conv_b_ml,
              b_ml_i, b_ml_f, g_ml, w_in_odd, w_gla_a2, b_gla_a, g_gla, w_out, w_ff1, w_ff2):
    params = (norm_mix, norm_ffn, norm_final, w_in_even, b_fox_f, conv_w_ml, conv_b_ml,
              b_ml_i, b_ml_f, g_ml, w_in_odd, w_gla_a2, b_gla_a, g_gla, w_out, w_ff1, w_ff2)
    B, dt = x_prompt.shape[0], x_prompt.dtype
    z_fk = jnp.zeros((N_EVEN, B, 0, FOX_HEADS, FOX_HEAD_DIM), dt)
    z_flf = jnp.zeros((N_EVEN, B, 0, FOX_HEADS), dt)
    z_c = jnp.zeros((N_EVEN, B, ML_HEADS, ML_HEAD_DIM, ML_HEAD_DIM), dt)
    z_n = jnp.zeros((N_EVEN, B, ML_HEADS, ML_HEAD_DIM), dt)
    z_m = jnp.zeros((N_EVEN, B, ML_HEADS), dt)
    z_buf = jnp.zeros((N_EVEN, B, ML_CONV - 1, 2 * ML_WIDTH), dt)
    z_s = jnp.zeros((N_ODD, B, GLA_HEADS, GLA_DK, GLA_DV), dt)
    y_prompt, p_ev, p_gla_s = _trunk(x_prompt, z_fk, z_fk, z_flf, z_c, z_n, z_m, z_buf, z_s, *params)
    p_fox_k, p_fox_v, p_fox_logf, p_mlstm_c, p_mlstm_n, p_mlstm_m, p_mlstm_conv = p_ev
    y_sample, s_ev, s_gla_s = _trunk(x_sample, cache_fox_k, cache_fox_v, cache_fox_logf, state_mlstm_c,
                                     state_mlstm_n, state_mlstm_m, state_mlstm_conv, state_gla_s, *params)
    s_fox_k, s_fox_v, s_fox_logf, s_mlstm_c, s_mlstm_n, s_mlstm_m, s_mlstm_conv = s_ev
    return (y_prompt, y_sample,
            p_fox_k, p_fox_v, p_fox_logf, p_mlstm_c, p_mlstm_n, p_mlstm_m, p_mlstm_conv, p_gla_s,
            s_fox_k, s_fox_v, s_fox_logf, s_mlstm_c, s_mlstm_n, s_mlstm_m, s_mlstm_conv, s_gla_s)
```

```python
import functools

import numpy as np
import jax
import jax.numpy as jnp
from jax import lax
from jax.experimental import pallas as pl
from jax.experimental.pallas import tpu as pltpu

F32 = jnp.float32
BF16 = jnp.bfloat16
EPS = 1e-6
NEG = -1e30

LANES = 128
VMEM_LIMIT = 56 * 1024 * 1024

D_MODEL = 1024
D_FF = 4 * D_MODEL
FOX_HEADS, FOX_HEAD_DIM = 8, 64
FOX_WIDTH = FOX_HEADS * FOX_HEAD_DIM
ML_HEADS, ML_HEAD_DIM = 4, 128
ML_WIDTH = ML_HEADS * ML_HEAD_DIM
ML_CONV = 4
GLA_HEADS, GLA_DK, GLA_DV = 4, 128, 256
GLA_KW = GLA_HEADS * GLA_DK
GLA_VW = GLA_HEADS * GLA_DV
GLA_RANK = 16
GLA_TAU = 16.0

E_Q, E_K, E_V, E_MQK, E_MV, E_MO, E_G, E_END = 0, 512, 1024, 1536, 2560, 3072, 3584, 3712
G_FOX, G_MI, G_MF = 0, 8, 12
O_Q, O_K, O_V, O_G, O_A, O_END = 0, 512, 1024, 2048, 3072, 3200

ML_CHUNK = 256
GLA_CHUNK = 128
GLA_LEVELS = (64, 32, 16, 8, 4, 2, 1)


def _cparams(sem):
    return pltpu.CompilerParams(dimension_semantics=sem, vmem_limit_bytes=VMEM_LIMIT)


def _const_spec(shape):
    nd = len(shape)
    return pl.BlockSpec(shape, lambda *_: (0,) * nd, pipeline_mode=pl.Buffered(1))


def _rms(x, g):
    return x * lax.rsqrt(jnp.mean(x * x, axis=-1, keepdims=True) + EPS) * g


def _sigmoid(x):
    return 1.0 / (1.0 + jnp.exp(-x))


def _log_sigmoid(x):
    return -(jnp.maximum(-x, 0.0) + jnp.log1p(jnp.exp(-jnp.abs(x))))


def _dot(a, b):
    return jnp.dot(a, b, preferred_element_type=F32)


def _dot_nt(a, b):
    return lax.dot_general(a, b, (((1,), (1,)), ((), ())), preferred_element_type=F32)


def _dot_tn(a, b):
    return lax.dot_general(a, b, (((0,), (0,)), ((), ())), preferred_element_type=F32)


def _split3(x):
    hi = x.astype(BF16)
    r1 = x - hi.astype(F32)
    mid = r1.astype(BF16)
    lo = (r1 - mid.astype(F32)).astype(BF16)
    return hi, mid, lo


def _mat_f32(m, parts):
    return _dot(m, parts[0]) + _dot(m, parts[1]) + _dot(m, parts[2])


def _f32_mat(parts, m):
    return _dot(parts[0], m) + _dot(parts[1], m) + _dot(parts[2], m)


def _tri(n, lower):
    r = lax.broadcasted_iota(jnp.int32, (n, n), 0)
    c = lax.broadcasted_iota(jnp.int32, (n, n), 1)
    keep = (c <= r) if lower else (r <= c)
    return jnp.where(keep, 1.0, 0.0).astype(BF16)


def _pad_rows(a, n):
    if a.shape[0] == n:
        return a
    return jnp.concatenate([a, jnp.zeros((n - a.shape[0], a.shape[1]), a.dtype)], axis=0)


def _inproj_even_kernel(x_ref, g_ref, w_ref, bias_ref,
                        q_ref, kf_ref, vf_ref, kb_ref, vb_ref, mqk_ref, mv_ref, mo_ref, gc_ref):
    h = _rms(x_ref[...], g_ref[...]).astype(BF16)

    def seg(a, b):
        return _dot(h, w_ref[:, a:b])

    q_ref[...] = (seg(E_Q, E_K) * (FOX_HEAD_DIM ** -0.5)).astype(BF16)
    k = seg(E_K, E_V)
    kf_ref[...] = k
    kb_ref[...] = k.astype(BF16)
    v = seg(E_V, E_MQK)
    vf_ref[...] = v
    vb_ref[...] = v.astype(BF16)
    mqk_ref[...] = seg(E_MQK, E_MV)
    mv_ref[...] = seg(E_MV, E_MO).astype(BF16)
    mo_ref[...] = seg(E_MO, E_G).astype(BF16)
    gz = seg(E_G, E_END) + bias_ref[...]
    lane = lax.broadcasted_iota(jnp.int32, gz.shape, 1)
    is_log = (lane < G_MI) | (lane >= G_MF)
    gc_ref[...] = jnp.where(is_log, _log_sigmoid(gz), gz)


def _inproj_even(x, g, w, bias, tm):
    n = x.shape[0]
    row = lambda c: pl.BlockSpec((tm, c), lambda i: (i, 0))
    outs = [(FOX_WIDTH, BF16), (FOX_WIDTH, F32), (FOX_WIDTH, F32), (FOX_WIDTH, BF16), (FOX_WIDTH, BF16),
            (2 * ML_WIDTH, F32), (ML_WIDTH, BF16), (ML_WIDTH, BF16), (LANES, F32)]
    return pl.pallas_call(
        _inproj_even_kernel,
        grid=(n // tm,),
        in_specs=[row(D_MODEL), _const_spec((1, D_MODEL)), _const_spec((D_MODEL, E_END)), _const_spec((1, LANES))],
        out_specs=[row(c) for c, _ in outs],
        out_shape=[jax.ShapeDtypeStruct((n, c), dt) for c, dt in outs],
        compiler_params=_cparams(("parallel",)),
        name="inproj_even",
    )(x, g, w, bias)


def _inproj_odd_kernel(x_ref, g_ref, w_ref, wa2_ref, ba_ref, q_ref, k_ref, v_ref, gg_ref, la_ref):
    h = _rms(x_ref[...], g_ref[...]).astype(BF16)

    def seg(a, b):
        return _dot(h, w_ref[:, a:b])

    q_ref[...] = (seg(O_Q, O_K) * (GLA_DK ** -0.5)).astype(BF16)
    k_ref[...] = seg(O_K, O_V).astype(BF16)
    v_ref[...] = seg(O_V, O_G).astype(BF16)
    gg_ref[...] = seg(O_G, O_A).astype(BF16)
    ga = seg(O_A, O_END).astype(BF16)
    la_ref[...] = _log_sigmoid(_dot(ga, wa2_ref[...]) + ba_ref[...]) * (1.0 / GLA_TAU)


def _inproj_odd(x, g, w, wa2, ba, tm):
    n = x.shape[0]
    row = lambda c: pl.BlockSpec((tm, c), lambda i: (i, 0))
    outs = [(GLA_KW, BF16), (GLA_KW, BF16), (GLA_VW, BF16), (GLA_VW, BF16), (GLA_KW, F32)]
    return pl.pallas_call(
        _inproj_odd_kernel,
        grid=(n // tm,),
        in_specs=[row(D_MODEL), _const_spec((1, D_MODEL)), _const_spec((D_MODEL, O_END)),
                  _const_spec((LANES, GLA_KW)), _const_spec((1, GLA_KW))],
        out_specs=[row(c) for c, _ in outs],
        out_shape=[jax.ShapeDtypeStruct((n, c), dt) for c, dt in outs],
        compiler_params=_cparams(("parallel",)),
        name="inproj_odd",
    )(x, g, w, wa2, ba)


def _post_kernel(*refs, n_mix, final):
    x_ref = refs[0]
    mix_refs = refs[1:1 + n_mix]
    wo_ref, gf_ref, w1_ref, w2_ref = refs[1 + n_mix:5 + n_mix]
    rest = refs[5 + n_mix:]
    mix = mix_refs[0][...] if n_mix == 1 else jnp.concatenate([r[...] for r in mix_refs], axis=1)
    x1 = x_ref[...] + _dot(mix, wo_ref[...])
    h = _rms(x1, gf_ref[...]).astype(BF16)
    y = x1
    for c in range(D_FF // D_MODEL):
        sl = slice(c * D_MODEL, (c + 1) * D_MODEL)
        t = jnp.maximum(_dot(h, w1_ref[:, sl]), 0.0)
        y = y + _dot((t * t).astype(BF16), w2_ref[sl, :])
    if final:
        gfin_ref, out_ref = rest
        out_ref[...] = _rms(y, gfin_ref[...])
    else:
        (out_ref,) = rest
        out_ref[...] = y


def _post(x, mixes, wo, gf, w1, w2, gfin, tm):
    n = x.shape[0]
    row = lambda c: pl.BlockSpec((tm, c), lambda i: (i, 0))
    final = gfin is not None
    ins = [x, *mixes, wo, gf, w1, w2]
    specs = [row(D_MODEL)] + [row(m.shape[1]) for m in mixes] + [
        _const_spec((D_MODEL, D_MODEL)), _const_spec((1, D_MODEL)),
        _const_spec((D_MODEL, D_FF)), _const_spec((D_FF, D_MODEL))]
    if final:
        ins.append(gfin)
        specs.append(_const_spec((1, D_MODEL)))
    return pl.pallas_call(
        functools.partial(_post_kernel, n_mix=len(mixes), final=final),
        grid=(n // tm,),
        in_specs=specs,
        out_specs=row(D_MODEL),
        out_shape=jax.ShapeDtypeStruct((n, D_MODEL), F32),
        compiler_params=_cparams(("parallel",)),
        name="post_final" if final else "post",
    )(*ins)


def _cumsum_kernel(c_ref, r_ref, fc_ref, fr_ref, carry_c, carry_r):
    @pl.when(pl.program_id(1) == 0)
    def _():
        carry_c[...] = jnp.zeros_like(carry_c)
        carry_r[...] = jnp.zeros_like(carry_r)

    tc = c_ref.shape[1]
    cs = _mat_f32(_tri(tc, True), _split3(c_ref[0])) + carry_c[...]
    fc_ref[0] = cs
    carry_c[...] = cs[tc - 1:tc, :]
    rs = _f32_mat(_split3(r_ref[0]), _tri(tc, False)) + carry_r[...]
    fr_ref[0] = rs
    carry_r[...] = rs[:, tc - 1:tc]


def _fox_cumsum(lf_col, lf_row, tc):
    b, t, _ = lf_col.shape
    return pl.pallas_call(
        _cumsum_kernel,
        grid=(b, t // tc),
        in_specs=[pl.BlockSpec((1, tc, LANES), lambda i, j: (i, j, 0)),
                  pl.BlockSpec((1, 16, tc), lambda i, j: (i, 0, j))],
        out_specs=[pl.BlockSpec((1, tc, LANES), lambda i, j: (i, j, 0)),
                   pl.BlockSpec((1, 16, tc), lambda i, j: (i, 0, j))],
        out_shape=[jax.ShapeDtypeStruct(lf_col.shape, F32), jax.ShapeDtypeStruct(lf_row.shape, F32)],
        scratch_shapes=[pltpu.VMEM((1, LANES), F32), pltpu.VMEM((16, 1), F32)],
        compiler_params=_cparams(("parallel", "arbitrary")),
        name="fox_cumsum",
    )(lf_col, lf_row)


def _fox_attn_kernel(q_ref, k_ref, v_ref, fq_ref, fk_ref, o_ref, m_scr, l_scr, acc_scr, *, tq, tk, past, nk):
    hp = pl.program_id(1)
    qi = pl.program_id(2)
    ki = pl.program_id(3)

    @pl.when(ki == 0)
    def _():
        m_scr[...] = jnp.full_like(m_scr, NEG)
        l_scr[...] = jnp.zeros_like(l_scr)
        acc_scr[...] = jnp.zeros_like(acc_scr)

    q_first = past + qi * tq

    @pl.when(ki * tk <= q_first + tq - 1)
    def _():
        q = q_ref[...]
        k = k_ref[...]
        v = v_ref[...]
        qpos = q_first + lax.broadcasted_iota(jnp.int32, (tq, 1), 0)
        kpos = ki * tk + lax.broadcasted_iota(jnp.int32, (1, tk), 1)
        visible = kpos <= qpos
        lane = lax.broadcasted_iota(jnp.int32, (1, LANES), 1)
        fq_all = fq_ref[0]
        for hh in range(2):
            head = 2 * hp + hh
            in_head = (lane >= FOX_HEAD_DIM) if hh else (lane < FOX_HEAD_DIM)
            qh = jnp.where(in_head, q, jnp.zeros_like(q))
            fq = jnp.sum(jnp.where(lane == head, fq_all, 0.0), axis=-1, keepdims=True)
            fk = fk_ref[0, pl.ds(head, 1), :]
            s = _dot_nt(qh, k) + fq - fk
            s = jnp.where(visible, s, NEG)
            m_prev = m_scr[hh]
            m_new = jnp.maximum(m_prev, jnp.max(s, axis=-1, keepdims=True))
            alpha = jnp.exp(m_prev - m_new)
            p = jnp.exp(s - m_new)
            l_scr[hh] = alpha * l_scr[hh] + jnp.sum(p, axis=-1, keepdims=True)
            acc_scr[hh] = alpha * acc_scr[hh] + _dot(p.astype(BF16), v)
            m_scr[hh] = m_new

    @pl.when(ki == nk - 1)
    def _():
        lane = lax.broadcasted_iota(jnp.int32, (1, LANES), 1)
        o0 = acc_scr[0] / l_scr[0]
        o1 = acc_scr[1] / l_scr[1]
        o_ref[...] = jnp.where(lane < FOX_HEAD_DIM, o0, o1).astype(o_ref.dtype)


def _fox_attention(q, k, v, f_col, f_row, batch, tq_len, past, tq, tk):
    tk_len = f_col.shape[1]
    nq, nk = tq_len // tq, tk_len // tk
    pairs = FOX_HEADS // 2

    def last_kv(i):
        return (past + i * tq + tq - 1) // tk

    kv_spec = pl.BlockSpec((tk, LANES), lambda b, h, i, j: (b * nk + jnp.minimum(j, last_kv(i)), h))
    return pl.pallas_call(
        functools.partial(_fox_attn_kernel, tq=tq, tk=tk, past=past, nk=nk),
        grid=(batch, pairs, nq, nk),
        in_specs=[pl.BlockSpec((tq, LANES), lambda b, h, i, j: (b * nq + i, h)),
                  kv_spec, kv_spec,
                  pl.BlockSpec((1, tq, LANES), lambda b, h, i, j: (b, past // tq + i, 0)),
                  pl.BlockSpec((1, 16, tk), lambda b, h, i, j: (b, 0, jnp.minimum(j, last_kv(i))))],
        out_specs=pl.BlockSpec((tq, LANES), lambda b, h, i, j: (b * nq + i, h)),
        out_shape=jax.ShapeDtypeStruct((batch * tq_len, FOX_WIDTH), BF16),
        scratch_shapes=[pltpu.VMEM((2, tq, 1), F32), pltpu.VMEM((2, tq, 1), F32),
                        pltpu.VMEM((2, tq, LANES), F32)],
        compiler_params=_cparams(("parallel", "parallel", "parallel", "arbitrary")),
        name="fox_attention",
    )(q, k, v, f_col, f_row)


def _mlstm_kernel(mqk_ref, mv_ref, mo_ref, gc_ref, gr_ref, c0_ref, n0_ref, m0_ref, buf_ref,
                  cw_ref, cb_ref, gml_ref,
                  h_ref, c_out, n_out, m_out, buf_out,
                  c_scr, n_scr, m_scr, prev_scr, *, L, t_last, nc):
    ci = pl.program_id(1)
    rows = mqk_ref.shape[0]

    @pl.when(ci == 0)
    def _():
        c_scr[...] = c0_ref[0]
        n_scr[...] = n0_ref[0]
        m_scr[...] = m0_ref[0]
        prev_scr[...] = buf_ref[0]

    u = _pad_rows(mqk_ref[...], L)
    ext = jnp.concatenate([prev_scr[...], u], axis=0)
    y = cb_ref[...] + cw_ref[ML_CONV - 1:ML_CONV, :] * u
    for s in range(1, ML_CONV):
        y = y + cw_ref[ML_CONV - 1 - s:ML_CONV - s, :] * pltpu.roll(ext, s, axis=0)[8:8 + L]
    qk = y * _sigmoid(y)
    prev_scr[...] = u[L - 8:L]

    gc = _pad_rows(gc_ref[...], L)
    gr = gr_ref[0]
    b_col = _mat_f32(_tri(L, True), _split3(gc))
    b_row = _f32_mat(_split3(gr), _tri(L, False))
    v_all = _pad_rows(mv_ref[...], L)
    o_all = _pad_rows(mo_ref[...], L)
    t_idx = lax.broadcasted_iota(jnp.int32, (L, 1), 0)
    causal = lax.broadcasted_iota(jnp.int32, (1, L), 1) <= t_idx

    for h in range(ML_HEADS):
        hs = slice(h * ML_HEAD_DIM, (h + 1) * ML_HEAD_DIM)
        q32 = qk[:, hs]
        k32 = qk[:, ML_WIDTH + h * ML_HEAD_DIM:ML_WIDTH + (h + 1) * ML_HEAD_DIM] * (ML_HEAD_DIM ** -0.5)
        qb, kb, vb = q32.astype(BF16), k32.astype(BF16), v_all[:, hs]
        bc = b_col[:, G_MF + h:G_MF + h + 1]
        ic = gc[:, G_MI + h:G_MI + h + 1]
        br = b_row[G_MF + h:G_MF + h + 1, :]
        ir = gr[G_MI + h:G_MI + h + 1, :]
        m_prev = m_scr[h][:, 0:1]
        c_prev = c_scr[h]
        n_prev = n_scr[h]

        dmat = jnp.where(causal, bc - br + ir, NEG)
        inter = bc + m_prev
        m_t = jnp.maximum(inter, jnp.max(dmat, axis=-1, keepdims=True))
        w = jnp.exp(dmat - m_t)
        g = jnp.exp(inter - m_t)
        a = w * _dot_nt(qb, kb)
        num = g * _dot_nt(qb, c_prev.astype(BF16)) + _dot(a.astype(BF16), vb)
        den = g * jnp.sum(q32 * n_prev, axis=-1, keepdims=True) + jnp.sum(a, axis=-1, keepdims=True)
        hh = num / jnp.maximum(jnp.abs(den), jnp.exp(-m_t))

        b_last = bc[t_last:t_last + 1, :]
        m_last = m_t[t_last:t_last + 1, :]
        g_end = g[t_last:t_last + 1, :]
        w_end = jnp.where(t_idx <= t_last, jnp.exp(b_last - bc + ic - m_last), 0.0)
        c_scr[h] = g_end * c_prev + _dot_tn((vb.astype(F32) * w_end).astype(BF16), kb)
        n_scr[h] = g_end * n_prev + jnp.sum(k32 * w_end, axis=0, keepdims=True)
        m_scr[h] = jnp.broadcast_to(m_last, (1, LANES))

        yh = hh * lax.rsqrt(jnp.mean(hh * hh, axis=-1, keepdims=True) + EPS) * gml_ref[:, hs]
        yh = yh * _sigmoid(o_all[:, hs].astype(F32))
        h_ref[:, hs] = yh[:rows].astype(h_ref.dtype)

    @pl.when(ci == nc - 1)
    def _():
        c_out[0] = c_scr[...]
        n_out[0] = n_scr[...]
        m_out[0] = m_scr[...]
        buf_out[0] = ext[t_last + 1:t_last + 9]


def _mlstm(mqk, mv, mo, gc, gr, c0, n0, m0, buf8, cw, cb, gml, batch, t_len):
    L = ML_CHUNK if t_len >= ML_CHUNK else LANES
    rows = min(L, t_len)
    nc = t_len // rows
    t_last = rows - 1
    assert t_len % rows == 0 and (nc == 1 or rows == L) and (t_last + 1) % 8 == 0
    tok = lambda c: pl.BlockSpec((rows, c), lambda b, i: (b * nc + i, 0))
    st = lambda *s: pl.BlockSpec((1,) + s, lambda b, i: (b,) + (0,) * len(s))
    shp_c = (ML_HEADS, ML_HEAD_DIM, ML_HEAD_DIM)
    shp_n = (ML_HEADS, 1, ML_HEAD_DIM)
    return pl.pallas_call(
        functools.partial(_mlstm_kernel, L=L, t_last=t_last, nc=nc),
        grid=(batch, nc),
        in_specs=[tok(2 * ML_WIDTH), tok(ML_WIDTH), tok(ML_WIDTH), tok(LANES),
                  pl.BlockSpec((1, 16, L), lambda b, i: (b, 0, i)),
                  st(*shp_c), st(*shp_n), st(*shp_n), st(8, 2 * ML_WIDTH),
                  _const_spec((ML_CONV, 2 * ML_WIDTH)), _const_spec((1, 2 * ML_WIDTH)), _const_spec((1, ML_WIDTH))],
        out_specs=[tok(ML_WIDTH), st(*shp_c), st(*shp_n), st(*shp_n), st(8, 2 * ML_WIDTH)],
        out_shape=[jax.ShapeDtypeStruct((batch * t_len, ML_WIDTH), BF16),
                   jax.ShapeDtypeStruct((batch,) + shp_c, F32),
                   jax.ShapeDtypeStruct((batch,) + shp_n, F32),
                   jax.ShapeDtypeStruct((batch,) + shp_n, F32),
                   jax.ShapeDtypeStruct((batch, 8, 2 * ML_WIDTH), F32)],
        scratch_shapes=[pltpu.VMEM(shp_c, F32), pltpu.VMEM(shp_n, F32), pltpu.VMEM(shp_n, F32),
                        pltpu.VMEM((8, 2 * ML_WIDTH), F32)],
        compiler_params=_cparams(("parallel", "arbitrary")),
        name="mlstm",
    )(mqk, mv, mo, gc, gr, c0, n0, m0, buf8, cw, cb, gml)


def _gla_level_matrices(L):
    mats = np.zeros((len(GLA_LEVELS), L, L), np.float32)
    for li, b in enumerate(GLA_LEVELS):
        for t in range(L):
            base = (t // (2 * b)) * 2 * b
            bound = base + b - 1
            if t > bound:
                mats[li, t, bound + 1:t + 1] = 1.0
            else:
                mats[li, t, t + 1:bound + 1] = 1.0
    return jnp.asarray(mats, BF16)


def _gla_kernel(q_ref, k_ref, v_ref, gg_ref, la_ref, s0_ref, lvl_ref, gn_ref, o_ref, s_out, s_scr, *, L, t_last, nc):
    ci = pl.program_id(1)
    rows = q_ref.shape[0]

    @pl.when(ci == 0)
    def _():
        s_scr[...] = s0_ref[0]

    la3 = _split3(_pad_rows(la_ref[...], L))
    cb = _mat_f32(_tri(L, True), la3)
    q_all = _pad_rows(q_ref[...], L)
    k_all = _pad_rows(k_ref[...], L)
    v_all = _pad_rows(v_ref[...], L)
    g_all = _pad_rows(gg_ref[...], L)
    t_idx = lax.broadcasted_iota(jnp.int32, (L, 1), 0)
    s_idx = lax.broadcasted_iota(jnp.int32, (1, L), 1)

    a_heads = []
    q32s, k32s = [], []
    for h in range(GLA_HEADS):
        ks = slice(h * GLA_DK, (h + 1) * GLA_DK)
        q32s.append(q_all[:, ks].astype(F32))
        k32s.append(k_all[:, ks].astype(F32))
        a_heads.append(jnp.where(t_idx == s_idx, _dot_nt(q_all[:, ks], k_all[:, ks]), 0.0))
    for li, b in enumerate(GLA_LEVELS):
        e_all = jnp.exp(_mat_f32(lvl_ref[li], la3))
        sh = b.bit_length() - 1
        second = (jnp.right_shift(t_idx, sh) & 1) == 1
        same_pair = jnp.right_shift(t_idx, sh + 1) == jnp.right_shift(s_idx, sh + 1)
        for h in range(GLA_HEADS):
            ks = slice(h * GLA_DK, (h + 1) * GLA_DK)
            e = e_all[:, ks]
            qe = jnp.where(second, q32s[h] * e, 0.0).astype(BF16)
            ke = jnp.where(second, 0.0, k32s[h] * e).astype(BF16)
            a_heads[h] = a_heads[h] + jnp.where(same_pair, _dot_nt(qe, ke), 0.0)

    for h in range(GLA_HEADS):
        ks = slice(h * GLA_DK, (h + 1) * GLA_DK)
        vs = slice(h * GLA_DV, (h + 1) * GLA_DV)
        cbh = cb[:, ks]
        vb = v_all[:, vs]
        s_t = s_scr[h]
        o = _dot_nt((q32s[h] * jnp.exp(cbh)).astype(BF16), s_t.astype(BF16)) + _dot(a_heads[h].astype(BF16), vb)
        cl = cbh[t_last:t_last + 1, :]
        kd = jnp.where(t_idx <= t_last, k32s[h] * jnp.exp(cl - cbh), 0.0).astype(BF16)
        s_scr[h] = jnp.exp(cl) * s_t + _dot_tn(vb, kd)
        y = o * lax.rsqrt(jnp.mean(o * o, axis=-1, keepdims=True) + EPS) * gn_ref[:, vs]
        gate = g_all[:, vs].astype(F32)
        y = y * (gate * _sigmoid(gate))
        o_ref[:, vs] = y[:rows].astype(o_ref.dtype)

    @pl.when(ci == nc - 1)
    def _():
        s_out[0] = s_scr[...]


def _gla(q, k, v, gg, la, s0t, gn, batch, t_len):
    L = GLA_CHUNK
    rows = min(L, t_len)
    nc = t_len // rows
    t_last = rows - 1
    assert t_len % rows == 0 and (nc == 1 or rows == L)
    tok = lambda c: pl.BlockSpec((rows, c), lambda b, i: (b * nc + i, 0))
    shp_s = (GLA_HEADS, GLA_DV, GLA_DK)
    st = pl.BlockSpec((1,) + shp_s, lambda b, i: (b, 0, 0, 0))
    levels = _gla_level_matrices(L)
    return pl.pallas_call(
        functools.partial(_gla_kernel, L=L, t_last=t_last, nc=nc),
        grid=(batch, nc),
        in_specs=[tok(GLA_KW), tok(GLA_KW), tok(GLA_VW), tok(GLA_VW), tok(GLA_KW), st,
                  _const_spec(levels.shape), _const_spec((1, GLA_VW))],
        out_specs=[tok(GLA_VW), st],
        out_shape=[jax.ShapeDtypeStruct((batch * t_len, GLA_VW), BF16),
                   jax.ShapeDtypeStruct((batch,) + shp_s, F32)],
        scratch_shapes=[pltpu.VMEM(shp_s, F32)],
        compiler_params=_cparams(("parallel", "arbitrary")),
        name="gla",
    )(q, k, v, gg, la, s0t, levels, gn)


def _pack_even(w, b_fox_f, b_i, b_f):
    d = w.shape[0]
    o = np.cumsum((0, FOX_WIDTH, FOX_WIDTH, FOX_WIDTH, FOX_HEADS, 2 * ML_WIDTH, ML_WIDTH, ML_WIDTH, ML_HEADS, ML_HEADS))
    n_gate = FOX_HEADS + 2 * ML_HEADS
    wp = jnp.concatenate([w[:, o[0]:o[3]], w[:, o[4]:o[7]], w[:, o[3]:o[4]], w[:, o[7]:o[9]],
                          jnp.zeros((d, LANES - n_gate), w.dtype)], axis=1).astype(BF16)
    bias = jnp.concatenate([b_fox_f, b_i, b_f, jnp.zeros((LANES - n_gate,), F32)]).reshape(1, LANES)
    return wp, bias


def _pack_odd(w, w_a2):
    d = w.shape[0]
    wp = jnp.concatenate([w, jnp.zeros((d, LANES - GLA_RANK), w.dtype)], axis=1).astype(BF16)
    wa2 = jnp.concatenate([w_a2, jnp.zeros((LANES - GLA_RANK, w_a2.shape[1]), w_a2.dtype)], axis=0).astype(BF16)
    return wp, wa2


def _gate_rows(gc, batch, t_len, t_pad):
    g = gc.reshape(batch, t_len, LANES)[:, :, :16].transpose(0, 2, 1)
    if t_pad > t_len:
        g = jnp.pad(g, ((0, 0), (0, 0), (0, t_pad - t_len)))
    return g


def _trunk(x, fox_k, fox_v, fox_lf, ml_c, ml_n, ml_m, ml_buf, gla_s, params):
    (norm_mix, norm_ffn, norm_final, w_even, bias_even, conv_w, conv_b, g_ml,
     w_odd, w_a2, b_a, g_gla, w_out, w_ff1, w_ff2) = params
    batch, t_len, d = x.shape
    past = fox_k.shape[2]
    n = batch * t_len
    tm = 512 if n % 512 == 0 else 256 if n % 256 == 0 else t_len
    depth = norm_mix.shape[0]
    xf = x.reshape(n, d)
    ev_states, odd_states = [], []
    y = None
    for layer in range(depth):
        j = layer // 2
        if layer % 2 == 0:
            q, kf, vf, kb, vb, mqk, mv, mo, gc = _inproj_even(xf, norm_mix[layer][None], w_even[j], bias_even[j], tm)
            tq = min(512, t_len)
            if past == 0:
                tk_len, tk = t_len, tq
                k_all, v_all = kb, vb
                lf_col = gc.reshape(batch, t_len, LANES)
            else:
                tk_len = -(-(past + t_len) // LANES) * LANES
                tk = tk_len
                extra = tk_len - past - t_len
                cat = lambda old, new: jnp.concatenate(
                    [old.reshape(batch, past, FOX_WIDTH), new.reshape(batch, t_len, FOX_WIDTH),
                     jnp.zeros((batch, extra, FOX_WIDTH), F32)], axis=1).astype(BF16).reshape(batch * tk_len, FOX_WIDTH)
                k_all, v_all = cat(fox_k[j], kf), cat(fox_v[j], vf)
                lf_col = jnp.concatenate(
                    [jnp.pad(fox_lf[j], ((0, 0), (0, 0), (0, LANES - FOX_HEADS))), gc.reshape(batch, t_len, LANES),
                     jnp.zeros((batch, extra, LANES), F32)], axis=1)
            lf_row = lf_col[:, :, :16].transpose(0, 2, 1)
            f_col, f_row = _fox_cumsum(lf_col, lf_row, min(512, tk_len) if past == 0 else LANES)
            attn = _fox_attention(q, k_all, v_all, f_col, f_row, batch, t_len, past, tq, tk)
            l_ml = ML_CHUNK if t_len >= ML_CHUNK else LANES
            gr = _gate_rows(gc, batch, t_len, max(t_len, l_ml))
            c0 = ml_c[j]
            n0 = ml_n[j][:, :, None, :]
            m0 = jnp.broadcast_to(ml_m[j][:, :, None, None], (batch, ML_HEADS, 1, LANES))
            buf8 = jnp.pad(ml_buf[j], ((0, 0), (8 - (ML_CONV - 1), 0), (0, 0)))
            h_ml, c_new, n_new, m_new, buf_new = _mlstm(mqk, mv, mo, gc, gr, c0, n0, m0, buf8,
                                                        conv_w[j], conv_b[j][None], g_ml[j][None], batch, t_len)
            ev_states.append((kf.reshape(batch, t_len, FOX_HEADS, FOX_HEAD_DIM),
                              vf.reshape(batch, t_len, FOX_HEADS, FOX_HEAD_DIM),
                              gc[:, :FOX_HEADS].reshape(batch, t_len, FOX_HEADS),
                              c_new, n_new[:, :, 0, :], m_new[:, :, 0, 0], buf_new[:, 8 - (ML_CONV - 1):, :]))
            mixes = [attn, h_ml]
        else:
            q, k, v, gg, la = _inproj_odd(xf, norm_mix[layer][None], w_odd[j], w_a2[j], b_a[j][None], tm)
            s0t = gla_s[j].transpose(0, 1, 3, 2)
            o, s_new = _gla(q, k, v, gg, la, s0t, g_gla[j][None], batch, t_len)
            odd_states.append(s_new.transpose(0, 1, 3, 2))
            mixes = [o]
        last = layer == depth - 1
        out = _post(xf, mixes, w_out[layer], norm_ffn[layer][None], w_ff1[layer], w_ff2[layer],
                    norm_final[None] if last else None, tm)
        if last:
            y = out
        else:
            xf = out
    ev = [jnp.stack([s[i] for s in ev_states]) for i in range(7)]
    return y.reshape(batch, t_len, d), ev, jnp.stack(odd_states)


def kernel(x_prompt, x_sample, cache_fox_k, cache_fox_v, cache_fox_logf, state_mlstm_c, state_mlstm_n, state_mlstm_m, state_mlstm_conv, state_gla_s, norm_mix, norm_ffn, norm_final, w_in_even, b_fox_f, conv_w_ml, conv_b_ml, b_ml_i, b_ml_f, g_ml, w_in_odd, w_gla_a2, b_gla_a, g_gla, w_out, w_ff1, w_ff2):
    n_even, n_odd = w_in_even.shape[0], w_in_odd.shape[0]
    packed_even = [_pack_even(w_in_even[j], b_fox_f[j], b_ml_i[j], b_ml_f[j]) for j in range(n_even)]
    packed_odd = [_pack_odd(w_in_odd[j], w_gla_a2[j]) for j in range(n_odd)]
    params = (norm_mix, norm_ffn, norm_final,
              [p[0] for p in packed_even], [p[1] for p in packed_even], conv_w_ml, conv_b_ml, g_ml,
              [p[0] for p in packed_odd], [p[1] for p in packed_odd], b_gla_a, g_gla,
              w_out.astype(BF16), w_ff1.astype(BF16), w_ff2.astype(BF16))

    bp = x_prompt.shape[0]
    dt = x_prompt.dtype
    zeros = lambda *s: jnp.zeros(s, dt)
    y_p, ev_p, gla_p = _trunk(
        x_prompt,
        zeros(n_even, bp, 0, FOX_HEADS, FOX_HEAD_DIM), zeros(n_even, bp, 0, FOX_HEADS, FOX_HEAD_DIM),
        zeros(n_even, bp, 0, FOX_HEADS),
        zeros(n_even, bp, ML_HEADS, ML_HEAD_DIM, ML_HEAD_DIM), zeros(n_even, bp, ML_HEADS, ML_HEAD_DIM),
        zeros(n_even, bp, ML_HEADS), zeros(n_even, bp, ML_CONV - 1, 2 * ML_WIDTH),
        zeros(n_odd, bp, GLA_HEADS, GLA_DK, GLA_DV), params)
    y_s, ev_s, gla_s = _trunk(x_sample, cache_fox_k, cache_fox_v, cache_fox_logf, state_mlstm_c, state_mlstm_n,
                              state_mlstm_m, state_mlstm_conv, state_gla_s, params)
    return (y_p, y_s, *ev_p, gla_p, *ev_s, gla_s)
```

```python
import functools

import numpy as np
import jax
import jax.numpy as jnp
from jax import lax
from jax.experimental import pallas as pl
from jax.experimental.pallas import tpu as pltpu

F32 = jnp.float32
BF16 = jnp.bfloat16
EPS = 1e-6
NEG = -1e30
LOG2E = 1.4426950408889634

LANES = 128
VMEM_LIMIT = 56 * 1024 * 1024

D_MODEL = 1024
D_FF = 4 * D_MODEL
FOX_HEADS, FOX_HEAD_DIM = 8, 64
FOX_WIDTH = FOX_HEADS * FOX_HEAD_DIM
ML_HEADS, ML_HEAD_DIM = 4, 128
ML_WIDTH = ML_HEADS * ML_HEAD_DIM
ML_CONV = 4
GLA_HEADS, GLA_DK, GLA_DV = 4, 128, 256
GLA_KW = GLA_HEADS * GLA_DK
GLA_VW = GLA_HEADS * GLA_DV
GLA_RANK = 16
GLA_TAU = 16.0

E_Q, E_K, E_V, E_MQK, E_MV, E_MO, E_G, E_END = 0, 512, 1024, 1536, 2560, 3072, 3584, 3712
G_FOX, G_MI, G_MF = 0, 8, 12
O_Q, O_K, O_V, O_G, O_A, O_END = 0, 512, 1024, 2048, 3072, 3200

ML_CHUNK = 256
GLA_CHUNK = 128
GLA_LEVELS = (64, 32, 16, 8, 4, 2, 1)


def _cparams(sem):
    return pltpu.CompilerParams(dimension_semantics=sem, vmem_limit_bytes=VMEM_LIMIT)


def _const_spec(shape):
    nd = len(shape)
    return pl.BlockSpec(shape, lambda *_: (0,) * nd, pipeline_mode=pl.Buffered(1))


def _rms(x, g):
    return x * lax.rsqrt(jnp.mean(x * x, axis=-1, keepdims=True) + EPS) * g


def _sigmoid(x):
    return 1.0 / (1.0 + jnp.exp(-x))


def _log_sigmoid(x):
    return -(jnp.maximum(-x, 0.0) + jnp.log1p(jnp.exp(-jnp.abs(x))))


def _dot(a, b):
    return jnp.dot(a, b, preferred_element_type=F32)


def _dot_nt(a, b):
    return lax.dot_general(a, b, (((1,), (1,)), ((), ())), preferred_element_type=F32)


def _dot_tn(a, b):
    return lax.dot_general(a, b, (((0,), (0,)), ((), ())), preferred_element_type=F32)


def _split3(x):
    hi = x.astype(BF16)
    r1 = x - hi.astype(F32)
    mid = r1.astype(BF16)
    lo = (r1 - mid.astype(F32)).astype(BF16)
    return hi, mid, lo


def _mat_f32(m, parts):
    return _dot(m, parts[0]) + _dot(m, parts[1]) + _dot(m, parts[2])


def _f32_mat(parts, m):
    return _dot(parts[0], m) + _dot(parts[1], m) + _dot(parts[2], m)


def _tri(n, lower):
    r = lax.broadcasted_iota(jnp.int32, (n, n), 0)
    c = lax.broadcasted_iota(jnp.int32, (n, n), 1)
    keep = (c <= r) if lower else (r <= c)
    return jnp.where(keep, 1.0, 0.0).astype(BF16)


def _pad_rows(a, n):
    if a.shape[0] == n:
        return a
    return jnp.concatenate([a, jnp.zeros((n - a.shape[0], a.shape[1]), a.dtype)], axis=0)


def _inproj_even_kernel(x_ref, g_ref, w_ref, bias_ref,
                        qt_ref, kf_ref, vf_ref, kb_ref, vt_ref, mqk_ref, mv_ref, mo_ref, gc_ref):
    h = _rms(x_ref[...], g_ref[...]).astype(BF16)

    def seg(a, b):
        return _dot(h, w_ref[:, a:b])

    qt_ref[...] = (seg(E_Q, E_K) * (FOX_HEAD_DIM ** -0.5 * LOG2E)).T.astype(BF16)
    k = seg(E_K, E_V)
    kf_ref[...] = k
    kb_ref[...] = k.astype(BF16)
    v = seg(E_V, E_MQK)
    vf_ref[...] = v
    vt_ref[...] = v.T.astype(BF16)
    mqk_ref[...] = seg(E_MQK, E_MV)
    mv_ref[...] = seg(E_MV, E_MO).astype(BF16)
    mo_ref[...] = seg(E_MO, E_G).astype(BF16)
    gz = seg(E_G, E_END) + bias_ref[...]
    lane = lax.broadcasted_iota(jnp.int32, gz.shape, 1)
    is_log = (lane < G_MI) | (lane >= G_MF)
    gc_ref[...] = jnp.where(is_log, _log_sigmoid(gz), gz)


def _inproj_even(x, g, w, bias, tm):
    n = x.shape[0]
    row = lambda c: pl.BlockSpec((tm, c), lambda i: (i, 0))
    col = pl.BlockSpec((FOX_WIDTH, tm), lambda i: (0, i))
    outs = [(None, BF16), (FOX_WIDTH, F32), (FOX_WIDTH, F32), (FOX_WIDTH, BF16), (None, BF16),
            (2 * ML_WIDTH, F32), (ML_WIDTH, BF16), (ML_WIDTH, BF16), (LANES, F32)]
    return pl.pallas_call(
        _inproj_even_kernel,
        grid=(n // tm,),
        in_specs=[row(D_MODEL), _const_spec((1, D_MODEL)), _const_spec((D_MODEL, E_END)), _const_spec((1, LANES))],
        out_specs=[col if c is None else row(c) for c, _ in outs],
        out_shape=[jax.ShapeDtypeStruct((FOX_WIDTH, n) if c is None else (n, c), dt) for c, dt in outs],
        compiler_params=_cparams(("parallel",)),
        name="inproj_even",
    )(x, g, w, bias)


def _inproj_odd_kernel(x_ref, g_ref, w_ref, wa2_ref, ba_ref, q_ref, k_ref, v_ref, gg_ref, la_ref):
    h = _rms(x_ref[...], g_ref[...]).astype(BF16)

    def seg(a, b):
        return _dot(h, w_ref[:, a:b])

    q_ref[...] = (seg(O_Q, O_K) * (GLA_DK ** -0.5)).astype(BF16)
    k_ref[...] = seg(O_K, O_V).astype(BF16)
    v_ref[...] = seg(O_V, O_G).astype(BF16)
    gg_ref[...] = seg(O_G, O_A).astype(BF16)
    ga = seg(O_A, O_END).astype(BF16)
    la_ref[...] = _log_sigmoid(_dot(ga, wa2_ref[...]) + ba_ref[...]) * (1.0 / GLA_TAU)


def _inproj_odd(x, g, w, wa2, ba, tm):
    n = x.shape[0]
    row = lambda c: pl.BlockSpec((tm, c), lambda i: (i, 0))
    outs = [(GLA_KW, BF16), (GLA_KW, BF16), (GLA_VW, BF16), (GLA_VW, BF16), (GLA_KW, F32)]
    return pl.pallas_call(
        _inproj_odd_kernel,
        grid=(n // tm,),
        in_specs=[row(D_MODEL), _const_spec((1, D_MODEL)), _const_spec((D_MODEL, O_END)),
                  _const_spec((LANES, GLA_KW)), _const_spec((1, GLA_KW))],
        out_specs=[row(c) for c, _ in outs],
        out_shape=[jax.ShapeDtypeStruct((n, c), dt) for c, dt in outs],
        compiler_params=_cparams(("parallel",)),
        name="inproj_odd",
    )(x, g, w, wa2, ba)


def _post_kernel(*refs, n_mix, final):
    x_ref = refs[0]
    mix_refs = refs[1:1 + n_mix]
    wo_ref, gf_ref, w1_ref, w2_ref = refs[1 + n_mix:5 + n_mix]
    rest = refs[5 + n_mix:]
    mix = mix_refs[0][...] if n_mix == 1 else jnp.concatenate([r[...] for r in mix_refs], axis=1)
    x1 = x_ref[...] + _dot(mix, wo_ref[...])
    h = _rms(x1, gf_ref[...]).astype(BF16)
    y = x1
    for c in range(D_FF // D_MODEL):
        sl = slice(c * D_MODEL, (c + 1) * D_MODEL)
        t = jnp.maximum(_dot(h, w1_ref[:, sl]), 0.0)
        y = y + _dot((t * t).astype(BF16), w2_ref[sl, :])
    if final:
        gfin_ref, out_ref = rest
        out_ref[...] = _rms(y, gfin_ref[...])
    else:
        (out_ref,) = rest
        out_ref[...] = y


def _post(x, mixes, wo, gf, w1, w2, gfin, tm):
    n = x.shape[0]
    row = lambda c: pl.BlockSpec((tm, c), lambda i: (i, 0))
    final = gfin is not None
    ins = [x, *mixes, wo, gf, w1, w2]
    specs = [row(D_MODEL)] + [row(m.shape[1]) for m in mixes] + [
        _const_spec((D_MODEL, D_MODEL)), _const_spec((1, D_MODEL)),
        _const_spec((D_MODEL, D_FF)), _const_spec((D_FF, D_MODEL))]
    if final:
        ins.append(gfin)
        specs.append(_const_spec((1, D_MODEL)))
    return pl.pallas_call(
        functools.partial(_post_kernel, n_mix=len(mixes), final=final),
        grid=(n // tm,),
        in_specs=specs,
        out_specs=row(D_MODEL),
        out_shape=jax.ShapeDtypeStruct((n, D_MODEL), F32),
        compiler_params=_cparams(("parallel",)),
        name="post_final" if final else "post",
    )(*ins)


F_TERMS = 3


def _fox_placement():
    p = np.zeros((F_TERMS, LANES, FOX_HEADS // 2 * LANES), np.float32)
    for x in range(F_TERMS):
        for h in range(FOX_HEADS):
            p[x, h, (h // 2) * LANES + F_TERMS * (h % 2) + x] = 1.0
    return jnp.asarray(p, BF16)


def _fox_prep_kernel(lf_ref, k_ref, place_ref, ka_ref, carry):
    @pl.when(pl.program_id(1) == 0)
    def _():
        carry[...] = jnp.zeros_like(carry)

    tc = lf_ref.shape[1]
    f = _mat_f32(_tri(tc, True), _split3(lf_ref[0])) + carry[...]
    carry[...] = f[tc - 1:tc, :]
    parts = _split3(f * LOG2E)
    cols = _dot(parts[0], place_ref[0]) + _dot(parts[1], place_ref[1]) + _dot(parts[2], place_ref[2])
    for hp in range(FOX_HEADS // 2):
        ka_ref[:, 2 * hp * LANES:(2 * hp + 1) * LANES] = k_ref[:, hp * LANES:(hp + 1) * LANES]
        ka_ref[:, (2 * hp + 1) * LANES:(2 * hp + 2) * LANES] = cols[:, hp * LANES:(hp + 1) * LANES].astype(BF16)


def _fox_prep(lf_col, k, tc):
    b, t, _ = lf_col.shape
    nt = t // tc
    place = _fox_placement()
    return pl.pallas_call(
        _fox_prep_kernel,
        grid=(b, nt),
        in_specs=[pl.BlockSpec((1, tc, LANES), lambda i, j: (i, j, 0)),
                  pl.BlockSpec((tc, FOX_WIDTH), lambda i, j: (i * nt + j, 0)),
                  _const_spec(place.shape)],
        out_specs=pl.BlockSpec((tc, 2 * FOX_WIDTH), lambda i, j: (i * nt + j, 0)),
        out_shape=jax.ShapeDtypeStruct((b * t, 2 * FOX_WIDTH), BF16),
        scratch_shapes=[pltpu.VMEM((1, LANES), F32)],
        compiler_params=_cparams(("parallel", "arbitrary")),
        name="fox_prep",
    )(lf_col, k, place)


ONES_ROWS = 16


def _fox_attn_kernel(qt_ref, ka_ref, vt_ref, o_ref, m_scr, acc_scr, *, tq, tkc, past):
    q_first = past + pl.program_id(2) * tq
    n_full = q_first // tkc
    drow = lax.broadcasted_iota(jnp.int32, (LANES, 1), 0)
    lane = lax.broadcasted_iota(jnp.int32, (1, 2 * tq), 1)
    lane_head = jnp.where(lane < tq, 0, 1)
    own_head = jnp.where(jnp.where(drow < FOX_HEAD_DIM, 0, 1) == lane_head, 1.0, 0.0).astype(BF16)
    q2 = qt_ref[...] * own_head
    f_sel = (drow >= F_TERMS * lane_head) & (drow < F_TERMS * lane_head + F_TERMS)
    qa = jnp.concatenate([q2, jnp.where(f_sel, -1.0, 0.0).astype(BF16)], axis=0)
    ones = jnp.ones((ONES_ROWS, tkc), BF16)
    m_scr[...] = jnp.full_like(m_scr, NEG)
    acc_scr[...] = jnp.zeros_like(acc_scr)

    def chunk(j, masked):
        start = pl.multiple_of(j * tkc, tkc)
        st = _dot(ka_ref[pl.ds(start, tkc), :], qa)
        if masked:
            kpos = start + lax.broadcasted_iota(jnp.int32, (tkc, 1), 0)
            qpos = q_first + (lane & (tq - 1))
            st = jnp.where(kpos <= qpos, st, NEG)
        m_prev = m_scr[...]
        m_new = jnp.maximum(m_prev, jnp.max(st, axis=0, keepdims=True))
        alpha = jnp.exp2(m_prev - m_new)
        p = jnp.exp2(st - m_new).astype(BF16)
        va = jnp.concatenate([vt_ref[:, pl.ds(start, tkc)], ones], axis=0)
        acc_scr[...] = alpha * acc_scr[...] + _dot(va, p)
        m_scr[...] = m_new

    def body(j, carry):
        chunk(j, False)
        return carry

    lax.fori_loop(0, n_full, body, 0)
    chunk(n_full, True)
    acc = acc_scr[...]
    out = acc[0:LANES] / acc[LANES:LANES + 1]
    out = jnp.concatenate([out[0:FOX_HEAD_DIM, 0:tq], out[FOX_HEAD_DIM:LANES, tq:2 * tq]], axis=0)
    o_ref[...] = out.T.astype(o_ref.dtype)


def _fox_attention(qt, ka, vt, batch, tq_len, tk_len, past, tq, tkc):
    nq = tq_len // tq
    assert tq & (tq - 1) == 0 and tq % LANES == 0 and tkc % tq == 0 and past % tkc == 0
    assert tq_len % tq == 0 and tk_len >= past + tq_len and tk_len % tkc == 0
    pairs = FOX_HEADS // 2
    qd = qt.reshape(FOX_WIDTH, batch * nq, 1, tq)
    qd = jnp.broadcast_to(qd, (FOX_WIDTH, batch * nq, 2, tq)).reshape(FOX_WIDTH, batch * nq * 2 * tq)
    return pl.pallas_call(
        functools.partial(_fox_attn_kernel, tq=tq, tkc=tkc, past=past),
        grid=(batch, pairs, nq),
        in_specs=[pl.BlockSpec((LANES, 2 * tq), lambda b, h, i: (h, b * nq + i)),
                  pl.BlockSpec((tk_len, 2 * LANES), lambda b, h, i: (b, h)),
                  pl.BlockSpec((LANES, tk_len), lambda b, h, i: (h, b))],
        out_specs=pl.BlockSpec((tq, LANES), lambda b, h, i: (b * nq + i, h)),
        out_shape=jax.ShapeDtypeStruct((batch * tq_len, FOX_WIDTH), BF16),
        scratch_shapes=[pltpu.VMEM((1, 2 * tq), F32), pltpu.VMEM((LANES + ONES_ROWS, 2 * tq), F32)],
        compiler_params=_cparams(("parallel", "parallel", "arbitrary")),
        name="fox_attention",
    )(qd, ka, vt)


def _mlstm_kernel(mqk_ref, mv_ref, mo_ref, gc_ref, gr_ref, c0_ref, n0_ref, m0_ref, buf_ref,
                  cw_ref, cb_ref, gml_ref,
                  h_ref, c_out, n_out, m_out, buf_out,
                  c_scr, n_scr, m_scr, prev_scr, *, L, t_last, nc):
    ci = pl.program_id(1)
    rows = mqk_ref.shape[0]

    @pl.when(ci == 0)
    def _():
        c_scr[...] = c0_ref[0]
        n_scr[...] = n0_ref[0]
        m_scr[...] = m0_ref[0]
        prev_scr[...] = buf_ref[0]

    u = _pad_rows(mqk_ref[...], L)
    ext = jnp.concatenate([prev_scr[...], u], axis=0)
    y = cb_ref[...] + cw_ref[ML_CONV - 1:ML_CONV, :] * u
    for s in range(1, ML_CONV):
        y = y + cw_ref[ML_CONV - 1 - s:ML_CONV - s, :] * pltpu.roll(ext, s, axis=0)[8:8 + L]
    qk = y * _sigmoid(y)
    prev_scr[...] = u[L - 8:L]

    gc = _pad_rows(gc_ref[...], L)
    gr = gr_ref[0]
    b_col = _mat_f32(_tri(L, True), _split3(gc))
    b_row = _f32_mat(_split3(gr), _tri(L, False))
    v_all = _pad_rows(mv_ref[...], L)
    o_all = _pad_rows(mo_ref[...], L)
    t_idx = lax.broadcasted_iota(jnp.int32, (L, 1), 0)
    causal = lax.broadcasted_iota(jnp.int32, (1, L), 1) <= t_idx

    for h in range(ML_HEADS):
        hs = slice(h * ML_HEAD_DIM, (h + 1) * ML_HEAD_DIM)
        q32 = qk[:, hs]
        k32 = qk[:, ML_WIDTH + h * ML_HEAD_DIM:ML_WIDTH + (h + 1) * ML_HEAD_DIM] * (ML_HEAD_DIM ** -0.5)
        qb, kb, vb = q32.astype(BF16), k32.astype(BF16), v_all[:, hs]
        bc = b_col[:, G_MF + h:G_MF + h + 1]
        ic = gc[:, G_MI + h:G_MI + h + 1]
        br = b_row[G_MF + h:G_MF + h + 1, :]
        ir = gr[G_MI + h:G_MI + h + 1, :]
        m_prev = m_scr[h][:, 0:1]
        c_prev = c_scr[h]
        n_prev = n_scr[h]

        dmat = jnp.where(causal, bc - br + ir, NEG)
        inter = bc + m_prev
        m_t = jnp.maximum(inter, jnp.max(dmat, axis=-1, keepdims=True))
        w = jnp.exp(dmat - m_t)
        g = jnp.exp(inter - m_t)
        a = w * _dot_nt(qb, kb)
        num = g * _dot_nt(qb, c_prev.astype(BF16)) + _dot(a.astype(BF16), vb)
        den = g * jnp.sum(q32 * n_prev, axis=-1, keepdims=True) + jnp.sum(a, axis=-1, keepdims=True)
        hh = num / jnp.maximum(jnp.abs(den), jnp.exp(-m_t))

        b_last = bc[t_last:t_last + 1, :]
        m_last = m_t[t_last:t_last + 1, :]
        g_end = g[t_last:t_last + 1, :]
        w_end = jnp.where(t_idx <= t_last, jnp.exp(b_last - bc + ic - m_last), 0.0)
        c_scr[h] = g_end * c_prev + _dot_tn((vb.astype(F32) * w_end).astype(BF16), kb)
        n_scr[h] = g_end * n_prev + jnp.sum(k32 * w_end, axis=0, keepdims=True)
        m_scr[h] = jnp.broadcast_to(m_last, (1, LANES))

        yh = hh * lax.rsqrt(jnp.mean(hh * hh, axis=-1, keepdims=True) + EPS) * gml_ref[:, hs]
        yh = yh * _sigmoid(o_all[:, hs].astype(F32))
        h_ref[:, hs] = yh[:rows].astype(h_ref.dtype)

    @pl.when(ci == nc - 1)
    def _():
        c_out[0] = c_scr[...]
        n_out[0] = n_scr[...]
        m_out[0] = m_scr[...]
        buf_out[0] = ext[t_last + 1:t_last + 9]


def _mlstm(mqk, mv, mo, gc, gr, c0, n0, m0, buf8, cw, cb, gml, batch, t_len):
    L = ML_CHUNK if t_len >= ML_CHUNK else LANES
    rows = min(L, t_len)
    nc = t_len // rows
    t_last = rows - 1
    assert t_len % rows == 0 and (nc == 1 or rows == L) and (t_last + 1) % 8 == 0
    tok = lambda c: pl.BlockSpec((rows, c), lambda b, i: (b * nc + i, 0))
    st = lambda *s: pl.BlockSpec((1,) + s, lambda b, i: (b,) + (0,) * len(s))
    shp_c = (ML_HEADS, ML_HEAD_DIM, ML_HEAD_DIM)
    shp_n = (ML_HEADS, 1, ML_HEAD_DIM)
    return pl.pallas_call(
        functools.partial(_mlstm_kernel, L=L, t_last=t_last, nc=nc),
        grid=(batch, nc),
        in_specs=[tok(2 * ML_WIDTH), tok(ML_WIDTH), tok(ML_WIDTH), tok(LANES),
                  pl.BlockSpec((1, 16, L), lambda b, i: (b, 0, i)),
                  st(*shp_c), st(*shp_n), st(*shp_n), st(8, 2 * ML_WIDTH),
                  _const_spec((ML_CONV, 2 * ML_WIDTH)), _const_spec((1, 2 * ML_WIDTH)), _const_spec((1, ML_WIDTH))],
        out_specs=[tok(ML_WIDTH), st(*shp_c), st(*shp_n), st(*shp_n), st(8, 2 * ML_WIDTH)],
        out_shape=[jax.ShapeDtypeStruct((batch * t_len, ML_WIDTH), BF16),
                   jax.ShapeDtypeStruct((batch,) + shp_c, F32),
                   jax.ShapeDtypeStruct((batch,) + shp_n, F32),
                   jax.ShapeDtypeStruct((batch,) + shp_n, F32),
                   jax.ShapeDtypeStruct((batch, 8, 2 * ML_WIDTH), F32)],
        scratch_shapes=[pltpu.VMEM(shp_c, F32), pltpu.VMEM(shp_n, F32), pltpu.VMEM(shp_n, F32),
                        pltpu.VMEM((8, 2 * ML_WIDTH), F32)],
        compiler_params=_cparams(("parallel", "arbitrary")),
        name="mlstm",
    )(mqk, mv, mo, gc, gr, c0, n0, m0, buf8, cw, cb, gml)


def _gla_level_matrices(L):
    mats = np.zeros((len(GLA_LEVELS), L, L), np.float32)
    for li, b in enumerate(GLA_LEVELS):
        for t in range(L):
            base = (t // (2 * b)) * 2 * b
            bound = base + b - 1
            if t > bound:
                mats[li, t, bound + 1:t + 1] = 1.0
            else:
                mats[li, t, t + 1:bound + 1] = 1.0
    return jnp.asarray(mats, BF16)


def _gla_kernel(q_ref, k_ref, v_ref, gg_ref, la_ref, s0_ref, lvl_ref, gn_ref, o_ref, s_out, s_scr, *, L, t_last, nc):
    ci = pl.program_id(1)
    rows = q_ref.shape[0]

    @pl.when(ci == 0)
    def _():
        s_scr[...] = s0_ref[0]

    la3 = _split3(_pad_rows(la_ref[...], L))
    cb = _mat_f32(_tri(L, True), la3)
    q_all = _pad_rows(q_ref[...], L)
    k_all = _pad_rows(k_ref[...], L)
    v_all = _pad_rows(v_ref[...], L)
    g_all = _pad_rows(gg_ref[...], L)
    t_idx = lax.broadcasted_iota(jnp.int32, (L, 1), 0)
    s_idx = lax.broadcasted_iota(jnp.int32, (1, L), 1)

    a_heads = []
    q32s, k32s = [], []
    for h in range(GLA_HEADS):
        ks = slice(h * GLA_DK, (h + 1) * GLA_DK)
        q32s.append(q_all[:, ks].astype(F32))
        k32s.append(k_all[:, ks].astype(F32))
        a_heads.append(jnp.where(t_idx == s_idx, _dot_nt(q_all[:, ks], k_all[:, ks]), 0.0))
    for li, b in enumerate(GLA_LEVELS):
        e_all = jnp.exp(_mat_f32(lvl_ref[li], la3))
        sh = b.bit_length() - 1
        second = (jnp.right_shift(t_idx, sh) & 1) == 1
        same_pair = jnp.right_shift(t_idx, sh + 1) == jnp.right_shift(s_idx, sh + 1)
        for h in range(GLA_HEADS):
            ks = slice(h * GLA_DK, (h + 1) * GLA_DK)
            e = e_all[:, ks]
            qe = jnp.where(second, q32s[h] * e, 0.0).astype(BF16)
            ke = jnp.where(second, 0.0, k32s[h] * e).astype(BF16)
            a_heads[h] = a_heads[h] + jnp.where(same_pair, _dot_nt(qe, ke), 0.0)

    for h in range(GLA_HEADS):
        ks = slice(h * GLA_DK, (h + 1) * GLA_DK)
        vs = slice(h * GLA_DV, (h + 1) * GLA_DV)
        cbh = cb[:, ks]
        vb = v_all[:, vs]
        s_t = s_scr[h]
        o = _dot_nt((q32s[h] * jnp.exp(cbh)).astype(BF16), s_t.astype(BF16)) + _dot(a_heads[h].astype(BF16), vb)
        cl = cbh[t_last:t_last + 1, :]
        kd = jnp.where(t_idx <= t_last, k32s[h] * jnp.exp(cl - cbh), 0.0).astype(BF16)
        s_scr[h] = jnp.exp(cl) * s_t + _dot_tn(vb, kd)
        y = o * lax.rsqrt(jnp.mean(o * o, axis=-1, keepdims=True) + EPS) * gn_ref[:, vs]
        gate = g_all[:, vs].astype(F32)
        y = y * (gate * _sigmoid(gate))
        o_ref[:, vs] = y[:rows].astype(o_ref.dtype)

    @pl.when(ci == nc - 1)
    def _():
        s_out[0] = s_scr[...]


def _gla(q, k, v, gg, la, s0t, gn, batch, t_len):
    L = GLA_CHUNK
    rows = min(L, t_len)
    nc = t_len // rows
    t_last = rows - 1
    assert t_len % rows == 0 and (nc == 1 or rows == L)
    tok = lambda c: pl.BlockSpec((rows, c), lambda b, i: (b * nc + i, 0))
    shp_s = (GLA_HEADS, GLA_DV, GLA_DK)
    st = pl.BlockSpec((1,) + shp_s, lambda b, i: (b, 0, 0, 0))
    levels = _gla_level_matrices(L)
    return pl.pallas_call(
        functools.partial(_gla_kernel, L=L, t_last=t_last, nc=nc),
        grid=(batch, nc),
        in_specs=[tok(GLA_KW), tok(GLA_KW), tok(GLA_VW), tok(GLA_VW), tok(GLA_KW), st,
                  _const_spec(levels.shape), _const_spec((1, GLA_VW))],
        out_specs=[tok(GLA_VW), st],
        out_shape=[jax.ShapeDtypeStruct((batch * t_len, GLA_VW), BF16),
                   jax.ShapeDtypeStruct((batch,) + shp_s, F32)],
        scratch_shapes=[pltpu.VMEM(shp_s, F32)],
        compiler_params=_cparams(("parallel", "arbitrary")),
        name="gla",
    )(q, k, v, gg, la, s0t, levels, gn)


def _pack_even(w, b_fox_f, b_i, b_f):
    d = w.shape[0]
    o = np.cumsum((0, FOX_WIDTH, FOX_WIDTH, FOX_WIDTH, FOX_HEADS, 2 * ML_WIDTH, ML_WIDTH, ML_WIDTH, ML_HEADS, ML_HEADS))
    n_gate = FOX_HEADS + 2 * ML_HEADS
    wp = jnp.concatenate([w[:, o[0]:o[3]], w[:, o[4]:o[7]], w[:, o[3]:o[4]], w[:, o[7]:o[9]],
                          jnp.zeros((d, LANES - n_gate), w.dtype)], axis=1).astype(BF16)
    bias = jnp.concatenate([b_fox_f, b_i, b_f, jnp.zeros((LANES - n_gate,), F32)]).reshape(1, LANES)
    return wp, bias


def _pack_odd(w, w_a2):
    d = w.shape[0]
    wp = jnp.concatenate([w, jnp.zeros((d, LANES - GLA_RANK), w.dtype)], axis=1).astype(BF16)
    wa2 = jnp.concatenate([w_a2, jnp.zeros((LANES - GLA_RANK, w_a2.shape[1]), w_a2.dtype)], axis=0).astype(BF16)
    return wp, wa2


def _gate_rows(gc, batch, t_len, t_pad):
    g = gc.reshape(batch, t_len, LANES)[:, :, :16].transpose(0, 2, 1)
    if t_pad > t_len:
        g = jnp.pad(g, ((0, 0), (0, 0), (0, t_pad - t_len)))
    return g


def _trunk(x, fox_k, fox_v, fox_lf, ml_c, ml_n, ml_m, ml_buf, gla_s, params):
    (norm_mix, norm_ffn, norm_final, w_even, bias_even, conv_w, conv_b, g_ml,
     w_odd, w_a2, b_a, g_gla, w_out, w_ff1, w_ff2) = params
    batch, t_len, d = x.shape
    past = fox_k.shape[2]
    n = batch * t_len
    tm = 512 if n % 512 == 0 else 256 if n % 256 == 0 else n
    depth = norm_mix.shape[0]
    xf = x.reshape(n, d)
    ev_states, odd_states = [], []
    y = None
    for layer in range(depth):
        j = layer // 2
        if layer % 2 == 0:
            qt, kf, vf, kb, vt, mqk, mv, mo, gc = _inproj_even(xf, norm_mix[layer][None], w_even[j], bias_even[j], tm)
            lf_col = gc.reshape(batch, t_len, LANES)
            if past == 0:
                tq, tq_len = min(256, t_len), t_len
                tkc = min(512, t_len)
                tk_len = t_len
                k_all, vt_all = kb, vt
            else:
                tq = tq_len = LANES
                tkc = min(512, past)
                tk_len = past + tkc
                extra = tk_len - past - t_len
                k_all = jnp.concatenate(
                    [fox_k[j].reshape(batch, past, FOX_WIDTH).astype(BF16), kb.reshape(batch, t_len, FOX_WIDTH),
                     jnp.zeros((batch, extra, FOX_WIDTH), BF16)], axis=1).reshape(batch * tk_len, FOX_WIDTH)
                vt_all = jnp.concatenate(
                    [fox_v[j].reshape(batch, past, FOX_WIDTH).transpose(2, 0, 1).astype(BF16),
                     vt.reshape(FOX_WIDTH, batch, t_len), jnp.zeros((FOX_WIDTH, batch, extra), BF16)],
                    axis=2).reshape(FOX_WIDTH, batch * tk_len)
                lf_col = jnp.concatenate(
                    [jnp.pad(fox_lf[j], ((0, 0), (0, 0), (0, LANES - FOX_HEADS))), lf_col,
                     jnp.zeros((batch, extra, LANES), F32)], axis=1)
                qt = jnp.pad(qt.reshape(FOX_WIDTH, batch, t_len), ((0, 0), (0, 0), (0, tq_len - t_len)))
                qt = qt.reshape(FOX_WIDTH, batch * tq_len)
            ka = _fox_prep(lf_col, k_all, min(512, tkc))
            attn = _fox_attention(qt, ka, vt_all, batch, tq_len, tk_len, past, tq, tkc)
            if tq_len != t_len:
                attn = attn.reshape(batch, tq_len, FOX_WIDTH)[:, :t_len].reshape(n, FOX_WIDTH)
            l_ml = ML_CHUNK if t_len >= ML_CHUNK else LANES
            gr = _gate_rows(gc, batch, t_len, max(t_len, l_ml))
            c0 = ml_c[j]
            n0 = ml_n[j][:, :, None, :]
            m0 = jnp.broadcast_to(ml_m[j][:, :, None, None], (batch, ML_HEADS, 1, LANES))
            buf8 = jnp.pad(ml_buf[j], ((0, 0), (8 - (ML_CONV - 1), 0), (0, 0)))
            h_ml, c_new, n_new, m_new, buf_new = _mlstm(mqk, mv, mo, gc, gr, c0, n0, m0, buf8,
                                                        conv_w[j], conv_b[j][None], g_ml[j][None], batch, t_len)
            ev_states.append((kf.reshape(batch, t_len, FOX_HEADS, FOX_HEAD_DIM),
                              vf.reshape(batch, t_len, FOX_HEADS, FOX_HEAD_DIM),
                              gc[:, :FOX_HEADS].reshape(batch, t_len, FOX_HEADS),
                              c_new, n_new[:, :, 0, :], m_new[:, :, 0, 0], buf_new[:, 8 - (ML_CONV - 1):, :]))
            mixes = [attn, h_ml]
        else:
            q, k, v, gg, la = _inproj_odd(xf, norm_mix[layer][None], w_odd[j], w_a2[j], b_a[j][None], tm)
            s0t = gla_s[j].transpose(0, 1, 3, 2)
            o, s_new = _gla(q, k, v, gg, la, s0t, g_gla[j][None], batch, t_len)
            odd_states.append(s_new.transpose(0, 1, 3, 2))
            mixes = [o]
        last = layer == depth - 1
        out = _post(xf, mixes, w_out[layer], norm_ffn[layer][None], w_ff1[layer], w_ff2[layer],
                    norm_final[None] if last else None, tm)
        if last:
            y = out
        else:
            xf = out
    ev = [jnp.stack([s[i] for s in ev_states]) for i in range(7)]
    return y.reshape(batch, t_len, d), ev, jnp.stack(odd_states)


def kernel(x_prompt, x_sample, cache_fox_k, cache_fox_v, cache_fox_logf, state_mlstm_c, state_mlstm_n, state_mlstm_m, state_mlstm_conv, state_gla_s, norm_mix, norm_ffn, norm_final, w_in_even, b_fox_f, conv_w_ml, conv_b_ml, b_ml_i, b_ml_f, g_ml, w_in_odd, w_gla_a2, b_gla_a, g_gla, w_out, w_ff1, w_ff2):
    n_even, n_odd = w_in_even.shape[0], w_in_odd.shape[0]
    packed_even = [_pack_even(w_in_even[j], b_fox_f[j], b_ml_i[j], b_ml_f[j]) for j in range(n_even)]
    packed_odd = [_pack_odd(w_in_odd[j], w_gla_a2[j]) for j in range(n_odd)]
    params = (norm_mix, norm_ffn, norm_final,
              [p[0] for p in packed_even], [p[1] for p in packed_even], conv_w_ml, conv_b_ml, g_ml,
              [p[0] for p in packed_odd], [p[1] for p in packed_odd], b_gla_a, g_gla,
              w_out.astype(BF16), w_ff1.astype(BF16), w_ff2.astype(BF16))

    bp = x_prompt.shape[0]
    dt = x_prompt.dtype
    zeros = lambda *s: jnp.zeros(s, dt)
    y_p, ev_p, gla_p = _trunk(
        x_prompt,
        zeros(n_even, bp, 0, FOX_HEADS, FOX_HEAD_DIM), zeros(n_even, bp, 0, FOX_HEADS, FOX_HEAD_DIM),
        zeros(n_even, bp, 0, FOX_HEADS),
        zeros(n_even, bp, ML_HEADS, ML_HEAD_DIM, ML_HEAD_DIM), zeros(n_even, bp, ML_HEADS, ML_HEAD_DIM),
        zeros(n_even, bp, ML_HEADS), zeros(n_even, bp, ML_CONV - 1, 2 * ML_WIDTH),
        zeros(n_odd, bp, GLA_HEADS, GLA_DK, GLA_DV), params)
    y_s, ev_s, gla_s = _trunk(x_sample, cache_fox_k, cache_fox_v, cache_fox_logf, state_mlstm_c, state_mlstm_n,
                              state_mlstm_m, state_mlstm_conv, state_gla_s, params)
    return (y_p, y_s, *ev_p, gla_p, *ev_s, gla_s)
```

```python
import functools

import numpy as np
import jax
import jax.numpy as jnp
from jax import lax
from jax.experimental import pallas as pl
from jax.experimental.pallas import tpu as pltpu

F32 = jnp.float32
BF16 = jnp.bfloat16
EPS = 1e-6
NEG = -1e30
LOG2E = 1.4426950408889634

LANES = 128
VMEM_LIMIT = 56 * 1024 * 1024

D_MODEL = 1024
D_FF = 4 * D_MODEL
FOX_HEADS, FOX_HEAD_DIM = 8, 64
FOX_WIDTH = FOX_HEADS * FOX_HEAD_DIM
ML_HEADS, ML_HEAD_DIM = 4, 128
ML_WIDTH = ML_HEADS * ML_HEAD_DIM
ML_CONV = 4
GLA_HEADS, GLA_DK, GLA_DV = 4, 128, 256
GLA_KW = GLA_HEADS * GLA_DK
GLA_VW = GLA_HEADS * GLA_DV
GLA_RANK = 16
GLA_TAU = 16.0

E_Q, E_K, E_V, E_MQK, E_MV, E_MO, E_G, E_END = 0, 512, 1024, 1536, 2560, 3072, 3584, 3712
G_FOX, G_MI, G_MF = 0, 8, 12
O_Q, O_K, O_V, O_G, O_A, O_END = 0, 512, 1024, 2048, 3072, 3200

ML_CHUNK = 256
GLA_CHUNK = 128
GLA_LEVELS = (64, 32, 16, 8, 4, 2, 1)


def _cparams(sem):
    return pltpu.CompilerParams(dimension_semantics=sem, vmem_limit_bytes=VMEM_LIMIT)


def _const_spec(shape):
    nd = len(shape)
    return pl.BlockSpec(shape, lambda *_: (0,) * nd, pipeline_mode=pl.Buffered(1))


def _rms(x, g):
    return x * lax.rsqrt(jnp.mean(x * x, axis=-1, keepdims=True) + EPS) * g


def _sigmoid(x):
    return 1.0 / (1.0 + jnp.exp(-x))


def _log_sigmoid(x):
    return -(jnp.maximum(-x, 0.0) + jnp.log1p(jnp.exp(-jnp.abs(x))))


def _dot(a, b):
    return jnp.dot(a, b, preferred_element_type=F32)


def _dot_nt(a, b):
    return lax.dot_general(a, b, (((1,), (1,)), ((), ())), preferred_element_type=F32)


def _dot_tn(a, b):
    return lax.dot_general(a, b, (((0,), (0,)), ((), ())), preferred_element_type=F32)


def _split3(x):
    hi = x.astype(BF16)
    r1 = x - hi.astype(F32)
    mid = r1.astype(BF16)
    lo = (r1 - mid.astype(F32)).astype(BF16)
    return hi, mid, lo


def _mat_f32(m, parts):
    return _dot(m, parts[0]) + _dot(m, parts[1]) + _dot(m, parts[2])


def _f32_mat(parts, m):
    return _dot(parts[0], m) + _dot(parts[1], m) + _dot(parts[2], m)


def _tri(n, lower):
    r = lax.broadcasted_iota(jnp.int32, (n, n), 0)
    c = lax.broadcasted_iota(jnp.int32, (n, n), 1)
    keep = (c <= r) if lower else (r <= c)
    return jnp.where(keep, 1.0, 0.0).astype(BF16)


def _pad_rows(a, n):
    if a.shape[0] == n:
        return a
    return jnp.concatenate([a, jnp.zeros((n - a.shape[0], a.shape[1]), a.dtype)], axis=0)


def _inproj_even_kernel(x_ref, g_ref, w_ref, bias_ref,
                        qt_ref, kf_ref, vf_ref, kb_ref, vt_ref, mqk_ref, mv_ref, mo_ref, gc_ref):
    h = _rms(x_ref[...], g_ref[...]).astype(BF16)

    def seg(a, b):
        return _dot(h, w_ref[:, a:b])

    qt_ref[...] = (seg(E_Q, E_K) * (FOX_HEAD_DIM ** -0.5 * LOG2E)).T.astype(BF16)
    k = seg(E_K, E_V)
    kf_ref[...] = k
    kb_ref[...] = k.astype(BF16)
    v = seg(E_V, E_MQK)
    vf_ref[...] = v
    vt_ref[...] = v.T.astype(BF16)
    mqk_ref[...] = seg(E_MQK, E_MV)
    mv_ref[...] = seg(E_MV, E_MO).astype(BF16)
    mo_ref[...] = seg(E_MO, E_G).astype(BF16)
    gz = seg(E_G, E_END) + bias_ref[...]
    lane = lax.broadcasted_iota(jnp.int32, gz.shape, 1)
    is_log = (lane < G_MI) | (lane >= G_MF)
    gc_ref[...] = jnp.where(is_log, _log_sigmoid(gz), gz)


def _inproj_even(x, g, w, bias, tm):
    n = x.shape[0]
    row = lambda c: pl.BlockSpec((tm, c), lambda i: (i, 0))
    col = pl.BlockSpec((FOX_WIDTH, tm), lambda i: (0, i))
    outs = [(None, BF16), (FOX_WIDTH, F32), (FOX_WIDTH, F32), (FOX_WIDTH, BF16), (None, BF16),
            (2 * ML_WIDTH, F32), (ML_WIDTH, BF16), (ML_WIDTH, BF16), (LANES, F32)]
    return pl.pallas_call(
        _inproj_even_kernel,
        grid=(n // tm,),
        in_specs=[row(D_MODEL), _const_spec((1, D_MODEL)), _const_spec((D_MODEL, E_END)), _const_spec((1, LANES))],
        out_specs=[col if c is None else row(c) for c, _ in outs],
        out_shape=[jax.ShapeDtypeStruct((FOX_WIDTH, n) if c is None else (n, c), dt) for c, dt in outs],
        compiler_params=_cparams(("parallel",)),
        name="inproj_even",
    )(x, g, w, bias)


def _inproj_odd_kernel(x_ref, g_ref, w_ref, wa2_ref, ba_ref, q_ref, k_ref, v_ref, gg_ref, la_ref):
    h = _rms(x_ref[...], g_ref[...]).astype(BF16)

    def seg(a, b):
        return _dot(h, w_ref[:, a:b])

    q_ref[...] = (seg(O_Q, O_K) * (GLA_DK ** -0.5)).astype(BF16)
    k_ref[...] = seg(O_K, O_V).astype(BF16)
    v_ref[...] = seg(O_V, O_G).astype(BF16)
    gg_ref[...] = seg(O_G, O_A).astype(BF16)
    ga = seg(O_A, O_END).astype(BF16)
    la_ref[...] = _log_sigmoid(_dot(ga, wa2_ref[...]) + ba_ref[...]) * (1.0 / GLA_TAU)


def _inproj_odd(x, g, w, wa2, ba, tm):
    n = x.shape[0]
    row = lambda c: pl.BlockSpec((tm, c), lambda i: (i, 0))
    outs = [(GLA_KW, BF16), (GLA_KW, BF16), (GLA_VW, BF16), (GLA_VW, BF16), (GLA_KW, F32)]
    return pl.pallas_call(
        _inproj_odd_kernel,
        grid=(n // tm,),
        in_specs=[row(D_MODEL), _const_spec((1, D_MODEL)), _const_spec((D_MODEL, O_END)),
                  _const_spec((LANES, GLA_KW)), _const_spec((1, GLA_KW))],
        out_specs=[row(c) for c, _ in outs],
        out_shape=[jax.ShapeDtypeStruct((n, c), dt) for c, dt in outs],
        compiler_params=_cparams(("parallel",)),
        name="inproj_odd",
    )(x, g, w, wa2, ba)


def _post_kernel(*refs, n_mix, final):
    x_ref = refs[0]
    mix_refs = refs[1:1 + n_mix]
    wo_ref, gf_ref, w1_ref, w2_ref = refs[1 + n_mix:5 + n_mix]
    rest = refs[5 + n_mix:]
    mix = mix_refs[0][...] if n_mix == 1 else jnp.concatenate([r[...] for r in mix_refs], axis=1)
    x1 = x_ref[...] + _dot(mix, wo_ref[...])
    h = _rms(x1, gf_ref[...]).astype(BF16)
    y = x1
    for c in range(D_FF // D_MODEL):
        sl = slice(c * D_MODEL, (c + 1) * D_MODEL)
        t = jnp.maximum(_dot(h, w1_ref[:, sl]), 0.0)
        y = y + _dot((t * t).astype(BF16), w2_ref[sl, :])
    if final:
        gfin_ref, out_ref = rest
        out_ref[...] = _rms(y, gfin_ref[...])
    else:
        (out_ref,) = rest
        out_ref[...] = y


def _post(x, mixes, wo, gf, w1, w2, gfin, tm):
    n = x.shape[0]
    row = lambda c: pl.BlockSpec((tm, c), lambda i: (i, 0))
    final = gfin is not None
    ins = [x, *mixes, wo, gf, w1, w2]
    specs = [row(D_MODEL)] + [row(m.shape[1]) for m in mixes] + [
        _const_spec((D_MODEL, D_MODEL)), _const_spec((1, D_MODEL)),
        _const_spec((D_MODEL, D_FF)), _const_spec((D_FF, D_MODEL))]
    if final:
        ins.append(gfin)
        specs.append(_const_spec((1, D_MODEL)))
    return pl.pallas_call(
        functools.partial(_post_kernel, n_mix=len(mixes), final=final),
        grid=(n // tm,),
        in_specs=specs,
        out_specs=row(D_MODEL),
        out_shape=jax.ShapeDtypeStruct((n, D_MODEL), F32),
        compiler_params=_cparams(("parallel",)),
        name="post_final" if final else "post",
    )(*ins)


F_TERMS = 3


def _fox_placement():
    p = np.zeros((F_TERMS, LANES, FOX_HEADS // 2 * LANES), np.float32)
    for x in range(F_TERMS):
        for h in range(FOX_HEADS):
            p[x, h, (h // 2) * LANES + F_TERMS * (h % 2) + x] = 1.0
    return jnp.asarray(p, BF16)


def _fox_prep_kernel(lf_ref, k_ref, place_ref, ka_ref, carry):
    @pl.when(pl.program_id(1) == 0)
    def _():
        carry[...] = jnp.zeros_like(carry)

    tc = lf_ref.shape[1]
    f = _mat_f32(_tri(tc, True), _split3(lf_ref[0])) + carry[...]
    carry[...] = f[tc - 1:tc, :]
    parts = _split3(f * LOG2E)
    cols = _dot(parts[0], place_ref[0]) + _dot(parts[1], place_ref[1]) + _dot(parts[2], place_ref[2])
    for hp in range(FOX_HEADS // 2):
        ka_ref[:, 2 * hp * LANES:(2 * hp + 1) * LANES] = k_ref[:, hp * LANES:(hp + 1) * LANES]
        ka_ref[:, (2 * hp + 1) * LANES:(2 * hp + 2) * LANES] = cols[:, hp * LANES:(hp + 1) * LANES].astype(BF16)


def _fox_prep(lf_col, k, tc):
    b, t, _ = lf_col.shape
    nt = t // tc
    place = _fox_placement()
    return pl.pallas_call(
        _fox_prep_kernel,
        grid=(b, nt),
        in_specs=[pl.BlockSpec((1, tc, LANES), lambda i, j: (i, j, 0)),
                  pl.BlockSpec((tc, FOX_WIDTH), lambda i, j: (i * nt + j, 0)),
                  _const_spec(place.shape)],
        out_specs=pl.BlockSpec((tc, 2 * FOX_WIDTH), lambda i, j: (i * nt + j, 0)),
        out_shape=jax.ShapeDtypeStruct((b * t, 2 * FOX_WIDTH), BF16),
        scratch_shapes=[pltpu.VMEM((1, LANES), F32)],
        compiler_params=_cparams(("parallel", "arbitrary")),
        name="fox_prep",
    )(lf_col, k, place)


ONES_ROWS = 16


def _fox_attn_kernel(qt_ref, ka_ref, vt_ref, o_ref, sa_scr, sb_scr, acc_scr, *, tq, tkc, past, n_diag):
    q_first = past + pl.program_id(2) * tq
    n_full = q_first // tkc
    drow = lax.broadcasted_iota(jnp.int32, (LANES, 1), 0)
    lane = lax.broadcasted_iota(jnp.int32, (1, 2 * tq), 1)
    lane_head = jnp.where(lane < tq, 0, 1)
    own_head = jnp.where(jnp.where(drow < FOX_HEAD_DIM, 0, 1) == lane_head, 1.0, 0.0).astype(BF16)
    q2 = qt_ref[...] * own_head
    f_sel = (drow >= F_TERMS * lane_head) & (drow < F_TERMS * lane_head + F_TERMS)
    qa = jnp.concatenate([q2, jnp.where(f_sel, -1.0, 0.0).astype(BF16)], axis=0)
    ones = jnp.ones((ONES_ROWS, tkc), BF16)
    acc_scr[...] = jnp.zeros_like(acc_scr)

    def produce(j, dst):
        start = pl.multiple_of(j * tkc, tkc)
        dst[...] = _dot(ka_ref[pl.ds(start, tkc), :], qa)

    def consume(j, src, m_prev, masked):
        start = pl.multiple_of(j * tkc, tkc)
        st = src[...]
        if masked:
            kpos = start + lax.broadcasted_iota(jnp.int32, (tkc, 1), 0)
            qpos = q_first + (lane & (tq - 1))
            st = jnp.where(kpos <= qpos, st, NEG)
        m_new = jnp.maximum(m_prev, jnp.max(st, axis=0, keepdims=True))
        alpha = jnp.exp2(m_prev - m_new)
        p = jnp.exp2(st - m_new).astype(BF16)
        for hh in range(2):
            rows = slice(hh * FOX_HEAD_DIM, (hh + 1) * FOX_HEAD_DIM)
            cols = slice(hh * tq, (hh + 1) * tq)
            va = jnp.concatenate([vt_ref[rows, pl.ds(start, tkc)], ones], axis=0)
            acc_scr[hh] = alpha[:, cols] * acc_scr[hh] + _dot(va, p[:, cols])
        return m_new

    produce(0, sa_scr)

    def body(i, m):
        produce(2 * i + 1, sb_scr)
        m = consume(2 * i, sa_scr, m, False)
        produce(2 * i + 2, sa_scr)
        return consume(2 * i + 1, sb_scr, m, False)

    m = lax.fori_loop(0, n_full // 2, body, jnp.full((1, 2 * tq), NEG, F32))
    bufs = (sa_scr, sb_scr)
    for d in range(n_diag):
        if d + 1 < n_diag:
            produce(n_full + d + 1, bufs[(d + 1) % 2])
        m = consume(n_full + d, bufs[d % 2], m, True)
    out = jnp.concatenate([acc_scr[hh, 0:FOX_HEAD_DIM] / acc_scr[hh, FOX_HEAD_DIM:FOX_HEAD_DIM + 1]
                           for hh in range(2)], axis=0)
    o_ref[...] = out.T.astype(o_ref.dtype)


def _fox_attention(qt, ka, vt, batch, tq_len, tk_len, past, tq, tkc):
    nq = tq_len // tq
    n_diag = max(1, tq // tkc)
    assert tq & (tq - 1) == 0 and tq % LANES == 0 and (tkc % tq == 0 or tq % tkc == 0) and tq_len % tq == 0
    assert past % (2 * tkc) == 0 and (nq == 1 or tq % (2 * tkc) == 0)
    assert tk_len >= past + max(tq_len, n_diag * tkc) and tk_len % tkc == 0
    pairs = FOX_HEADS // 2
    qd = qt.reshape(FOX_WIDTH, batch * nq, 1, tq)
    qd = jnp.broadcast_to(qd, (FOX_WIDTH, batch * nq, 2, tq)).reshape(FOX_WIDTH, batch * nq * 2 * tq)
    return pl.pallas_call(
        functools.partial(_fox_attn_kernel, tq=tq, tkc=tkc, past=past, n_diag=n_diag),
        grid=(batch, pairs, nq),
        in_specs=[pl.BlockSpec((LANES, 2 * tq), lambda b, h, i: (h, b * nq + i)),
                  pl.BlockSpec((tk_len, 2 * LANES), lambda b, h, i: (b, h)),
                  pl.BlockSpec((LANES, tk_len), lambda b, h, i: (h, b))],
        out_specs=pl.BlockSpec((tq, LANES), lambda b, h, i: (b * nq + i, h)),
        out_shape=jax.ShapeDtypeStruct((batch * tq_len, FOX_WIDTH), BF16),
        scratch_shapes=[pltpu.VMEM((tkc, 2 * tq), F32), pltpu.VMEM((tkc, 2 * tq), F32),
                        pltpu.VMEM((2, FOX_HEAD_DIM + ONES_ROWS, tq), F32)],
        compiler_params=_cparams(("parallel", "parallel", "arbitrary")),
        name="fox_attention",
    )(qd, ka, vt)


def _mlstm_kernel(mqk_ref, mv_ref, mo_ref, gc_ref, gr_ref, c0_ref, n0_ref, m0_ref, buf_ref,
                  cw_ref, cb_ref, gml_ref,
                  h_ref, c_out, n_out, m_out, buf_out,
                  c_scr, n_scr, m_scr, prev_scr, *, L, t_last, nc):
    ci = pl.program_id(1)
    rows = mqk_ref.shape[0]

    @pl.when(ci == 0)
    def _():
        c_scr[...] = c0_ref[0]
        n_scr[...] = n0_ref[0]
        m_scr[...] = m0_ref[0]
        prev_scr[...] = buf_ref[0]

    u = _pad_rows(mqk_ref[...], L)
    ext = jnp.concatenate([prev_scr[...], u], axis=0)
    y = cb_ref[...] + cw_ref[ML_CONV - 1:ML_CONV, :] * u
    for s in range(1, ML_CONV):
        y = y + cw_ref[ML_CONV - 1 - s:ML_CONV - s, :] * pltpu.roll(ext, s, axis=0)[8:8 + L]
    qk = y * _sigmoid(y)
    prev_scr[...] = u[L - 8:L]

    gc = _pad_rows(gc_ref[...], L)
    gr = gr_ref[0]
    b_col = _mat_f32(_tri(L, True), _split3(gc))
    b_row = _f32_mat(_split3(gr), _tri(L, False))
    v_all = _pad_rows(mv_ref[...], L)
    o_all = _pad_rows(mo_ref[...], L)
    t_idx = lax.broadcasted_iota(jnp.int32, (L, 1), 0)
    causal = lax.broadcasted_iota(jnp.int32, (1, L), 1) <= t_idx

    for h in range(ML_HEADS):
        hs = slice(h * ML_HEAD_DIM, (h + 1) * ML_HEAD_DIM)
        q32 = qk[:, hs]
        k32 = qk[:, ML_WIDTH + h * ML_HEAD_DIM:ML_WIDTH + (h + 1) * ML_HEAD_DIM] * (ML_HEAD_DIM ** -0.5)
        qb, kb, vb = q32.astype(BF16), k32.astype(BF16), v_all[:, hs]
        bc = b_col[:, G_MF + h:G_MF + h + 1]
        ic = gc[:, G_MI + h:G_MI + h + 1]
        br = b_row[G_MF + h:G_MF + h + 1, :]
        ir = gr[G_MI + h:G_MI + h + 1, :]
        m_prev = m_scr[h][:, 0:1]
        c_prev = c_scr[h]
        n_prev = n_scr[h]

        dmat = jnp.where(causal, bc - br + ir, NEG)
        inter = bc + m_prev
        m_t = jnp.maximum(inter, jnp.max(dmat, axis=-1, keepdims=True))
        w = jnp.exp(dmat - m_t)
        g = jnp.exp(inter - m_t)
        a = w * _dot_nt(qb, kb)
        num = g * _dot_nt(qb, c_prev.astype(BF16)) + _dot(a.astype(BF16), vb)
        den = g * jnp.sum(q32 * n_prev, axis=-1, keepdims=True) + jnp.sum(a, axis=-1, keepdims=True)
        hh = num / jnp.maximum(jnp.abs(den), jnp.exp(-m_t))

        b_last = bc[t_last:t_last + 1, :]
        m_last = m_t[t_last:t_last + 1, :]
        g_end = g[t_last:t_last + 1, :]
        w_end = jnp.where(t_idx <= t_last, jnp.exp(b_last - bc + ic - m_last), 0.0)
        c_scr[h] = g_end * c_prev + _dot_tn((vb.astype(F32) * w_end).astype(BF16), kb)
        n_scr[h] = g_end * n_prev + jnp.sum(k32 * w_end, axis=0, keepdims=True)
        m_scr[h] = jnp.broadcast_to(m_last, (1, LANES))

        yh = hh * lax.rsqrt(jnp.mean(hh * hh, axis=-1, keepdims=True) + EPS) * gml_ref[:, hs]
        yh = yh * _sigmoid(o_all[:, hs].astype(F32))
        h_ref[:, hs] = yh[:rows].astype(h_ref.dtype)

    @pl.when(ci == nc - 1)
    def _():
        c_out[0] = c_scr[...]
        n_out[0] = n_scr[...]
        m_out[0] = m_scr[...]
        buf_out[0] = ext[t_last + 1:t_last + 9]


def _mlstm(mqk, mv, mo, gc, gr, c0, n0, m0, buf8, cw, cb, gml, batch, t_len):
    L = ML_CHUNK if t_len >= ML_CHUNK else LANES
    rows = min(L, t_len)
    nc = t_len // rows
    t_last = rows - 1
    assert t_len % rows == 0 and (nc == 1 or rows == L) and (t_last + 1) % 8 == 0
    tok = lambda c: pl.BlockSpec((rows, c), lambda b, i: (b * nc + i, 0))
    st = lambda *s: pl.BlockSpec((1,) + s, lambda b, i: (b,) + (0,) * len(s))
    shp_c = (ML_HEADS, ML_HEAD_DIM, ML_HEAD_DIM)
    shp_n = (ML_HEADS, 1, ML_HEAD_DIM)
    return pl.pallas_call(
        functools.partial(_mlstm_kernel, L=L, t_last=t_last, nc=nc),
        grid=(batch, nc),
        in_specs=[tok(2 * ML_WIDTH), tok(ML_WIDTH), tok(ML_WIDTH), tok(LANES),
                  pl.BlockSpec((1, 16, L), lambda b, i: (b, 0, i)),
                  st(*shp_c), st(*shp_n), st(*shp_n), st(8, 2 * ML_WIDTH),
                  _const_spec((ML_CONV, 2 * ML_WIDTH)), _const_spec((1, 2 * ML_WIDTH)), _const_spec((1, ML_WIDTH))],
        out_specs=[tok(ML_WIDTH), st(*shp_c), st(*shp_n), st(*shp_n), st(8, 2 * ML_WIDTH)],
        out_shape=[jax.ShapeDtypeStruct((batch * t_len, ML_WIDTH), BF16),
                   jax.ShapeDtypeStruct((batch,) + shp_c, F32),
                   jax.ShapeDtypeStruct((batch,) + shp_n, F32),
                   jax.ShapeDtypeStruct((batch,) + shp_n, F32),
                   jax.ShapeDtypeStruct((batch, 8, 2 * ML_WIDTH), F32)],
        scratch_shapes=[pltpu.VMEM(shp_c, F32), pltpu.VMEM(shp_n, F32), pltpu.VMEM(shp_n, F32),
                        pltpu.VMEM((8, 2 * ML_WIDTH), F32)],
        compiler_params=_cparams(("parallel", "arbitrary")),
        name="mlstm",
    )(mqk, mv, mo, gc, gr, c0, n0, m0, buf8, cw, cb, gml)


def _gla_level_matrices(L):
    mats = np.zeros((len(GLA_LEVELS), L, L), np.float32)
    for li, b in enumerate(GLA_LEVELS):
        for t in range(L):
            base = (t // (2 * b)) * 2 * b
            bound = base + b - 1
            if t > bound:
                mats[li, t, bound + 1:t + 1] = 1.0
            else:
                mats[li, t, t + 1:bound + 1] = 1.0
    return jnp.asarray(mats, BF16)


def _gla_kernel(q_ref, k_ref, v_ref, gg_ref, la_ref, s0_ref, lvl_ref, gn_ref, o_ref, s_out, s_scr, *, L, t_last, nc):
    ci = pl.program_id(1)
    rows = q_ref.shape[0]

    @pl.when(ci == 0)
    def _():
        s_scr[...] = s0_ref[0]

    la3 = _split3(_pad_rows(la_ref[...], L))
    cb = _mat_f32(_tri(L, True), la3)
    q_all = _pad_rows(q_ref[...], L)
    k_all = _pad_rows(k_ref[...], L)
    v_all = _pad_rows(v_ref[...], L)
    g_all = _pad_rows(gg_ref[...], L)
    t_idx = lax.broadcasted_iota(jnp.int32, (L, 1), 0)
    s_idx = lax.broadcasted_iota(jnp.int32, (1, L), 1)

    a_heads = []
    q32s, k32s = [], []
    for h in range(GLA_HEADS):
        ks = slice(h * GLA_DK, (h + 1) * GLA_DK)
        q32s.append(q_all[:, ks].astype(F32))
        k32s.append(k_all[:, ks].astype(F32))
        a_heads.append(jnp.where(t_idx == s_idx, _dot_nt(q_all[:, ks], k_all[:, ks]), 0.0))
    for li, b in enumerate(GLA_LEVELS):
        e_all = jnp.exp(_mat_f32(lvl_ref[li], la3))
        sh = b.bit_length() - 1
        second = (jnp.right_shift(t_idx, sh) & 1) == 1
        same_pair = jnp.right_shift(t_idx, sh + 1) == jnp.right_shift(s_idx, sh + 1)
        for h in range(GLA_HEADS):
            ks = slice(h * GLA_DK, (h + 1) * GLA_DK)
            e = e_all[:, ks]
            qe = jnp.where(second, q32s[h] * e, 0.0).astype(BF16)
            ke = jnp.where(second, 0.0, k32s[h] * e).astype(BF16)
            a_heads[h] = a_heads[h] + jnp.where(same_pair, _dot_nt(qe, ke), 0.0)

    for h in range(GLA_HEADS):
        ks = slice(h * GLA_DK, (h + 1) * GLA_DK)
        vs = slice(h * GLA_DV, (h + 1) * GLA_DV)
        cbh = cb[:, ks]
        vb = v_all[:, vs]
        s_t = s_scr[h]
        o = _dot_nt((q32s[h] * jnp.exp(cbh)).astype(BF16), s_t.astype(BF16)) + _dot(a_heads[h].astype(BF16), vb)
        cl = cbh[t_last:t_last + 1, :]
        kd = jnp.where(t_idx <= t_last, k32s[h] * jnp.exp(cl - cbh), 0.0).astype(BF16)
        s_scr[h] = jnp.exp(cl) * s_t + _dot_tn(vb, kd)
        y = o * lax.rsqrt(jnp.mean(o * o, axis=-1, keepdims=True) + EPS) * gn_ref[:, vs]
        gate = g_all[:, vs].astype(F32)
        y = y * (gate * _sigmoid(gate))
        o_ref[:, vs] = y[:rows].astype(o_ref.dtype)

    @pl.when(ci == nc - 1)
    def _():
        s_out[0] = s_scr[...]


def _gla(q, k, v, gg, la, s0t, gn, batch, t_len):
    L = GLA_CHUNK
    rows = min(L, t_len)
    nc = t_len // rows
    t_last = rows - 1
    assert t_len % rows == 0 and (nc == 1 or rows == L)
    tok = lambda c: pl.BlockSpec((rows, c), lambda b, i: (b * nc + i, 0))
    shp_s = (GLA_HEADS, GLA_DV, GLA_DK)
    st = pl.BlockSpec((1,) + shp_s, lambda b, i: (b, 0, 0, 0))
    levels = _gla_level_matrices(L)
    return pl.pallas_call(
        functools.partial(_gla_kernel, L=L, t_last=t_last, nc=nc),
        grid=(batch, nc),
        in_specs=[tok(GLA_KW), tok(GLA_KW), tok(GLA_VW), tok(GLA_VW), tok(GLA_KW), st,
                  _const_spec(levels.shape), _const_spec((1, GLA_VW))],
        out_specs=[tok(GLA_VW), st],
        out_shape=[jax.ShapeDtypeStruct((batch * t_len, GLA_VW), BF16),
                   jax.ShapeDtypeStruct((batch,) + shp_s, F32)],
        scratch_shapes=[pltpu.VMEM(shp_s, F32)],
        compiler_params=_cparams(("parallel", "arbitrary")),
        name="gla",
    )(q, k, v, gg, la, s0t, levels, gn)


def _pack_even(w, b_fox_f, b_i, b_f):
    d = w.shape[0]
    o = np.cumsum((0, FOX_WIDTH, FOX_WIDTH, FOX_WIDTH, FOX_HEADS, 2 * ML_WIDTH, ML_WIDTH, ML_WIDTH, ML_HEADS, ML_HEADS))
    n_gate = FOX_HEADS + 2 * ML_HEADS
    wp = jnp.concatenate([w[:, o[0]:o[3]], w[:, o[4]:o[7]], w[:, o[3]:o[4]], w[:, o[7]:o[9]],
                          jnp.zeros((d, LANES - n_gate), w.dtype)], axis=1).astype(BF16)
    bias = jnp.concatenate([b_fox_f, b_i, b_f, jnp.zeros((LANES - n_gate,), F32)]).reshape(1, LANES)
    return wp, bias


def _pack_odd(w, w_a2):
    d = w.shape[0]
    wp = jnp.concatenate([w, jnp.zeros((d, LANES - GLA_RANK), w.dtype)], axis=1).astype(BF16)
    wa2 = jnp.concatenate([w_a2, jnp.zeros((LANES - GLA_RANK, w_a2.shape[1]), w_a2.dtype)], axis=0).astype(BF16)
    return wp, wa2


def _gate_rows(gc, batch, t_len, t_pad):
    g = gc.reshape(batch, t_len, LANES)[:, :, :16].transpose(0, 2, 1)
    if t_pad > t_len:
        g = jnp.pad(g, ((0, 0), (0, 0), (0, t_pad - t_len)))
    return g


def _trunk(x, fox_k, fox_v, fox_lf, ml_c, ml_n, ml_m, ml_buf, gla_s, params):
    (norm_mix, norm_ffn, norm_final, w_even, bias_even, conv_w, conv_b, g_ml,
     w_odd, w_a2, b_a, g_gla, w_out, w_ff1, w_ff2) = params
    batch, t_len, d = x.shape
    past = fox_k.shape[2]
    n = batch * t_len
    tm = 512 if n % 512 == 0 else 256 if n % 256 == 0 else n
    depth = norm_mix.shape[0]
    xf = x.reshape(n, d)
    ev_states, odd_states = [], []
    y = None
    for layer in range(depth):
        j = layer // 2
        if layer % 2 == 0:
            qt, kf, vf, kb, vt, mqk, mv, mo, gc = _inproj_even(xf, norm_mix[layer][None], w_even[j], bias_even[j], tm)
            lf_col = gc.reshape(batch, t_len, LANES)
            tkc = 256
            if past == 0:
                tq, tq_len = min(512, t_len), t_len
                tk_len = t_len
                k_all, vt_all = kb, vt
            else:
                tq = tq_len = LANES
                tk_len = past + tkc
                extra = tk_len - past - t_len
                k_all = jnp.concatenate(
                    [fox_k[j].reshape(batch, past, FOX_WIDTH).astype(BF16), kb.reshape(batch, t_len, FOX_WIDTH),
                     jnp.zeros((batch, extra, FOX_WIDTH), BF16)], axis=1).reshape(batch * tk_len, FOX_WIDTH)
                vt_all = jnp.concatenate(
                    [fox_v[j].reshape(batch, past, FOX_WIDTH).transpose(2, 0, 1).astype(BF16),
                     vt.reshape(FOX_WIDTH, batch, t_len), jnp.zeros((FOX_WIDTH, batch, extra), BF16)],
                    axis=2).reshape(FOX_WIDTH, batch * tk_len)
                lf_col = jnp.concatenate(
                    [jnp.pad(fox_lf[j], ((0, 0), (0, 0), (0, LANES - FOX_HEADS))), lf_col,
                     jnp.zeros((batch, extra, LANES), F32)], axis=1)
                qt = jnp.pad(qt.reshape(FOX_WIDTH, batch, t_len), ((0, 0), (0, 0), (0, tq_len - t_len)))
                qt = qt.reshape(FOX_WIDTH, batch * tq_len)
            ka = _fox_prep(lf_col, k_all, min(512, tkc))
            attn = _fox_attention(qt, ka, vt_all, batch, tq_len, tk_len, past, tq, tkc)
            if tq_len != t_len:
                attn = attn.reshape(batch, tq_len, FOX_WIDTH)[:, :t_len].reshape(n, FOX_WIDTH)
            l_ml = ML_CHUNK if t_len >= ML_CHUNK else LANES
            gr = _gate_rows(gc, batch, t_len, max(t_len, l_ml))
            c0 = ml_c[j]
            n0 = ml_n[j][:, :, None, :]
            m0 = jnp.broadcast_to(ml_m[j][:, :, None, None], (batch, ML_HEADS, 1, LANES))
            buf8 = jnp.pad(ml_buf[j], ((0, 0), (8 - (ML_CONV - 1), 0), (0, 0)))
            h_ml, c_new, n_new, m_new, buf_new = _mlstm(mqk, mv, mo, gc, gr, c0, n0, m0, buf8,
                                                        conv_w[j], conv_b[j][None], g_ml[j][None], batch, t_len)
            ev_states.append((kf.reshape(batch, t_len, FOX_HEADS, FOX_HEAD_DIM),
                              vf.reshape(batch, t_len, FOX_HEADS, FOX_HEAD_DIM),
                              gc[:, :FOX_HEADS].reshape(batch, t_len, FOX_HEADS),
                              c_new, n_new[:, :, 0, :], m_new[:, :, 0, 0], buf_new[:, 8 - (ML_CONV - 1):, :]))
            mixes = [attn, h_ml]
        else:
            q, k, v, gg, la = _inproj_odd(xf, norm_mix[layer][None], w_odd[j], w_a2[j], b_a[j][None], tm)
            s0t = gla_s[j].transpose(0, 1, 3, 2)
            o, s_new = _gla(q, k, v, gg, la, s0t, g_gla[j][None], batch, t_len)
            odd_states.append(s_new.transpose(0, 1, 3, 2))
            mixes = [o]
        last = layer == depth - 1
        out = _post(xf, mixes, w_out[layer], norm_ffn[layer][None], w_ff1[layer], w_ff2[layer],
                    norm_final[None] if last else None, tm)
        if last:
            y = out
        else:
            xf = out
    ev = [jnp.stack([s[i] for s in ev_states]) for i in range(7)]
    return y.reshape(batch, t_len, d), ev, jnp.stack(odd_states)


def kernel(x_prompt, x_sample, cache_fox_k, cache_fox_v, cache_fox_logf, state_mlstm_c, state_mlstm_n, state_mlstm_m, state_mlstm_conv, state_gla_s, norm_mix, norm_ffn, norm_final, w_in_even, b_fox_f, conv_w_ml, conv_b_ml, b_ml_i, b_ml_f, g_ml, w_in_odd, w_gla_a2, b_gla_a, g_gla, w_out, w_ff1, w_ff2):
    n_even, n_odd = w_in_even.shape[0], w_in_odd.shape[0]
    packed_even = [_pack_even(w_in_even[j], b_fox_f[j], b_ml_i[j], b_ml_f[j]) for j in range(n_even)]
    packed_odd = [_pack_odd(w_in_odd[j], w_gla_a2[j]) for j in range(n_odd)]
    params = (norm_mix, norm_ffn, norm_final,
              [p[0] for p in packed_even], [p[1] for p in packed_even], conv_w_ml, conv_b_ml, g_ml,
              [p[0] for p in packed_odd], [p[1] for p in packed_odd], b_gla_a, g_gla,
              w_out.astype(BF16), w_ff1.astype(BF16), w_ff2.astype(BF16))

    bp = x_prompt.shape[0]
    dt = x_prompt.dtype
    zeros = lambda *s: jnp.zeros(s, dt)
    y_p, ev_p, gla_p = _trunk(
        x_prompt,
        zeros(n_even, bp, 0, FOX_HEADS, FOX_HEAD_DIM), zeros(n_even, bp, 0, FOX_HEADS, FOX_HEAD_DIM),
        zeros(n_even, bp, 0, FOX_HEADS),
        zeros(n_even, bp, ML_HEADS, ML_HEAD_DIM, ML_HEAD_DIM), zeros(n_even, bp, ML_HEADS, ML_HEAD_DIM),
        zeros(n_even, bp, ML_HEADS), zeros(n_even, bp, ML_CONV - 1, 2 * ML_WIDTH),
        zeros(n_odd, bp, GLA_HEADS, GLA_DK, GLA_DV), params)
    y_s, ev_s, gla_s = _trunk(x_sample, cache_fox_k, cache_fox_v, cache_fox_logf, state_mlstm_c, state_mlstm_n,
                              state_mlstm_m, state_mlstm_conv, state_gla_s, params)
    return (y_p, y_s, *ev_p, gla_p, *ev_s, gla_s)
```

```python
import functools

import numpy as np
import jax
import jax.numpy as jnp
from jax import lax
from jax.experimental import pallas as pl
from jax.experimental.pallas import tpu as pltpu

F32 = jnp.float32
BF16 = jnp.bfloat16
EPS = 1e-6
NEG = -1e30
LOG2E = 1.4426950408889634

LANES = 128
VMEM_LIMIT = 56 * 1024 * 1024

D_MODEL = 1024
D_FF = 4 * D_MODEL
FOX_HEADS, FOX_HEAD_DIM = 8, 64
FOX_WIDTH = FOX_HEADS * FOX_HEAD_DIM
ML_HEADS, ML_HEAD_DIM = 4, 128
ML_WIDTH = ML_HEADS * ML_HEAD_DIM
ML_CONV = 4
GLA_HEADS, GLA_DK, GLA_DV = 4, 128, 256
GLA_KW = GLA_HEADS * GLA_DK
GLA_VW = GLA_HEADS * GLA_DV
GLA_RANK = 16
GLA_TAU = 16.0

E_Q, E_K, E_V, E_MQK, E_MV, E_MO, E_G, E_END = 0, 512, 1024, 1536, 2560, 3072, 3584, 3712
G_FOX, G_MI, G_MF = 0, 8, 12
O_Q, O_K, O_V, O_G, O_A, O_END = 0, 512, 1024, 2048, 3072, 3200

ML_CHUNK = 256
GLA_CHUNK = 128
GLA_LEVELS = (64, 32, 16, 8, 4, 2, 1)


def _cparams(sem):
    return pltpu.CompilerParams(dimension_semantics=sem, vmem_limit_bytes=VMEM_LIMIT)


def _const_spec(shape):
    nd = len(shape)
    return pl.BlockSpec(shape, lambda *_: (0,) * nd, pipeline_mode=pl.Buffered(1))


def _rms(x, g):
    return x * lax.rsqrt(jnp.mean(x * x, axis=-1, keepdims=True) + EPS) * g


def _sigmoid(x):
    return 1.0 / (1.0 + jnp.exp(-x))


def _log_sigmoid(x):
    return -(jnp.maximum(-x, 0.0) + jnp.log1p(jnp.exp(-jnp.abs(x))))


def _dot(a, b):
    return jnp.dot(a, b, preferred_element_type=F32)


def _dot_nt(a, b):
    return lax.dot_general(a, b, (((1,), (1,)), ((), ())), preferred_element_type=F32)


def _dot_tn(a, b):
    return lax.dot_general(a, b, (((0,), (0,)), ((), ())), preferred_element_type=F32)


def _split3(x):
    hi = x.astype(BF16)
    r1 = x - hi.astype(F32)
    mid = r1.astype(BF16)
    lo = (r1 - mid.astype(F32)).astype(BF16)
    return hi, mid, lo


def _split2(x):
    hi = x.astype(BF16)
    return hi, (x - hi.astype(F32)).astype(BF16)


def _mat_f32(m, parts):
    return functools.reduce(lambda a, b: a + b, [_dot(m, p) for p in parts])


def _f32_mat(parts, m):
    return functools.reduce(lambda a, b: a + b, [_dot(p, m) for p in parts])


def _tri(n, lower):
    r = lax.broadcasted_iota(jnp.int32, (n, n), 0)
    c = lax.broadcasted_iota(jnp.int32, (n, n), 1)
    keep = (c <= r) if lower else (r <= c)
    return jnp.where(keep, 1.0, 0.0).astype(BF16)


def _pad_rows(a, n):
    if a.shape[0] == n:
        return a
    return jnp.concatenate([a, jnp.zeros((n - a.shape[0], a.shape[1]), a.dtype)], axis=0)


def _inproj_even_kernel(x_ref, g_ref, w_ref, bias_ref,
                        qt_ref, kf_ref, vf_ref, kb_ref, vt_ref, mqk_ref, mv_ref, mo_ref, gc_ref, *, state_t):
    h = _rms(x_ref[...], g_ref[...]).astype(BF16)

    def seg(a, b):
        return _dot(h, w_ref[:, a:b])

    qt_ref[...] = (seg(E_Q, E_K) * (FOX_HEAD_DIM ** -0.5 * LOG2E)).T.astype(BF16)
    k = seg(E_K, E_V)
    kb_ref[...] = k.astype(BF16)
    v = seg(E_V, E_MQK)
    v_t = v.T
    vt_ref[...] = v_t.astype(BF16)
    if state_t:
        kf_ref[0] = k.T
        vf_ref[0] = v_t
    else:
        kf_ref[...] = k
        vf_ref[...] = v
    mqk_ref[...] = seg(E_MQK, E_MV)
    mv_ref[...] = seg(E_MV, E_MO).astype(BF16)
    mo_ref[...] = seg(E_MO, E_G).astype(BF16)
    gz = seg(E_G, E_END) + bias_ref[...]
    lane = lax.broadcasted_iota(jnp.int32, gz.shape, 1)
    is_log = (lane < G_MI) | (lane >= G_MF)
    gc_ref[...] = jnp.where(is_log, _log_sigmoid(gz), gz)


def _inproj_even(x, g, w, bias, tm, batch, t_len):
    n = x.shape[0]
    state_t = t_len % tm == 0
    tpb = max(1, t_len // tm)
    row = lambda c: (pl.BlockSpec((tm, c), lambda i: (i, 0)), (n, c))
    col = (pl.BlockSpec((FOX_WIDTH, tm), lambda i: (0, i)), (FOX_WIDTH, n))
    if state_t:
        state = (pl.BlockSpec((1, FOX_WIDTH, tm), lambda i: (i // tpb, 0, i % tpb)), (batch, FOX_WIDTH, t_len))
    else:
        state = row(FOX_WIDTH)
    outs = [(col, BF16), (state, F32), (state, F32), (row(FOX_WIDTH), BF16), (col, BF16),
            (row(2 * ML_WIDTH), F32), (row(ML_WIDTH), BF16), (row(ML_WIDTH), BF16), (row(LANES), F32)]
    return pl.pallas_call(
        functools.partial(_inproj_even_kernel, state_t=state_t),
        grid=(n // tm,),
        in_specs=[row(D_MODEL)[0], _const_spec((1, D_MODEL)), _const_spec((D_MODEL, E_END)),
                  _const_spec((1, LANES))],
        out_specs=[spec for (spec, _), _ in outs],
        out_shape=[jax.ShapeDtypeStruct(shape, dt) for (_, shape), dt in outs],
        compiler_params=_cparams(("parallel",)),
        name="inproj_even",
    )(x, g, w, bias)


def _inproj_odd_kernel(x_ref, g_ref, w_ref, wa2_ref, ba_ref, q_ref, k_ref, v_ref, gg_ref, la_ref):
    h = _rms(x_ref[...], g_ref[...]).astype(BF16)

    def seg(a, b):
        return _dot(h, w_ref[:, a:b])

    q_ref[...] = (seg(O_Q, O_K) * (GLA_DK ** -0.5)).astype(BF16)
    k_ref[...] = seg(O_K, O_V).astype(BF16)
    v_ref[...] = seg(O_V, O_G).astype(BF16)
    gg_ref[...] = seg(O_G, O_A).astype(BF16)
    ga = seg(O_A, O_END).astype(BF16)
    la_ref[...] = _log_sigmoid(_dot(ga, wa2_ref[...]) + ba_ref[...]) * (1.0 / GLA_TAU)


def _inproj_odd(x, g, w, wa2, ba, tm):
    n = x.shape[0]
    row = lambda c: pl.BlockSpec((tm, c), lambda i: (i, 0))
    outs = [(GLA_KW, BF16), (GLA_KW, BF16), (GLA_VW, BF16), (GLA_VW, BF16), (GLA_KW, F32)]
    return pl.pallas_call(
        _inproj_odd_kernel,
        grid=(n // tm,),
        in_specs=[row(D_MODEL), _const_spec((1, D_MODEL)), _const_spec((D_MODEL, O_END)),
                  _const_spec((LANES, GLA_KW)), _const_spec((1, GLA_KW))],
        out_specs=[row(c) for c, _ in outs],
        out_shape=[jax.ShapeDtypeStruct((n, c), dt) for c, dt in outs],
        compiler_params=_cparams(("parallel",)),
        name="inproj_odd",
    )(x, g, w, wa2, ba)


def _post_kernel(*refs, n_mix, final):
    x_ref = refs[0]
    mix_refs = refs[1:1 + n_mix]
    wo_ref, gf_ref, w1_ref, w2_ref = refs[1 + n_mix:5 + n_mix]
    rest = refs[5 + n_mix:]
    mix = mix_refs[0][...] if n_mix == 1 else jnp.concatenate([r[...] for r in mix_refs], axis=1)
    x1 = x_ref[...] + _dot(mix, wo_ref[...])
    h = _rms(x1, gf_ref[...]).astype(BF16)
    y = x1
    for c in range(D_FF // D_MODEL):
        sl = slice(c * D_MODEL, (c + 1) * D_MODEL)
        t = jnp.maximum(_dot(h, w1_ref[:, sl]), 0.0)
        y = y + _dot((t * t).astype(BF16), w2_ref[sl, :])
    if final:
        gfin_ref, out_ref = rest
        out_ref[...] = _rms(y, gfin_ref[...])
    else:
        (out_ref,) = rest
        out_ref[...] = y


def _post(x, mixes, wo, gf, w1, w2, gfin, tm):
    n = x.shape[0]
    row = lambda c: pl.BlockSpec((tm, c), lambda i: (i, 0))
    final = gfin is not None
    ins = [x, *mixes, wo, gf, w1, w2]
    specs = [row(D_MODEL)] + [row(m.shape[1]) for m in mixes] + [
        _const_spec((D_MODEL, D_MODEL)), _const_spec((1, D_MODEL)),
        _const_spec((D_MODEL, D_FF)), _const_spec((D_FF, D_MODEL))]
    if final:
        ins.append(gfin)
        specs.append(_const_spec((1, D_MODEL)))
    return pl.pallas_call(
        functools.partial(_post_kernel, n_mix=len(mixes), final=final),
        grid=(n // tm,),
        in_specs=specs,
        out_specs=row(D_MODEL),
        out_shape=jax.ShapeDtypeStruct((n, D_MODEL), F32),
        compiler_params=_cparams(("parallel",)),
        name="post_final" if final else "post",
    )(*ins)


F_TERMS = 3


def _fox_placement():
    p = np.zeros((F_TERMS, LANES, FOX_HEADS // 2 * LANES), np.float32)
    for x in range(F_TERMS):
        for h in range(FOX_HEADS):
            p[x, h, (h // 2) * LANES + F_TERMS * (h % 2) + x] = 1.0
    return jnp.asarray(p, BF16)


def _fox_prep_kernel(lf_ref, k_ref, place_ref, ka_ref, carry):
    @pl.when(pl.program_id(1) == 0)
    def _():
        carry[...] = jnp.zeros_like(carry)

    tc = lf_ref.shape[1]
    f = _mat_f32(_tri(tc, True), _split3(lf_ref[0])) + carry[...]
    carry[...] = f[tc - 1:tc, :]
    parts = _split3(f * LOG2E)
    cols = _dot(parts[0], place_ref[0]) + _dot(parts[1], place_ref[1]) + _dot(parts[2], place_ref[2])
    for hp in range(FOX_HEADS // 2):
        ka_ref[:, 2 * hp * LANES:(2 * hp + 1) * LANES] = k_ref[:, hp * LANES:(hp + 1) * LANES]
        ka_ref[:, (2 * hp + 1) * LANES:(2 * hp + 2) * LANES] = cols[:, hp * LANES:(hp + 1) * LANES].astype(BF16)


def _fox_prep(lf_col, k, tc):
    b, t, _ = lf_col.shape
    nt = t // tc
    place = _fox_placement()
    return pl.pallas_call(
        _fox_prep_kernel,
        grid=(b, nt),
        in_specs=[pl.BlockSpec((1, tc, LANES), lambda i, j: (i, j, 0)),
                  pl.BlockSpec((tc, FOX_WIDTH), lambda i, j: (i * nt + j, 0)),
                  _const_spec(place.shape)],
        out_specs=pl.BlockSpec((tc, 2 * FOX_WIDTH), lambda i, j: (i * nt + j, 0)),
        out_shape=jax.ShapeDtypeStruct((b * t, 2 * FOX_WIDTH), BF16),
        scratch_shapes=[pltpu.VMEM((1, LANES), F32)],
        compiler_params=_cparams(("parallel", "arbitrary")),
        name="fox_prep",
    )(lf_col, k, place)


ONES_ROWS = 16


def _fox_attn_kernel(qt_ref, ka_ref, vt_ref, o_ref, sa_scr, sb_scr, acc_scr, *, tq, tkc, past, n_diag):
    q_first = past + pl.program_id(2) * tq
    n_full = q_first // tkc
    drow = lax.broadcasted_iota(jnp.int32, (LANES, 1), 0)
    lane = lax.broadcasted_iota(jnp.int32, (1, 2 * tq), 1)
    lane_head = jnp.where(lane < tq, 0, 1)
    own_head = jnp.where(jnp.where(drow < FOX_HEAD_DIM, 0, 1) == lane_head, 1.0, 0.0).astype(BF16)
    qt = qt_ref[...]
    q2 = jnp.concatenate([qt, qt], axis=1) * own_head
    f_sel = (drow >= F_TERMS * lane_head) & (drow < F_TERMS * lane_head + F_TERMS)
    qa = jnp.concatenate([q2, jnp.where(f_sel, -1.0, 0.0).astype(BF16)], axis=0)
    ones = jnp.ones((ONES_ROWS, tkc), BF16)
    acc_scr[...] = jnp.zeros_like(acc_scr)

    def produce(j, dst):
        start = pl.multiple_of(j * tkc, tkc)
        dst[...] = _dot(ka_ref[pl.ds(start, tkc), :], qa)

    def consume(j, src, m_prev, masked):
        start = pl.multiple_of(j * tkc, tkc)
        st = src[...]
        if masked:
            kpos = start + lax.broadcasted_iota(jnp.int32, (tkc, 1), 0)
            qpos = q_first + (lane & (tq - 1))
            st = jnp.where(kpos <= qpos, st, NEG)
        m_new = jnp.maximum(m_prev, jnp.max(st, axis=0, keepdims=True))
        alpha = jnp.exp2(m_prev - m_new)
        p = jnp.exp2(st - m_new).astype(BF16)
        for hh in range(2):
            rows = slice(hh * FOX_HEAD_DIM, (hh + 1) * FOX_HEAD_DIM)
            cols = slice(hh * tq, (hh + 1) * tq)
            va = jnp.concatenate([vt_ref[rows, pl.ds(start, tkc)], ones], axis=0)
            acc_scr[hh] = alpha[:, cols] * acc_scr[hh] + _dot(va, p[:, cols])
        return m_new

    produce(0, sa_scr)

    def body(i, m):
        produce(2 * i + 1, sb_scr)
        m = consume(2 * i, sa_scr, m, False)
        produce(2 * i + 2, sa_scr)
        return consume(2 * i + 1, sb_scr, m, False)

    m = lax.fori_loop(0, n_full // 2, body, jnp.full((1, 2 * tq), NEG, F32))
    bufs = (sa_scr, sb_scr)
    for d in range(n_diag):
        if d + 1 < n_diag:
            produce(n_full + d + 1, bufs[(d + 1) % 2])
        m = consume(n_full + d, bufs[d % 2], m, True)
    out = jnp.concatenate([acc_scr[hh, 0:FOX_HEAD_DIM] / acc_scr[hh, FOX_HEAD_DIM:FOX_HEAD_DIM + 1]
                           for hh in range(2)], axis=0)
    o_ref[...] = out.T.astype(o_ref.dtype)


def _fox_attention(qt, ka, vt, batch, tq_len, tk_len, past, tq, tkc):
    nq = tq_len // tq
    n_diag = max(1, tq // tkc)
    assert tq & (tq - 1) == 0 and tq % LANES == 0 and (tkc % tq == 0 or tq % tkc == 0) and tq_len % tq == 0
    assert past % (2 * tkc) == 0 and (nq == 1 or tq % (2 * tkc) == 0)
    assert tk_len >= past + max(tq_len, n_diag * tkc) and tk_len % tkc == 0
    pairs = FOX_HEADS // 2
    return pl.pallas_call(
        functools.partial(_fox_attn_kernel, tq=tq, tkc=tkc, past=past, n_diag=n_diag),
        grid=(batch, pairs, nq),
        in_specs=[pl.BlockSpec((LANES, tq), lambda b, h, i: (h, b * nq + i)),
                  pl.BlockSpec((tk_len, 2 * LANES), lambda b, h, i: (b, h)),
                  pl.BlockSpec((LANES, tk_len), lambda b, h, i: (h, b))],
        out_specs=pl.BlockSpec((tq, LANES), lambda b, h, i: (b * nq + i, h)),
        out_shape=jax.ShapeDtypeStruct((batch * tq_len, FOX_WIDTH), BF16),
        scratch_shapes=[pltpu.VMEM((tkc, 2 * tq), F32), pltpu.VMEM((tkc, 2 * tq), F32),
                        pltpu.VMEM((2, FOX_HEAD_DIM + ONES_ROWS, tq), F32)],
        compiler_params=_cparams(("parallel", "parallel", "arbitrary")),
        name="fox_attention",
    )(qt, ka, vt)


def _mlstm_kernel(mqk_ref, mv_ref, mo_ref, gc_ref, gr_ref, c0_ref, n0_ref, m0_ref, buf_ref,
                  cw_ref, cb_ref, gml_ref,
                  h_ref, c_out, n_out, m_out, buf_out,
                  c_scr, n_scr, m_scr, prev_scr, *, L, t_last, nc):
    ci = pl.program_id(1)
    rows = mqk_ref.shape[0]

    @pl.when(ci == 0)
    def _():
        c_scr[...] = c0_ref[0]
        n_scr[...] = n0_ref[0]
        m_scr[...] = m0_ref[0]
        prev_scr[...] = buf_ref[0]

    u = _pad_rows(mqk_ref[...], L)
    ext = jnp.concatenate([prev_scr[...], u], axis=0)
    y = cb_ref[...] + cw_ref[ML_CONV - 1:ML_CONV, :] * u
    for s in range(1, ML_CONV):
        y = y + cw_ref[ML_CONV - 1 - s:ML_CONV - s, :] * pltpu.roll(ext, s, axis=0)[8:8 + L]
    qk = y * _sigmoid(y)
    prev_scr[...] = u[L - 8:L]

    gc = _pad_rows(gc_ref[...], L)
    gr = gr_ref[0]
    b_col = _mat_f32(_tri(L, True), _split3(gc))
    b_row = _f32_mat(_split3(gr), _tri(L, False))
    v_all = _pad_rows(mv_ref[...], L)
    o_all = _pad_rows(mo_ref[...], L)
    t_idx = lax.broadcasted_iota(jnp.int32, (L, 1), 0)
    causal = lax.broadcasted_iota(jnp.int32, (1, L), 1) <= t_idx

    for h in range(ML_HEADS):
        hs = slice(h * ML_HEAD_DIM, (h + 1) * ML_HEAD_DIM)
        q32 = qk[:, hs]
        k32 = qk[:, ML_WIDTH + h * ML_HEAD_DIM:ML_WIDTH + (h + 1) * ML_HEAD_DIM] * (ML_HEAD_DIM ** -0.5)
        qb, kb, vb = q32.astype(BF16), k32.astype(BF16), v_all[:, hs]
        bc = b_col[:, G_MF + h:G_MF + h + 1]
        ic = gc[:, G_MI + h:G_MI + h + 1]
        br = b_row[G_MF + h:G_MF + h + 1, :]
        ir = gr[G_MI + h:G_MI + h + 1, :]
        m_prev = m_scr[h][:, 0:1]
        c_prev = c_scr[h]
        n_prev = n_scr[h]

        dmat = jnp.where(causal, bc - br + ir, NEG)
        inter = bc + m_prev
        m_t = jnp.maximum(inter, jnp.max(dmat, axis=-1, keepdims=True))
        w = jnp.exp(dmat - m_t)
        g = jnp.exp(inter - m_t)
        a = w * _dot_nt(qb, kb)
        num = g * _dot_nt(qb, c_prev.astype(BF16)) + _dot(a.astype(BF16), vb)
        den = g * jnp.sum(q32 * n_prev, axis=-1, keepdims=True) + jnp.sum(a, axis=-1, keepdims=True)
        hh = num / jnp.maximum(jnp.abs(den), jnp.exp(-m_t))

        b_last = bc[t_last:t_last + 1, :]
        m_last = m_t[t_last:t_last + 1, :]
        g_end = g[t_last:t_last + 1, :]
        w_end = jnp.where(t_idx <= t_last, jnp.exp(b_last - bc + ic - m_last), 0.0)
        c_scr[h] = g_end * c_prev + _dot_tn((vb.astype(F32) * w_end).astype(BF16), kb)
        n_scr[h] = g_end * n_prev + jnp.sum(k32 * w_end, axis=0, keepdims=True)
        m_scr[h] = jnp.broadcast_to(m_last, (1, LANES))

        yh = hh * lax.rsqrt(jnp.mean(hh * hh, axis=-1, keepdims=True) + EPS) * gml_ref[:, hs]
        yh = yh * _sigmoid(o_all[:, hs].astype(F32))
        h_ref[:, hs] = yh[:rows].astype(h_ref.dtype)

    @pl.when(ci == nc - 1)
    def _():
        c_out[0] = c_scr[...]
        n_out[0] = n_scr[...]
        m_out[0] = m_scr[...]
        buf_out[0] = ext[t_last + 1:t_last + 9]


def _mlstm(mqk, mv, mo, gc, gr, c0, n0, m0, buf8, cw, cb, gml, batch, t_len):
    L = ML_CHUNK if t_len >= ML_CHUNK else LANES
    rows = min(L, t_len)
    nc = t_len // rows
    t_last = rows - 1
    assert t_len % rows == 0 and (nc == 1 or rows == L) and (t_last + 1) % 8 == 0
    tok = lambda c: pl.BlockSpec((rows, c), lambda b, i: (b * nc + i, 0))
    st = lambda *s: pl.BlockSpec((1,) + s, lambda b, i: (b,) + (0,) * len(s))
    shp_c = (ML_HEADS, ML_HEAD_DIM, ML_HEAD_DIM)
    shp_n = (ML_HEADS, 1, ML_HEAD_DIM)
    return pl.pallas_call(
        functools.partial(_mlstm_kernel, L=L, t_last=t_last, nc=nc),
        grid=(batch, nc),
        in_specs=[tok(2 * ML_WIDTH), tok(ML_WIDTH), tok(ML_WIDTH), tok(LANES),
                  pl.BlockSpec((1, 16, L), lambda b, i: (b, 0, i)),
                  st(*shp_c), st(*shp_n), st(*shp_n), st(8, 2 * ML_WIDTH),
                  _const_spec((ML_CONV, 2 * ML_WIDTH)), _const_spec((1, 2 * ML_WIDTH)), _const_spec((1, ML_WIDTH))],
        out_specs=[tok(ML_WIDTH), st(*shp_c), st(*shp_n), st(*shp_n), st(8, 2 * ML_WIDTH)],
        out_shape=[jax.ShapeDtypeStruct((batch * t_len, ML_WIDTH), BF16),
                   jax.ShapeDtypeStruct((batch,) + shp_c, F32),
                   jax.ShapeDtypeStruct((batch,) + shp_n, F32),
                   jax.ShapeDtypeStruct((batch,) + shp_n, F32),
                   jax.ShapeDtypeStruct((batch, 8, 2 * ML_WIDTH), F32)],
        scratch_shapes=[pltpu.VMEM(shp_c, F32), pltpu.VMEM(shp_n, F32), pltpu.VMEM(shp_n, F32),
                        pltpu.VMEM((8, 2 * ML_WIDTH), F32)],
        compiler_params=_cparams(("parallel", "arbitrary")),
        name="mlstm",
    )(mqk, mv, mo, gc, gr, c0, n0, m0, buf8, cw, cb, gml)


def _gla_level_matrices(L):
    mats = np.zeros((len(GLA_LEVELS), L, L), np.float32)
    for li, b in enumerate(GLA_LEVELS):
        for t in range(L):
            base = (t // (2 * b)) * 2 * b
            bound = base + b - 1
            if t > bound:
                mats[li, t, bound + 1:t + 1] = 1.0
            else:
                mats[li, t, t + 1:bound + 1] = 1.0
    return jnp.asarray(mats, BF16)


def _gla_kernel(q_ref, k_ref, v_ref, gg_ref, la_ref, s0_ref, lvl_ref, gn_ref, o_ref, s_out, s_scr, *, L, t_last, nc):
    ci = pl.program_id(1)
    rows = q_ref.shape[0]

    @pl.when(ci == 0)
    def _():
        s_scr[...] = s0_ref[0]

    la3 = _split2(_pad_rows(la_ref[...], L))
    cb = _mat_f32(_tri(L, True), la3)
    q_all = _pad_rows(q_ref[...], L)
    k_all = _pad_rows(k_ref[...], L)
    v_all = _pad_rows(v_ref[...], L)
    g_all = _pad_rows(gg_ref[...], L)
    t_idx = lax.broadcasted_iota(jnp.int32, (L, 1), 0)
    s_idx = lax.broadcasted_iota(jnp.int32, (1, L), 1)

    a_heads = []
    q32s, k32s = [], []
    for h in range(GLA_HEADS):
        ks = slice(h * GLA_DK, (h + 1) * GLA_DK)
        q32s.append(q_all[:, ks].astype(F32))
        k32s.append(k_all[:, ks].astype(F32))
        a_heads.append(jnp.where(t_idx == s_idx, _dot_nt(q_all[:, ks], k_all[:, ks]), 0.0))
    for li, b in enumerate(GLA_LEVELS):
        e_all = jnp.exp(_mat_f32(lvl_ref[li], la3))
        sh = b.bit_length() - 1
        second = (jnp.right_shift(t_idx, sh) & 1) == 1
        same_pair = jnp.right_shift(t_idx, sh + 1) == jnp.right_shift(s_idx, sh + 1)
        for h in range(GLA_HEADS):
            ks = slice(h * GLA_DK, (h + 1) * GLA_DK)
            e = e_all[:, ks]
            qe = jnp.where(second, q32s[h] * e, 0.0).astype(BF16)
            ke = jnp.where(second, 0.0, k32s[h] * e).astype(BF16)
            a_heads[h] = a_heads[h] + jnp.where(same_pair, _dot_nt(qe, ke), 0.0)

    for h in range(GLA_HEADS):
        ks = slice(h * GLA_DK, (h + 1) * GLA_DK)
        vs = slice(h * GLA_DV, (h + 1) * GLA_DV)
        cbh = cb[:, ks]
        vb = v_all[:, vs]
        s_t = s_scr[h]
        o = _dot_nt((q32s[h] * jnp.exp(cbh)).astype(BF16), s_t.astype(BF16)) + _dot(a_heads[h].astype(BF16), vb)
        cl = cbh[t_last:t_last + 1, :]
        kd = jnp.where(t_idx <= t_last, k32s[h] * jnp.exp(cl - cbh), 0.0).astype(BF16)
        s_scr[h] = jnp.exp(cl) * s_t + _dot_tn(vb, kd)
        y = o * lax.rsqrt(jnp.mean(o * o, axis=-1, keepdims=True) + EPS) * gn_ref[:, vs]
        gate = g_all[:, vs].astype(F32)
        y = y * (gate * _sigmoid(gate))
        o_ref[:, vs] = y[:rows].astype(o_ref.dtype)

    @pl.when(ci == nc - 1)
    def _():
        s_out[0] = s_scr[...]


def _gla(q, k, v, gg, la, s0t, gn, batch, t_len):
    L = GLA_CHUNK
    rows = min(L, t_len)
    nc = t_len // rows
    t_last = rows - 1
    assert t_len % rows == 0 and (nc == 1 or rows == L)
    tok = lambda c: pl.BlockSpec((rows, c), lambda b, i: (b * nc + i, 0))
    shp_s = (GLA_HEADS, GLA_DV, GLA_DK)
    st = pl.BlockSpec((1,) + shp_s, lambda b, i: (b, 0, 0, 0))
    levels = _gla_level_matrices(L)
    return pl.pallas_call(
        functools.partial(_gla_kernel, L=L, t_last=t_last, nc=nc),
        grid=(batch, nc),
        in_specs=[tok(GLA_KW), tok(GLA_KW), tok(GLA_VW), tok(GLA_VW), tok(GLA_KW), st,
                  _const_spec(levels.shape), _const_spec((1, GLA_VW))],
        out_specs=[tok(GLA_VW), st],
        out_shape=[jax.ShapeDtypeStruct((batch * t_len, GLA_VW), BF16),
                   jax.ShapeDtypeStruct((batch,) + shp_s, F32)],
        scratch_shapes=[pltpu.VMEM(shp_s, F32)],
        compiler_params=_cparams(("parallel", "arbitrary")),
        name="gla",
    )(q, k, v, gg, la, s0t, levels, gn)


def _pack_even(w, b_fox_f, b_i, b_f):
    d = w.shape[0]
    o = np.cumsum((0, FOX_WIDTH, FOX_WIDTH, FOX_WIDTH, FOX_HEADS, 2 * ML_WIDTH, ML_WIDTH, ML_WIDTH, ML_HEADS, ML_HEADS))
    n_gate = FOX_HEADS + 2 * ML_HEADS
    wp = jnp.concatenate([w[:, o[0]:o[3]], w[:, o[4]:o[7]], w[:, o[3]:o[4]], w[:, o[7]:o[9]],
                          jnp.zeros((d, LANES - n_gate), w.dtype)], axis=1).astype(BF16)
    bias = jnp.concatenate([b_fox_f, b_i, b_f, jnp.zeros((LANES - n_gate,), F32)]).reshape(1, LANES)
    return wp, bias


def _pack_odd(w, w_a2):
    d = w.shape[0]
    wp = jnp.concatenate([w, jnp.zeros((d, LANES - GLA_RANK), w.dtype)], axis=1).astype(BF16)
    wa2 = jnp.concatenate([w_a2, jnp.zeros((LANES - GLA_RANK, w_a2.shape[1]), w_a2.dtype)], axis=0).astype(BF16)
    return wp, wa2


def _gate_rows(gc, batch, t_len, t_pad):
    g = gc.reshape(batch, t_len, LANES)[:, :, :16].transpose(0, 2, 1)
    if t_pad > t_len:
        g = jnp.pad(g, ((0, 0), (0, 0), (0, t_pad - t_len)))
    return g


def _trunk(x, fox_k, fox_v, fox_lf, ml_c, ml_n, ml_m, ml_buf, gla_s, params):
    (norm_mix, norm_ffn, norm_final, w_even, bias_even, conv_w, conv_b, g_ml,
     w_odd, w_a2, b_a, g_gla, w_out, w_ff1, w_ff2) = params
    batch, t_len, d = x.shape
    past = fox_k.shape[2]
    n = batch * t_len
    tm = 512 if n % 512 == 0 else 256 if n % 256 == 0 else n
    depth = norm_mix.shape[0]
    xf = x.reshape(n, d)
    ev_states, odd_states = [], []
    y = None
    for layer in range(depth):
        j = layer // 2
        if layer % 2 == 0:
            qt, kf, vf, kb, vt, mqk, mv, mo, gc = _inproj_even(xf, norm_mix[layer][None], w_even[j], bias_even[j],
                                                               tm, batch, t_len)
            lf_col = gc.reshape(batch, t_len, LANES)
            tkc = 256
            if past == 0:
                tq, tq_len = min(512, t_len), t_len
                tk_len = t_len
                k_all, vt_all = kb, vt
            else:
                tq = tq_len = LANES
                tk_len = past + tkc
                extra = tk_len - past - t_len
                k_all = jnp.concatenate(
                    [fox_k[j].reshape(batch, past, FOX_WIDTH).astype(BF16), kb.reshape(batch, t_len, FOX_WIDTH),
                     jnp.zeros((batch, extra, FOX_WIDTH), BF16)], axis=1).reshape(batch * tk_len, FOX_WIDTH)
                vt_all = jnp.concatenate(
                    [fox_v[j].reshape(batch, past, FOX_WIDTH).transpose(2, 0, 1).astype(BF16),
                     vt.reshape(FOX_WIDTH, batch, t_len), jnp.zeros((FOX_WIDTH, batch, extra), BF16)],
                    axis=2).reshape(FOX_WIDTH, batch * tk_len)
                lf_col = jnp.concatenate(
                    [jnp.pad(fox_lf[j], ((0, 0), (0, 0), (0, LANES - FOX_HEADS))), lf_col,
                     jnp.zeros((batch, extra, LANES), F32)], axis=1)
                qt = jnp.pad(qt.reshape(FOX_WIDTH, batch, t_len), ((0, 0), (0, 0), (0, tq_len - t_len)))
                qt = qt.reshape(FOX_WIDTH, batch * tq_len)
            ka = _fox_prep(lf_col, k_all, 512 if tk_len % 512 == 0 else tkc)
            attn = _fox_attention(qt, ka, vt_all, batch, tq_len, tk_len, past, tq, tkc)
            if tq_len != t_len:
                attn = attn.reshape(batch, tq_len, FOX_WIDTH)[:, :t_len].reshape(n, FOX_WIDTH)
            l_ml = ML_CHUNK if t_len >= ML_CHUNK else LANES
            gr = _gate_rows(gc, batch, t_len, max(t_len, l_ml))
            c0 = ml_c[j]
            n0 = ml_n[j][:, :, None, :]
            m0 = jnp.broadcast_to(ml_m[j][:, :, None, None], (batch, ML_HEADS, 1, LANES))
            buf8 = jnp.pad(ml_buf[j], ((0, 0), (8 - (ML_CONV - 1), 0), (0, 0)))
            h_ml, c_new, n_new, m_new, buf_new = _mlstm(mqk, mv, mo, gc, gr, c0, n0, m0, buf8,
                                                        conv_w[j], conv_b[j][None], g_ml[j][None], batch, t_len)
            if kf.ndim == 3:
                heads = lambda a: a.reshape(batch, FOX_HEADS, FOX_HEAD_DIM, t_len).transpose(0, 3, 1, 2)
            else:
                heads = lambda a: a.reshape(batch, t_len, FOX_HEADS, FOX_HEAD_DIM)
            ev_states.append((heads(kf), heads(vf),
                              gc[:, :FOX_HEADS].reshape(batch, t_len, FOX_HEADS),
                              c_new, n_new[:, :, 0, :], m_new[:, :, 0, 0], buf_new[:, 8 - (ML_CONV - 1):, :]))
            mixes = [attn, h_ml]
        else:
            q, k, v, gg, la = _inproj_odd(xf, norm_mix[layer][None], w_odd[j], w_a2[j], b_a[j][None], tm)
            s0t = gla_s[j].transpose(0, 1, 3, 2)
            o, s_new = _gla(q, k, v, gg, la, s0t, g_gla[j][None], batch, t_len)
            odd_states.append(s_new.transpose(0, 1, 3, 2))
            mixes = [o]
        last = layer == depth - 1
        out = _post(xf, mixes, w_out[layer], norm_ffn[layer][None], w_ff1[layer], w_ff2[layer],
                    norm_final[None] if last else None, tm)
        if last:
            y = out
        else:
            xf = out
    ev = [jnp.stack([s[i] for s in ev_states]) for i in range(7)]
    return y.reshape(batch, t_len, d), ev, jnp.stack(odd_states)


def kernel(x_prompt, x_sample, cache_fox_k, cache_fox_v, cache_fox_logf, state_mlstm_c, state_mlstm_n, state_mlstm_m, state_mlstm_conv, state_gla_s, norm_mix, norm_ffn, norm_final, w_in_even, b_fox_f, conv_w_ml, conv_b_ml, b_ml_i, b_ml_f, g_ml, w_in_odd, w_gla_a2, b_gla_a, g_gla, w_out, w_ff1, w_ff2):
    n_even, n_odd = w_in_even.shape[0], w_in_odd.shape[0]
    packed_even = [_pack_even(w_in_even[j], b_fox_f[j], b_ml_i[j], b_ml_f[j]) for j in range(n_even)]
    packed_odd = [_pack_odd(w_in_odd[j], w_gla_a2[j]) for j in range(n_odd)]
    params = (norm_mix, norm_ffn, norm_final,
              [p[0] for p in packed_even], [p[1] for p in packed_even], conv_w_ml, conv_b_ml, g_ml,
              [p[0] for p in packed_odd], [p[1] for p in packed_odd], b_gla_a, g_gla,
              w_out.astype(BF16), w_ff1.astype(BF16), w_ff2.astype(BF16))

    bp = x_prompt.shape[0]
    dt = x_prompt.dtype
    zeros = lambda *s: jnp.zeros(s, dt)
    y_p, ev_p, gla_p = _trunk(
        x_prompt,
        zeros(n_even, bp, 0, FOX_HEADS, FOX_HEAD_DIM), zeros(n_even, bp, 0, FOX_HEADS, FOX_HEAD_DIM),
        zeros(n_even, bp, 0, FOX_HEADS),
        zeros(n_even, bp, ML_HEADS, ML_HEAD_DIM, ML_HEAD_DIM), zeros(n_even, bp, ML_HEADS, ML_HEAD_DIM),
        zeros(n_even, bp, ML_HEADS), zeros(n_even, bp, ML_CONV - 1, 2 * ML_WIDTH),
        zeros(n_odd, bp, GLA_HEADS, GLA_DK, GLA_DV), params)
    y_s, ev_s, gla_s = _trunk(x_sample, cache_fox_k, cache_fox_v, cache_fox_logf, state_mlstm_c, state_mlstm_n,
                              state_mlstm_m, state_mlstm_conv, state_gla_s, params)
    return (y_p, y_s, *ev_p, gla_p, *ev_s, gla_s)
```

```python
import functools

import numpy as np
import jax
import jax.numpy as jnp
from jax import lax
from jax.experimental import pallas as pl
from jax.experimental.pallas import tpu as pltpu

F32 = jnp.float32
BF16 = jnp.bfloat16
EPS = 1e-6
NEG = -1e30
LOG2E = 1.4426950408889634

LANES = 128
VMEM_LIMIT = 56 * 1024 * 1024

D_MODEL = 1024
D_FF = 4 * D_MODEL
FOX_HEADS, FOX_HEAD_DIM = 8, 64
FOX_WIDTH = FOX_HEADS * FOX_HEAD_DIM
ML_HEADS, ML_HEAD_DIM = 4, 128
ML_WIDTH = ML_HEADS * ML_HEAD_DIM
ML_CONV = 4
GLA_HEADS, GLA_DK, GLA_DV = 4, 128, 256
GLA_KW = GLA_HEADS * GLA_DK
GLA_VW = GLA_HEADS * GLA_DV
GLA_RANK = 16
GLA_TAU = 16.0

E_Q, E_K, E_V, E_MQK, E_MV, E_MO, E_G, E_END = 0, 512, 1024, 1536, 2560, 3072, 3584, 3712
G_FOX, G_MI, G_MF = 0, 8, 12
O_Q, O_K, O_V, O_G, O_A, O_END = 0, 512, 1024, 2048, 3072, 3200

ML_CHUNK = 256
GLA_CHUNK = 128
GLA_LEVELS = (64, 32, 16, 8, 4, 2, 1)


def _cparams(sem):
    return pltpu.CompilerParams(dimension_semantics=sem, vmem_limit_bytes=VMEM_LIMIT)


def _const_spec(shape):
    nd = len(shape)
    return pl.BlockSpec(shape, lambda *_: (0,) * nd, pipeline_mode=pl.Buffered(1))


def _rms(x, g):
    return x * lax.rsqrt(jnp.mean(x * x, axis=-1, keepdims=True) + EPS) * g


def _sigmoid(x):
    return 1.0 / (1.0 + jnp.exp(-x))


def _log_sigmoid(x):
    return -(jnp.maximum(-x, 0.0) + jnp.log1p(jnp.exp(-jnp.abs(x))))


def _dot(a, b):
    return jnp.dot(a, b, preferred_element_type=F32)


def _dot_nt(a, b):
    return lax.dot_general(a, b, (((1,), (1,)), ((), ())), preferred_element_type=F32)


def _dot_tn(a, b):
    return lax.dot_general(a, b, (((0,), (0,)), ((), ())), preferred_element_type=F32)


def _split3(x):
    hi = x.astype(BF16)
    r1 = x - hi.astype(F32)
    mid = r1.astype(BF16)
    lo = (r1 - mid.astype(F32)).astype(BF16)
    return hi, mid, lo


def _split2(x):
    hi = x.astype(BF16)
    return hi, (x - hi.astype(F32)).astype(BF16)


def _mat_f32(m, parts):
    return functools.reduce(lambda a, b: a + b, [_dot(m, p) for p in parts])


def _f32_mat(parts, m):
    return functools.reduce(lambda a, b: a + b, [_dot(p, m) for p in parts])


def _tri(n, lower):
    r = lax.broadcasted_iota(jnp.int32, (n, n), 0)
    c = lax.broadcasted_iota(jnp.int32, (n, n), 1)
    keep = (c <= r) if lower else (r <= c)
    return jnp.where(keep, 1.0, 0.0).astype(BF16)


def _pad_rows(a, n):
    if a.shape[0] == n:
        return a
    return jnp.concatenate([a, jnp.zeros((n - a.shape[0], a.shape[1]), a.dtype)], axis=0)


def _even_project(x_ref, g_ref, w_ref, bias_ref, qt_ref, vt_ref, mqk_ref, mv_ref, mo_ref, gc_ref):
    h = _rms(x_ref[...], g_ref[...]).astype(BF16)

    def seg(a, b):
        return _dot(h, w_ref[:, a:b])

    qt_ref[...] = (seg(E_Q, E_K) * (FOX_HEAD_DIM ** -0.5 * LOG2E)).T.astype(BF16)
    k = seg(E_K, E_V)
    v = seg(E_V, E_MQK)
    v_t = v.T
    vt_ref[...] = v_t.astype(BF16)
    mqk_ref[...] = seg(E_MQK, E_MV)
    mv_ref[...] = seg(E_MV, E_MO).astype(BF16)
    mo_ref[...] = seg(E_MO, E_G).astype(BF16)
    gz = seg(E_G, E_END) + bias_ref[...]
    lane = lax.broadcasted_iota(jnp.int32, gz.shape, 1)
    is_log = (lane < G_MI) | (lane >= G_MF)
    gates = jnp.where(is_log, _log_sigmoid(gz), gz)
    gc_ref[...] = gates
    return k, v, v_t, gates


def _inproj_even_rows_kernel(x_ref, g_ref, w_ref, bias_ref,
                             qt_ref, kf_ref, vf_ref, kb_ref, vt_ref, mqk_ref, mv_ref, mo_ref, gc_ref):
    k, v, _, _ = _even_project(x_ref, g_ref, w_ref, bias_ref, qt_ref, vt_ref, mqk_ref, mv_ref, mo_ref, gc_ref)
    kf_ref[...] = k
    vf_ref[...] = v
    kb_ref[...] = k.astype(BF16)


def _inproj_even_seq_kernel(*refs, tpb, aliased):
    x_ref, g_ref, w_ref, bias_ref, place_ref = refs[:5]
    refs = refs[5 + (2 if aliased else 0):]
    qt_ref, kf_ref, vf_ref, ka_ref, vt_ref, mqk_ref, mv_ref, mo_ref, gc_ref, carry = refs
    k, _, v_t, gates = _even_project(x_ref, g_ref, w_ref, bias_ref, qt_ref, vt_ref, mqk_ref, mv_ref, mo_ref, gc_ref)
    kf_ref[0, 0] = k.T
    vf_ref[0, 0] = v_t

    @pl.when(pl.program_id(0) % tpb == 0)
    def _():
        carry[...] = jnp.zeros_like(carry)

    _append_f_terms(gates, carry, k.astype(BF16), place_ref, ka_ref)


def _inproj_even(x, g, w, bias, tm, batch, t_len, slot, n_slots, kv_prev):
    n = x.shape[0]
    row = lambda c: (pl.BlockSpec((tm, c), lambda i: (i, 0)), (n, c))
    col = (pl.BlockSpec((FOX_WIDTH, tm), lambda i: (0, i)), (FOX_WIDTH, n))
    common_in = [row(D_MODEL)[0], _const_spec((1, D_MODEL)), _const_spec((D_MODEL, E_END)), _const_spec((1, LANES))]
    tail = [(row(2 * ML_WIDTH), F32), (row(ML_WIDTH), BF16), (row(ML_WIDTH), BF16), (row(LANES), F32)]
    if t_len % tm != 0:
        outs = [(col, BF16), (row(FOX_WIDTH), F32), (row(FOX_WIDTH), F32), (row(FOX_WIDTH), BF16), (col, BF16)] + tail
        return pl.pallas_call(
            _inproj_even_rows_kernel,
            grid=(n // tm,),
            in_specs=common_in,
            out_specs=[spec for (spec, _), _ in outs],
            out_shape=[jax.ShapeDtypeStruct(shape, dt) for (_, shape), dt in outs],
            compiler_params=_cparams(("parallel",)),
            name="inproj_even_rows",
        )(x, g, w, bias)
    tpb = t_len // tm
    state = (pl.BlockSpec((1, 1, FOX_WIDTH, tm), lambda i: (slot, i // tpb, 0, i % tpb)),
             (n_slots, batch, FOX_WIDTH, t_len))
    outs = [(col, BF16), (state, F32), (state, F32), (row(2 * FOX_WIDTH), BF16), (col, BF16)] + tail
    place = _fox_placement()
    ins = [x, g, w, bias, place]
    in_specs = common_in + [_const_spec(place.shape)]
    aliases = {}
    if kv_prev is not None:
        ins += list(kv_prev)
        in_specs += [pl.BlockSpec(memory_space=pl.ANY)] * 2
        aliases = {5: 1, 6: 2}
    return pl.pallas_call(
        functools.partial(_inproj_even_seq_kernel, tpb=tpb, aliased=kv_prev is not None),
        grid=(n // tm,),
        in_specs=in_specs,
        out_specs=[spec for (spec, _), _ in outs],
        out_shape=[jax.ShapeDtypeStruct(shape, dt) for (_, shape), dt in outs],
        scratch_shapes=[pltpu.VMEM((1, LANES), F32)],
        input_output_aliases=aliases,
        compiler_params=_cparams(("arbitrary",)),
        name="inproj_even_seq",
    )(*ins)


def _inproj_odd_kernel(x_ref, g_ref, w_ref, wa2_ref, ba_ref, q_ref, k_ref, v_ref, gg_ref, la_ref):
    h = _rms(x_ref[...], g_ref[...]).astype(BF16)

    def seg(a, b):
        return _dot(h, w_ref[:, a:b])

    q_ref[...] = (seg(O_Q, O_K) * (GLA_DK ** -0.5)).astype(BF16)
    k_ref[...] = seg(O_K, O_V).astype(BF16)
    v_ref[...] = seg(O_V, O_G).astype(BF16)
    gg_ref[...] = seg(O_G, O_A).astype(BF16)
    ga = seg(O_A, O_END).astype(BF16)
    la_ref[...] = _log_sigmoid(_dot(ga, wa2_ref[...]) + ba_ref[...]) * (1.0 / GLA_TAU)


def _inproj_odd(x, g, w, wa2, ba, tm):
    n = x.shape[0]
    row = lambda c: pl.BlockSpec((tm, c), lambda i: (i, 0))
    outs = [(GLA_KW, BF16), (GLA_KW, BF16), (GLA_VW, BF16), (GLA_VW, BF16), (GLA_KW, F32)]
    return pl.pallas_call(
        _inproj_odd_kernel,
        grid=(n // tm,),
        in_specs=[row(D_MODEL), _const_spec((1, D_MODEL)), _const_spec((D_MODEL, O_END)),
                  _const_spec((LANES, GLA_KW)), _const_spec((1, GLA_KW))],
        out_specs=[row(c) for c, _ in outs],
        out_shape=[jax.ShapeDtypeStruct((n, c), dt) for c, dt in outs],
        compiler_params=_cparams(("parallel",)),
        name="inproj_odd",
    )(x, g, w, wa2, ba)


def _post_kernel(*refs, n_mix, final):
    x_ref = refs[0]
    mix_refs = refs[1:1 + n_mix]
    wo_ref, gf_ref, w1_ref, w2_ref = refs[1 + n_mix:5 + n_mix]
    rest = refs[5 + n_mix:]
    mix = mix_refs[0][...] if n_mix == 1 else jnp.concatenate([r[...] for r in mix_refs], axis=1)
    x1 = x_ref[...] + _dot(mix, wo_ref[...])
    h = _rms(x1, gf_ref[...]).astype(BF16)
    y = x1
    for c in range(D_FF // D_MODEL):
        sl = slice(c * D_MODEL, (c + 1) * D_MODEL)
        t = jnp.maximum(_dot(h, w1_ref[:, sl]), 0.0)
        y = y + _dot((t * t).astype(BF16), w2_ref[sl, :])
    if final:
        gfin_ref, out_ref = rest
        out_ref[...] = _rms(y, gfin_ref[...])
    else:
        (out_ref,) = rest
        out_ref[...] = y


def _post(x, mixes, wo, gf, w1, w2, gfin, tm):
    n = x.shape[0]
    row = lambda c: pl.BlockSpec((tm, c), lambda i: (i, 0))
    final = gfin is not None
    ins = [x, *mixes, wo, gf, w1, w2]
    specs = [row(D_MODEL)] + [row(m.shape[1]) for m in mixes] + [
        _const_spec((D_MODEL, D_MODEL)), _const_spec((1, D_MODEL)),
        _const_spec((D_MODEL, D_FF)), _const_spec((D_FF, D_MODEL))]
    if final:
        ins.append(gfin)
        specs.append(_const_spec((1, D_MODEL)))
    return pl.pallas_call(
        functools.partial(_post_kernel, n_mix=len(mixes), final=final),
        grid=(n // tm,),
        in_specs=specs,
        out_specs=row(D_MODEL),
        out_shape=jax.ShapeDtypeStruct((n, D_MODEL), F32),
        compiler_params=_cparams(("parallel",)),
        name="post_final" if final else "post",
    )(*ins)


F_TERMS = 3


def _fox_placement():
    p = np.zeros((F_TERMS, LANES, FOX_HEADS // 2 * LANES), np.float32)
    for x in range(F_TERMS):
        for h in range(FOX_HEADS):
            p[x, h, (h // 2) * LANES + F_TERMS * (h % 2) + x] = 1.0
    return jnp.asarray(p, BF16)


def _append_f_terms(lf, carry, k, place_ref, ka_ref):
    tc = lf.shape[0]
    f = _mat_f32(_tri(tc, True), _split3(lf)) + carry[...]
    carry[...] = f[tc - 1:tc, :]
    parts = _split3(f * LOG2E)
    cols = _dot(parts[0], place_ref[0]) + _dot(parts[1], place_ref[1]) + _dot(parts[2], place_ref[2])
    for hp in range(FOX_HEADS // 2):
        ka_ref[:, 2 * hp * LANES:(2 * hp + 1) * LANES] = k[:, hp * LANES:(hp + 1) * LANES]
        ka_ref[:, (2 * hp + 1) * LANES:(2 * hp + 2) * LANES] = cols[:, hp * LANES:(hp + 1) * LANES].astype(BF16)


def _fox_prep_kernel(lf_ref, k_ref, place_ref, ka_ref, carry):
    @pl.when(pl.program_id(1) == 0)
    def _():
        carry[...] = jnp.zeros_like(carry)

    _append_f_terms(lf_ref[0], carry, k_ref[...], place_ref, ka_ref)


def _fox_prep(lf_col, k, tc):
    b, t, _ = lf_col.shape
    nt = t // tc
    place = _fox_placement()
    return pl.pallas_call(
        _fox_prep_kernel,
        grid=(b, nt),
        in_specs=[pl.BlockSpec((1, tc, LANES), lambda i, j: (i, j, 0)),
                  pl.BlockSpec((tc, FOX_WIDTH), lambda i, j: (i * nt + j, 0)),
                  _const_spec(place.shape)],
        out_specs=pl.BlockSpec((tc, 2 * FOX_WIDTH), lambda i, j: (i * nt + j, 0)),
        out_shape=jax.ShapeDtypeStruct((b * t, 2 * FOX_WIDTH), BF16),
        scratch_shapes=[pltpu.VMEM((1, LANES), F32)],
        compiler_params=_cparams(("parallel", "arbitrary")),
        name="fox_prep",
    )(lf_col, k, place)


ONES_ROWS = 16


def _fox_attn_kernel(qt_ref, ka_ref, vt_ref, o_ref, sa_scr, sb_scr, acc_scr, *, tq, tkc, past, n_diag):
    q_first = past + pl.program_id(2) * tq
    n_full = q_first // tkc
    drow = lax.broadcasted_iota(jnp.int32, (LANES, 1), 0)
    lane = lax.broadcasted_iota(jnp.int32, (1, 2 * tq), 1)
    lane_head = jnp.where(lane < tq, 0, 1)
    own_head = jnp.where(jnp.where(drow < FOX_HEAD_DIM, 0, 1) == lane_head, 1.0, 0.0).astype(BF16)
    qt = qt_ref[...]
    q2 = jnp.concatenate([qt, qt], axis=1) * own_head
    f_sel = (drow >= F_TERMS * lane_head) & (drow < F_TERMS * lane_head + F_TERMS)
    qa = jnp.concatenate([q2, jnp.where(f_sel, -1.0, 0.0).astype(BF16)], axis=0)
    ones = jnp.ones((ONES_ROWS, tkc), BF16)
    acc_scr[...] = jnp.zeros_like(acc_scr)

    def produce(j, dst):
        start = pl.multiple_of(j * tkc, tkc)
        dst[...] = _dot(ka_ref[pl.ds(start, tkc), :], qa)

    def consume(j, src, m_prev, masked):
        start = pl.multiple_of(j * tkc, tkc)
        st = src[...]
        if masked:
            kpos = start + lax.broadcasted_iota(jnp.int32, (tkc, 1), 0)
            qpos = q_first + (lane & (tq - 1))
            st = jnp.where(kpos <= qpos, st, NEG)
        m_new = jnp.maximum(m_prev, jnp.max(st, axis=0, keepdims=True))
        alpha = jnp.exp2(m_prev - m_new)
        p = jnp.exp2(st - m_new).astype(BF16)
        for hh in range(2):
            rows = slice(hh * FOX_HEAD_DIM, (hh + 1) * FOX_HEAD_DIM)
            cols = slice(hh * tq, (hh + 1) * tq)
            va = jnp.concatenate([vt_ref[rows, pl.ds(start, tkc)], ones], axis=0)
            acc_scr[hh] = alpha[:, cols] * acc_scr[hh] + _dot(va, p[:, cols])
        return m_new

    produce(0, sa_scr)

    def body(i, m):
        produce(2 * i + 1, sb_scr)
        m = consume(2 * i, sa_scr, m, False)
        produce(2 * i + 2, sa_scr)
        return consume(2 * i + 1, sb_scr, m, False)

    m = lax.fori_loop(0, n_full // 2, body, jnp.full((1, 2 * tq), NEG, F32))
    bufs = (sa_scr, sb_scr)
    for d in range(n_diag):
        if d + 1 < n_diag:
            produce(n_full + d + 1, bufs[(d + 1) % 2])
        m = consume(n_full + d, bufs[d % 2], m, True)
    out = jnp.concatenate([acc_scr[hh, 0:FOX_HEAD_DIM] / acc_scr[hh, FOX_HEAD_DIM:FOX_HEAD_DIM + 1]
                           for hh in range(2)], axis=0)
    o_ref[...] = out.T.astype(o_ref.dtype)


def _fox_attention(qt, ka, vt, batch, tq_len, tk_len, past, tq, tkc):
    nq = tq_len // tq
    n_diag = max(1, tq // tkc)
    assert tq & (tq - 1) == 0 and tq % LANES == 0 and (tkc % tq == 0 or tq % tkc == 0) and tq_len % tq == 0
    assert past % (2 * tkc) == 0 and (nq == 1 or tq % (2 * tkc) == 0)
    assert tk_len >= past + max(tq_len, n_diag * tkc) and tk_len % tkc == 0
    pairs = FOX_HEADS // 2
    return pl.pallas_call(
        functools.partial(_fox_attn_kernel, tq=tq, tkc=tkc, past=past, n_diag=n_diag),
        grid=(batch, pairs, nq),
        in_specs=[pl.BlockSpec((LANES, tq), lambda b, h, i: (h, b * nq + i)),
                  pl.BlockSpec((tk_len, 2 * LANES), lambda b, h, i: (b, h)),
                  pl.BlockSpec((LANES, tk_len), lambda b, h, i: (h, b))],
        out_specs=pl.BlockSpec((tq, LANES), lambda b, h, i: (b * nq + i, h)),
        out_shape=jax.ShapeDtypeStruct((batch * tq_len, FOX_WIDTH), BF16),
        scratch_shapes=[pltpu.VMEM((tkc, 2 * tq), F32), pltpu.VMEM((tkc, 2 * tq), F32),
                        pltpu.VMEM((2, FOX_HEAD_DIM + ONES_ROWS, tq), F32)],
        compiler_params=_cparams(("parallel", "parallel", "arbitrary")),
        name="fox_attention",
    )(qt, ka, vt)


def _mlstm_kernel(mqk_ref, mv_ref, mo_ref, gc_ref, gr_ref, c0_ref, n0_ref, m0_ref, buf_ref,
                  cw_ref, cb_ref, gml_ref,
                  h_ref, c_out, n_out, m_out, buf_out,
                  c_scr, n_scr, m_scr, prev_scr, *, L, t_last, nc):
    ci = pl.program_id(1)
    rows = mqk_ref.shape[0]

    @pl.when(ci == 0)
    def _():
        c_scr[...] = c0_ref[0]
        n_scr[...] = n0_ref[0]
        m_scr[...] = m0_ref[0]
        prev_scr[...] = buf_ref[0]

    u = _pad_rows(mqk_ref[...], L)
    ext = jnp.concatenate([prev_scr[...], u], axis=0)
    y = cb_ref[...] + cw_ref[ML_CONV - 1:ML_CONV, :] * u
    for s in range(1, ML_CONV):
        y = y + cw_ref[ML_CONV - 1 - s:ML_CONV - s, :] * pltpu.roll(ext, s, axis=0)[8:8 + L]
    qk = y * _sigmoid(y)
    prev_scr[...] = u[L - 8:L]

    gc = _pad_rows(gc_ref[...], L)
    gr = gr_ref[0]
    b_col = _mat_f32(_tri(L, True), _split3(gc))
    b_row = _f32_mat(_split3(gr), _tri(L, False))
    v_all = _pad_rows(mv_ref[...], L)
    o_all = _pad_rows(mo_ref[...], L)
    t_idx = lax.broadcasted_iota(jnp.int32, (L, 1), 0)
    causal = lax.broadcasted_iota(jnp.int32, (1, L), 1) <= t_idx

    for h in range(ML_HEADS):
        hs = slice(h * ML_HEAD_DIM, (h + 1) * ML_HEAD_DIM)
        q32 = qk[:, hs]
        k32 = qk[:, ML_WIDTH + h * ML_HEAD_DIM:ML_WIDTH + (h + 1) * ML_HEAD_DIM] * (ML_HEAD_DIM ** -0.5)
        qb, kb, vb = q32.astype(BF16), k32.astype(BF16), v_all[:, hs]
        bc = b_col[:, G_MF + h:G_MF + h + 1]
        ic = gc[:, G_MI + h:G_MI + h + 1]
        br = b_row[G_MF + h:G_MF + h + 1, :]
        ir = gr[G_MI + h:G_MI + h + 1, :]
        m_prev = m_scr[h][:, 0:1]
        c_prev = c_scr[h]
        n_prev = n_scr[h]

        dmat = jnp.where(causal, bc - br + ir, NEG)
        inter = bc + m_prev
        m_t = jnp.maximum(inter, jnp.max(dmat, axis=-1, keepdims=True))
        w = jnp.exp(dmat - m_t)
        g = jnp.exp(inter - m_t)
        a = w * _dot_nt(qb, kb)
        num = g * _dot_nt(qb, c_prev.astype(BF16)) + _dot(a.astype(BF16), vb)
        den = g * jnp.sum(q32 * n_prev, axis=-1, keepdims=True) + jnp.sum(a, axis=-1, keepdims=True)
        hh = num / jnp.maximum(jnp.abs(den), jnp.exp(-m_t))

        b_last = bc[t_last:t_last + 1, :]
        m_last = m_t[t_last:t_last + 1, :]
        g_end = g[t_last:t_last + 1, :]
        w_end = jnp.where(t_idx <= t_last, jnp.exp(b_last - bc + ic - m_last), 0.0)
        c_scr[h] = g_end * c_prev + _dot_tn((vb.astype(F32) * w_end).astype(BF16), kb)
        n_scr[h] = g_end * n_prev + jnp.sum(k32 * w_end, axis=0, keepdims=True)
        m_scr[h] = jnp.broadcast_to(m_last, (1, LANES))

        yh = hh * lax.rsqrt(jnp.mean(hh * hh, axis=-1, keepdims=True) + EPS) * gml_ref[:, hs]
        yh = yh * _sigmoid(o_all[:, hs].astype(F32))
        h_ref[:, hs] = yh[:rows].astype(h_ref.dtype)

    @pl.when(ci == nc - 1)
    def _():
        c_out[0] = c_scr[...]
        n_out[0] = n_scr[...]
        m_out[0] = m_scr[...]
        buf_out[0] = ext[t_last + 1:t_last + 9]


def _mlstm(mqk, mv, mo, gc, gr, c0, n0, m0, buf8, cw, cb, gml, batch, t_len):
    L = ML_CHUNK if t_len >= ML_CHUNK else LANES
    rows = min(L, t_len)
    nc = t_len // rows
    t_last = rows - 1
    assert t_len % rows == 0 and (nc == 1 or rows == L) and (t_last + 1) % 8 == 0
    tok = lambda c: pl.BlockSpec((rows, c), lambda b, i: (b * nc + i, 0))
    st = lambda *s: pl.BlockSpec((1,) + s, lambda b, i: (b,) + (0,) * len(s))
    shp_c = (ML_HEADS, ML_HEAD_DIM, ML_HEAD_DIM)
    shp_n = (ML_HEADS, 1, ML_HEAD_DIM)
    return pl.pallas_call(
        functools.partial(_mlstm_kernel, L=L, t_last=t_last, nc=nc),
        grid=(batch, nc),
        in_specs=[tok(2 * ML_WIDTH), tok(ML_WIDTH), tok(ML_WIDTH), tok(LANES),
                  pl.BlockSpec((1, 16, L), lambda b, i: (b, 0, i)),
                  st(*shp_c), st(*shp_n), st(*shp_n), st(8, 2 * ML_WIDTH),
                  _const_spec((ML_CONV, 2 * ML_WIDTH)), _const_spec((1, 2 * ML_WIDTH)), _const_spec((1, ML_WIDTH))],
        out_specs=[tok(ML_WIDTH), st(*shp_c), st(*shp_n), st(*shp_n), st(8, 2 * ML_WIDTH)],
        out_shape=[jax.ShapeDtypeStruct((batch * t_len, ML_WIDTH), BF16),
                   jax.ShapeDtypeStruct((batch,) + shp_c, F32),
                   jax.ShapeDtypeStruct((batch,) + shp_n, F32),
                   jax.ShapeDtypeStruct((batch,) + shp_n, F32),
                   jax.ShapeDtypeStruct((batch, 8, 2 * ML_WIDTH), F32)],
        scratch_shapes=[pltpu.VMEM(shp_c, F32), pltpu.VMEM(shp_n, F32), pltpu.VMEM(shp_n, F32),
                        pltpu.VMEM((8, 2 * ML_WIDTH), F32)],
        compiler_params=_cparams(("parallel", "arbitrary")),
        name="mlstm",
    )(mqk, mv, mo, gc, gr, c0, n0, m0, buf8, cw, cb, gml)


def _gla_level_matrices(L):
    mats = np.zeros((len(GLA_LEVELS), L, L), np.float32)
    for li, b in enumerate(GLA_LEVELS):
        for t in range(L):
            base = (t // (2 * b)) * 2 * b
            bound = base + b - 1
            if t > bound:
                mats[li, t, bound + 1:t + 1] = 1.0
            else:
                mats[li, t, t + 1:bound + 1] = 1.0
    return jnp.asarray(mats, BF16)


def _gla_kernel(q_ref, k_ref, v_ref, gg_ref, la_ref, s0_ref, lvl_ref, gn_ref, o_ref, s_out, s_scr, *, L, t_last, nc):
    ci = pl.program_id(1)
    nb, rows = q_ref.shape[0], q_ref.shape[1]

    @pl.when(ci == 0)
    def _():
        s_scr[...] = s0_ref[...]

    t_idx = lax.broadcasted_iota(jnp.int32, (L, 1), 0)
    s_idx = lax.broadcasted_iota(jnp.int32, (1, L), 1)
    tril = _tri(L, True)

    for bi in range(nb):
        la3 = _split2(_pad_rows(la_ref[bi], L))
        cb = _mat_f32(tril, la3)
        q_all = _pad_rows(q_ref[bi], L)
        k_all = _pad_rows(k_ref[bi], L)
        v_all = _pad_rows(v_ref[bi], L)
        g_all = _pad_rows(gg_ref[bi], L)

        a_heads = []
        q32s, k32s = [], []
        for h in range(GLA_HEADS):
            ks = slice(h * GLA_DK, (h + 1) * GLA_DK)
            q32s.append(q_all[:, ks].astype(F32))
            k32s.append(k_all[:, ks].astype(F32))
            a_heads.append(jnp.where(t_idx == s_idx, _dot_nt(q_all[:, ks], k_all[:, ks]), 0.0))
        for li, b in enumerate(GLA_LEVELS):
            e_all = jnp.exp(_mat_f32(lvl_ref[li], la3))
            sh = b.bit_length() - 1
            second = (jnp.right_shift(t_idx, sh) & 1) == 1
            same_pair = jnp.right_shift(t_idx, sh + 1) == jnp.right_shift(s_idx, sh + 1)
            for h in range(GLA_HEADS):
                ks = slice(h * GLA_DK, (h + 1) * GLA_DK)
                e = e_all[:, ks]
                qe = jnp.where(second, q32s[h] * e, 0.0).astype(BF16)
                ke = jnp.where(second, 0.0, k32s[h] * e).astype(BF16)
                a_heads[h] = a_heads[h] + jnp.where(same_pair, _dot_nt(qe, ke), 0.0)

        for h in range(GLA_HEADS):
            ks = slice(h * GLA_DK, (h + 1) * GLA_DK)
            vs = slice(h * GLA_DV, (h + 1) * GLA_DV)
            cbh = cb[:, ks]
            vb = v_all[:, vs]
            s_t = s_scr[bi, h]
            o = (_dot_nt((q32s[h] * jnp.exp(cbh)).astype(BF16), s_t.astype(BF16))
                 + _dot(a_heads[h].astype(BF16), vb))
            cl = cbh[t_last:t_last + 1, :]
            kd = jnp.where(t_idx <= t_last, k32s[h] * jnp.exp(cl - cbh), 0.0).astype(BF16)
            s_scr[bi, h] = jnp.exp(cl) * s_t + _dot_tn(vb, kd)
            y = o * lax.rsqrt(jnp.mean(o * o, axis=-1, keepdims=True) + EPS) * gn_ref[:, vs]
            gate = g_all[:, vs].astype(F32)
            y = y * (gate * _sigmoid(gate))
            o_ref[bi, :, vs] = y[:rows].astype(o_ref.dtype)

    @pl.when(ci == nc - 1)
    def _():
        s_out[...] = s_scr[...]


GLA_BATCH_PER_STEP = 2


def _gla(q, k, v, gg, la, s0t, gn, batch, t_len):
    L = GLA_CHUNK
    rows = min(L, t_len)
    nc = t_len // rows
    t_last = rows - 1
    nb = GLA_BATCH_PER_STEP
    assert t_len % rows == 0 and (nc == 1 or rows == L) and batch % nb == 0
    tok = lambda c: pl.BlockSpec((nb, rows, c), lambda b, i: (b, i, 0))
    shp_s = (GLA_HEADS, GLA_DV, GLA_DK)
    st = pl.BlockSpec((nb,) + shp_s, lambda b, i: (b, 0, 0, 0))
    levels = _gla_level_matrices(L)
    seq = lambda a: a.reshape(batch, t_len, a.shape[-1])
    o, s_new = pl.pallas_call(
        functools.partial(_gla_kernel, L=L, t_last=t_last, nc=nc),
        grid=(batch // nb, nc),
        in_specs=[tok(GLA_KW), tok(GLA_KW), tok(GLA_VW), tok(GLA_VW), tok(GLA_KW), st,
                  _const_spec(levels.shape), _const_spec((1, GLA_VW))],
        out_specs=[tok(GLA_VW), st],
        out_shape=[jax.ShapeDtypeStruct((batch, t_len, GLA_VW), BF16),
                   jax.ShapeDtypeStruct((batch,) + shp_s, F32)],
        scratch_shapes=[pltpu.VMEM((nb,) + shp_s, F32)],
        compiler_params=_cparams(("parallel", "arbitrary")),
        name="gla",
    )(seq(q), seq(k), seq(v), seq(gg), seq(la), s0t, levels, gn)
    return o.reshape(batch * t_len, GLA_VW), s_new


def _pack_even(w, b_fox_f, b_i, b_f):
    d = w.shape[0]
    o = np.cumsum((0, FOX_WIDTH, FOX_WIDTH, FOX_WIDTH, FOX_HEADS, 2 * ML_WIDTH, ML_WIDTH, ML_WIDTH, ML_HEADS, ML_HEADS))
    n_gate = FOX_HEADS + 2 * ML_HEADS
    wp = jnp.concatenate([w[:, o[0]:o[3]], w[:, o[4]:o[7]], w[:, o[3]:o[4]], w[:, o[7]:o[9]],
                          jnp.zeros((d, LANES - n_gate), w.dtype)], axis=1).astype(BF16)
    bias = jnp.concatenate([b_fox_f, b_i, b_f, jnp.zeros((LANES - n_gate,), F32)]).reshape(1, LANES)
    return wp, bias


def _pack_odd(w, w_a2):
    d = w.shape[0]
    wp = jnp.concatenate([w, jnp.zeros((d, LANES - GLA_RANK), w.dtype)], axis=1).astype(BF16)
    wa2 = jnp.concatenate([w_a2, jnp.zeros((LANES - GLA_RANK, w_a2.shape[1]), w_a2.dtype)], axis=0).astype(BF16)
    return wp, wa2


def _gate_rows(gc, batch, t_len, t_pad):
    g = gc.reshape(batch, t_len, LANES)[:, :, :16].transpose(0, 2, 1)
    if t_pad > t_len:
        g = jnp.pad(g, ((0, 0), (0, 0), (0, t_pad - t_len)))
    return g


def _trunk(x, fox_k, fox_v, fox_lf, ml_c, ml_n, ml_m, ml_buf, gla_s, params):
    (norm_mix, norm_ffn, norm_final, w_even, bias_even, conv_w, conv_b, g_ml,
     w_odd, w_a2, b_a, g_gla, w_out, w_ff1, w_ff2) = params
    batch, t_len, d = x.shape
    past = fox_k.shape[2]
    n = batch * t_len
    tm = 512 if n % 512 == 0 else 256 if n % 256 == 0 else n
    depth = norm_mix.shape[0]
    xf = x.reshape(n, d)
    ev_states, odd_states = [], []
    kv_stacked = None
    y = None
    for layer in range(depth):
        j = layer // 2
        if layer % 2 == 0:
            assert (t_len % tm == 0) == (past == 0)
            qt, kf, vf, kb, vt, mqk, mv, mo, gc = _inproj_even(xf, norm_mix[layer][None], w_even[j], bias_even[j],
                                                               tm, batch, t_len, j, (depth + 1) // 2, kv_stacked)
            tkc = 256
            if past == 0:
                tq, tq_len = min(512, t_len), t_len
                tk_len = t_len
                ka, vt_all = kb, vt
                kv_stacked = (kf, vf)
            else:
                lf_col = gc.reshape(batch, t_len, LANES)
                tq = tq_len = LANES
                tk_len = past + tkc
                extra = tk_len - past - t_len
                k_all = jnp.concatenate(
                    [fox_k[j].reshape(batch, past, FOX_WIDTH).astype(BF16), kb.reshape(batch, t_len, FOX_WIDTH),
                     jnp.zeros((batch, extra, FOX_WIDTH), BF16)], axis=1).reshape(batch * tk_len, FOX_WIDTH)
                vt_all = jnp.concatenate(
                    [fox_v[j].reshape(batch, past, FOX_WIDTH).transpose(2, 0, 1).astype(BF16),
                     vt.reshape(FOX_WIDTH, batch, t_len), jnp.zeros((FOX_WIDTH, batch, extra), BF16)],
                    axis=2).reshape(FOX_WIDTH, batch * tk_len)
                lf_col = jnp.concatenate(
                    [jnp.pad(fox_lf[j], ((0, 0), (0, 0), (0, LANES - FOX_HEADS))), lf_col,
                     jnp.zeros((batch, extra, LANES), F32)], axis=1)
                qt = jnp.pad(qt.reshape(FOX_WIDTH, batch, t_len), ((0, 0), (0, 0), (0, tq_len - t_len)))
                qt = qt.reshape(FOX_WIDTH, batch * tq_len)
                ka = _fox_prep(lf_col, k_all, 512 if tk_len % 512 == 0 else tkc)
            attn = _fox_attention(qt, ka, vt_all, batch, tq_len, tk_len, past, tq, tkc)
            if tq_len != t_len:
                attn = attn.reshape(batch, tq_len, FOX_WIDTH)[:, :t_len].reshape(n, FOX_WIDTH)
            l_ml = ML_CHUNK if t_len >= ML_CHUNK else LANES
            gr = _gate_rows(gc, batch, t_len, max(t_len, l_ml))
            c0 = ml_c[j]
            n0 = ml_n[j][:, :, None, :]
            m0 = jnp.broadcast_to(ml_m[j][:, :, None, None], (batch, ML_HEADS, 1, LANES))
            buf8 = jnp.pad(ml_buf[j], ((0, 0), (8 - (ML_CONV - 1), 0), (0, 0)))
            h_ml, c_new, n_new, m_new, buf_new = _mlstm(mqk, mv, mo, gc, gr, c0, n0, m0, buf8,
                                                        conv_w[j], conv_b[j][None], g_ml[j][None], batch, t_len)
            heads = lambda a: None if past == 0 else a.reshape(batch, t_len, FOX_HEADS, FOX_HEAD_DIM)
            ev_states.append((heads(kf), heads(vf),
                              gc[:, :FOX_HEADS].reshape(batch, t_len, FOX_HEADS),
                              c_new, n_new[:, :, 0, :], m_new[:, :, 0, 0], buf_new[:, 8 - (ML_CONV - 1):, :]))
            mixes = [attn, h_ml]
        else:
            q, k, v, gg, la = _inproj_odd(xf, norm_mix[layer][None], w_odd[j], w_a2[j], b_a[j][None], tm)
            s0t = gla_s[j].transpose(0, 1, 3, 2)
            o, s_new = _gla(q, k, v, gg, la, s0t, g_gla[j][None], batch, t_len)
            odd_states.append(s_new.transpose(0, 1, 3, 2))
            mixes = [o]
        last = layer == depth - 1
        out = _post(xf, mixes, w_out[layer], norm_ffn[layer][None], w_ff1[layer], w_ff2[layer],
                    norm_final[None] if last else None, tm)
        if last:
            y = out
        else:
            xf = out
    ev = [jnp.stack([s[i] for s in ev_states]) for i in range(2 if kv_stacked else 0, 7)]
    if kv_stacked:
        ev = [a.reshape(a.shape[0], batch, FOX_HEADS, FOX_HEAD_DIM, t_len).transpose(0, 1, 4, 2, 3)
              for a in kv_stacked] + ev
    return y.reshape(batch, t_len, d), ev, jnp.stack(odd_states)


def kernel(x_prompt, x_sample, cache_fox_k, cache_fox_v, cache_fox_logf, state_mlstm_c, state_mlstm_n, state_mlstm_m, state_mlstm_conv, state_gla_s, norm_mix, norm_ffn, norm_final, w_in_even, b_fox_f, conv_w_ml, conv_b_ml, b_ml_i, b_ml_f, g_ml, w_in_odd, w_gla_a2, b_gla_a, g_gla, w_out, w_ff1, w_ff2):
    n_even, n_odd = w_in_even.shape[0], w_in_odd.shape[0]
    packed_even = [_pack_even(w_in_even[j], b_fox_f[j], b_ml_i[j], b_ml_f[j]) for j in range(n_even)]
    packed_odd = [_pack_odd(w_in_odd[j], w_gla_a2[j]) for j in range(n_odd)]
    params = (norm_mix, norm_ffn, norm_final,
              [p[0] for p in packed_even], [p[1] for p in packed_even], conv_w_ml, conv_b_ml, g_ml,
              [p[0] for p in packed_odd], [p[1] for p in packed_odd], b_gla_a, g_gla,
              w_out.astype(BF16), w_ff1.astype(BF16), w_ff2.astype(BF16))

    bp = x_prompt.shape[0]
    dt = x_prompt.dtype
    zeros = lambda *s: jnp.zeros(s, dt)
    y_p, ev_p, gla_p = _trunk(
        x_prompt,
        zeros(n_even, bp, 0, FOX_HEADS, FOX_HEAD_DIM), zeros(n_even, bp, 0, FOX_HEADS, FOX_HEAD_DIM),
        zeros(n_even, bp, 0, FOX_HEADS),
        zeros(n_even, bp, ML_HEADS, ML_HEAD_DIM, ML_HEAD_DIM), zeros(n_even, bp, ML_HEADS, ML_HEAD_DIM),
        zeros(n_even, bp, ML_HEADS), zeros(n_even, bp, ML_CONV - 1, 2 * ML_WIDTH),
        zeros(n_odd, bp, GLA_HEADS, GLA_DK, GLA_DV), params)
    y_s, ev_s, gla_s = _trunk(x_sample, cache_fox_k, cache_fox_v, cache_fox_logf, state_mlstm_c, state_mlstm_n,
                              state_mlstm_m, state_mlstm_conv, state_gla_s, params)
    return (y_p, y_s, *ev_p, gla_p, *ev_s, gla_s)
```

```python
import functools

import numpy as np
import jax
import jax.numpy as jnp
from jax import lax
from jax.experimental import pallas as pl
from jax.experimental.pallas import tpu as pltpu

F32 = jnp.float32
BF16 = jnp.bfloat16
EPS = 1e-6
NEG = -1e30
LOG2E = 1.4426950408889634

LANES = 128
VMEM_LIMIT = 56 * 1024 * 1024

D_MODEL = 1024
D_FF = 4 * D_MODEL
FOX_HEADS, FOX_HEAD_DIM = 8, 64
FOX_WIDTH = FOX_HEADS * FOX_HEAD_DIM
ML_HEADS, ML_HEAD_DIM = 4, 128
ML_WIDTH = ML_HEADS * ML_HEAD_DIM
ML_CONV = 4
GLA_HEADS, GLA_DK, GLA_DV = 4, 128, 256
GLA_KW = GLA_HEADS * GLA_DK
GLA_VW = GLA_HEADS * GLA_DV
GLA_RANK = 16
GLA_TAU = 16.0

E_Q, E_K, E_V, E_MQK, E_MV, E_MO, E_G, E_END = 0, 512, 1024, 1536, 2560, 3072, 3584, 3712
G_FOX, G_MI, G_MF = 0, 8, 12
O_Q, O_K, O_V, O_G, O_A, O_END = 0, 512, 1024, 2048, 3072, 3200

ML_CHUNK = 256
GLA_CHUNK = 128
GLA_LEVELS = (64, 32, 16, 8, 4, 2, 1)


def _cparams(sem):
    return pltpu.CompilerParams(dimension_semantics=sem, vmem_limit_bytes=VMEM_LIMIT)


def _const_spec(shape):
    nd = len(shape)
    return pl.BlockSpec(shape, lambda *_: (0,) * nd, pipeline_mode=pl.Buffered(1))


def _rms(x, g):
    return x * lax.rsqrt(jnp.mean(x * x, axis=-1, keepdims=True) + EPS) * g


def _sigmoid(x):
    return 1.0 / (1.0 + jnp.exp(-x))


def _log_sigmoid(x):
    return -(jnp.maximum(-x, 0.0) + jnp.log1p(jnp.exp(-jnp.abs(x))))


def _dot(a, b):
    return jnp.dot(a, b, preferred_element_type=F32)


def _dot_nt(a, b):
    return lax.dot_general(a, b, (((1,), (1,)), ((), ())), preferred_element_type=F32)


def _dot_tn(a, b):
    return lax.dot_general(a, b, (((0,), (0,)), ((), ())), preferred_element_type=F32)


def _split3(x):
    hi = x.astype(BF16)
    r1 = x - hi.astype(F32)
    mid = r1.astype(BF16)
    lo = (r1 - mid.astype(F32)).astype(BF16)
    return hi, mid, lo


def _split2(x):
    hi = x.astype(BF16)
    return hi, (x - hi.astype(F32)).astype(BF16)


def _mat_f32(m, parts):
    return functools.reduce(lambda a, b: a + b, [_dot(m, p) for p in parts])


def _f32_mat(parts, m):
    return functools.reduce(lambda a, b: a + b, [_dot(p, m) for p in parts])


def _tri(n, lower):
    r = lax.broadcasted_iota(jnp.int32, (n, n), 0)
    c = lax.broadcasted_iota(jnp.int32, (n, n), 1)
    keep = (c <= r) if lower else (r <= c)
    return jnp.where(keep, 1.0, 0.0).astype(BF16)


def _pad_rows(a, n):
    if a.shape[0] == n:
        return a
    return jnp.concatenate([a, jnp.zeros((n - a.shape[0], a.shape[1]), a.dtype)], axis=0)


def _even_project(x_ref, g_ref, w_ref, bias_ref, mqk_ref, mv_ref, mo_ref, gc_ref):
    h = _rms(x_ref[...], g_ref[...]).astype(BF16)

    def seg(a, b):
        return _dot(h, w_ref[:, a:b])

    q = seg(E_Q, E_K) * (FOX_HEAD_DIM ** -0.5 * LOG2E)
    k = seg(E_K, E_V)
    v = seg(E_V, E_MQK)
    mqk_ref[...] = seg(E_MQK, E_MV)
    mv_ref[...] = seg(E_MV, E_MO).astype(BF16)
    mo_ref[...] = seg(E_MO, E_G).astype(BF16)
    gz = seg(E_G, E_END) + bias_ref[...]
    lane = lax.broadcasted_iota(jnp.int32, gz.shape, 1)
    is_log = (lane < G_MI) | (lane >= G_MF)
    gates = jnp.where(is_log, _log_sigmoid(gz), gz)
    gc_ref[...] = gates
    return q, k, v, gates


def _inproj_even_rows_kernel(x_ref, g_ref, w_ref, bias_ref,
                             q_ref, kf_ref, vf_ref, mqk_ref, mv_ref, mo_ref, gc_ref):
    q, k, v, _ = _even_project(x_ref, g_ref, w_ref, bias_ref, mqk_ref, mv_ref, mo_ref, gc_ref)
    q_ref[...] = q.astype(BF16)
    kf_ref[...] = k
    vf_ref[...] = v


def _inproj_even_seq_kernel(*refs, tpb, aliased):
    x_ref, g_ref, w_ref, bias_ref = refs[:4]
    refs = refs[4 + (2 if aliased else 0):]
    qt_ref, kf_ref, vf_ref, ka_ref, vt_ref, mqk_ref, mv_ref, mo_ref, gc_ref, carry = refs
    q, k, v, gates = _even_project(x_ref, g_ref, w_ref, bias_ref, mqk_ref, mv_ref, mo_ref, gc_ref)
    qt_ref[...] = q.T.astype(BF16)
    v_t = v.T
    vt_ref[...] = v_t.astype(BF16)
    kf_ref[0, 0] = k.T
    vf_ref[0, 0] = v_t

    @pl.when(pl.program_id(0) % tpb == 0)
    def _():
        carry[...] = jnp.zeros_like(carry)

    _append_f_terms(gates, carry, k.astype(BF16), ka_ref)


def _inproj_even(x, g, w, bias, tm, batch, t_len, slot, n_slots, kv_prev):
    n = x.shape[0]
    row = lambda c: (pl.BlockSpec((tm, c), lambda i: (i, 0)), (n, c))
    col = (pl.BlockSpec((FOX_WIDTH, tm), lambda i: (0, i)), (FOX_WIDTH, n))
    common_in = [row(D_MODEL)[0], _const_spec((1, D_MODEL)), _const_spec((D_MODEL, E_END)), _const_spec((1, LANES))]
    tail = [(row(2 * ML_WIDTH), F32), (row(ML_WIDTH), BF16), (row(ML_WIDTH), BF16), (row(LANES), F32)]
    if t_len % tm != 0:
        outs = [(row(FOX_WIDTH), BF16), (row(FOX_WIDTH), F32), (row(FOX_WIDTH), F32)] + tail
        return pl.pallas_call(
            _inproj_even_rows_kernel,
            grid=(n // tm,),
            in_specs=common_in,
            out_specs=[spec for (spec, _), _ in outs],
            out_shape=[jax.ShapeDtypeStruct(shape, dt) for (_, shape), dt in outs],
            compiler_params=_cparams(("parallel",)),
            name="inproj_even_rows",
        )(x, g, w, bias)
    tpb = t_len // tm
    state = (pl.BlockSpec((1, 1, FOX_WIDTH, tm), lambda i: (slot, i // tpb, 0, i % tpb)),
             (n_slots, batch, FOX_WIDTH, t_len))
    outs = [(col, BF16), (state, F32), (state, F32), (row(2 * FOX_WIDTH), BF16), (col, BF16)] + tail
    ins = [x, g, w, bias]
    in_specs = list(common_in)
    aliases = {}
    if kv_prev is not None:
        ins += list(kv_prev)
        in_specs += [pl.BlockSpec(memory_space=pl.ANY)] * 2
        aliases = {4: 1, 5: 2}
    return pl.pallas_call(
        functools.partial(_inproj_even_seq_kernel, tpb=tpb, aliased=kv_prev is not None),
        grid=(n // tm,),
        in_specs=in_specs,
        out_specs=[spec for (spec, _), _ in outs],
        out_shape=[jax.ShapeDtypeStruct(shape, dt) for (_, shape), dt in outs],
        scratch_shapes=[pltpu.VMEM((1, LANES), F32)],
        input_output_aliases=aliases,
        compiler_params=_cparams(("arbitrary",)),
        name="inproj_even_seq",
    )(*ins)


def _inproj_odd_kernel(x_ref, g_ref, w_ref, wa2_ref, ba_ref, q_ref, k_ref, v_ref, gg_ref, la_ref):
    h = _rms(x_ref[...], g_ref[...]).astype(BF16)

    def seg(a, b):
        return _dot(h, w_ref[:, a:b])

    q_ref[...] = (seg(O_Q, O_K) * (GLA_DK ** -0.5)).astype(BF16)
    k_ref[...] = seg(O_K, O_V).astype(BF16)
    v_ref[...] = seg(O_V, O_G).astype(BF16)
    gg_ref[...] = seg(O_G, O_A).astype(BF16)
    ga = seg(O_A, O_END).astype(BF16)
    la_ref[...] = _log_sigmoid(_dot(ga, wa2_ref[...]) + ba_ref[...]) * (1.0 / GLA_TAU)


def _inproj_odd(x, g, w, wa2, ba, tm):
    n = x.shape[0]
    row = lambda c: pl.BlockSpec((tm, c), lambda i: (i, 0))
    outs = [(GLA_KW, BF16), (GLA_KW, BF16), (GLA_VW, BF16), (GLA_VW, BF16), (GLA_KW, F32)]
    return pl.pallas_call(
        _inproj_odd_kernel,
        grid=(n // tm,),
        in_specs=[row(D_MODEL), _const_spec((1, D_MODEL)), _const_spec((D_MODEL, O_END)),
                  _const_spec((LANES, GLA_KW)), _const_spec((1, GLA_KW))],
        out_specs=[row(c) for c, _ in outs],
        out_shape=[jax.ShapeDtypeStruct((n, c), dt) for c, dt in outs],
        compiler_params=_cparams(("parallel",)),
        name="inproj_odd",
    )(x, g, w, wa2, ba)


def _post_kernel(*refs, n_mix, final):
    x_ref = refs[0]
    mix_refs = refs[1:1 + n_mix]
    wo_ref, gf_ref, w1_ref, w2_ref = refs[1 + n_mix:5 + n_mix]
    rest = refs[5 + n_mix:]
    mix = mix_refs[0][...] if n_mix == 1 else jnp.concatenate([r[...] for r in mix_refs], axis=1)
    x1 = x_ref[...] + _dot(mix, wo_ref[...])
    h = _rms(x1, gf_ref[...]).astype(BF16)
    y = x1
    for c in range(D_FF // D_MODEL):
        sl = slice(c * D_MODEL, (c + 1) * D_MODEL)
        t = jnp.maximum(_dot(h, w1_ref[:, sl]), 0.0)
        y = y + _dot((t * t).astype(BF16), w2_ref[sl, :])
    if final:
        gfin_ref, out_ref = rest
        out_ref[...] = _rms(y, gfin_ref[...])
    else:
        (out_ref,) = rest
        out_ref[...] = y


def _post(x, mixes, wo, gf, w1, w2, gfin, tm):
    n = x.shape[0]
    row = lambda c: pl.BlockSpec((tm, c), lambda i: (i, 0))
    final = gfin is not None
    ins = [x, *mixes, wo, gf, w1, w2]
    specs = [row(D_MODEL)] + [row(m.shape[1]) for m in mixes] + [
        _const_spec((D_MODEL, D_MODEL)), _const_spec((1, D_MODEL)),
        _const_spec((D_MODEL, D_FF)), _const_spec((D_FF, D_MODEL))]
    if final:
        ins.append(gfin)
        specs.append(_const_spec((1, D_MODEL)))
    return pl.pallas_call(
        functools.partial(_post_kernel, n_mix=len(mixes), final=final),
        grid=(n // tm,),
        in_specs=specs,
        out_specs=row(D_MODEL),
        out_shape=jax.ShapeDtypeStruct((n, D_MODEL), F32),
        compiler_params=_cparams(("parallel",)),
        name="post_final" if final else "post",
    )(*ins)


F_TERMS = 3


def _append_f_terms(lf, carry, k, ka_ref):
    tc = lf.shape[0]
    f = _mat_f32(_tri(tc, True), _split3(lf)) + carry[...]
    carry[...] = f[tc - 1:tc, :]
    hi, mid, lo = [p.astype(F32) for p in _split3(f * LOG2E)]
    lane = lax.broadcasted_iota(jnp.int32, (1, LANES), 1)
    cols = jnp.where(lane < FOX_HEADS, hi,
                     jnp.where(lane < 2 * FOX_HEADS, pltpu.roll(mid, FOX_HEADS, axis=1),
                               jnp.where(lane < 3 * FOX_HEADS, pltpu.roll(lo, 2 * FOX_HEADS, axis=1), 0.0)))
    cols = cols.astype(BF16)
    for hp in range(FOX_HEADS // 2):
        ka_ref[:, 2 * hp * LANES:(2 * hp + 1) * LANES] = k[:, hp * LANES:(hp + 1) * LANES]
        ka_ref[:, (2 * hp + 1) * LANES:(2 * hp + 2) * LANES] = cols


def _fox_decode_kernel(q_ref, kc_ref, vc_ref, kn_ref, vn_ref, lf_ref, o_ref, *, t_new, past):
    pad = LANES - t_new
    nkeys = past + LANES
    lane_f = lax.broadcasted_iota(jnp.int32, (1, FOX_WIDTH), 1)
    head_mask = [jnp.where((lane_f >= h * FOX_HEAD_DIM) & (lane_f < (h + 1) * FOX_HEAD_DIM), 1.0, 0.0)
                 for h in range(FOX_HEADS)]
    q = q_ref[0].astype(F32)
    qb = jnp.concatenate([q * hm for hm in head_mask], axis=0).astype(BF16)

    def keys(cache_ref, new_ref):
        return jnp.concatenate([cache_ref[0].astype(BF16), new_ref[0].astype(BF16),
                                jnp.zeros((pad, FOX_WIDTH), BF16)], axis=0)

    s = _dot_nt(qb, keys(kc_ref, kn_ref))
    lf = lf_ref[0]
    triu = _tri(LANES, False)
    carry = jnp.zeros((lf.shape[0], 1), F32)
    blocks = []
    for c in range(nkeys // LANES):
        cs = _f32_mat(_split3(lf[:, c * LANES:(c + 1) * LANES]), triu) + carry
        carry = cs[:, LANES - 1:LANES]
        blocks.append(cs)
    f_all = jnp.concatenate(blocks, axis=1) * LOG2E
    s = s - jnp.concatenate([jnp.broadcast_to(f_all[h:h + 1, :], (t_new, nkeys)) for h in range(FOX_HEADS)], axis=0)
    kpos = lax.broadcasted_iota(jnp.int32, (1, nkeys), 1)
    qpos = past + (lax.broadcasted_iota(jnp.int32, (FOX_HEADS * t_new, 1), 0) & (t_new - 1))
    s = jnp.where(kpos <= qpos, s, NEG)
    p = jnp.exp2(s - jnp.max(s, axis=-1, keepdims=True))
    ob = _dot(p.astype(BF16), keys(vc_ref, vn_ref)) / jnp.sum(p, axis=-1, keepdims=True)
    out = ob[0:t_new] * head_mask[0]
    for h in range(1, FOX_HEADS):
        out = out + ob[h * t_new:(h + 1) * t_new] * head_mask[h]
    o_ref[0] = out.astype(o_ref.dtype)


def _fox_decode(q, k_cache, v_cache, k_new, v_new, lf_row):
    batch, t_new, _ = q.shape
    past = k_cache.shape[1]
    assert t_new & (t_new - 1) == 0 and t_new <= LANES and t_new % 16 == 0 and past % LANES == 0
    blk = lambda a: pl.BlockSpec((1,) + a.shape[1:], lambda b: (b, 0, 0))
    args = (q, k_cache, v_cache, k_new, v_new, lf_row)
    return pl.pallas_call(
        functools.partial(_fox_decode_kernel, t_new=t_new, past=past),
        grid=(batch,),
        in_specs=[blk(a) for a in args],
        out_specs=blk(q),
        out_shape=jax.ShapeDtypeStruct(q.shape, BF16),
        compiler_params=_cparams(("parallel",)),
        name="fox_decode",
    )(*args)


ONES_ROWS = 16


def _fox_attn_kernel(qt_ref, ka_ref, vt_ref, o_ref, sa_scr, sb_scr, acc_scr, *, tq, tkc, past, n_diag):
    q_first = past + pl.program_id(2) * tq
    n_full = q_first // tkc
    drow = lax.broadcasted_iota(jnp.int32, (LANES, 1), 0)
    lane = lax.broadcasted_iota(jnp.int32, (1, 2 * tq), 1)
    lane_head = jnp.where(lane < tq, 0, 1)
    own_head = jnp.where(jnp.where(drow < FOX_HEAD_DIM, 0, 1) == lane_head, 1.0, 0.0).astype(BF16)
    qt = qt_ref[...]
    q2 = jnp.concatenate([qt, qt], axis=1) * own_head
    head = 2 * pl.program_id(1) + lane_head
    f_sel = ((drow & (FOX_HEADS - 1)) == head) & (drow < F_TERMS * FOX_HEADS)
    qa = jnp.concatenate([q2, jnp.where(f_sel, -1.0, 0.0).astype(BF16)], axis=0)
    ones = jnp.ones((ONES_ROWS, tkc), BF16)
    acc_scr[...] = jnp.zeros_like(acc_scr)

    def produce(j, dst):
        start = pl.multiple_of(j * tkc, tkc)
        dst[...] = _dot(ka_ref[pl.ds(start, tkc), :], qa)

    def consume(j, src, m_prev, masked):
        start = pl.multiple_of(j * tkc, tkc)
        st = src[...]
        if masked:
            kpos = start + lax.broadcasted_iota(jnp.int32, (tkc, 1), 0)
            qpos = q_first + (lane & (tq - 1))
            st = jnp.where(kpos <= qpos, st, NEG)
        m_new = jnp.maximum(m_prev, jnp.max(st, axis=0, keepdims=True))
        alpha = jnp.exp2(m_prev - m_new)
        p = jnp.exp2(st - m_new).astype(BF16)
        for hh in range(2):
            rows = slice(hh * FOX_HEAD_DIM, (hh + 1) * FOX_HEAD_DIM)
            cols = slice(hh * tq, (hh + 1) * tq)
            va = jnp.concatenate([vt_ref[rows, pl.ds(start, tkc)], ones], axis=0)
            acc_scr[hh] = alpha[:, cols] * acc_scr[hh] + _dot(va, p[:, cols])
        return m_new

    produce(0, sa_scr)

    def body(i, m):
        produce(2 * i + 1, sb_scr)
        m = consume(2 * i, sa_scr, m, False)
        produce(2 * i + 2, sa_scr)
        return consume(2 * i + 1, sb_scr, m, False)

    m = lax.fori_loop(0, n_full // 2, body, jnp.full((1, 2 * tq), NEG, F32))
    bufs = (sa_scr, sb_scr)
    for d in range(n_diag):
        if d + 1 < n_diag:
            produce(n_full + d + 1, bufs[(d + 1) % 2])
        m = consume(n_full + d, bufs[d % 2], m, True)
    out = jnp.concatenate([acc_scr[hh, 0:FOX_HEAD_DIM] / acc_scr[hh, FOX_HEAD_DIM:FOX_HEAD_DIM + 1]
                           for hh in range(2)], axis=0)
    o_ref[...] = out.T.astype(o_ref.dtype)


def _fox_attention(qt, ka, vt, batch, tq_len, tk_len, past, tq, tkc):
    nq = tq_len // tq
    n_diag = max(1, tq // tkc)
    assert tq & (tq - 1) == 0 and tq % LANES == 0 and (tkc % tq == 0 or tq % tkc == 0) and tq_len % tq == 0
    assert past % (2 * tkc) == 0 and (nq == 1 or tq % (2 * tkc) == 0)
    assert tk_len >= past + max(tq_len, n_diag * tkc) and tk_len % tkc == 0
    pairs = FOX_HEADS // 2
    return pl.pallas_call(
        functools.partial(_fox_attn_kernel, tq=tq, tkc=tkc, past=past, n_diag=n_diag),
        grid=(batch, pairs, nq),
        in_specs=[pl.BlockSpec((LANES, tq), lambda b, h, i: (h, b * nq + i)),
                  pl.BlockSpec((tk_len, 2 * LANES), lambda b, h, i: (b, h)),
                  pl.BlockSpec((LANES, tk_len), lambda b, h, i: (h, b))],
        out_specs=pl.BlockSpec((tq, LANES), lambda b, h, i: (b * nq + i, h)),
        out_shape=jax.ShapeDtypeStruct((batch * tq_len, FOX_WIDTH), BF16),
        scratch_shapes=[pltpu.VMEM((tkc, 2 * tq), F32), pltpu.VMEM((tkc, 2 * tq), F32),
                        pltpu.VMEM((2, FOX_HEAD_DIM + ONES_ROWS, tq), F32)],
        compiler_params=_cparams(("parallel", "parallel", "arbitrary")),
        name="fox_attention",
    )(qt, ka, vt)


def _mlstm_kernel(mqk_ref, mv_ref, mo_ref, gc_ref, gr_ref, c0_ref, n0_ref, m0_ref, buf_ref,
                  cw_ref, cb_ref, gml_ref,
                  h_ref, c_out, n_out, m_out, buf_out,
                  c_scr, n_scr, m_scr, prev_scr, *, L, t_last, nc):
    ci = pl.program_id(1)
    rows = mqk_ref.shape[0]

    @pl.when(ci == 0)
    def _():
        c_scr[...] = c0_ref[0]
        n_scr[...] = n0_ref[0]
        m_scr[...] = m0_ref[0]
        prev_scr[...] = buf_ref[0]

    u = _pad_rows(mqk_ref[...], L)
    ext = jnp.concatenate([prev_scr[...], u], axis=0)
    y = cb_ref[...] + cw_ref[ML_CONV - 1:ML_CONV, :] * u
    for s in range(1, ML_CONV):
        y = y + cw_ref[ML_CONV - 1 - s:ML_CONV - s, :] * pltpu.roll(ext, s, axis=0)[8:8 + L]
    qk = y * _sigmoid(y)
    prev_scr[...] = u[L - 8:L]

    gc = _pad_rows(gc_ref[...], L)
    gr = gr_ref[0]
    b_col = _mat_f32(_tri(L, True), _split3(gc))
    b_row = _f32_mat(_split3(gr), _tri(L, False))
    v_all = _pad_rows(mv_ref[...], L)
    o_all = _pad_rows(mo_ref[...], L)
    t_idx = lax.broadcasted_iota(jnp.int32, (L, 1), 0)
    causal = lax.broadcasted_iota(jnp.int32, (1, L), 1) <= t_idx

    for h in range(ML_HEADS):
        hs = slice(h * ML_HEAD_DIM, (h + 1) * ML_HEAD_DIM)
        q32 = qk[:, hs]
        k32 = qk[:, ML_WIDTH + h * ML_HEAD_DIM:ML_WIDTH + (h + 1) * ML_HEAD_DIM] * (ML_HEAD_DIM ** -0.5)
        qb, kb, vb = q32.astype(BF16), k32.astype(BF16), v_all[:, hs]
        bc = b_col[:, G_MF + h:G_MF + h + 1]
        ic = gc[:, G_MI + h:G_MI + h + 1]
        br = b_row[G_MF + h:G_MF + h + 1, :]
        ir = gr[G_MI + h:G_MI + h + 1, :]
        m_prev = m_scr[h][:, 0:1]
        c_prev = c_scr[h]
        n_prev = n_scr[h]

        dmat = jnp.where(causal, bc - br + ir, NEG)
        inter = bc + m_prev
        m_t = jnp.maximum(inter, jnp.max(dmat, axis=-1, keepdims=True))
        w = jnp.exp(dmat - m_t)
        g = jnp.exp(inter - m_t)
        a = w * _dot_nt(qb, kb)
        num = g * _dot_nt(qb, c_prev.astype(BF16)) + _dot(a.astype(BF16), vb)
        den = g * jnp.sum(q32 * n_prev, axis=-1, keepdims=True) + jnp.sum(a, axis=-1, keepdims=True)
        hh = num / jnp.maximum(jnp.abs(den), jnp.exp(-m_t))

        b_last = bc[t_last:t_last + 1, :]
        m_last = m_t[t_last:t_last + 1, :]
        g_end = g[t_last:t_last + 1, :]
        w_end = jnp.where(t_idx <= t_last, jnp.exp(b_last - bc + ic - m_last), 0.0)
        c_scr[h] = g_end * c_prev + _dot_tn((vb.astype(F32) * w_end).astype(BF16), kb)
        n_scr[h] = g_end * n_prev + jnp.sum(k32 * w_end, axis=0, keepdims=True)
        m_scr[h] = jnp.broadcast_to(m_last, (1, LANES))

        yh = hh * lax.rsqrt(jnp.mean(hh * hh, axis=-1, keepdims=True) + EPS) * gml_ref[:, hs]
        yh = yh * _sigmoid(o_all[:, hs].astype(F32))
        h_ref[:, hs] = yh[:rows].astype(h_ref.dtype)

    @pl.when(ci == nc - 1)
    def _():
        c_out[0] = c_scr[...]
        n_out[0] = n_scr[...]
        m_out[0] = m_scr[...]
        buf_out[0] = ext[t_last + 1:t_last + 9]


def _mlstm(mqk, mv, mo, gc, gr, c0, n0, m0, buf8, cw, cb, gml, batch, t_len):
    L = ML_CHUNK if t_len >= ML_CHUNK else LANES
    rows = min(L, t_len)
    nc = t_len // rows
    t_last = rows - 1
    assert t_len % rows == 0 and (nc == 1 or rows == L) and (t_last + 1) % 8 == 0
    tok = lambda c: pl.BlockSpec((rows, c), lambda b, i: (b * nc + i, 0))
    st = lambda *s: pl.BlockSpec((1,) + s, lambda b, i: (b,) + (0,) * len(s))
    shp_c = (ML_HEADS, ML_HEAD_DIM, ML_HEAD_DIM)
    shp_n = (ML_HEADS, 1, ML_HEAD_DIM)
    return pl.pallas_call(
        functools.partial(_mlstm_kernel, L=L, t_last=t_last, nc=nc),
        grid=(batch, nc),
        in_specs=[tok(2 * ML_WIDTH), tok(ML_WIDTH), tok(ML_WIDTH), tok(LANES),
                  pl.BlockSpec((1, 16, L), lambda b, i: (b, 0, i)),
                  st(*shp_c), st(*shp_n), st(*shp_n), st(8, 2 * ML_WIDTH),
                  _const_spec((ML_CONV, 2 * ML_WIDTH)), _const_spec((1, 2 * ML_WIDTH)), _const_spec((1, ML_WIDTH))],
        out_specs=[tok(ML_WIDTH), st(*shp_c), st(*shp_n), st(*shp_n), st(8, 2 * ML_WIDTH)],
        out_shape=[jax.ShapeDtypeStruct((batch * t_len, ML_WIDTH), BF16),
                   jax.ShapeDtypeStruct((batch,) + shp_c, F32),
                   jax.ShapeDtypeStruct((batch,) + shp_n, F32),
                   jax.ShapeDtypeStruct((batch,) + shp_n, F32),
                   jax.ShapeDtypeStruct((batch, 8, 2 * ML_WIDTH), F32)],
        scratch_shapes=[pltpu.VMEM(shp_c, F32), pltpu.VMEM(shp_n, F32), pltpu.VMEM(shp_n, F32),
                        pltpu.VMEM((8, 2 * ML_WIDTH), F32)],
        compiler_params=_cparams(("parallel", "arbitrary")),
        name="mlstm",
    )(mqk, mv, mo, gc, gr, c0, n0, m0, buf8, cw, cb, gml)


def _gla_level_matrices(L):
    mats = np.zeros((len(GLA_LEVELS), L, L), np.float32)
    for li, b in enumerate(GLA_LEVELS):
        for t in range(L):
            base = (t // (2 * b)) * 2 * b
            bound = base + b - 1
            if t > bound:
                mats[li, t, bound + 1:t + 1] = 1.0
            else:
                mats[li, t, t + 1:bound + 1] = 1.0
    return jnp.asarray(mats, BF16)


def _gla_kernel(q_ref, k_ref, v_ref, gg_ref, la_ref, s0_ref, lvl_ref, gn_ref, o_ref, s_out, s_scr, *, L, t_last, nc):
    ci = pl.program_id(1)
    nb, rows = q_ref.shape[0], q_ref.shape[1]

    @pl.when(ci == 0)
    def _():
        s_scr[...] = s0_ref[...]

    t_idx = lax.broadcasted_iota(jnp.int32, (L, 1), 0)
    s_idx = lax.broadcasted_iota(jnp.int32, (1, L), 1)
    tril = _tri(L, True)

    for bi in range(nb):
        la3 = _split2(_pad_rows(la_ref[bi], L))
        cb = _mat_f32(tril, la3)
        q_all = _pad_rows(q_ref[bi], L)
        k_all = _pad_rows(k_ref[bi], L)
        v_all = _pad_rows(v_ref[bi], L)
        g_all = _pad_rows(gg_ref[bi], L)

        a_heads = []
        q32s, k32s = [], []
        for h in range(GLA_HEADS):
            ks = slice(h * GLA_DK, (h + 1) * GLA_DK)
            q32s.append(q_all[:, ks].astype(F32))
            k32s.append(k_all[:, ks].astype(F32))
            a_heads.append(jnp.where(t_idx == s_idx, _dot_nt(q_all[:, ks], k_all[:, ks]), 0.0))
        for li, b in enumerate(GLA_LEVELS):
            e_all = jnp.exp(_mat_f32(lvl_ref[li], la3))
            sh = b.bit_length() - 1
            second = (jnp.right_shift(t_idx, sh) & 1) == 1
            same_pair = jnp.right_shift(t_idx, sh + 1) == jnp.right_shift(s_idx, sh + 1)
            for h in range(GLA_HEADS):
                ks = slice(h * GLA_DK, (h + 1) * GLA_DK)
                e = e_all[:, ks]
                qe = jnp.where(second, q32s[h] * e, 0.0).astype(BF16)
                ke = jnp.where(second, 0.0, k32s[h] * e).astype(BF16)
                a_heads[h] = a_heads[h] + jnp.where(same_pair, _dot_nt(qe, ke), 0.0)

        for h in range(GLA_HEADS):
            ks = slice(h * GLA_DK, (h + 1) * GLA_DK)
            vs = slice(h * GLA_DV, (h + 1) * GLA_DV)
            cbh = cb[:, ks]
            vb = v_all[:, vs]
            s_t = s_scr[bi, h]
            o = (_dot_nt((q32s[h] * jnp.exp(cbh)).astype(BF16), s_t.astype(BF16))
                 + _dot(a_heads[h].astype(BF16), vb))
            cl = cbh[t_last:t_last + 1, :]
            kd = jnp.where(t_idx <= t_last, k32s[h] * jnp.exp(cl - cbh), 0.0).astype(BF16)
            s_scr[bi, h] = jnp.exp(cl) * s_t + _dot_tn(vb, kd)
            y = o * lax.rsqrt(jnp.mean(o * o, axis=-1, keepdims=True) + EPS) * gn_ref[:, vs]
            gate = g_all[:, vs].astype(F32)
            y = y * (gate * _sigmoid(gate))
            o_ref[bi, :, vs] = y[:rows].astype(o_ref.dtype)

    @pl.when(ci == nc - 1)
    def _():
        s_out[...] = s_scr[...]


GLA_BATCH_PER_STEP = 2


def _gla(q, k, v, gg, la, s0t, gn, batch, t_len):
    L = GLA_CHUNK
    rows = min(L, t_len)
    nc = t_len // rows
    t_last = rows - 1
    nb = GLA_BATCH_PER_STEP
    assert t_len % rows == 0 and (nc == 1 or rows == L) and batch % nb == 0
    tok = lambda c: pl.BlockSpec((nb, rows, c), lambda b, i: (b, i, 0))
    shp_s = (GLA_HEADS, GLA_DV, GLA_DK)
    st = pl.BlockSpec((nb,) + shp_s, lambda b, i: (b, 0, 0, 0))
    levels = _gla_level_matrices(L)
    seq = lambda a: a.reshape(batch, t_len, a.shape[-1])
    o, s_new = pl.pallas_call(
        functools.partial(_gla_kernel, L=L, t_last=t_last, nc=nc),
        grid=(batch // nb, nc),
        in_specs=[tok(GLA_KW), tok(GLA_KW), tok(GLA_VW), tok(GLA_VW), tok(GLA_KW), st,
                  _const_spec(levels.shape), _const_spec((1, GLA_VW))],
        out_specs=[tok(GLA_VW), st],
        out_shape=[jax.ShapeDtypeStruct((batch, t_len, GLA_VW), BF16),
                   jax.ShapeDtypeStruct((batch,) + shp_s, F32)],
        scratch_shapes=[pltpu.VMEM((nb,) + shp_s, F32)],
        compiler_params=_cparams(("parallel", "arbitrary")),
        name="gla",
    )(seq(q), seq(k), seq(v), seq(gg), seq(la), s0t, levels, gn)
    return o.reshape(batch * t_len, GLA_VW), s_new


def _pack_even(w, b_fox_f, b_i, b_f):
    d = w.shape[0]
    o = np.cumsum((0, FOX_WIDTH, FOX_WIDTH, FOX_WIDTH, FOX_HEADS, 2 * ML_WIDTH, ML_WIDTH, ML_WIDTH, ML_HEADS, ML_HEADS))
    n_gate = FOX_HEADS + 2 * ML_HEADS
    wp = jnp.concatenate([w[:, o[0]:o[3]], w[:, o[4]:o[7]], w[:, o[3]:o[4]], w[:, o[7]:o[9]],
                          jnp.zeros((d, LANES - n_gate), w.dtype)], axis=1).astype(BF16)
    bias = jnp.concatenate([b_fox_f, b_i, b_f, jnp.zeros((LANES - n_gate,), F32)]).reshape(1, LANES)
    return wp, bias


def _pack_odd(w, w_a2):
    d = w.shape[0]
    wp = jnp.concatenate([w, jnp.zeros((d, LANES - GLA_RANK), w.dtype)], axis=1).astype(BF16)
    wa2 = jnp.concatenate([w_a2, jnp.zeros((LANES - GLA_RANK, w_a2.shape[1]), w_a2.dtype)], axis=0).astype(BF16)
    return wp, wa2


def _gate_rows(gc, batch, t_len, t_pad):
    g = gc.reshape(batch, t_len, LANES)[:, :, :16].transpose(0, 2, 1)
    if t_pad > t_len:
        g = jnp.pad(g, ((0, 0), (0, 0), (0, t_pad - t_len)))
    return g


def _trunk(x, fox_k, fox_v, fox_lf, ml_c, ml_n, ml_m, ml_buf, gla_s, params):
    (norm_mix, norm_ffn, norm_final, w_even, bias_even, conv_w, conv_b, g_ml,
     w_odd, w_a2, b_a, g_gla, w_out, w_ff1, w_ff2) = params
    batch, t_len, d = x.shape
    past = fox_k.shape[2]
    n = batch * t_len
    tm = 512 if n % 512 == 0 else 256 if n % 256 == 0 else n
    depth = norm_mix.shape[0]
    xf = x.reshape(n, d)
    ev_states, odd_states = [], []
    kv_stacked = None
    y = None
    for layer in range(depth):
        j = layer // 2
        if layer % 2 == 0:
            assert (t_len % tm == 0) == (past == 0)
            proj = _inproj_even(xf, norm_mix[layer][None], w_even[j], bias_even[j],
                                tm, batch, t_len, j, (depth + 1) // 2, kv_stacked)
            if past == 0:
                qt, kf, vf, ka, vt, mqk, mv, mo, gc = proj
                kv_stacked = (kf, vf)
                attn = _fox_attention(qt, ka, vt, batch, t_len, t_len, 0, min(512, t_len), 256)
            else:
                q, kf, vf, mqk, mv, mo, gc = proj
                seq = lambda a: a.reshape(batch, -1, FOX_WIDTH)
                lf_new = gc.reshape(batch, t_len, LANES)[:, :, :FOX_HEADS]
                lf_row = jnp.concatenate([fox_lf[j], lf_new, jnp.zeros((batch, LANES - t_len, FOX_HEADS), F32)], axis=1)
                lf_row = jnp.pad(lf_row.transpose(0, 2, 1), ((0, 0), (0, 16 - FOX_HEADS), (0, 0)))
                attn = _fox_decode(seq(q), seq(fox_k[j]), seq(fox_v[j]), seq(kf), seq(vf), lf_row).reshape(n, FOX_WIDTH)
            l_ml = ML_CHUNK if t_len >= ML_CHUNK else LANES
            gr = _gate_rows(gc, batch, t_len, max(t_len, l_ml))
            c0 = ml_c[j]
            n0 = ml_n[j][:, :, None, :]
            m0 = jnp.broadcast_to(ml_m[j][:, :, None, None], (batch, ML_HEADS, 1, LANES))
            buf8 = jnp.pad(ml_buf[j], ((0, 0), (8 - (ML_CONV - 1), 0), (0, 0)))
            h_ml, c_new, n_new, m_new, buf_new = _mlstm(mqk, mv, mo, gc, gr, c0, n0, m0, buf8,
                                                        conv_w[j], conv_b[j][None], g_ml[j][None], batch, t_len)
            heads = lambda a: None if past == 0 else a.reshape(batch, t_len, FOX_HEADS, FOX_HEAD_DIM)
            ev_states.append((heads(kf), heads(vf),
                              gc[:, :FOX_HEADS].reshape(batch, t_len, FOX_HEADS),
                              c_new, n_new[:, :, 0, :], m_new[:, :, 0, 0], buf_new[:, 8 - (ML_CONV - 1):, :]))
            mixes = [attn, h_ml]
        else:
            q, k, v, gg, la = _inproj_odd(xf, norm_mix[layer][None], w_odd[j], w_a2[j], b_a[j][None], tm)
            s0t = gla_s[j].transpose(0, 1, 3, 2)
            o, s_new = _gla(q, k, v, gg, la, s0t, g_gla[j][None], batch, t_len)
            odd_states.append(s_new.transpose(0, 1, 3, 2))
            mixes = [o]
        last = layer == depth - 1
        out = _post(xf, mixes, w_out[layer], norm_ffn[layer][None], w_ff1[layer], w_ff2[layer],
                    norm_final[None] if last else None, tm)
        if last:
            y = out
        else:
            xf = out
    ev = [jnp.stack([s[i] for s in ev_states]) for i in range(2 if kv_stacked else 0, 7)]
    if kv_stacked:
        ev = [a.reshape(a.shape[0], batch, FOX_HEADS, FOX_HEAD_DIM, t_len).transpose(0, 1, 4, 2, 3)
              for a in kv_stacked] + ev
    return y.reshape(batch, t_len, d), ev, jnp.stack(odd_states)


def kernel(x_prompt, x_sample, cache_fox_k, cache_fox_v, cache_fox_logf, state_mlstm_c, state_mlstm_n, state_mlstm_m, state_mlstm_conv, state_gla_s, norm_mix, norm_ffn, norm_final, w_in_even, b_fox_f, conv_w_ml, conv_b_ml, b_ml_i, b_ml_f, g_ml, w_in_odd, w_gla_a2, b_gla_a, g_gla, w_out, w_ff1, w_ff2):
    n_even, n_odd = w_in_even.shape[0], w_in_odd.shape[0]
    packed_even = [_pack_even(w_in_even[j], b_fox_f[j], b_ml_i[j], b_ml_f[j]) for j in range(n_even)]
    packed_odd = [_pack_odd(w_in_odd[j], w_gla_a2[j]) for j in range(n_odd)]
    params = (norm_mix, norm_ffn, norm_final,
              [p[0] for p in packed_even], [p[1] for p in packed_even], conv_w_ml, conv_b_ml, g_ml,
              [p[0] for p in packed_odd], [p[1] for p in packed_odd], b_gla_a, g_gla,
              w_out.astype(BF16), w_ff1.astype(BF16), w_ff2.astype(BF16))

    bp = x_prompt.shape[0]
    dt = x_prompt.dtype
    zeros = lambda *s: jnp.zeros(s, dt)
    y_p, ev_p, gla_p = _trunk(
        x_prompt,
        zeros(n_even, bp, 0, FOX_HEADS, FOX_HEAD_DIM), zeros(n_even, bp, 0, FOX_HEADS, FOX_HEAD_DIM),
        zeros(n_even, bp, 0, FOX_HEADS),
        zeros(n_even, bp, ML_HEADS, ML_HEAD_DIM, ML_HEAD_DIM), zeros(n_even, bp, ML_HEADS, ML_HEAD_DIM),
        zeros(n_even, bp, ML_HEADS), zeros(n_even, bp, ML_CONV - 1, 2 * ML_WIDTH),
        zeros(n_odd, bp, GLA_HEADS, GLA_DK, GLA_DV), params)
    y_s, ev_s, gla_s = _trunk(x_sample, cache_fox_k, cache_fox_v, cache_fox_logf, state_mlstm_c, state_mlstm_n,
                              state_mlstm_m, state_mlstm_conv, state_gla_s, params)
    return (y_p, y_s, *ev_p, gla_p, *ev_s, gla_s)
```

```python
import functools

import numpy as np
import jax
import jax.numpy as jnp
from jax import lax
from jax.experimental import pallas as pl
from jax.experimental.pallas import tpu as pltpu

F32 = jnp.float32
BF16 = jnp.bfloat16
EPS = 1e-6
NEG = -1e30
LOG2E = 1.4426950408889634

LANES = 128
VMEM_LIMIT = 56 * 1024 * 1024

D_MODEL = 1024
D_FF = 4 * D_MODEL
FOX_HEADS, FOX_HEAD_DIM = 8, 64
FOX_WIDTH = FOX_HEADS * FOX_HEAD_DIM
ML_HEADS, ML_HEAD_DIM = 4, 128
ML_WIDTH = ML_HEADS * ML_HEAD_DIM
ML_CONV = 4
GLA_HEADS, GLA_DK, GLA_DV = 4, 128, 256
GLA_KW = GLA_HEADS * GLA_DK
GLA_VW = GLA_HEADS * GLA_DV
GLA_RANK = 16
GLA_TAU = 16.0

E_Q, E_K, E_V, E_MQK, E_MV, E_MO, E_G, E_END = 0, 512, 1024, 1536, 2560, 3072, 3584, 3712
G_FOX, G_MI, G_MF = 0, 8, 12
O_Q, O_K, O_V, O_G, O_A, O_END = 0, 512, 1024, 2048, 3072, 3200

ML_CHUNK = 256
GLA_CHUNK = 128
GLA_LEVELS = (64, 32, 16, 8, 4, 2, 1)


def _cparams(sem):
    return pltpu.CompilerParams(dimension_semantics=sem, vmem_limit_bytes=VMEM_LIMIT)


def _const_spec(shape):
    nd = len(shape)
    return pl.BlockSpec(shape, lambda *_: (0,) * nd, pipeline_mode=pl.Buffered(1))


def _rms(x, g):
    return x * lax.rsqrt(jnp.mean(x * x, axis=-1, keepdims=True) + EPS) * g


def _sigmoid(x):
    return 1.0 / (1.0 + jnp.exp(-x))


def _log_sigmoid(x):
    return -(jnp.maximum(-x, 0.0) + jnp.log1p(jnp.exp(-jnp.abs(x))))


def _dot(a, b):
    return jnp.dot(a, b, preferred_element_type=F32)


def _dot_nt(a, b):
    return lax.dot_general(a, b, (((1,), (1,)), ((), ())), preferred_element_type=F32)


def _dot_tn(a, b):
    return lax.dot_general(a, b, (((0,), (0,)), ((), ())), preferred_element_type=F32)


def _split3(x):
    hi = x.astype(BF16)
    r1 = x - hi.astype(F32)
    mid = r1.astype(BF16)
    lo = (r1 - mid.astype(F32)).astype(BF16)
    return hi, mid, lo


def _split2(x):
    hi = x.astype(BF16)
    return hi, (x - hi.astype(F32)).astype(BF16)


def _mat_f32(m, parts):
    return functools.reduce(lambda a, b: a + b, [_dot(m, p) for p in parts])


def _f32_mat(parts, m):
    return functools.reduce(lambda a, b: a + b, [_dot(p, m) for p in parts])


def _tri(n, lower):
    r = lax.broadcasted_iota(jnp.int32, (n, n), 0)
    c = lax.broadcasted_iota(jnp.int32, (n, n), 1)
    keep = (c <= r) if lower else (r <= c)
    return jnp.where(keep, 1.0, 0.0).astype(BF16)


def _pad_rows(a, n):
    if a.shape[0] == n:
        return a
    return jnp.concatenate([a, jnp.zeros((n - a.shape[0], a.shape[1]), a.dtype)], axis=0)


def _even_project(x_ref, g_ref, w_ref, bias_ref, mqk_ref, mv_ref, mo_ref, gc_ref):
    h = _rms(x_ref[...], g_ref[...]).astype(BF16)

    def seg(a, b):
        return _dot(h, w_ref[:, a:b])

    q = seg(E_Q, E_K) * (FOX_HEAD_DIM ** -0.5 * LOG2E)
    k = seg(E_K, E_V)
    v = seg(E_V, E_MQK)
    mqk_ref[...] = seg(E_MQK, E_MV)
    mv_ref[...] = seg(E_MV, E_MO).astype(BF16)
    mo_ref[...] = seg(E_MO, E_G).astype(BF16)
    gz = seg(E_G, E_END) + bias_ref[...]
    lane = lax.broadcasted_iota(jnp.int32, gz.shape, 1)
    is_log = (lane < G_MI) | (lane >= G_MF)
    gates = jnp.where(is_log, _log_sigmoid(gz), gz)
    gc_ref[...] = gates
    return q, k, v, gates


def _inproj_even_rows_kernel(x_ref, g_ref, w_ref, bias_ref,
                             q_ref, kf_ref, vf_ref, mqk_ref, mv_ref, mo_ref, gc_ref):
    q, k, v, _ = _even_project(x_ref, g_ref, w_ref, bias_ref, mqk_ref, mv_ref, mo_ref, gc_ref)
    q_ref[...] = q.astype(BF16)
    kf_ref[...] = k
    vf_ref[...] = v


def _inproj_even_seq_kernel(*refs, tpb, aliased):
    x_ref, g_ref, w_ref, bias_ref = refs[:4]
    refs = refs[4 + (2 if aliased else 0):]
    qt_ref, kf_ref, vf_ref, ka_ref, vt_ref, mqk_ref, mv_ref, mo_ref, gc_ref, carry = refs
    q, k, v, gates = _even_project(x_ref, g_ref, w_ref, bias_ref, mqk_ref, mv_ref, mo_ref, gc_ref)
    qt_ref[...] = q.T.astype(BF16)
    v_t = v.T
    vt_ref[...] = v_t.astype(BF16)
    kf_ref[0, 0] = k.T
    vf_ref[0, 0] = v_t

    @pl.when(pl.program_id(0) % tpb == 0)
    def _():
        carry[...] = jnp.zeros_like(carry)

    _append_f_terms(gates, carry, k.astype(BF16), ka_ref)


def _inproj_even(x, g, w, bias, tm, batch, t_len, slot, n_slots, kv_prev):
    n = x.shape[0]
    row = lambda c: (pl.BlockSpec((tm, c), lambda i: (i, 0)), (n, c))
    col = (pl.BlockSpec((FOX_WIDTH, tm), lambda i: (0, i)), (FOX_WIDTH, n))
    common_in = [row(D_MODEL)[0], _const_spec((1, D_MODEL)), _const_spec((D_MODEL, E_END)), _const_spec((1, LANES))]
    tail = [(row(2 * ML_WIDTH), F32), (row(ML_WIDTH), BF16), (row(ML_WIDTH), BF16), (row(LANES), F32)]
    if t_len % tm != 0:
        outs = [(row(FOX_WIDTH), BF16), (row(FOX_WIDTH), F32), (row(FOX_WIDTH), F32)] + tail
        return pl.pallas_call(
            _inproj_even_rows_kernel,
            grid=(n // tm,),
            in_specs=common_in,
            out_specs=[spec for (spec, _), _ in outs],
            out_shape=[jax.ShapeDtypeStruct(shape, dt) for (_, shape), dt in outs],
            compiler_params=_cparams(("parallel",)),
            name="inproj_even_rows",
        )(x, g, w, bias)
    tpb = t_len // tm
    state = (pl.BlockSpec((1, 1, FOX_WIDTH, tm), lambda i: (slot, i // tpb, 0, i % tpb)),
             (n_slots, batch, FOX_WIDTH, t_len))
    outs = [(col, BF16), (state, F32), (state, F32), (row(2 * FOX_WIDTH), BF16), (col, BF16)] + tail
    ins = [x, g, w, bias]
    in_specs = list(common_in)
    aliases = {}
    if kv_prev is not None:
        ins += list(kv_prev)
        in_specs += [pl.BlockSpec(memory_space=pl.ANY)] * 2
        aliases = {4: 1, 5: 2}
    return pl.pallas_call(
        functools.partial(_inproj_even_seq_kernel, tpb=tpb, aliased=kv_prev is not None),
        grid=(n // tm,),
        in_specs=in_specs,
        out_specs=[spec for (spec, _), _ in outs],
        out_shape=[jax.ShapeDtypeStruct(shape, dt) for (_, shape), dt in outs],
        scratch_shapes=[pltpu.VMEM((1, LANES), F32)],
        input_output_aliases=aliases,
        compiler_params=_cparams(("arbitrary",)),
        name="inproj_even_seq",
    )(*ins)


def _inproj_odd_kernel(x_ref, g_ref, w_ref, wa2_ref, ba_ref, q_ref, k_ref, v_ref, gg_ref, la_ref):
    h = _rms(x_ref[...], g_ref[...]).astype(BF16)

    def seg(a, b):
        return _dot(h, w_ref[:, a:b])

    q_ref[...] = (seg(O_Q, O_K) * (GLA_DK ** -0.5)).astype(BF16)
    k_ref[...] = seg(O_K, O_V).astype(BF16)
    v_ref[...] = seg(O_V, O_G).astype(BF16)
    gg_ref[...] = seg(O_G, O_A).astype(BF16)
    ga = seg(O_A, O_END).astype(BF16)
    la_ref[...] = _log_sigmoid(_dot(ga, wa2_ref[...]) + ba_ref[...]) * (1.0 / GLA_TAU)


def _inproj_odd(x, g, w, wa2, ba, tm):
    n = x.shape[0]
    row = lambda c: pl.BlockSpec((tm, c), lambda i: (i, 0))
    outs = [(GLA_KW, BF16), (GLA_KW, BF16), (GLA_VW, BF16), (GLA_VW, BF16), (GLA_KW, F32)]
    return pl.pallas_call(
        _inproj_odd_kernel,
        grid=(n // tm,),
        in_specs=[row(D_MODEL), _const_spec((1, D_MODEL)), _const_spec((D_MODEL, O_END)),
                  _const_spec((LANES, GLA_KW)), _const_spec((1, GLA_KW))],
        out_specs=[row(c) for c, _ in outs],
        out_shape=[jax.ShapeDtypeStruct((n, c), dt) for c, dt in outs],
        compiler_params=_cparams(("parallel",)),
        name="inproj_odd",
    )(x, g, w, wa2, ba)


def _post_kernel(*refs, n_mix, final):
    x_ref = refs[0]
    mix_refs = refs[1:1 + n_mix]
    wo_ref, gf_ref, w1_ref, w2_ref = refs[1 + n_mix:5 + n_mix]
    rest = refs[5 + n_mix:]
    mix = mix_refs[0][...] if n_mix == 1 else jnp.concatenate([r[...] for r in mix_refs], axis=1)
    x1 = x_ref[...] + _dot(mix, wo_ref[...])
    h = _rms(x1, gf_ref[...]).astype(BF16)
    y = x1
    for c in range(D_FF // D_MODEL):
        sl = slice(c * D_MODEL, (c + 1) * D_MODEL)
        t = jnp.maximum(_dot(h, w1_ref[:, sl]), 0.0)
        y = y + _dot((t * t).astype(BF16), w2_ref[sl, :])
    if final:
        gfin_ref, out_ref = rest
        out_ref[...] = _rms(y, gfin_ref[...])
    else:
        (out_ref,) = rest
        out_ref[...] = y


def _post(x, mixes, wo, gf, w1, w2, gfin, tm):
    n = x.shape[0]
    row = lambda c: pl.BlockSpec((tm, c), lambda i: (i, 0))
    final = gfin is not None
    ins = [x, *mixes, wo, gf, w1, w2]
    specs = [row(D_MODEL)] + [row(m.shape[1]) for m in mixes] + [
        _const_spec((D_MODEL, D_MODEL)), _const_spec((1, D_MODEL)),
        _const_spec((D_MODEL, D_FF)), _const_spec((D_FF, D_MODEL))]
    if final:
        ins.append(gfin)
        specs.append(_const_spec((1, D_MODEL)))
    return pl.pallas_call(
        functools.partial(_post_kernel, n_mix=len(mixes), final=final),
        grid=(n // tm,),
        in_specs=specs,
        out_specs=row(D_MODEL),
        out_shape=jax.ShapeDtypeStruct((n, D_MODEL), F32),
        compiler_params=_cparams(("parallel",)),
        name="post_final" if final else "post",
    )(*ins)


F_TERMS = 3


def _append_f_terms(lf, carry, k, ka_ref):
    tc = lf.shape[0]
    f = _mat_f32(_tri(tc, True), _split3(lf)) + carry[...]
    carry[...] = f[tc - 1:tc, :]
    hi, mid, lo = [p.astype(F32) for p in _split3(f * LOG2E)]
    lane = lax.broadcasted_iota(jnp.int32, (1, LANES), 1)
    cols = jnp.where(lane < FOX_HEADS, hi,
                     jnp.where(lane < 2 * FOX_HEADS, pltpu.roll(mid, FOX_HEADS, axis=1),
                               jnp.where(lane < 3 * FOX_HEADS, pltpu.roll(lo, 2 * FOX_HEADS, axis=1), 0.0)))
    cols = cols.astype(BF16)
    for hp in range(FOX_HEADS // 2):
        ka_ref[:, 2 * hp * LANES:(2 * hp + 1) * LANES] = k[:, hp * LANES:(hp + 1) * LANES]
        ka_ref[:, (2 * hp + 1) * LANES:(2 * hp + 2) * LANES] = cols


def _fox_decode_kernel(q_ref, kc_ref, vc_ref, kn_ref, vn_ref, lf_ref, o_ref, *, t_new, past):
    nkeys = past + LANES
    lane_f = lax.broadcasted_iota(jnp.int32, (1, FOX_WIDTH), 1)
    head_mask = [jnp.where((lane_f >= h * FOX_HEAD_DIM) & (lane_f < (h + 1) * FOX_HEAD_DIM), 1.0, 0.0)
                 for h in range(FOX_HEADS)]
    q = q_ref[0].astype(F32)
    qb = jnp.concatenate([q * hm for hm in head_mask], axis=0).astype(BF16)

    def keys(cache_ref, new_ref):
        return jnp.concatenate([cache_ref[0, 0].astype(BF16), new_ref[0]], axis=1)

    s = _dot(qb, keys(kc_ref, kn_ref))
    lf = lf_ref[0]
    triu = _tri(LANES, False)
    carry = jnp.zeros((lf.shape[0], 1), F32)
    blocks = []
    for c in range(nkeys // LANES):
        cs = _f32_mat(_split3(lf[:, c * LANES:(c + 1) * LANES]), triu) + carry
        carry = cs[:, LANES - 1:LANES]
        blocks.append(cs)
    f_all = jnp.concatenate(blocks, axis=1) * LOG2E
    s = s - jnp.concatenate([jnp.broadcast_to(f_all[h:h + 1, :], (t_new, nkeys)) for h in range(FOX_HEADS)], axis=0)
    kpos = lax.broadcasted_iota(jnp.int32, (1, nkeys), 1)
    qpos = past + (lax.broadcasted_iota(jnp.int32, (FOX_HEADS * t_new, 1), 0) & (t_new - 1))
    s = jnp.where(kpos <= qpos, s, NEG)
    p = jnp.exp2(s - jnp.max(s, axis=-1, keepdims=True))
    ob = _dot_nt(p.astype(BF16), keys(vc_ref, vn_ref)) / jnp.sum(p, axis=-1, keepdims=True)
    out = ob[0:t_new] * head_mask[0]
    for h in range(1, FOX_HEADS):
        out = out + ob[h * t_new:(h + 1) * t_new] * head_mask[h]
    o_ref[0] = out.astype(o_ref.dtype)


def _fox_decode(q, k_cache, v_cache, layer, k_new, v_new, lf_row):
    batch, t_new, _ = q.shape
    past = k_cache.shape[3]
    assert t_new & (t_new - 1) == 0 and t_new <= LANES and t_new % 16 == 0 and past % LANES == 0
    blk = lambda a: pl.BlockSpec((1,) + a.shape[1:], lambda b: (b, 0, 0))
    cache = pl.BlockSpec((1, 1, FOX_WIDTH, past), lambda b: (layer, b, 0, 0))
    args = (q, k_cache, v_cache, k_new, v_new, lf_row)
    return pl.pallas_call(
        functools.partial(_fox_decode_kernel, t_new=t_new, past=past),
        grid=(batch,),
        in_specs=[blk(q), cache, cache, blk(k_new), blk(v_new), blk(lf_row)],
        out_specs=blk(q),
        out_shape=jax.ShapeDtypeStruct(q.shape, BF16),
        compiler_params=_cparams(("parallel",)),
        name="fox_decode",
    )(*args)


ONES_ROWS = 16


def _fox_attn_kernel(qt_ref, ka_ref, vt_ref, o_ref, sa_scr, sb_scr, acc_scr, *, tq, tkc, past, n_diag):
    q_first = past + pl.program_id(2) * tq
    n_full = q_first // tkc
    drow = lax.broadcasted_iota(jnp.int32, (LANES, 1), 0)
    lane = lax.broadcasted_iota(jnp.int32, (1, 2 * tq), 1)
    lane_head = jnp.where(lane < tq, 0, 1)
    own_head = jnp.where(jnp.where(drow < FOX_HEAD_DIM, 0, 1) == lane_head, 1.0, 0.0).astype(BF16)
    qt = qt_ref[...]
    q2 = jnp.concatenate([qt, qt], axis=1) * own_head
    head = 2 * pl.program_id(1) + lane_head
    f_sel = ((drow & (FOX_HEADS - 1)) == head) & (drow < F_TERMS * FOX_HEADS)
    qa = jnp.concatenate([q2, jnp.where(f_sel, -1.0, 0.0).astype(BF16)], axis=0)
    ones = jnp.ones((ONES_ROWS, tkc), BF16)
    acc_scr[...] = jnp.zeros_like(acc_scr)

    def produce(j, dst):
        start = pl.multiple_of(j * tkc, tkc)
        dst[...] = _dot(ka_ref[pl.ds(start, tkc), :], qa)

    def consume(j, src, m_prev, masked):
        start = pl.multiple_of(j * tkc, tkc)
        st = src[...]
        if masked:
            kpos = start + lax.broadcasted_iota(jnp.int32, (tkc, 1), 0)
            qpos = q_first + (lane & (tq - 1))
            st = jnp.where(kpos <= qpos, st, NEG)
        m_new = jnp.maximum(m_prev, jnp.max(st, axis=0, keepdims=True))
        alpha = jnp.exp2(m_prev - m_new)
        p = jnp.exp2(st - m_new).astype(BF16)
        for hh in range(2):
            rows = slice(hh * FOX_HEAD_DIM, (hh + 1) * FOX_HEAD_DIM)
            cols = slice(hh * tq, (hh + 1) * tq)
            va = jnp.concatenate([vt_ref[rows, pl.ds(start, tkc)], ones], axis=0)
            acc_scr[hh] = alpha[:, cols] * acc_scr[hh] + _dot(va, p[:, cols])
        return m_new

    produce(0, sa_scr)

    def body(i, m):
        produce(2 * i + 1, sb_scr)
        m = consume(2 * i, sa_scr, m, False)
        produce(2 * i + 2, sa_scr)
        return consume(2 * i + 1, sb_scr, m, False)

    m = lax.fori_loop(0, n_full // 2, body, jnp.full((1, 2 * tq), NEG, F32))
    bufs = (sa_scr, sb_scr)
    for d in range(n_diag):
        if d + 1 < n_diag:
            produce(n_full + d + 1, bufs[(d + 1) % 2])
        m = consume(n_full + d, bufs[d % 2], m, True)
    out = jnp.concatenate([acc_scr[hh, 0:FOX_HEAD_DIM] / acc_scr[hh, FOX_HEAD_DIM:FOX_HEAD_DIM + 1]
                           for hh in range(2)], axis=0)
    o_ref[...] = out.T.astype(o_ref.dtype)


def _fox_attention(qt, ka, vt, batch, tq_len, tk_len, past, tq, tkc):
    nq = tq_len // tq
    n_diag = max(1, tq // tkc)
    assert tq & (tq - 1) == 0 and tq % LANES == 0 and (tkc % tq == 0 or tq % tkc == 0) and tq_len % tq == 0
    assert past % (2 * tkc) == 0 and (nq == 1 or tq % (2 * tkc) == 0)
    assert tk_len >= past + max(tq_len, n_diag * tkc) and tk_len % tkc == 0
    pairs = FOX_HEADS // 2
    return pl.pallas_call(
        functools.partial(_fox_attn_kernel, tq=tq, tkc=tkc, past=past, n_diag=n_diag),
        grid=(batch, pairs, nq),
        in_specs=[pl.BlockSpec((LANES, tq), lambda b, h, i: (h, b * nq + i)),
                  pl.BlockSpec((tk_len, 2 * LANES), lambda b, h, i: (b, h)),
                  pl.BlockSpec((LANES, tk_len), lambda b, h, i: (h, b))],
        out_specs=pl.BlockSpec((tq, LANES), lambda b, h, i: (b * nq + i, h)),
        out_shape=jax.ShapeDtypeStruct((batch * tq_len, FOX_WIDTH), BF16),
        scratch_shapes=[pltpu.VMEM((tkc, 2 * tq), F32), pltpu.VMEM((tkc, 2 * tq), F32),
                        pltpu.VMEM((2, FOX_HEAD_DIM + ONES_ROWS, tq), F32)],
        compiler_params=_cparams(("parallel", "parallel", "arbitrary")),
        name="fox_attention",
    )(qt, ka, vt)


def _mlstm_kernel(mqk_ref, mv_ref, mo_ref, gc_ref, gr_ref, c0_ref, n0_ref, m0_ref, buf_ref,
                  cw_ref, cb_ref, gml_ref,
                  h_ref, c_out, n_out, m_out, buf_out,
                  c_scr, n_scr, m_scr, prev_scr, *, L, t_last, nc):
    ci = pl.program_id(1)
    rows = mqk_ref.shape[0]

    @pl.when(ci == 0)
    def _():
        c_scr[...] = c0_ref[0]
        n_scr[...] = n0_ref[0]
        m_scr[...] = m0_ref[0]
        prev_scr[...] = buf_ref[0]

    u = _pad_rows(mqk_ref[...], L)
    ext = jnp.concatenate([prev_scr[...], u], axis=0)
    y = cb_ref[...] + cw_ref[ML_CONV - 1:ML_CONV, :] * u
    for s in range(1, ML_CONV):
        y = y + cw_ref[ML_CONV - 1 - s:ML_CONV - s, :] * pltpu.roll(ext, s, axis=0)[8:8 + L]
    qk = y * _sigmoid(y)
    prev_scr[...] = u[L - 8:L]

    gc = _pad_rows(gc_ref[...], L)
    gr = gr_ref[0]
    b_col = _mat_f32(_tri(L, True), _split3(gc))
    b_row = _f32_mat(_split3(gr), _tri(L, False))
    v_all = _pad_rows(mv_ref[...], L)
    o_all = _pad_rows(mo_ref[...], L)
    t_idx = lax.broadcasted_iota(jnp.int32, (L, 1), 0)
    causal = lax.broadcasted_iota(jnp.int32, (1, L), 1) <= t_idx

    heads = []
    for h in range(ML_HEADS):
        hs = slice(h * ML_HEAD_DIM, (h + 1) * ML_HEAD_DIM)
        q32 = qk[:, hs]
        k32 = qk[:, ML_WIDTH + h * ML_HEAD_DIM:ML_WIDTH + (h + 1) * ML_HEAD_DIM] * (ML_HEAD_DIM ** -0.5)
        qb, kb = q32.astype(BF16), k32.astype(BF16)
        c_prev = c_scr[h]
        heads.append(dict(hs=hs, q32=q32, k32=k32, qb=qb, kb=kb, vb=v_all[:, hs], c_prev=c_prev,
                          s=_dot_nt(qb, kb), qc=_dot_nt(qb, c_prev.astype(BF16))))

    for h, hd in enumerate(heads):
        bc = b_col[:, G_MF + h:G_MF + h + 1]
        ic = gc[:, G_MI + h:G_MI + h + 1]
        br = b_row[G_MF + h:G_MF + h + 1, :]
        ir = gr[G_MI + h:G_MI + h + 1, :]
        m_prev = m_scr[h][:, 0:1]
        dmat = jnp.where(causal, bc - br + ir, NEG)
        inter = bc + m_prev
        m_t = jnp.maximum(inter, jnp.max(dmat, axis=-1, keepdims=True))
        w = jnp.exp(dmat - m_t)
        g = jnp.exp(inter - m_t)
        a = w * hd["s"]
        b_last = bc[t_last:t_last + 1, :]
        m_last = m_t[t_last:t_last + 1, :]
        w_end = jnp.where(t_idx <= t_last, jnp.exp(b_last - bc + ic - m_last), 0.0)
        hd.update(m_t=m_t, g=g, a=a, m_last=m_last, g_end=g[t_last:t_last + 1, :], w_end=w_end,
                  av=_dot(a.astype(BF16), hd["vb"]),
                  vk=_dot_tn((hd["vb"].astype(F32) * w_end).astype(BF16), hd["kb"]))

    for h, hd in enumerate(heads):
        hs, g, a, m_t = hd["hs"], hd["g"], hd["a"], hd["m_t"]
        n_prev = n_scr[h]
        num = g * hd["qc"] + hd["av"]
        den = g * jnp.sum(hd["q32"] * n_prev, axis=-1, keepdims=True) + jnp.sum(a, axis=-1, keepdims=True)
        hh = num / jnp.maximum(jnp.abs(den), jnp.exp(-m_t))
        c_scr[h] = hd["g_end"] * hd["c_prev"] + hd["vk"]
        n_scr[h] = hd["g_end"] * n_prev + jnp.sum(hd["k32"] * hd["w_end"], axis=0, keepdims=True)
        m_scr[h] = jnp.broadcast_to(hd["m_last"], (1, LANES))
        yh = hh * lax.rsqrt(jnp.mean(hh * hh, axis=-1, keepdims=True) + EPS) * gml_ref[:, hs]
        yh = yh * _sigmoid(o_all[:, hs].astype(F32))
        h_ref[:, hs] = yh[:rows].astype(h_ref.dtype)

    @pl.when(ci == nc - 1)
    def _():
        c_out[0] = c_scr[...]
        n_out[0] = n_scr[...]
        m_out[0] = m_scr[...]
        buf_out[0] = ext[t_last + 1:t_last + 9]


def _mlstm(mqk, mv, mo, gc, gr, c0, n0, m0, buf8, cw, cb, gml, batch, t_len):
    L = ML_CHUNK if t_len >= ML_CHUNK else LANES
    rows = min(L, t_len)
    nc = t_len // rows
    t_last = rows - 1
    assert t_len % rows == 0 and (nc == 1 or rows == L) and (t_last + 1) % 8 == 0
    tok = lambda c: pl.BlockSpec((rows, c), lambda b, i: (b * nc + i, 0))
    st = lambda *s: pl.BlockSpec((1,) + s, lambda b, i: (b,) + (0,) * len(s))
    shp_c = (ML_HEADS, ML_HEAD_DIM, ML_HEAD_DIM)
    shp_n = (ML_HEADS, 1, ML_HEAD_DIM)
    return pl.pallas_call(
        functools.partial(_mlstm_kernel, L=L, t_last=t_last, nc=nc),
        grid=(batch, nc),
        in_specs=[tok(2 * ML_WIDTH), tok(ML_WIDTH), tok(ML_WIDTH), tok(LANES),
                  pl.BlockSpec((1, 16, L), lambda b, i: (b, 0, i)),
                  st(*shp_c), st(*shp_n), st(*shp_n), st(8, 2 * ML_WIDTH),
                  _const_spec((ML_CONV, 2 * ML_WIDTH)), _const_spec((1, 2 * ML_WIDTH)), _const_spec((1, ML_WIDTH))],
        out_specs=[tok(ML_WIDTH), st(*shp_c), st(*shp_n), st(*shp_n), st(8, 2 * ML_WIDTH)],
        out_shape=[jax.ShapeDtypeStruct((batch * t_len, ML_WIDTH), BF16),
                   jax.ShapeDtypeStruct((batch,) + shp_c, F32),
                   jax.ShapeDtypeStruct((batch,) + shp_n, F32),
                   jax.ShapeDtypeStruct((batch,) + shp_n, F32),
                   jax.ShapeDtypeStruct((batch, 8, 2 * ML_WIDTH), F32)],
        scratch_shapes=[pltpu.VMEM(shp_c, F32), pltpu.VMEM(shp_n, F32), pltpu.VMEM(shp_n, F32),
                        pltpu.VMEM((8, 2 * ML_WIDTH), F32)],
        compiler_params=_cparams(("parallel", "arbitrary")),
        name="mlstm",
    )(mqk, mv, mo, gc, gr, c0, n0, m0, buf8, cw, cb, gml)


def _gla_level_matrices(L):
    mats = np.zeros((len(GLA_LEVELS), L, L), np.float32)
    for li, b in enumerate(GLA_LEVELS):
        for t in range(L):
            base = (t // (2 * b)) * 2 * b
            bound = base + b - 1
            if t > bound:
                mats[li, t, bound + 1:t + 1] = 1.0
            else:
                mats[li, t, t + 1:bound + 1] = 1.0
    return jnp.asarray(mats, BF16)


def _gla_kernel(q_ref, k_ref, v_ref, gg_ref, la_ref, s0_ref, lvl_ref, gn_ref, o_ref, s_out, s_scr, *, L, t_last, nc):
    ci = pl.program_id(1)
    nb, rows = q_ref.shape[0], q_ref.shape[1]

    @pl.when(ci == 0)
    def _():
        s_scr[...] = s0_ref[...]

    t_idx = lax.broadcasted_iota(jnp.int32, (L, 1), 0)
    s_idx = lax.broadcasted_iota(jnp.int32, (1, L), 1)
    tril = _tri(L, True)

    for bi in range(nb):
        la3 = _split2(_pad_rows(la_ref[bi], L))
        cb = _mat_f32(tril, la3)
        q_all = _pad_rows(q_ref[bi], L)
        k_all = _pad_rows(k_ref[bi], L)
        v_all = _pad_rows(v_ref[bi], L)
        g_all = _pad_rows(gg_ref[bi], L)

        a_heads = []
        q32s, k32s = [], []
        for h in range(GLA_HEADS):
            ks = slice(h * GLA_DK, (h + 1) * GLA_DK)
            q32s.append(q_all[:, ks].astype(F32))
            k32s.append(k_all[:, ks].astype(F32))
            a_heads.append(jnp.where(t_idx == s_idx, _dot_nt(q_all[:, ks], k_all[:, ks]), 0.0))
        e_next = jnp.exp(_mat_f32(lvl_ref[0], la3))
        pending = None
        for li, b in enumerate(GLA_LEVELS):
            e_all = e_next
            if li + 1 < len(GLA_LEVELS):
                e_next = jnp.exp(_mat_f32(lvl_ref[li + 1], la3))
            sh = b.bit_length() - 1
            second = (jnp.right_shift(t_idx, sh) & 1) == 1
            same_pair = jnp.right_shift(t_idx, sh + 1) == jnp.right_shift(s_idx, sh + 1)
            prods = []
            for h in range(GLA_HEADS):
                e = e_all[:, h * GLA_DK:(h + 1) * GLA_DK]
                qe = jnp.where(second, q32s[h] * e, 0.0).astype(BF16)
                ke = jnp.where(second, 0.0, k32s[h] * e).astype(BF16)
                prods.append(_dot_nt(qe, ke))
            if pending is not None:
                for h in range(GLA_HEADS):
                    a_heads[h] = a_heads[h] + jnp.where(pending[1], pending[0][h], 0.0)
            pending = (prods, same_pair)
        for h in range(GLA_HEADS):
            a_heads[h] = a_heads[h] + jnp.where(pending[1], pending[0][h], 0.0)

        outs = []
        for h in range(GLA_HEADS):
            ks = slice(h * GLA_DK, (h + 1) * GLA_DK)
            vs = slice(h * GLA_DV, (h + 1) * GLA_DV)
            cbh = cb[:, ks]
            vb = v_all[:, vs]
            s_t = s_scr[bi, h]
            o = (_dot_nt((q32s[h] * jnp.exp(cbh)).astype(BF16), s_t.astype(BF16))
                 + _dot(a_heads[h].astype(BF16), vb))
            cl = cbh[t_last:t_last + 1, :]
            kd = jnp.where(t_idx <= t_last, k32s[h] * jnp.exp(cl - cbh), 0.0).astype(BF16)
            s_scr[bi, h] = jnp.exp(cl) * s_t + _dot_tn(vb, kd)
            outs.append(o)
        for h in range(GLA_HEADS):
            vs = slice(h * GLA_DV, (h + 1) * GLA_DV)
            o = outs[h]
            y = o * lax.rsqrt(jnp.mean(o * o, axis=-1, keepdims=True) + EPS) * gn_ref[:, vs]
            gate = g_all[:, vs].astype(F32)
            y = y * (gate * _sigmoid(gate))
            o_ref[bi, :, vs] = y[:rows].astype(o_ref.dtype)

    @pl.when(ci == nc - 1)
    def _():
        s_out[...] = s_scr[...]


GLA_BATCH_PER_STEP = 2


def _gla(q, k, v, gg, la, s0t, gn, batch, t_len):
    L = GLA_CHUNK
    rows = min(L, t_len)
    nc = t_len // rows
    t_last = rows - 1
    nb = GLA_BATCH_PER_STEP
    assert t_len % rows == 0 and (nc == 1 or rows == L) and batch % nb == 0
    tok = lambda c: pl.BlockSpec((nb, rows, c), lambda b, i: (b, i, 0))
    shp_s = (GLA_HEADS, GLA_DV, GLA_DK)
    st = pl.BlockSpec((nb,) + shp_s, lambda b, i: (b, 0, 0, 0))
    levels = _gla_level_matrices(L)
    seq = lambda a: a.reshape(batch, t_len, a.shape[-1])
    o, s_new = pl.pallas_call(
        functools.partial(_gla_kernel, L=L, t_last=t_last, nc=nc),
        grid=(batch // nb, nc),
        in_specs=[tok(GLA_KW), tok(GLA_KW), tok(GLA_VW), tok(GLA_VW), tok(GLA_KW), st,
                  _const_spec(levels.shape), _const_spec((1, GLA_VW))],
        out_specs=[tok(GLA_VW), st],
        out_shape=[jax.ShapeDtypeStruct((batch, t_len, GLA_VW), BF16),
                   jax.ShapeDtypeStruct((batch,) + shp_s, F32)],
        scratch_shapes=[pltpu.VMEM((nb,) + shp_s, F32)],
        compiler_params=_cparams(("parallel", "arbitrary")),
        name="gla",
    )(seq(q), seq(k), seq(v), seq(gg), seq(la), s0t, levels, gn)
    return o.reshape(batch * t_len, GLA_VW), s_new


def _pack_even(w, b_fox_f, b_i, b_f):
    d = w.shape[0]
    o = np.cumsum((0, FOX_WIDTH, FOX_WIDTH, FOX_WIDTH, FOX_HEADS, 2 * ML_WIDTH, ML_WIDTH, ML_WIDTH, ML_HEADS, ML_HEADS))
    n_gate = FOX_HEADS + 2 * ML_HEADS
    wp = jnp.concatenate([w[:, o[0]:o[3]], w[:, o[4]:o[7]], w[:, o[3]:o[4]], w[:, o[7]:o[9]],
                          jnp.zeros((d, LANES - n_gate), w.dtype)], axis=1).astype(BF16)
    bias = jnp.concatenate([b_fox_f, b_i, b_f, jnp.zeros((LANES - n_gate,), F32)]).reshape(1, LANES)
    return wp, bias


def _pack_odd(w, w_a2):
    d = w.shape[0]
    wp = jnp.concatenate([w, jnp.zeros((d, LANES - GLA_RANK), w.dtype)], axis=1).astype(BF16)
    wa2 = jnp.concatenate([w_a2, jnp.zeros((LANES - GLA_RANK, w_a2.shape[1]), w_a2.dtype)], axis=0).astype(BF16)
    return wp, wa2


def _gate_rows(gc, batch, t_len, t_pad):
    g = gc.reshape(batch, t_len, LANES)[:, :, :16].transpose(0, 2, 1)
    if t_pad > t_len:
        g = jnp.pad(g, ((0, 0), (0, 0), (0, t_pad - t_len)))
    return g


def _trunk(x, fox_k, fox_v, fox_lf, ml_c, ml_n, ml_m, ml_buf, gla_s, params):
    (norm_mix, norm_ffn, norm_final, w_even, bias_even, conv_w, conv_b, g_ml,
     w_odd, w_a2, b_a, g_gla, w_out, w_ff1, w_ff2) = params
    batch, t_len, d = x.shape
    past = fox_k.shape[2]
    n = batch * t_len
    tm = 512 if n % 512 == 0 else 256 if n % 256 == 0 else n
    depth = norm_mix.shape[0]
    xf = x.reshape(n, d)
    ev_states, odd_states = [], []
    kv_stacked = None
    y = None
    for layer in range(depth):
        j = layer // 2
        if layer % 2 == 0:
            assert (t_len % tm == 0) == (past == 0)
            proj = _inproj_even(xf, norm_mix[layer][None], w_even[j], bias_even[j],
                                tm, batch, t_len, j, (depth + 1) // 2, kv_stacked)
            if past == 0:
                qt, kf, vf, ka, vt, mqk, mv, mo, gc = proj
                kv_stacked = (kf, vf)
                attn = _fox_attention(qt, ka, vt, batch, t_len, t_len, 0, min(512, t_len), 256)
            else:
                q, kf, vf, mqk, mv, mo, gc = proj
                lf_new = gc.reshape(batch, t_len, LANES)[:, :, :FOX_HEADS]
                lf_row = jnp.concatenate([fox_lf[j], lf_new, jnp.zeros((batch, LANES - t_len, FOX_HEADS), F32)], axis=1)
                lf_row = jnp.pad(lf_row.transpose(0, 2, 1), ((0, 0), (0, 16 - FOX_HEADS), (0, 0)))
                cache_t = lambda a: a.transpose(0, 1, 3, 4, 2).reshape(a.shape[0], batch, FOX_WIDTH, past)
                new_t = lambda a: jnp.pad(a.reshape(batch, t_len, FOX_WIDTH).transpose(0, 2, 1).astype(BF16),
                                          ((0, 0), (0, 0), (0, LANES - t_len)))
                attn = _fox_decode(q.reshape(batch, t_len, FOX_WIDTH), cache_t(fox_k), cache_t(fox_v), j,
                                   new_t(kf), new_t(vf), lf_row).reshape(n, FOX_WIDTH)
            l_ml = ML_CHUNK if t_len >= ML_CHUNK else LANES
            gr = _gate_rows(gc, batch, t_len, max(t_len, l_ml))
            c0 = ml_c[j]
            n0 = ml_n[j][:, :, None, :]
            m0 = jnp.broadcast_to(ml_m[j][:, :, None, None], (batch, ML_HEADS, 1, LANES))
            buf8 = jnp.pad(ml_buf[j], ((0, 0), (8 - (ML_CONV - 1), 0), (0, 0)))
            h_ml, c_new, n_new, m_new, buf_new = _mlstm(mqk, mv, mo, gc, gr, c0, n0, m0, buf8,
                                                        conv_w[j], conv_b[j][None], g_ml[j][None], batch, t_len)
            heads = lambda a: None if past == 0 else a.reshape(batch, t_len, FOX_HEADS, FOX_HEAD_DIM)
            ev_states.append((heads(kf), heads(vf),
                              gc[:, :FOX_HEADS].reshape(batch, t_len, FOX_HEADS),
                              c_new, n_new[:, :, 0, :], m_new[:, :, 0, 0], buf_new[:, 8 - (ML_CONV - 1):, :]))
            mixes = [attn, h_ml]
        else:
            q, k, v, gg, la = _inproj_odd(xf, norm_mix[layer][None], w_odd[j], w_a2[j], b_a[j][None], tm)
            s0t = gla_s[j].transpose(0, 1, 3, 2)
            o, s_new = _gla(q, k, v, gg, la, s0t, g_gla[j][None], batch, t_len)
            odd_states.append(s_new.transpose(0, 1, 3, 2))
            mixes = [o]
        last = layer == depth - 1
        out = _post(xf, mixes, w_out[layer], norm_ffn[layer][None], w_ff1[layer], w_ff2[layer],
                    norm_final[None] if last else None, tm)
        if last:
            y = out
        else:
            xf = out
    ev = [jnp.stack([s[i] for s in ev_states]) for i in range(2 if kv_stacked else 0, 7)]
    if kv_stacked:
        ev = [a.reshape(a.shape[0], batch, FOX_HEADS, FOX_HEAD_DIM, t_len).transpose(0, 1, 4, 2, 3)
              for a in kv_stacked] + ev
    return y.reshape(batch, t_len, d), ev, jnp.stack(odd_states)


def kernel(x_prompt, x_sample, cache_fox_k, cache_fox_v, cache_fox_logf, state_mlstm_c, state_mlstm_n, state_mlstm_m, state_mlstm_conv, state_gla_s, norm_mix, norm_ffn, norm_final, w_in_even, b_fox_f, conv_w_ml, conv_b_ml, b_ml_i, b_ml_f, g_ml, w_in_odd, w_gla_a2, b_gla_a, g_gla, w_out, w_ff1, w_ff2):
    n_even, n_odd = w_in_even.shape[0], w_in_odd.shape[0]
    packed_even = [_pack_even(w_in_even[j], b_fox_f[j], b_ml_i[j], b_ml_f[j]) for j in range(n_even)]
    packed_odd = [_pack_odd(w_in_odd[j], w_gla_a2[j]) for j in range(n_odd)]
    params = (norm_mix, norm_ffn, norm_final,
              [p[0] for p in packed_even], [p[1] for p in packed_even], conv_w_ml, conv_b_ml, g_ml,
              [p[0] for p in packed_odd], [p[1] for p in packed_odd], b_gla_a, g_gla,
              w_out.astype(BF16), w_ff1.astype(BF16), w_ff2.astype(BF16))

    bp = x_prompt.shape[0]
    dt = x_prompt.dtype
    zeros = lambda *s: jnp.zeros(s, dt)
    y_p, ev_p, gla_p = _trunk(
        x_prompt,
        zeros(n_even, bp, 0, FOX_HEADS, FOX_HEAD_DIM), zeros(n_even, bp, 0, FOX_HEADS, FOX_HEAD_DIM),
        zeros(n_even, bp, 0, FOX_HEADS),
        zeros(n_even, bp, ML_HEADS, ML_HEAD_DIM, ML_HEAD_DIM), zeros(n_even, bp, ML_HEADS, ML_HEAD_DIM),
        zeros(n_even, bp, ML_HEADS), zeros(n_even, bp, ML_CONV - 1, 2 * ML_WIDTH),
        zeros(n_odd, bp, GLA_HEADS, GLA_DK, GLA_DV), params)
    y_s, ev_s, gla_s = _trunk(x_sample, cache_fox_k, cache_fox_v, cache_fox_logf, state_mlstm_c, state_mlstm_n,
                              state_mlstm_m, state_mlstm_conv, state_gla_s, params)
    return (y_p, y_s, *ev_p, gla_p, *ev_s, gla_s)
```

```python
import functools

import numpy as np
import jax
import jax.numpy as jnp
from jax import lax
from jax.experimental import pallas as pl
from jax.experimental.pallas import tpu as pltpu

F32 = jnp.float32
BF16 = jnp.bfloat16
EPS = 1e-6
NEG = -1e30
LOG2E = 1.4426950408889634

LANES = 128
VMEM_LIMIT = 56 * 1024 * 1024

D_MODEL = 1024
D_FF = 4 * D_MODEL
FOX_HEADS, FOX_HEAD_DIM = 8, 64
FOX_WIDTH = FOX_HEADS * FOX_HEAD_DIM
ML_HEADS, ML_HEAD_DIM = 4, 128
ML_WIDTH = ML_HEADS * ML_HEAD_DIM
ML_CONV = 4
GLA_HEADS, GLA_DK, GLA_DV = 4, 128, 256
GLA_KW = GLA_HEADS * GLA_DK
GLA_VW = GLA_HEADS * GLA_DV
GLA_RANK = 16
GLA_TAU = 16.0

E_Q, E_K, E_V, E_MQK, E_MV, E_MO, E_G, E_END = 0, 512, 1024, 1536, 2560, 3072, 3584, 3712
G_FOX, G_MI, G_MF = 0, 8, 12
O_Q, O_K, O_V, O_G, O_A, O_END = 0, 512, 1024, 2048, 3072, 3200

ML_CHUNK = 256
GLA_CHUNK = 128
GLA_LEVELS = (64, 32, 16, 8, 4, 2, 1)


def _cparams(sem):
    return pltpu.CompilerParams(dimension_semantics=sem, vmem_limit_bytes=VMEM_LIMIT)


def _const_spec(shape):
    nd = len(shape)
    return pl.BlockSpec(shape, lambda *_: (0,) * nd, pipeline_mode=pl.Buffered(1))


def _rms(x, g):
    return x * lax.rsqrt(jnp.mean(x * x, axis=-1, keepdims=True) + EPS) * g


def _sigmoid(x):
    return 1.0 / (1.0 + jnp.exp(-x))


def _log_sigmoid(x):
    return -(jnp.maximum(-x, 0.0) + jnp.log1p(jnp.exp(-jnp.abs(x))))


def _dot(a, b):
    return jnp.dot(a, b, preferred_element_type=F32)


def _dot_nt(a, b):
    return lax.dot_general(a, b, (((1,), (1,)), ((), ())), preferred_element_type=F32)


def _dot_tn(a, b):
    return lax.dot_general(a, b, (((0,), (0,)), ((), ())), preferred_element_type=F32)


def _split3(x):
    hi = x.astype(BF16)
    r1 = x - hi.astype(F32)
    mid = r1.astype(BF16)
    lo = (r1 - mid.astype(F32)).astype(BF16)
    return hi, mid, lo


def _split2(x):
    hi = x.astype(BF16)
    return hi, (x - hi.astype(F32)).astype(BF16)


def _mat_f32(m, parts):
    return functools.reduce(lambda a, b: a + b, [_dot(m, p) for p in parts])


def _f32_mat(parts, m):
    return functools.reduce(lambda a, b: a + b, [_dot(p, m) for p in parts])


def _tri(n, lower):
    r = lax.broadcasted_iota(jnp.int32, (n, n), 0)
    c = lax.broadcasted_iota(jnp.int32, (n, n), 1)
    keep = (c <= r) if lower else (r <= c)
    return jnp.where(keep, 1.0, 0.0).astype(BF16)


def _pad_rows(a, n):
    if a.shape[0] == n:
        return a
    return jnp.concatenate([a, jnp.zeros((n - a.shape[0], a.shape[1]), a.dtype)], axis=0)


def _even_project(x_ref, g_ref, w_ref, bias_ref, mqk_ref, mv_ref, mo_ref, gc_ref):
    h = _rms(x_ref[...], g_ref[...]).astype(BF16)

    def seg(a, b):
        return _dot(h, w_ref[:, a:b])

    q = seg(E_Q, E_K) * (FOX_HEAD_DIM ** -0.5 * LOG2E)
    k = seg(E_K, E_V)
    v = seg(E_V, E_MQK)
    mqk_ref[...] = seg(E_MQK, E_MV)
    mv_ref[...] = seg(E_MV, E_MO).astype(BF16)
    mo_ref[...] = seg(E_MO, E_G).astype(BF16)
    gz = seg(E_G, E_END) + bias_ref[...]
    lane = lax.broadcasted_iota(jnp.int32, gz.shape, 1)
    is_log = (lane < G_MI) | (lane >= G_MF)
    gates = jnp.where(is_log, _log_sigmoid(gz), gz)
    gc_ref[...] = gates
    return q, k, v, gates


def _inproj_even_rows_kernel(x_ref, g_ref, w_ref, bias_ref,
                             q_ref, kf_ref, vf_ref, mqk_ref, mv_ref, mo_ref, gc_ref):
    q, k, v, _ = _even_project(x_ref, g_ref, w_ref, bias_ref, mqk_ref, mv_ref, mo_ref, gc_ref)
    q_ref[...] = q.astype(BF16)
    kf_ref[...] = k
    vf_ref[...] = v


def _inproj_even_seq_kernel(*refs, tpb, aliased):
    x_ref, g_ref, w_ref, bias_ref, ind_ref = refs[:5]
    refs = refs[5 + (2 if aliased else 0):]
    qt_ref, kf_ref, vf_ref, ka_ref, vt_ref, mqk_ref, mv_ref, mo_ref, gc_ref, bnd_ref, carry = refs
    q, k, v, gates = _even_project(x_ref, g_ref, w_ref, bias_ref, mqk_ref, mv_ref, mo_ref, gc_ref)
    qt_ref[...] = q.T.astype(BF16)
    v_t = v.T
    vt_ref[...] = v_t.astype(BF16)
    kf_ref[0, 0] = k.T
    vf_ref[0, 0] = v_t

    @pl.when(pl.program_id(0) % tpb == 0)
    def _():
        carry[...] = jnp.zeros_like(carry)

    f = _append_f_terms(gates, carry, k.astype(BF16), ka_ref)
    lane = lax.broadcasted_iota(jnp.int32, (1, LANES), 1)
    bnd_ref[...] = (jnp.where(lane < FOX_HEADS, f, 0.0) + _dot((q * q).astype(BF16), ind_ref[0])
                    + _dot((k * k).astype(BF16), ind_ref[1]))


def _inproj_even(x, g, w, bias, tm, batch, t_len, slot, n_slots, kv_prev):
    n = x.shape[0]
    row = lambda c: (pl.BlockSpec((tm, c), lambda i: (i, 0)), (n, c))
    col = (pl.BlockSpec((FOX_WIDTH, tm), lambda i: (0, i)), (FOX_WIDTH, n))
    common_in = [row(D_MODEL)[0], _const_spec((1, D_MODEL)), _const_spec((D_MODEL, E_END)), _const_spec((1, LANES))]
    tail = [(row(2 * ML_WIDTH), F32), (row(ML_WIDTH), BF16), (row(ML_WIDTH), BF16), (row(LANES), F32)]
    if t_len % tm != 0:
        outs = [(row(FOX_WIDTH), BF16), (row(FOX_WIDTH), F32), (row(FOX_WIDTH), F32)] + tail
        return pl.pallas_call(
            _inproj_even_rows_kernel,
            grid=(n // tm,),
            in_specs=common_in,
            out_specs=[spec for (spec, _), _ in outs],
            out_shape=[jax.ShapeDtypeStruct(shape, dt) for (_, shape), dt in outs],
            compiler_params=_cparams(("parallel",)),
            name="inproj_even_rows",
        )(x, g, w, bias)
    tpb = t_len // tm
    state = (pl.BlockSpec((1, 1, FOX_WIDTH, tm), lambda i: (slot, i // tpb, 0, i % tpb)),
             (n_slots, batch, FOX_WIDTH, t_len))
    outs = [(col, BF16), (state, F32), (state, F32), (row(2 * FOX_WIDTH), BF16), (col, BF16)] + tail + [(row(LANES), F32)]
    ind = np.zeros((2, FOX_WIDTH, LANES), np.float32)
    for c in range(FOX_WIDTH):
        ind[0, c, FOX_HEADS + c // FOX_HEAD_DIM] = 1.0
        ind[1, c, 2 * FOX_HEADS + c // FOX_HEAD_DIM] = 1.0
    ins = [x, g, w, bias, jnp.asarray(ind, BF16)]
    in_specs = common_in + [_const_spec(ind.shape)]
    aliases = {}
    if kv_prev is not None:
        ins += list(kv_prev)
        in_specs += [pl.BlockSpec(memory_space=pl.ANY)] * 2
        aliases = {5: 1, 6: 2}
    return pl.pallas_call(
        functools.partial(_inproj_even_seq_kernel, tpb=tpb, aliased=kv_prev is not None),
        grid=(n // tm,),
        in_specs=in_specs,
        out_specs=[spec for (spec, _), _ in outs],
        out_shape=[jax.ShapeDtypeStruct(shape, dt) for (_, shape), dt in outs],
        scratch_shapes=[pltpu.VMEM((1, LANES), F32)],
        input_output_aliases=aliases,
        compiler_params=_cparams(("arbitrary",)),
        name="inproj_even_seq",
    )(*ins)


def _inproj_odd_kernel(x_ref, g_ref, w_ref, wa2_ref, ba_ref, q_ref, k_ref, v_ref, gg_ref, la_ref):
    h = _rms(x_ref[...], g_ref[...]).astype(BF16)

    def seg(a, b):
        return _dot(h, w_ref[:, a:b])

    q_ref[...] = (seg(O_Q, O_K) * (GLA_DK ** -0.5)).astype(BF16)
    k_ref[...] = seg(O_K, O_V).astype(BF16)
    v_ref[...] = seg(O_V, O_G).astype(BF16)
    gg_ref[...] = seg(O_G, O_A).astype(BF16)
    ga = seg(O_A, O_END).astype(BF16)
    la_ref[...] = _log_sigmoid(_dot(ga, wa2_ref[...]) + ba_ref[...]) * (1.0 / GLA_TAU)


def _inproj_odd(x, g, w, wa2, ba, tm):
    n = x.shape[0]
    row = lambda c: pl.BlockSpec((tm, c), lambda i: (i, 0))
    outs = [(GLA_KW, BF16), (GLA_KW, BF16), (GLA_VW, BF16), (GLA_VW, BF16), (GLA_KW, F32)]
    return pl.pallas_call(
        _inproj_odd_kernel,
        grid=(n // tm,),
        in_specs=[row(D_MODEL), _const_spec((1, D_MODEL)), _const_spec((D_MODEL, O_END)),
                  _const_spec((LANES, GLA_KW)), _const_spec((1, GLA_KW))],
        out_specs=[row(c) for c, _ in outs],
        out_shape=[jax.ShapeDtypeStruct((n, c), dt) for c, dt in outs],
        compiler_params=_cparams(("parallel",)),
        name="inproj_odd",
    )(x, g, w, wa2, ba)


def _post_kernel(*refs, n_mix, final):
    x_ref = refs[0]
    mix_refs = refs[1:1 + n_mix]
    wo_ref, gf_ref, w1_ref, w2_ref = refs[1 + n_mix:5 + n_mix]
    rest = refs[5 + n_mix:]
    mix = mix_refs[0][...] if n_mix == 1 else jnp.concatenate([r[...] for r in mix_refs], axis=1)
    x1 = x_ref[...] + _dot(mix, wo_ref[...])
    h = _rms(x1, gf_ref[...]).astype(BF16)
    y = x1
    for c in range(D_FF // D_MODEL):
        sl = slice(c * D_MODEL, (c + 1) * D_MODEL)
        t = jnp.maximum(_dot(h, w1_ref[:, sl]), 0.0)
        y = y + _dot((t * t).astype(BF16), w2_ref[sl, :])
    if final:
        gfin_ref, out_ref = rest
        out_ref[...] = _rms(y, gfin_ref[...])
    else:
        (out_ref,) = rest
        out_ref[...] = y


def _post(x, mixes, wo, gf, w1, w2, gfin, tm):
    n = x.shape[0]
    row = lambda c: pl.BlockSpec((tm, c), lambda i: (i, 0))
    final = gfin is not None
    ins = [x, *mixes, wo, gf, w1, w2]
    specs = [row(D_MODEL)] + [row(m.shape[1]) for m in mixes] + [
        _const_spec((D_MODEL, D_MODEL)), _const_spec((1, D_MODEL)),
        _const_spec((D_MODEL, D_FF)), _const_spec((D_FF, D_MODEL))]
    if final:
        ins.append(gfin)
        specs.append(_const_spec((1, D_MODEL)))
    return pl.pallas_call(
        functools.partial(_post_kernel, n_mix=len(mixes), final=final),
        grid=(n // tm,),
        in_specs=specs,
        out_specs=row(D_MODEL),
        out_shape=jax.ShapeDtypeStruct((n, D_MODEL), F32),
        compiler_params=_cparams(("parallel",)),
        name="post_final" if final else "post",
    )(*ins)


F_TERMS = 3


def _append_f_terms(lf, carry, k, ka_ref):
    tc = lf.shape[0]
    f = _mat_f32(_tri(tc, True), _split3(lf)) + carry[...]
    carry[...] = f[tc - 1:tc, :]
    hi, mid, lo = [p.astype(F32) for p in _split3(f * LOG2E)]
    lane = lax.broadcasted_iota(jnp.int32, (1, LANES), 1)
    cols = jnp.where(lane < FOX_HEADS, hi,
                     jnp.where(lane < 2 * FOX_HEADS, pltpu.roll(mid, FOX_HEADS, axis=1),
                               jnp.where(lane < 3 * FOX_HEADS, pltpu.roll(lo, 2 * FOX_HEADS, axis=1), 0.0)))
    cols = cols.astype(BF16)
    for hp in range(FOX_HEADS // 2):
        ka_ref[:, 2 * hp * LANES:(2 * hp + 1) * LANES] = k[:, hp * LANES:(hp + 1) * LANES]
        ka_ref[:, (2 * hp + 1) * LANES:(2 * hp + 2) * LANES] = cols
    return f


def _fox_decode_kernel(q_ref, kc_ref, vc_ref, kn_ref, vn_ref, lf_ref, o_ref, *, t_new, past):
    nkeys = past + LANES
    lane_f = lax.broadcasted_iota(jnp.int32, (1, FOX_WIDTH), 1)
    head_mask = [jnp.where((lane_f >= h * FOX_HEAD_DIM) & (lane_f < (h + 1) * FOX_HEAD_DIM), 1.0, 0.0)
                 for h in range(FOX_HEADS)]
    q = q_ref[0].astype(F32)
    qb = jnp.concatenate([q * hm for hm in head_mask], axis=0).astype(BF16)

    def keys(cache_ref, new_ref):
        return jnp.concatenate([cache_ref[0, 0].astype(BF16), new_ref[0]], axis=1)

    s = _dot(qb, keys(kc_ref, kn_ref))
    lf = lf_ref[0]
    triu = _tri(LANES, False)
    carry = jnp.zeros((lf.shape[0], 1), F32)
    blocks = []
    for c in range(nkeys // LANES):
        cs = _f32_mat(_split3(lf[:, c * LANES:(c + 1) * LANES]), triu) + carry
        carry = cs[:, LANES - 1:LANES]
        blocks.append(cs)
    f_all = jnp.concatenate(blocks, axis=1) * LOG2E
    s = s - jnp.concatenate([jnp.broadcast_to(f_all[h:h + 1, :], (t_new, nkeys)) for h in range(FOX_HEADS)], axis=0)
    kpos = lax.broadcasted_iota(jnp.int32, (1, nkeys), 1)
    qpos = past + (lax.broadcasted_iota(jnp.int32, (FOX_HEADS * t_new, 1), 0) & (t_new - 1))
    s = jnp.where(kpos <= qpos, s, NEG)
    p = jnp.exp2(s - jnp.max(s, axis=-1, keepdims=True))
    ob = _dot_nt(p.astype(BF16), keys(vc_ref, vn_ref)) / jnp.sum(p, axis=-1, keepdims=True)
    out = ob[0:t_new] * head_mask[0]
    for h in range(1, FOX_HEADS):
        out = out + ob[h * t_new:(h + 1) * t_new] * head_mask[h]
    o_ref[0] = out.astype(o_ref.dtype)


def _fox_decode(q, k_cache, v_cache, layer, k_new, v_new, lf_row):
    batch, t_new, _ = q.shape
    past = k_cache.shape[3]
    assert t_new & (t_new - 1) == 0 and t_new <= LANES and t_new % 16 == 0 and past % LANES == 0
    blk = lambda a: pl.BlockSpec((1,) + a.shape[1:], lambda b: (b, 0, 0))
    cache = pl.BlockSpec((1, 1, FOX_WIDTH, past), lambda b: (layer, b, 0, 0))
    args = (q, k_cache, v_cache, k_new, v_new, lf_row)
    return pl.pallas_call(
        functools.partial(_fox_decode_kernel, t_new=t_new, past=past),
        grid=(batch,),
        in_specs=[blk(q), cache, cache, blk(k_new), blk(v_new), blk(lf_row)],
        out_specs=blk(q),
        out_shape=jax.ShapeDtypeStruct(q.shape, BF16),
        compiler_params=_cparams(("parallel",)),
        name="fox_decode",
    )(*args)


ONES_ROWS = 16


def _fox_attn_kernel(first_ref, qt_ref, ka_ref, vt_ref, o_ref, s_scr, acc_scr, *, tq, tkc, past, n_diag):
    q_first = past + pl.program_id(2) * tq
    n_full = q_first // tkc
    drow = lax.broadcasted_iota(jnp.int32, (LANES, 1), 0)
    lane = lax.broadcasted_iota(jnp.int32, (1, 2 * tq), 1)
    lane_head = jnp.where(lane < tq, 0, 1)
    own_head = jnp.where(jnp.where(drow < FOX_HEAD_DIM, 0, 1) == lane_head, 1.0, 0.0).astype(BF16)
    qt = qt_ref[...]
    q2 = jnp.concatenate([qt, qt], axis=1) * own_head
    head = 2 * pl.program_id(1) + lane_head
    f_sel = ((drow & (FOX_HEADS - 1)) == head) & (drow < F_TERMS * FOX_HEADS)
    qa = jnp.concatenate([q2, jnp.where(f_sel, -1.0, 0.0).astype(BF16)], axis=0)
    ones = jnp.ones((ONES_ROWS, tkc), BF16)
    acc_scr[...] = jnp.zeros_like(acc_scr)

    qa_h = [qa[:, hh * tq:(hh + 1) * tq] for hh in range(2)]
    qpos = q_first + lax.broadcasted_iota(jnp.int32, (1, tq), 1)

    def produce(j, hh, buf):
        start = pl.multiple_of(j * tkc, tkc)
        s_scr[buf, hh] = _dot(ka_ref[pl.ds(start, tkc), :], qa_h[hh])

    def consume(j, hh, buf, m_prev, masked):
        start = pl.multiple_of(j * tkc, tkc)
        st = s_scr[buf, hh]
        if masked:
            kpos = start + lax.broadcasted_iota(jnp.int32, (tkc, 1), 0)
            st = jnp.where(kpos <= qpos, st, NEG)
        m_new = jnp.maximum(m_prev, jnp.max(st, axis=0, keepdims=True))
        alpha = jnp.exp2(m_prev - m_new)
        p = jnp.exp2(st - m_new).astype(BF16)
        rows = slice(hh * FOX_HEAD_DIM, (hh + 1) * FOX_HEAD_DIM)
        va = jnp.concatenate([vt_ref[rows, pl.ds(start, tkc)], ones], axis=0)
        acc_scr[hh] = alpha * acc_scr[hh] + _dot(va, p)
        return m_new

    nq = pl.num_programs(2)
    first = first_ref[(pl.program_id(0) * pl.num_programs(1) + pl.program_id(1)) * nq + pl.program_id(2)]
    for hh in range(2):
        produce(first, hh, 0)

    def body(i, ms):
        ms = list(ms)
        for step in range(2):
            for hh in range(2):
                produce(2 * i + step + 1, hh, 1 - step)
                ms[hh] = consume(2 * i + step, hh, step, ms[hh], False)
        return tuple(ms)

    m_init = jnp.full((1, tq), NEG, F32)
    ms = list(lax.fori_loop(first // 2, n_full // 2, body, (m_init, m_init)))
    for d in range(n_diag):
        for hh in range(2):
            if d + 1 < n_diag:
                produce(n_full + d + 1, hh, (d + 1) % 2)
            ms[hh] = consume(n_full + d, hh, d % 2, ms[hh], True)
    out = jnp.concatenate([acc_scr[hh, 0:FOX_HEAD_DIM] / acc_scr[hh, FOX_HEAD_DIM:FOX_HEAD_DIM + 1]
                           for hh in range(2)], axis=0)
    o_ref[...] = out.T.astype(o_ref.dtype)


SKIP_MARGIN = 40.0
NORM_SLACK = 1.02


def _fox_first_chunks(bnd, batch, t_len, tq, tkc):
    nq, nc, pairs = t_len // tq, t_len // tkc, FOX_HEADS // 2
    b3 = bnd.reshape(batch, t_len, LANES)
    f2 = b3[:, :, 0:FOX_HEADS] * LOG2E
    qn = jnp.sqrt(jnp.max(b3[:, :, FOX_HEADS:2 * FOX_HEADS].reshape(batch, nq, tq, FOX_HEADS), axis=2))
    kn = jnp.sqrt(jnp.max(b3[:, :, 2 * FOX_HEADS:3 * FOX_HEADS], axis=1))
    thr = 2.0 * NORM_SLACK * qn * kn[:, None, :] + SKIP_MARGIN
    decay = f2[:, tkc - 1::tkc, :][:, None, :, :] - f2[:, ::tq, :][:, :, None, :]
    ok = jnp.all((decay > thr[:, :, None, :]).reshape(batch, nq, nc, pairs, 2), axis=-1)
    before_tile = jnp.arange(nc)[None, :] < ((jnp.arange(nq) * tq) // tkc)[:, None]
    ok = ok & before_tile[None, :, :, None]
    lead = jnp.sum(jnp.cumprod(ok.astype(jnp.int32), axis=2), axis=2)
    return ((lead // 2) * 2).transpose(0, 2, 1).reshape(-1).astype(jnp.int32)


def _fox_attention(qt, ka, vt, bnd, batch, t_len, tq, tkc):
    past = 0
    nq = t_len // tq
    n_diag = max(1, tq // tkc)
    assert tq & (tq - 1) == 0 and tq % LANES == 0 and (tkc % tq == 0 or tq % tkc == 0) and t_len % tq == 0
    assert nq == 1 or tq % (2 * tkc) == 0
    assert t_len >= n_diag * tkc and t_len % tkc == 0
    pairs = FOX_HEADS // 2
    first = _fox_first_chunks(bnd, batch, t_len, tq, tkc)
    grid_spec = pltpu.PrefetchScalarGridSpec(
        num_scalar_prefetch=1,
        grid=(batch, pairs, nq),
        in_specs=[pl.BlockSpec((LANES, tq), lambda b, h, i, f: (h, b * nq + i)),
                  pl.BlockSpec((t_len, 2 * LANES), lambda b, h, i, f: (b, h)),
                  pl.BlockSpec((LANES, t_len), lambda b, h, i, f: (h, b))],
        out_specs=pl.BlockSpec((tq, LANES), lambda b, h, i, f: (b * nq + i, h)),
        scratch_shapes=[pltpu.VMEM((2, 2, tkc, tq), F32),
                        pltpu.VMEM((2, FOX_HEAD_DIM + ONES_ROWS, tq), F32)])
    return pl.pallas_call(
        functools.partial(_fox_attn_kernel, tq=tq, tkc=tkc, past=past, n_diag=n_diag),
        grid_spec=grid_spec,
        out_shape=jax.ShapeDtypeStruct((batch * t_len, FOX_WIDTH), BF16),
        compiler_params=_cparams(("parallel", "parallel", "arbitrary")),
        name="fox_attention",
    )(first, qt, ka, vt)


def _mlstm_kernel(mqk_ref, mv_ref, mo_ref, gc_ref, gr_ref, c0_ref, n0_ref, m0_ref, buf_ref,
                  cw_ref, cb_ref, gml_ref,
                  h_ref, c_out, n_out, m_out, buf_out,
                  c_scr, n_scr, m_scr, prev_scr, *, L, t_last, nc):
    ci = pl.program_id(1)
    rows = mqk_ref.shape[0]

    @pl.when(ci == 0)
    def _():
        c_scr[...] = c0_ref[0]
        n_scr[...] = n0_ref[0]
        m_scr[...] = m0_ref[0]
        prev_scr[...] = buf_ref[0]

    u = _pad_rows(mqk_ref[...], L)
    ext = jnp.concatenate([prev_scr[...], u], axis=0)
    y = cb_ref[...] + cw_ref[ML_CONV - 1:ML_CONV, :] * u
    for s in range(1, ML_CONV):
        y = y + cw_ref[ML_CONV - 1 - s:ML_CONV - s, :] * pltpu.roll(ext, s, axis=0)[8:8 + L]
    qk = y * _sigmoid(y)
    prev_scr[...] = u[L - 8:L]

    gc = _pad_rows(gc_ref[...], L)
    gr = gr_ref[0]
    b_col = _mat_f32(_tri(L, True), _split3(gc))
    b_row = _f32_mat(_split3(gr), _tri(L, False))
    v_all = _pad_rows(mv_ref[...], L)
    o_all = _pad_rows(mo_ref[...], L)
    t_idx = lax.broadcasted_iota(jnp.int32, (L, 1), 0)
    causal = lax.broadcasted_iota(jnp.int32, (1, L), 1) <= t_idx

    heads = []
    for h in range(ML_HEADS):
        hs = slice(h * ML_HEAD_DIM, (h + 1) * ML_HEAD_DIM)
        q32 = qk[:, hs]
        k32 = qk[:, ML_WIDTH + h * ML_HEAD_DIM:ML_WIDTH + (h + 1) * ML_HEAD_DIM] * (ML_HEAD_DIM ** -0.5)
        qb, kb = q32.astype(BF16), k32.astype(BF16)
        c_prev = c_scr[h]
        heads.append(dict(hs=hs, q32=q32, k32=k32, qb=qb, kb=kb, vb=v_all[:, hs], c_prev=c_prev,
                          s=_dot_nt(qb, kb), qc=_dot_nt(qb, c_prev.astype(BF16))))

    for h, hd in enumerate(heads):
        bc = b_col[:, G_MF + h:G_MF + h + 1]
        ic = gc[:, G_MI + h:G_MI + h + 1]
        br = b_row[G_MF + h:G_MF + h + 1, :]
        ir = gr[G_MI + h:G_MI + h + 1, :]
        m_prev = m_scr[h][:, 0:1]
        dmat = jnp.where(causal, bc - br + ir, NEG)
        inter = bc + m_prev
        m_t = jnp.maximum(inter, jnp.max(dmat, axis=-1, keepdims=True))
        w = jnp.exp(dmat - m_t)
        g = jnp.exp(inter - m_t)
        a = w * hd["s"]
        b_last = bc[t_last:t_last + 1, :]
        m_last = m_t[t_last:t_last + 1, :]
        w_end = jnp.where(t_idx <= t_last, jnp.exp(b_last - bc + ic - m_last), 0.0)
        hd.update(m_t=m_t, g=g, a=a, m_last=m_last, g_end=g[t_last:t_last + 1, :], w_end=w_end,
                  av=_dot(a.astype(BF16), hd["vb"]),
                  vk=_dot_tn((hd["vb"].astype(F32) * w_end).astype(BF16), hd["kb"]))

    for h, hd in enumerate(heads):
        hs, g, a, m_t = hd["hs"], hd["g"], hd["a"], hd["m_t"]
        n_prev = n_scr[h]
        num = g * hd["qc"] + hd["av"]
        den = g * jnp.sum(hd["q32"] * n_prev, axis=-1, keepdims=True) + jnp.sum(a, axis=-1, keepdims=True)
        hh = num / jnp.maximum(jnp.abs(den), jnp.exp(-m_t))
        c_scr[h] = hd["g_end"] * hd["c_prev"] + hd["vk"]
        n_scr[h] = hd["g_end"] * n_prev + jnp.sum(hd["k32"] * hd["w_end"], axis=0, keepdims=True)
        m_scr[h] = jnp.broadcast_to(hd["m_last"], (1, LANES))
        yh = hh * lax.rsqrt(jnp.mean(hh * hh, axis=-1, keepdims=True) + EPS) * gml_ref[:, hs]
        yh = yh * _sigmoid(o_all[:, hs].astype(F32))
        h_ref[:, hs] = yh[:rows].astype(h_ref.dtype)

    @pl.when(ci == nc - 1)
    def _():
        c_out[0] = c_scr[...]
        n_out[0] = n_scr[...]
        m_out[0] = m_scr[...]
        buf_out[0] = ext[t_last + 1:t_last + 9]


def _mlstm(mqk, mv, mo, gc, gr, c0, n0, m0, buf8, cw, cb, gml, batch, t_len):
    L = ML_CHUNK if t_len >= ML_CHUNK else LANES
    rows = min(L, t_len)
    nc = t_len // rows
    t_last = rows - 1
    assert t_len % rows == 0 and (nc == 1 or rows == L) and (t_last + 1) % 8 == 0
    tok = lambda c: pl.BlockSpec((rows, c), lambda b, i: (b * nc + i, 0))
    st = lambda *s: pl.BlockSpec((1,) + s, lambda b, i: (b,) + (0,) * len(s))
    shp_c = (ML_HEADS, ML_HEAD_DIM, ML_HEAD_DIM)
    shp_n = (ML_HEADS, 1, ML_HEAD_DIM)
    return pl.pallas_call(
        functools.partial(_mlstm_kernel, L=L, t_last=t_last, nc=nc),
        grid=(batch, nc),
        in_specs=[tok(2 * ML_WIDTH), tok(ML_WIDTH), tok(ML_WIDTH), tok(LANES),
                  pl.BlockSpec((1, 16, L), lambda b, i: (b, 0, i)),
                  st(*shp_c), st(*shp_n), st(*shp_n), st(8, 2 * ML_WIDTH),
                  _const_spec((ML_CONV, 2 * ML_WIDTH)), _const_spec((1, 2 * ML_WIDTH)), _const_spec((1, ML_WIDTH))],
        out_specs=[tok(ML_WIDTH), st(*shp_c), st(*shp_n), st(*shp_n), st(8, 2 * ML_WIDTH)],
        out_shape=[jax.ShapeDtypeStruct((batch * t_len, ML_WIDTH), BF16),
                   jax.ShapeDtypeStruct((batch,) + shp_c, F32),
                   jax.ShapeDtypeStruct((batch,) + shp_n, F32),
                   jax.ShapeDtypeStruct((batch,) + shp_n, F32),
                   jax.ShapeDtypeStruct((batch, 8, 2 * ML_WIDTH), F32)],
        scratch_shapes=[pltpu.VMEM(shp_c, F32), pltpu.VMEM(shp_n, F32), pltpu.VMEM(shp_n, F32),
                        pltpu.VMEM((8, 2 * ML_WIDTH), F32)],
        compiler_params=_cparams(("parallel", "arbitrary")),
        name="mlstm",
    )(mqk, mv, mo, gc, gr, c0, n0, m0, buf8, cw, cb, gml)


def _gla_level_matrices(L):
    mats = np.zeros((len(GLA_LEVELS), L, L), np.float32)
    for li, b in enumerate(GLA_LEVELS):
        for t in range(L):
            base = (t // (2 * b)) * 2 * b
            bound = base + b - 1
            if t > bound:
                mats[li, t, bound + 1:t + 1] = 1.0
            else:
                mats[li, t, t + 1:bound + 1] = 1.0
    return jnp.asarray(mats, BF16)


def _gla_kernel(q_ref, k_ref, v_ref, gg_ref, la_ref, s0_ref, lvl_ref, gn_ref, o_ref, s_out, s_scr, *, L, t_last, nc):
    ci = pl.program_id(1)
    nb, rows = q_ref.shape[0], q_ref.shape[1]

    @pl.when(ci == 0)
    def _():
        s_scr[...] = s0_ref[...]

    t_idx = lax.broadcasted_iota(jnp.int32, (L, 1), 0)
    s_idx = lax.broadcasted_iota(jnp.int32, (1, L), 1)
    tril = _tri(L, True)

    for bi in range(nb):
        la3 = _split2(_pad_rows(la_ref[bi], L))
        cb = _mat_f32(tril, la3)
        q_all = _pad_rows(q_ref[bi], L)
        k_all = _pad_rows(k_ref[bi], L)
        v_all = _pad_rows(v_ref[bi], L)
        g_all = _pad_rows(gg_ref[bi], L)

        a_heads = []
        q32s, k32s = [], []
        for h in range(GLA_HEADS):
            ks = slice(h * GLA_DK, (h + 1) * GLA_DK)
            q32s.append(q_all[:, ks].astype(F32))
            k32s.append(k_all[:, ks].astype(F32))
            a_heads.append(jnp.where(t_idx == s_idx, _dot_nt(q_all[:, ks], k_all[:, ks]), 0.0))
        e_next = jnp.exp(_mat_f32(lvl_ref[0], la3))
        pending = None
        for li, b in enumerate(GLA_LEVELS):
            e_all = e_next
            if li + 1 < len(GLA_LEVELS):
                e_next = jnp.exp(_mat_f32(lvl_ref[li + 1], la3))
            sh = b.bit_length() - 1
            second = (jnp.right_shift(t_idx, sh) & 1) == 1
            same_pair = jnp.right_shift(t_idx, sh + 1) == jnp.right_shift(s_idx, sh + 1)
            prods = []
            for h in range(GLA_HEADS):
                e = e_all[:, h * GLA_DK:(h + 1) * GLA_DK]
                qe = jnp.where(second, q32s[h] * e, 0.0).astype(BF16)
                ke = jnp.where(second, 0.0, k32s[h] * e).astype(BF16)
                prods.append(_dot_nt(qe, ke))
            if pending is not None:
                for h in range(GLA_HEADS):
                    a_heads[h] = a_heads[h] + jnp.where(pending[1], pending[0][h], 0.0)
            pending = (prods, same_pair)
        for h in range(GLA_HEADS):
            a_heads[h] = a_heads[h] + jnp.where(pending[1], pending[0][h], 0.0)

        outs = []
        for h in range(GLA_HEADS):
            ks = slice(h * GLA_DK, (h + 1) * GLA_DK)
            vs = slice(h * GLA_DV, (h + 1) * GLA_DV)
            cbh = cb[:, ks]
            vb = v_all[:, vs]
            s_t = s_scr[bi, h]
            o = (_dot_nt((q32s[h] * jnp.exp(cbh)).astype(BF16), s_t.astype(BF16))
                 + _dot(a_heads[h].astype(BF16), vb))
            cl = cbh[t_last:t_last + 1, :]
            kd = jnp.where(t_idx <= t_last, k32s[h] * jnp.exp(cl - cbh), 0.0).astype(BF16)
            s_scr[bi, h] = jnp.exp(cl) * s_t + _dot_tn(vb, kd)
            outs.append(o)
        for h in range(GLA_HEADS):
            vs = slice(h * GLA_DV, (h + 1) * GLA_DV)
            o = outs[h]
            y = o * lax.rsqrt(jnp.mean(o * o, axis=-1, keepdims=True) + EPS) * gn_ref[:, vs]
            gate = g_all[:, vs].astype(F32)
            y = y * (gate * _sigmoid(gate))
            o_ref[bi, :, vs] = y[:rows].astype(o_ref.dtype)

    @pl.when(ci == nc - 1)
    def _():
        s_out[...] = s_scr[...]


GLA_BATCH_PER_STEP = 2


def _gla(q, k, v, gg, la, s0t, gn, batch, t_len):
    L = GLA_CHUNK
    rows = min(L, t_len)
    nc = t_len // rows
    t_last = rows - 1
    nb = GLA_BATCH_PER_STEP
    assert t_len % rows == 0 and (nc == 1 or rows == L) and batch % nb == 0
    tok = lambda c: pl.BlockSpec((nb, rows, c), lambda b, i: (b, i, 0))
    shp_s = (GLA_HEADS, GLA_DV, GLA_DK)
    st = pl.BlockSpec((nb,) + shp_s, lambda b, i: (b, 0, 0, 0))
    levels = _gla_level_matrices(L)
    seq = lambda a: a.reshape(batch, t_len, a.shape[-1])
    o, s_new = pl.pallas_call(
        functools.partial(_gla_kernel, L=L, t_last=t_last, nc=nc),
        grid=(batch // nb, nc),
        in_specs=[tok(GLA_KW), tok(GLA_KW), tok(GLA_VW), tok(GLA_VW), tok(GLA_KW), st,
                  _const_spec(levels.shape), _const_spec((1, GLA_VW))],
        out_specs=[tok(GLA_VW), st],
        out_shape=[jax.ShapeDtypeStruct((batch, t_len, GLA_VW), BF16),
                   jax.ShapeDtypeStruct((batch,) + shp_s, F32)],
        scratch_shapes=[pltpu.VMEM((nb,) + shp_s, F32)],
        compiler_params=_cparams(("parallel", "arbitrary")),
        name="gla",
    )(seq(q), seq(k), seq(v), seq(gg), seq(la), s0t, levels, gn)
    return o.reshape(batch * t_len, GLA_VW), s_new


def _pack_even(w, b_fox_f, b_i, b_f):
    d = w.shape[0]
    o = np.cumsum((0, FOX_WIDTH, FOX_WIDTH, FOX_WIDTH, FOX_HEADS, 2 * ML_WIDTH, ML_WIDTH, ML_WIDTH, ML_HEADS, ML_HEADS))
    n_gate = FOX_HEADS + 2 * ML_HEADS
    wp = jnp.concatenate([w[:, o[0]:o[3]], w[:, o[4]:o[7]], w[:, o[3]:o[4]], w[:, o[7]:o[9]],
                          jnp.zeros((d, LANES - n_gate), w.dtype)], axis=1).astype(BF16)
    bias = jnp.concatenate([b_fox_f, b_i, b_f, jnp.zeros((LANES - n_gate,), F32)]).reshape(1, LANES)
    return wp, bias


def _pack_odd(w, w_a2):
    d = w.shape[0]
    wp = jnp.concatenate([w, jnp.zeros((d, LANES - GLA_RANK), w.dtype)], axis=1).astype(BF16)
    wa2 = jnp.concatenate([w_a2, jnp.zeros((LANES - GLA_RANK, w_a2.shape[1]), w_a2.dtype)], axis=0).astype(BF16)
    return wp, wa2


def _gate_rows(gc, batch, t_len, t_pad):
    g = gc.reshape(batch, t_len, LANES)[:, :, :16].transpose(0, 2, 1)
    if t_pad > t_len:
        g = jnp.pad(g, ((0, 0), (0, 0), (0, t_pad - t_len)))
    return g


def _trunk(x, fox_k, fox_v, fox_lf, ml_c, ml_n, ml_m, ml_buf, gla_s, params):
    (norm_mix, norm_ffn, norm_final, w_even, bias_even, conv_w, conv_b, g_ml,
     w_odd, w_a2, b_a, g_gla, w_out, w_ff1, w_ff2) = params
    batch, t_len, d = x.shape
    past = fox_k.shape[2]
    n = batch * t_len
    tm = 512 if n % 512 == 0 else 256 if n % 256 == 0 else n
    depth = norm_mix.shape[0]
    xf = x.reshape(n, d)
    ev_states, odd_states = [], []
    kv_stacked = None
    y = None
    for layer in range(depth):
        j = layer // 2
        if layer % 2 == 0:
            assert (t_len % tm == 0) == (past == 0)
            proj = _inproj_even(xf, norm_mix[layer][None], w_even[j], bias_even[j],
                                tm, batch, t_len, j, (depth + 1) // 2, kv_stacked)
            if past == 0:
                qt, kf, vf, ka, vt, mqk, mv, mo, gc, bnd = proj
                kv_stacked = (kf, vf)
                attn = _fox_attention(qt, ka, vt, bnd, batch, t_len, min(512, t_len), 256)
            else:
                q, kf, vf, mqk, mv, mo, gc = proj
                lf_new = gc.reshape(batch, t_len, LANES)[:, :, :FOX_HEADS]
                lf_row = jnp.concatenate([fox_lf[j], lf_new, jnp.zeros((batch, LANES - t_len, FOX_HEADS), F32)], axis=1)
                lf_row = jnp.pad(lf_row.transpose(0, 2, 1), ((0, 0), (0, 16 - FOX_HEADS), (0, 0)))
                cache_t = lambda a: a.transpose(0, 1, 3, 4, 2).reshape(a.shape[0], batch, FOX_WIDTH, past)
                new_t = lambda a: jnp.pad(a.reshape(batch, t_len, FOX_WIDTH).transpose(0, 2, 1).astype(BF16),
                                          ((0, 0), (0, 0), (0, LANES - t_len)))
                attn = _fox_decode(q.reshape(batch, t_len, FOX_WIDTH), cache_t(fox_k), cache_t(fox_v), j,
                                   new_t(kf), new_t(vf), lf_row).reshape(n, FOX_WIDTH)
            l_ml = ML_CHUNK if t_len >= ML_CHUNK else LANES
            gr = _gate_rows(gc, batch, t_len, max(t_len, l_ml))
            c0 = ml_c[j]
            n0 = ml_n[j][:, :, None, :]
            m0 = jnp.broadcast_to(ml_m[j][:, :, None, None], (batch, ML_HEADS, 1, LANES))
            buf8 = jnp.pad(ml_buf[j], ((0, 0), (8 - (ML_CONV - 1), 0), (0, 0)))
            h_ml, c_new, n_new, m_new, buf_new = _mlstm(mqk, mv, mo, gc, gr, c0, n0, m0, buf8,
                                                        conv_w[j], conv_b[j][None], g_ml[j][None], batch, t_len)
            heads = lambda a: None if past == 0 else a.reshape(batch, t_len, FOX_HEADS, FOX_HEAD_DIM)
            ev_states.append((heads(kf), heads(vf),
                              gc[:, :FOX_HEADS].reshape(batch, t_len, FOX_HEADS),
                              c_new, n_new[:, :, 0, :], m_new[:, :, 0, 0], buf_new[:, 8 - (ML_CONV - 1):, :]))
            mixes = [attn, h_ml]
        else:
            q, k, v, gg, la = _inproj_odd(xf, norm_mix[layer][None], w_odd[j], w_a2[j], b_a[j][None], tm)
            s0t = gla_s[j].transpose(0, 1, 3, 2)
            o, s_new = _gla(q, k, v, gg, la, s0t, g_gla[j][None], batch, t_len)
            odd_states.append(s_new.transpose(0, 1, 3, 2))
            mixes = [o]
        last = layer == depth - 1
        out = _post(xf, mixes, w_out[layer], norm_ffn[layer][None], w_ff1[layer], w_ff2[layer],
                    norm_final[None] if last else None, tm)
        if last:
            y = out
        else:
            xf = out
    ev = [jnp.stack([s[i] for s in ev_states]) for i in range(2 if kv_stacked else 0, 7)]
    if kv_stacked:
        ev = [a.reshape(a.shape[0], batch, FOX_HEADS, FOX_HEAD_DIM, t_len).transpose(0, 1, 4, 2, 3)
              for a in kv_stacked] + ev
    return y.reshape(batch, t_len, d), ev, jnp.stack(odd_states)


def kernel(x_prompt, x_sample, cache_fox_k, cache_fox_v, cache_fox_logf, state_mlstm_c, state_mlstm_n, state_mlstm_m, state_mlstm_conv, state_gla_s, norm_mix, norm_ffn, norm_final, w_in_even, b_fox_f, conv_w_ml, conv_b_ml, b_ml_i, b_ml_f, g_ml, w_in_odd, w_gla_a2, b_gla_a, g_gla, w_out, w_ff1, w_ff2):
    n_even, n_odd = w_in_even.shape[0], w_in_odd.shape[0]
    packed_even = [_pack_even(w_in_even[j], b_fox_f[j], b_ml_i[j], b_ml_f[j]) for j in range(n_even)]
    packed_odd = [_pack_odd(w_in_odd[j], w_gla_a2[j]) for j in range(n_odd)]
    params = (norm_mix, norm_ffn, norm_final,
              [p[0] for p in packed_even], [p[1] for p in packed_even], conv_w_ml, conv_b_ml, g_ml,
              [p[0] for p in packed_odd], [p[1] for p in packed_odd], b_gla_a, g_gla,
              w_out.astype(BF16), w_ff1.astype(BF16), w_ff2.astype(BF16))

    bp = x_prompt.shape[0]
    dt = x_prompt.dtype
    zeros = lambda *s: jnp.zeros(s, dt)
    y_p, ev_p, gla_p = _trunk(
        x_prompt,
        zeros(n_even, bp, 0, FOX_HEADS, FOX_HEAD_DIM), zeros(n_even, bp, 0, FOX_HEADS, FOX_HEAD_DIM),
        zeros(n_even, bp, 0, FOX_HEADS),
        zeros(n_even, bp, ML_HEADS, ML_HEAD_DIM, ML_HEAD_DIM), zeros(n_even, bp, ML_HEADS, ML_HEAD_DIM),
        zeros(n_even, bp, ML_HEADS), zeros(n_even, bp, ML_CONV - 1, 2 * ML_WIDTH),
        zeros(n_odd, bp, GLA_HEADS, GLA_DK, GLA_DV), params)
    y_s, ev_s, gla_s = _trunk(x_sample, cache_fox_k, cache_fox_v, cache_fox_logf, state_mlstm_c, state_mlstm_n,
                              state_mlstm_m, state_mlstm_conv, state_gla_s, params)
    return (y_p, y_s, *ev_p, gla_p, *ev_s, gla_s)
```

```python
import functools

import numpy as np
import jax
import jax.numpy as jnp
from jax import lax
from jax.experimental import pallas as pl
from jax.experimental.pallas import tpu as pltpu

F32 = jnp.float32
BF16 = jnp.bfloat16
EPS = 1e-6
NEG = -1e30
LOG2E = 1.4426950408889634

LANES = 128
VMEM_LIMIT = 56 * 1024 * 1024

D_MODEL = 1024
D_FF = 4 * D_MODEL
FOX_HEADS, FOX_HEAD_DIM = 8, 64
FOX_WIDTH = FOX_HEADS * FOX_HEAD_DIM
ML_HEADS, ML_HEAD_DIM = 4, 128
ML_WIDTH = ML_HEADS * ML_HEAD_DIM
ML_CONV = 4
GLA_HEADS, GLA_DK, GLA_DV = 4, 128, 256
GLA_KW = GLA_HEADS * GLA_DK
GLA_VW = GLA_HEADS * GLA_DV
GLA_RANK = 16
GLA_TAU = 16.0

E_Q, E_K, E_V, E_MQK, E_MV, E_MO, E_G, E_END = 0, 512, 1024, 1536, 2560, 3072, 3584, 3712
G_FOX, G_MI, G_MF = 0, 8, 12
O_Q, O_K, O_V, O_G, O_A, O_END = 0, 512, 1024, 2048, 3072, 3200

ML_CHUNK = 256
GLA_CHUNK = 128
GLA_LEVELS = (64, 32, 16, 8, 4, 2, 1)


def _cparams(sem):
    return pltpu.CompilerParams(dimension_semantics=sem, vmem_limit_bytes=VMEM_LIMIT)


def _const_spec(shape):
    nd = len(shape)
    return pl.BlockSpec(shape, lambda *_: (0,) * nd, pipeline_mode=pl.Buffered(1))


def _rms(x, g):
    return x * lax.rsqrt(jnp.mean(x * x, axis=-1, keepdims=True) + EPS) * g


def _sigmoid(x):
    return 1.0 / (1.0 + jnp.exp(-x))


def _log_sigmoid(x):
    return -(jnp.maximum(-x, 0.0) + jnp.log1p(jnp.exp(-jnp.abs(x))))


def _dot(a, b):
    return jnp.dot(a, b, preferred_element_type=F32)


def _dot_nt(a, b):
    return lax.dot_general(a, b, (((1,), (1,)), ((), ())), preferred_element_type=F32)


def _dot_tn(a, b):
    return lax.dot_general(a, b, (((0,), (0,)), ((), ())), preferred_element_type=F32)


def _split3(x):
    hi = x.astype(BF16)
    r1 = x - hi.astype(F32)
    mid = r1.astype(BF16)
    lo = (r1 - mid.astype(F32)).astype(BF16)
    return hi, mid, lo


def _split2(x):
    hi = x.astype(BF16)
    return hi, (x - hi.astype(F32)).astype(BF16)


def _mat_f32(m, parts):
    return functools.reduce(lambda a, b: a + b, [_dot(m, p) for p in parts])


def _f32_mat(parts, m):
    return functools.reduce(lambda a, b: a + b, [_dot(p, m) for p in parts])


def _tri(n, lower):
    r = lax.broadcasted_iota(jnp.int32, (n, n), 0)
    c = lax.broadcasted_iota(jnp.int32, (n, n), 1)
    keep = (c <= r) if lower else (r <= c)
    return jnp.where(keep, 1.0, 0.0).astype(BF16)


def _pad_rows(a, n):
    if a.shape[0] == n:
        return a
    return jnp.concatenate([a, jnp.zeros((n - a.shape[0], a.shape[1]), a.dtype)], axis=0)


def _even_project(x_ref, g_ref, w_ref, bias_ref, mqk_ref, mv_ref, mo_ref, gc_ref):
    h = _rms(x_ref[...], g_ref[...]).astype(BF16)

    def seg(a, b):
        return _dot(h, w_ref[:, a:b])

    q = seg(E_Q, E_K) * (FOX_HEAD_DIM ** -0.5 * LOG2E)
    k = seg(E_K, E_V)
    v = seg(E_V, E_MQK)
    mqk_ref[...] = seg(E_MQK, E_MV)
    mv_ref[...] = seg(E_MV, E_MO).astype(BF16)
    mo_ref[...] = seg(E_MO, E_G).astype(BF16)
    gz = seg(E_G, E_END) + bias_ref[...]
    lane = lax.broadcasted_iota(jnp.int32, gz.shape, 1)
    is_log = (lane < G_MI) | (lane >= G_MF)
    gates = jnp.where(is_log, _log_sigmoid(gz), gz)
    gc_ref[...] = gates
    return q, k, v, gates


def _inproj_even_rows_kernel(x_ref, g_ref, w_ref, bias_ref,
                             q_ref, kf_ref, vf_ref, mqk_ref, mv_ref, mo_ref, gc_ref):
    q, k, v, _ = _even_project(x_ref, g_ref, w_ref, bias_ref, mqk_ref, mv_ref, mo_ref, gc_ref)
    q_ref[...] = q.astype(BF16)
    kf_ref[...] = k
    vf_ref[...] = v


def _inproj_even_seq_kernel(*refs, tpb, aliased):
    x_ref, g_ref, w_ref, bias_ref, ind_ref = refs[:5]
    refs = refs[5 + (2 if aliased else 0):]
    qt_ref, kf_ref, vf_ref, ka_ref, vt_ref, mqk_ref, mv_ref, mo_ref, gc_ref, bnd_ref, carry = refs
    q, k, v, gates = _even_project(x_ref, g_ref, w_ref, bias_ref, mqk_ref, mv_ref, mo_ref, gc_ref)
    qt_ref[...] = q.T.astype(BF16)
    v_t = v.T
    vt_ref[...] = v_t.astype(BF16)
    kf_ref[0, 0] = k.T
    vf_ref[0, 0] = v_t

    @pl.when(pl.program_id(0) % tpb == 0)
    def _():
        carry[...] = jnp.zeros_like(carry)

    f = _append_f_terms(gates, carry, k.astype(BF16), ka_ref)
    lane = lax.broadcasted_iota(jnp.int32, (1, LANES), 1)
    bnd_ref[...] = (jnp.where(lane < FOX_HEADS, f, 0.0) + _dot((q * q).astype(BF16), ind_ref[0])
                    + _dot((k * k).astype(BF16), ind_ref[1]))


def _inproj_even(x, g, w, bias, tm, batch, t_len, slot, n_slots, kv_prev):
    n = x.shape[0]
    row = lambda c: (pl.BlockSpec((tm, c), lambda i: (i, 0)), (n, c))
    col = (pl.BlockSpec((FOX_WIDTH, tm), lambda i: (0, i)), (FOX_WIDTH, n))
    common_in = [row(D_MODEL)[0], _const_spec((1, D_MODEL)), _const_spec((D_MODEL, E_END)), _const_spec((1, LANES))]
    tail = [(row(2 * ML_WIDTH), F32), (row(ML_WIDTH), BF16), (row(ML_WIDTH), BF16), (row(LANES), F32)]
    if t_len % tm != 0:
        outs = [(row(FOX_WIDTH), BF16), (row(FOX_WIDTH), F32), (row(FOX_WIDTH), F32)] + tail
        return pl.pallas_call(
            _inproj_even_rows_kernel,
            grid=(n // tm,),
            in_specs=common_in,
            out_specs=[spec for (spec, _), _ in outs],
            out_shape=[jax.ShapeDtypeStruct(shape, dt) for (_, shape), dt in outs],
            compiler_params=_cparams(("parallel",)),
            name="inproj_even_rows",
        )(x, g, w, bias)
    tpb = t_len // tm
    state = (pl.BlockSpec((1, 1, FOX_WIDTH, tm), lambda i: (slot, i // tpb, 0, i % tpb)),
             (n_slots, batch, FOX_WIDTH, t_len))
    outs = [(col, BF16), (state, F32), (state, F32), (row(2 * FOX_WIDTH), BF16), (col, BF16)] + tail + [(row(LANES), F32)]
    ind = np.zeros((2, FOX_WIDTH, LANES), np.float32)
    for c in range(FOX_WIDTH):
        ind[0, c, FOX_HEADS + c // FOX_HEAD_DIM] = 1.0
        ind[1, c, 2 * FOX_HEADS + c // FOX_HEAD_DIM] = 1.0
    ins = [x, g, w, bias, jnp.asarray(ind, BF16)]
    in_specs = common_in + [_const_spec(ind.shape)]
    aliases = {}
    if kv_prev is not None:
        ins += list(kv_prev)
        in_specs += [pl.BlockSpec(memory_space=pl.ANY)] * 2
        aliases = {5: 1, 6: 2}
    return pl.pallas_call(
        functools.partial(_inproj_even_seq_kernel, tpb=tpb, aliased=kv_prev is not None),
        grid=(n // tm,),
        in_specs=in_specs,
        out_specs=[spec for (spec, _), _ in outs],
        out_shape=[jax.ShapeDtypeStruct(shape, dt) for (_, shape), dt in outs],
        scratch_shapes=[pltpu.VMEM((1, LANES), F32)],
        input_output_aliases=aliases,
        compiler_params=_cparams(("arbitrary",)),
        name="inproj_even_seq",
    )(*ins)


def _inproj_odd_kernel(x_ref, g_ref, w_ref, wa2_ref, ba_ref, q_ref, k_ref, v_ref, gg_ref, la_ref):
    h = _rms(x_ref[...], g_ref[...]).astype(BF16)

    def seg(a, b):
        return _dot(h, w_ref[:, a:b])

    q_ref[...] = (seg(O_Q, O_K) * (GLA_DK ** -0.5)).astype(BF16)
    k_ref[...] = seg(O_K, O_V).astype(BF16)
    v_ref[...] = seg(O_V, O_G).astype(BF16)
    gg_ref[...] = seg(O_G, O_A).astype(BF16)
    ga = seg(O_A, O_END).astype(BF16)
    la_ref[...] = _log_sigmoid(_dot(ga, wa2_ref[...]) + ba_ref[...]) * (1.0 / GLA_TAU)


def _inproj_odd(x, g, w, wa2, ba, tm):
    n = x.shape[0]
    row = lambda c: pl.BlockSpec((tm, c), lambda i: (i, 0))
    outs = [(GLA_KW, BF16), (GLA_KW, BF16), (GLA_VW, BF16), (GLA_VW, BF16), (GLA_KW, F32)]
    return pl.pallas_call(
        _inproj_odd_kernel,
        grid=(n // tm,),
        in_specs=[row(D_MODEL), _const_spec((1, D_MODEL)), _const_spec((D_MODEL, O_END)),
                  _const_spec((LANES, GLA_KW)), _const_spec((1, GLA_KW))],
        out_specs=[row(c) for c, _ in outs],
        out_shape=[jax.ShapeDtypeStruct((n, c), dt) for c, dt in outs],
        compiler_params=_cparams(("parallel",)),
        name="inproj_odd",
    )(x, g, w, wa2, ba)


def _post_kernel(*refs, n_mix, final):
    x_ref = refs[0]
    mix_refs = refs[1:1 + n_mix]
    wo_ref, gf_ref, w1_ref, w2_ref = refs[1 + n_mix:5 + n_mix]
    rest = refs[5 + n_mix:]
    mix = mix_refs[0][...] if n_mix == 1 else jnp.concatenate([r[...] for r in mix_refs], axis=1)
    x1 = x_ref[...] + _dot(mix, wo_ref[...])
    h = _rms(x1, gf_ref[...]).astype(BF16)
    y = x1
    for c in range(D_FF // D_MODEL):
        sl = slice(c * D_MODEL, (c + 1) * D_MODEL)
        t = jnp.maximum(_dot(h, w1_ref[:, sl]), 0.0)
        y = y + _dot((t * t).astype(BF16), w2_ref[sl, :])
    if final:
        gfin_ref, out_ref = rest
        out_ref[...] = _rms(y, gfin_ref[...])
    else:
        (out_ref,) = rest
        out_ref[...] = y


def _post(x, mixes, wo, gf, w1, w2, gfin, tm):
    n = x.shape[0]
    row = lambda c: pl.BlockSpec((tm, c), lambda i: (i, 0))
    final = gfin is not None
    ins = [x, *mixes, wo, gf, w1, w2]
    specs = [row(D_MODEL)] + [row(m.shape[1]) for m in mixes] + [
        _const_spec((D_MODEL, D_MODEL)), _const_spec((1, D_MODEL)),
        _const_spec((D_MODEL, D_FF)), _const_spec((D_FF, D_MODEL))]
    if final:
        ins.append(gfin)
        specs.append(_const_spec((1, D_MODEL)))
    return pl.pallas_call(
        functools.partial(_post_kernel, n_mix=len(mixes), final=final),
        grid=(n // tm,),
        in_specs=specs,
        out_specs=row(D_MODEL),
        out_shape=jax.ShapeDtypeStruct((n, D_MODEL), F32),
        compiler_params=_cparams(("parallel",)),
        name="post_final" if final else "post",
    )(*ins)


F_TERMS = 3


def _append_f_terms(lf, carry, k, ka_ref):
    tc = lf.shape[0]
    f = _mat_f32(_tri(tc, True), _split3(lf)) + carry[...]
    carry[...] = f[tc - 1:tc, :]
    hi, mid, lo = [p.astype(F32) for p in _split3(f * LOG2E)]
    lane = lax.broadcasted_iota(jnp.int32, (1, LANES), 1)
    cols = jnp.where(lane < FOX_HEADS, hi,
                     jnp.where(lane < 2 * FOX_HEADS, pltpu.roll(mid, FOX_HEADS, axis=1),
                               jnp.where(lane < 3 * FOX_HEADS, pltpu.roll(lo, 2 * FOX_HEADS, axis=1), 0.0)))
    cols = cols.astype(BF16)
    for hp in range(FOX_HEADS // 2):
        ka_ref[:, 2 * hp * LANES:(2 * hp + 1) * LANES] = k[:, hp * LANES:(hp + 1) * LANES]
        ka_ref[:, (2 * hp + 1) * LANES:(2 * hp + 2) * LANES] = cols
    return f


def _fox_decode_kernel(q_ref, kc_ref, vc_ref, kn_ref, vn_ref, lf_ref, o_ref, *, t_new, past):
    nkeys = past + LANES
    lane_f = lax.broadcasted_iota(jnp.int32, (1, FOX_WIDTH), 1)
    head_mask = [jnp.where((lane_f >= h * FOX_HEAD_DIM) & (lane_f < (h + 1) * FOX_HEAD_DIM), 1.0, 0.0)
                 for h in range(FOX_HEADS)]
    q = q_ref[0].astype(F32)
    qb = jnp.concatenate([q * hm for hm in head_mask], axis=0).astype(BF16)

    def keys(cache_ref, new_ref):
        return jnp.concatenate([cache_ref[0, 0].astype(BF16), new_ref[0]], axis=1)

    s = _dot(qb, keys(kc_ref, kn_ref))
    lf = lf_ref[0]
    triu = _tri(LANES, False)
    carry = jnp.zeros((lf.shape[0], 1), F32)
    blocks = []
    for c in range(nkeys // LANES):
        cs = _f32_mat(_split3(lf[:, c * LANES:(c + 1) * LANES]), triu) + carry
        carry = cs[:, LANES - 1:LANES]
        blocks.append(cs)
    f_all = jnp.concatenate(blocks, axis=1) * LOG2E
    s = s - jnp.concatenate([jnp.broadcast_to(f_all[h:h + 1, :], (t_new, nkeys)) for h in range(FOX_HEADS)], axis=0)
    kpos = lax.broadcasted_iota(jnp.int32, (1, nkeys), 1)
    qpos = past + (lax.broadcasted_iota(jnp.int32, (FOX_HEADS * t_new, 1), 0) & (t_new - 1))
    s = jnp.where(kpos <= qpos, s, NEG)
    p = jnp.exp2(s - jnp.max(s, axis=-1, keepdims=True))
    ob = _dot_nt(p.astype(BF16), keys(vc_ref, vn_ref)) / jnp.sum(p, axis=-1, keepdims=True)
    out = ob[0:t_new] * head_mask[0]
    for h in range(1, FOX_HEADS):
        out = out + ob[h * t_new:(h + 1) * t_new] * head_mask[h]
    o_ref[0] = out.astype(o_ref.dtype)


def _fox_decode(q, k_cache, v_cache, layer, k_new, v_new, lf_row):
    batch, t_new, _ = q.shape
    past = k_cache.shape[3]
    assert t_new & (t_new - 1) == 0 and t_new <= LANES and t_new % 16 == 0 and past % LANES == 0
    blk = lambda a: pl.BlockSpec((1,) + a.shape[1:], lambda b: (b, 0, 0))
    cache = pl.BlockSpec((1, 1, FOX_WIDTH, past), lambda b: (layer, b, 0, 0))
    args = (q, k_cache, v_cache, k_new, v_new, lf_row)
    return pl.pallas_call(
        functools.partial(_fox_decode_kernel, t_new=t_new, past=past),
        grid=(batch,),
        in_specs=[blk(q), cache, cache, blk(k_new), blk(v_new), blk(lf_row)],
        out_specs=blk(q),
        out_shape=jax.ShapeDtypeStruct(q.shape, BF16),
        compiler_params=_cparams(("parallel",)),
        name="fox_decode",
    )(*args)


ONES_ROWS = 16


def _fox_attn_kernel(first_ref, qt_ref, ka_ref, vt_ref, o_ref, s_scr, acc_scr, *, tq, tkc, past, n_diag):
    q_first = past + pl.program_id(2) * tq
    n_full = q_first // tkc
    drow = lax.broadcasted_iota(jnp.int32, (LANES, 1), 0)
    lane = lax.broadcasted_iota(jnp.int32, (1, 2 * tq), 1)
    lane_head = jnp.where(lane < tq, 0, 1)
    own_head = jnp.where(jnp.where(drow < FOX_HEAD_DIM, 0, 1) == lane_head, 1.0, 0.0).astype(BF16)
    qt = qt_ref[...]
    q2 = jnp.concatenate([qt, qt], axis=1) * own_head
    head = 2 * pl.program_id(1) + lane_head
    f_sel = ((drow & (FOX_HEADS - 1)) == head) & (drow < F_TERMS * FOX_HEADS)
    qa = jnp.concatenate([q2, jnp.where(f_sel, -1.0, 0.0).astype(BF16)], axis=0)
    ones = jnp.ones((ONES_ROWS, tkc), BF16)
    acc_scr[...] = jnp.zeros_like(acc_scr)

    qa_h = [qa[:, hh * tq:(hh + 1) * tq] for hh in range(2)]
    qpos = q_first + lax.broadcasted_iota(jnp.int32, (1, tq), 1)

    def produce(j, hh, buf):
        start = pl.multiple_of(j * tkc, tkc)
        s_scr[buf, hh] = _dot(ka_ref[pl.ds(start, tkc), :], qa_h[hh])

    def consume(j, hh, buf, m_prev, masked):
        start = pl.multiple_of(j * tkc, tkc)
        st = s_scr[buf, hh]
        if masked:
            kpos = start + lax.broadcasted_iota(jnp.int32, (tkc, 1), 0)
            st = jnp.where(kpos <= qpos, st, NEG)
        m_new = jnp.maximum(m_prev, jnp.max(st, axis=0, keepdims=True))
        alpha = jnp.exp2(m_prev - m_new)
        p = jnp.exp2(st - m_new).astype(BF16)
        rows = slice(hh * FOX_HEAD_DIM, (hh + 1) * FOX_HEAD_DIM)
        va = jnp.concatenate([vt_ref[rows, pl.ds(start, tkc)], ones], axis=0)
        acc_scr[hh] = alpha * acc_scr[hh] + _dot(va, p)
        return m_new

    nq = pl.num_programs(2)
    first = first_ref[(pl.program_id(0) * pl.num_programs(1) + pl.program_id(1)) * nq + pl.program_id(2)]
    for hh in range(2):
        produce(first, hh, 0)

    def body(i, ms):
        ms = list(ms)
        for step in range(2):
            for hh in range(2):
                produce(2 * i + step + 1, hh, 1 - step)
                ms[hh] = consume(2 * i + step, hh, step, ms[hh], False)
        return tuple(ms)

    m_init = jnp.full((1, tq), NEG, F32)
    ms = list(lax.fori_loop(first // 2, n_full // 2, body, (m_init, m_init)))
    for d in range(n_diag):
        for hh in range(2):
            if d + 1 < n_diag:
                produce(n_full + d + 1, hh, (d + 1) % 2)
            ms[hh] = consume(n_full + d, hh, d % 2, ms[hh], True)
    out = jnp.concatenate([acc_scr[hh, 0:FOX_HEAD_DIM] / acc_scr[hh, FOX_HEAD_DIM:FOX_HEAD_DIM + 1]
                           for hh in range(2)], axis=0)
    o_ref[...] = out.T.astype(o_ref.dtype)


SKIP_MARGIN = 40.0
NORM_SLACK = 1.02


def _fox_first_chunks(bnd, batch, t_len, tq, tkc):
    nq, nc, pairs = t_len // tq, t_len // tkc, FOX_HEADS // 2
    b3 = bnd.reshape(batch, t_len, LANES)
    f2 = b3[:, :, 0:FOX_HEADS] * LOG2E
    qn = jnp.sqrt(jnp.max(b3[:, :, FOX_HEADS:2 * FOX_HEADS].reshape(batch, nq, tq, FOX_HEADS), axis=2))
    kn = jnp.sqrt(jnp.max(b3[:, :, 2 * FOX_HEADS:3 * FOX_HEADS], axis=1))
    thr = 2.0 * NORM_SLACK * qn * kn[:, None, :] + SKIP_MARGIN
    decay = f2[:, tkc - 1::tkc, :][:, None, :, :] - f2[:, ::tq, :][:, :, None, :]
    ok = jnp.all((decay > thr[:, :, None, :]).reshape(batch, nq, nc, pairs, 2), axis=-1)
    before_tile = jnp.arange(nc)[None, :] < ((jnp.arange(nq) * tq) // tkc)[:, None]
    ok = ok & before_tile[None, :, :, None]
    lead = jnp.sum(jnp.cumprod(ok.astype(jnp.int32), axis=2), axis=2)
    return ((lead // 2) * 2).transpose(0, 2, 1).reshape(-1).astype(jnp.int32)


def _fox_attention(qt, ka, vt, bnd, batch, t_len, tq, tkc):
    past = 0
    nq = t_len // tq
    n_diag = max(1, tq // tkc)
    assert tq & (tq - 1) == 0 and tq % LANES == 0 and (tkc % tq == 0 or tq % tkc == 0) and t_len % tq == 0
    assert nq == 1 or tq % (2 * tkc) == 0
    assert t_len >= n_diag * tkc and t_len % tkc == 0
    pairs = FOX_HEADS // 2
    first = _fox_first_chunks(bnd, batch, t_len, tq, tkc)
    grid_spec = pltpu.PrefetchScalarGridSpec(
        num_scalar_prefetch=1,
        grid=(batch, pairs, nq),
        in_specs=[pl.BlockSpec((LANES, tq), lambda b, h, i, f: (h, b * nq + i)),
                  pl.BlockSpec((t_len, 2 * LANES), lambda b, h, i, f: (b, h)),
                  pl.BlockSpec((LANES, t_len), lambda b, h, i, f: (h, b))],
        out_specs=pl.BlockSpec((tq, LANES), lambda b, h, i, f: (b * nq + i, h)),
        scratch_shapes=[pltpu.VMEM((2, 2, tkc, tq), F32),
                        pltpu.VMEM((2, FOX_HEAD_DIM + ONES_ROWS, tq), F32)])
    return pl.pallas_call(
        functools.partial(_fox_attn_kernel, tq=tq, tkc=tkc, past=past, n_diag=n_diag),
        grid_spec=grid_spec,
        out_shape=jax.ShapeDtypeStruct((batch * t_len, FOX_WIDTH), BF16),
        compiler_params=_cparams(("parallel", "parallel", "arbitrary")),
        name="fox_attention",
    )(first, qt, ka, vt)


def _mlstm_kernel(mqk_ref, mv_ref, mo_ref, gc_ref, gr_ref, c0_ref, n0_ref, m0_ref, buf_ref,
                  cw_ref, cb_ref, gml_ref,
                  h_ref, c_out, n_out, m_out, buf_out,
                  c_scr, n_scr, m_scr, prev_scr, *, L, t_last, nc):
    ci = pl.program_id(1)
    rows = mqk_ref.shape[0]

    @pl.when(ci == 0)
    def _():
        c_scr[...] = c0_ref[0]
        n_scr[...] = n0_ref[0]
        m_scr[...] = m0_ref[0]
        prev_scr[...] = buf_ref[0]

    u = _pad_rows(mqk_ref[...], L)
    ext = jnp.concatenate([prev_scr[...], u], axis=0)
    y = cb_ref[...] + cw_ref[ML_CONV - 1:ML_CONV, :] * u
    for s in range(1, ML_CONV):
        y = y + cw_ref[ML_CONV - 1 - s:ML_CONV - s, :] * pltpu.roll(ext, s, axis=0)[8:8 + L]
    qk = y * _sigmoid(y)
    prev_scr[...] = u[L - 8:L]

    gc = _pad_rows(gc_ref[...], L)
    gr = gr_ref[0]
    b_col = _mat_f32(_tri(L, True), _split3(gc))
    b_row = _f32_mat(_split3(gr), _tri(L, False))
    v_all = _pad_rows(mv_ref[...], L)
    o_all = _pad_rows(mo_ref[...], L)
    t_idx = lax.broadcasted_iota(jnp.int32, (L, 1), 0)
    causal = lax.broadcasted_iota(jnp.int32, (1, L), 1) <= t_idx

    heads = []
    for h in range(ML_HEADS):
        hs = slice(h * ML_HEAD_DIM, (h + 1) * ML_HEAD_DIM)
        q32 = qk[:, hs]
        k32 = qk[:, ML_WIDTH + h * ML_HEAD_DIM:ML_WIDTH + (h + 1) * ML_HEAD_DIM] * (ML_HEAD_DIM ** -0.5)
        qb, kb = q32.astype(BF16), k32.astype(BF16)
        c_prev = c_scr[h]
        heads.append(dict(hs=hs, q32=q32, k32=k32, qb=qb, kb=kb, vb=v_all[:, hs], c_prev=c_prev,
                          s=_dot_nt(qb, kb), qc=_dot_nt(qb, c_prev.astype(BF16))))

    for h, hd in enumerate(heads):
        bc = b_col[:, G_MF + h:G_MF + h + 1]
        ic = gc[:, G_MI + h:G_MI + h + 1]
        br = b_row[G_MF + h:G_MF + h + 1, :]
        ir = gr[G_MI + h:G_MI + h + 1, :]
        m_prev = m_scr[h][:, 0:1]
        dmat = jnp.where(causal, bc + (ir - br), NEG)
        inter = bc + m_prev
        m_t = jnp.maximum(inter, jnp.max(dmat, axis=-1, keepdims=True))
        w = jnp.exp(dmat - m_t)
        g = jnp.exp(inter - m_t)
        a = w * hd["s"]
        b_last = bc[t_last:t_last + 1, :]
        m_last = m_t[t_last:t_last + 1, :]
        w_end = jnp.where(t_idx <= t_last, jnp.exp(b_last - bc + ic - m_last), 0.0)
        hd.update(m_t=m_t, g=g, a=a, m_last=m_last, g_end=g[t_last:t_last + 1, :], w_end=w_end,
                  av=_dot(a.astype(BF16), hd["vb"]),
                  vk=_dot_tn((hd["vb"].astype(F32) * w_end).astype(BF16), hd["kb"]))

    for h, hd in enumerate(heads):
        hs, g, a, m_t = hd["hs"], hd["g"], hd["a"], hd["m_t"]
        n_prev = n_scr[h]
        num = g * hd["qc"] + hd["av"]
        den = g * jnp.sum(hd["q32"] * n_prev, axis=-1, keepdims=True) + jnp.sum(a, axis=-1, keepdims=True)
        hh = num / jnp.maximum(jnp.abs(den), jnp.exp(-m_t))
        c_scr[h] = hd["g_end"] * hd["c_prev"] + hd["vk"]
        n_scr[h] = hd["g_end"] * n_prev + jnp.sum(hd["k32"] * hd["w_end"], axis=0, keepdims=True)
        m_scr[h] = jnp.broadcast_to(hd["m_last"], (1, LANES))
        yh = hh * lax.rsqrt(jnp.mean(hh * hh, axis=-1, keepdims=True) + EPS) * gml_ref[:, hs]
        yh = yh * _sigmoid(o_all[:, hs].astype(F32))
        h_ref[:, hs] = yh[:rows].astype(h_ref.dtype)

    @pl.when(ci == nc - 1)
    def _():
        c_out[0] = c_scr[...]
        n_out[0] = n_scr[...]
        m_out[0] = m_scr[...]
        buf_out[0] = ext[t_last + 1:t_last + 9]


def _mlstm(mqk, mv, mo, gc, gr, c0, n0, m0, buf8, cw, cb, gml, batch, t_len):
    L = ML_CHUNK if t_len >= ML_CHUNK else LANES
    rows = min(L, t_len)
    nc = t_len // rows
    t_last = rows - 1
    assert t_len % rows == 0 and (nc == 1 or rows == L) and (t_last + 1) % 8 == 0
    tok = lambda c: pl.BlockSpec((rows, c), lambda b, i: (b * nc + i, 0))
    st = lambda *s: pl.BlockSpec((1,) + s, lambda b, i: (b,) + (0,) * len(s))
    shp_c = (ML_HEADS, ML_HEAD_DIM, ML_HEAD_DIM)
    shp_n = (ML_HEADS, 1, ML_HEAD_DIM)
    return pl.pallas_call(
        functools.partial(_mlstm_kernel, L=L, t_last=t_last, nc=nc),
        grid=(batch, nc),
        in_specs=[tok(2 * ML_WIDTH), tok(ML_WIDTH), tok(ML_WIDTH), tok(LANES),
                  pl.BlockSpec((1, 16, L), lambda b, i: (b, 0, i)),
                  st(*shp_c), st(*shp_n), st(*shp_n), st(8, 2 * ML_WIDTH),
                  _const_spec((ML_CONV, 2 * ML_WIDTH)), _const_spec((1, 2 * ML_WIDTH)), _const_spec((1, ML_WIDTH))],
        out_specs=[tok(ML_WIDTH), st(*shp_c), st(*shp_n), st(*shp_n), st(8, 2 * ML_WIDTH)],
        out_shape=[jax.ShapeDtypeStruct((batch * t_len, ML_WIDTH), BF16),
                   jax.ShapeDtypeStruct((batch,) + shp_c, F32),
                   jax.ShapeDtypeStruct((batch,) + shp_n, F32),
                   jax.ShapeDtypeStruct((batch,) + shp_n, F32),
                   jax.ShapeDtypeStruct((batch, 8, 2 * ML_WIDTH), F32)],
        scratch_shapes=[pltpu.VMEM(shp_c, F32), pltpu.VMEM(shp_n, F32), pltpu.VMEM(shp_n, F32),
                        pltpu.VMEM((8, 2 * ML_WIDTH), F32)],
        compiler_params=_cparams(("parallel", "arbitrary")),
        name="mlstm",
    )(mqk, mv, mo, gc, gr, c0, n0, m0, buf8, cw, cb, gml)


def _gla_level_matrices(L):
    mats = np.zeros((len(GLA_LEVELS), L, L), np.float32)
    for li, b in enumerate(GLA_LEVELS):
        for t in range(L):
            base = (t // (2 * b)) * 2 * b
            bound = base + b - 1
            if t > bound:
                mats[li, t, bound + 1:t + 1] = 1.0
            else:
                mats[li, t, t + 1:bound + 1] = 1.0
    return jnp.asarray(mats, BF16)


def _gla_kernel(q_ref, k_ref, v_ref, gg_ref, la_ref, s0_ref, lvl_ref, gn_ref, o_ref, s_out, s_scr, *, L, t_last, nc):
    ci = pl.program_id(1)
    nb, rows = q_ref.shape[0], q_ref.shape[1]

    @pl.when(ci == 0)
    def _():
        s_scr[...] = s0_ref[...]

    t_idx = lax.broadcasted_iota(jnp.int32, (L, 1), 0)
    s_idx = lax.broadcasted_iota(jnp.int32, (1, L), 1)
    tril = _tri(L, True)
    level_mask = []
    for b in GLA_LEVELS:
        sh = b.bit_length() - 1
        level_mask.append((jnp.right_shift(t_idx, sh + 1) == jnp.right_shift(s_idx, sh + 1))
                          & ((jnp.right_shift(t_idx, sh) & 1) == 1) & ((jnp.right_shift(s_idx, sh) & 1) == 0))

    els = []
    for bi in range(nb):
        la3 = _split2(_pad_rows(la_ref[bi], L))
        q_all = _pad_rows(q_ref[bi], L)
        k_all = _pad_rows(k_ref[bi], L)
        el = dict(la3=la3, cb=_mat_f32(tril, la3), v=_pad_rows(v_ref[bi], L), g=_pad_rows(gg_ref[bi], L),
                  q32=[], k32=[], a=[], e_next=jnp.exp(_mat_f32(lvl_ref[0], la3)), pending=None)
        for h in range(GLA_HEADS):
            ks = slice(h * GLA_DK, (h + 1) * GLA_DK)
            el["q32"].append(q_all[:, ks].astype(F32))
            el["k32"].append(k_all[:, ks].astype(F32))
            el["a"].append(jnp.where(t_idx == s_idx, _dot_nt(q_all[:, ks], k_all[:, ks]), 0.0))
        els.append(el)

    def settle(el):
        if el["pending"] is not None:
            prods, mask = el["pending"]
            el["a"] = [jnp.where(mask, prods[h], el["a"][h]) for h in range(GLA_HEADS)]

    for li in range(len(GLA_LEVELS)):
        for el in els:
            e_all = el["e_next"]
            if li + 1 < len(GLA_LEVELS):
                el["e_next"] = jnp.exp(_mat_f32(lvl_ref[li + 1], el["la3"]))
            prods = []
            for h in range(GLA_HEADS):
                e = e_all[:, h * GLA_DK:(h + 1) * GLA_DK]
                prods.append(_dot_nt((el["q32"][h] * e).astype(BF16), (el["k32"][h] * e).astype(BF16)))
            settle(el)
            el["pending"] = (prods, level_mask[li])
    for el in els:
        settle(el)

    for bi, el in enumerate(els):
        el["o"] = []
        for h in range(GLA_HEADS):
            ks = slice(h * GLA_DK, (h + 1) * GLA_DK)
            vs = slice(h * GLA_DV, (h + 1) * GLA_DV)
            cbh = el["cb"][:, ks]
            vb = el["v"][:, vs]
            s_t = s_scr[bi, h]
            o = (_dot_nt((el["q32"][h] * jnp.exp(cbh)).astype(BF16), s_t.astype(BF16))
                 + _dot(el["a"][h].astype(BF16), vb))
            cl = cbh[t_last:t_last + 1, :]
            kd = jnp.where(t_idx <= t_last, el["k32"][h] * jnp.exp(cl - cbh), 0.0).astype(BF16)
            s_scr[bi, h] = jnp.exp(cl) * s_t + _dot_tn(vb, kd)
            el["o"].append(o)
    for bi, el in enumerate(els):
        for h in range(GLA_HEADS):
            vs = slice(h * GLA_DV, (h + 1) * GLA_DV)
            o = el["o"][h]
            y = o * lax.rsqrt(jnp.mean(o * o, axis=-1, keepdims=True) + EPS) * gn_ref[:, vs]
            gate = el["g"][:, vs].astype(F32)
            y = y * (gate * _sigmoid(gate))
            o_ref[bi, :, vs] = y[:rows].astype(o_ref.dtype)

    @pl.when(ci == nc - 1)
    def _():
        s_out[...] = s_scr[...]


GLA_BATCH_PER_STEP = 4


def _gla(q, k, v, gg, la, s0t, gn, batch, t_len):
    L = GLA_CHUNK
    rows = min(L, t_len)
    nc = t_len // rows
    t_last = rows - 1
    nb = max(d for d in range(1, GLA_BATCH_PER_STEP + 1) if batch % d == 0)
    assert t_len % rows == 0 and (nc == 1 or rows == L)
    tok = lambda c: pl.BlockSpec((nb, rows, c), lambda b, i: (b, i, 0))
    shp_s = (GLA_HEADS, GLA_DV, GLA_DK)
    st = pl.BlockSpec((nb,) + shp_s, lambda b, i: (b, 0, 0, 0))
    levels = _gla_level_matrices(L)
    seq = lambda a: a.reshape(batch, t_len, a.shape[-1])
    o, s_new = pl.pallas_call(
        functools.partial(_gla_kernel, L=L, t_last=t_last, nc=nc),
        grid=(batch // nb, nc),
        in_specs=[tok(GLA_KW), tok(GLA_KW), tok(GLA_VW), tok(GLA_VW), tok(GLA_KW), st,
                  _const_spec(levels.shape), _const_spec((1, GLA_VW))],
        out_specs=[tok(GLA_VW), st],
        out_shape=[jax.ShapeDtypeStruct((batch, t_len, GLA_VW), BF16),
                   jax.ShapeDtypeStruct((batch,) + shp_s, F32)],
        scratch_shapes=[pltpu.VMEM((nb,) + shp_s, F32)],
        compiler_params=_cparams(("parallel", "arbitrary")),
        name="gla",
    )(seq(q), seq(k), seq(v), seq(gg), seq(la), s0t, levels, gn)
    return o.reshape(batch * t_len, GLA_VW), s_new


def _pack_even(w, b_fox_f, b_i, b_f):
    d = w.shape[0]
    o = np.cumsum((0, FOX_WIDTH, FOX_WIDTH, FOX_WIDTH, FOX_HEADS, 2 * ML_WIDTH, ML_WIDTH, ML_WIDTH, ML_HEADS, ML_HEADS))
    n_gate = FOX_HEADS + 2 * ML_HEADS
    wp = jnp.concatenate([w[:, o[0]:o[3]], w[:, o[4]:o[7]], w[:, o[3]:o[4]], w[:, o[7]:o[9]],
                          jnp.zeros((d, LANES - n_gate), w.dtype)], axis=1).astype(BF16)
    bias = jnp.concatenate([b_fox_f, b_i, b_f, jnp.zeros((LANES - n_gate,), F32)]).reshape(1, LANES)
    return wp, bias


def _pack_odd(w, w_a2):
    d = w.shape[0]
    wp = jnp.concatenate([w, jnp.zeros((d, LANES - GLA_RANK), w.dtype)], axis=1).astype(BF16)
    wa2 = jnp.concatenate([w_a2, jnp.zeros((LANES - GLA_RANK, w_a2.shape[1]), w_a2.dtype)], axis=0).astype(BF16)
    return wp, wa2


def _gate_rows(gc, batch, t_len, t_pad):
    g = gc.reshape(batch, t_len, LANES)[:, :, :16].transpose(0, 2, 1)
    if t_pad > t_len:
        g = jnp.pad(g, ((0, 0), (0, 0), (0, t_pad - t_len)))
    return g


def _trunk(x, fox_k, fox_v, fox_lf, ml_c, ml_n, ml_m, ml_buf, gla_s, params):
    (norm_mix, norm_ffn, norm_final, w_even, bias_even, conv_w, conv_b, g_ml,
     w_odd, w_a2, b_a, g_gla, w_out, w_ff1, w_ff2) = params
    batch, t_len, d = x.shape
    past = fox_k.shape[2]
    n = batch * t_len
    tm = 512 if n % 512 == 0 else 256 if n % 256 == 0 else n
    depth = norm_mix.shape[0]
    xf = x.reshape(n, d)
    ev_states, odd_states = [], []
    kv_stacked = None
    y = None
    for layer in range(depth):
        j = layer // 2
        if layer % 2 == 0:
            assert (t_len % tm == 0) == (past == 0)
            proj = _inproj_even(xf, norm_mix[layer][None], w_even[j], bias_even[j],
                                tm, batch, t_len, j, (depth + 1) // 2, kv_stacked)
            if past == 0:
                qt, kf, vf, ka, vt, mqk, mv, mo, gc, bnd = proj
                kv_stacked = (kf, vf)
                attn = _fox_attention(qt, ka, vt, bnd, batch, t_len, min(512, t_len), 256)
            else:
                q, kf, vf, mqk, mv, mo, gc = proj
                lf_new = gc.reshape(batch, t_len, LANES)[:, :, :FOX_HEADS]
                lf_row = jnp.concatenate([fox_lf[j], lf_new, jnp.zeros((batch, LANES - t_len, FOX_HEADS), F32)], axis=1)
                lf_row = jnp.pad(lf_row.transpose(0, 2, 1), ((0, 0), (0, 16 - FOX_HEADS), (0, 0)))
                cache_t = lambda a: a.transpose(0, 1, 3, 4, 2).reshape(a.shape[0], batch, FOX_WIDTH, past)
                new_t = lambda a: jnp.pad(a.reshape(batch, t_len, FOX_WIDTH).transpose(0, 2, 1).astype(BF16),
                                          ((0, 0), (0, 0), (0, LANES - t_len)))
                attn = _fox_decode(q.reshape(batch, t_len, FOX_WIDTH), cache_t(fox_k), cache_t(fox_v), j,
                                   new_t(kf), new_t(vf), lf_row).reshape(n, FOX_WIDTH)
            l_ml = ML_CHUNK if t_len >= ML_CHUNK else LANES
            gr = _gate_rows(gc, batch, t_len, max(t_len, l_ml))
            c0 = ml_c[j]
            n0 = ml_n[j][:, :, None, :]
            m0 = jnp.broadcast_to(ml_m[j][:, :, None, None], (batch, ML_HEADS, 1, LANES))
            buf8 = jnp.pad(ml_buf[j], ((0, 0), (8 - (ML_CONV - 1), 0), (0, 0)))
            h_ml, c_new, n_new, m_new, buf_new = _mlstm(mqk, mv, mo, gc, gr, c0, n0, m0, buf8,
                                                        conv_w[j], conv_b[j][None], g_ml[j][None], batch, t_len)
            heads = lambda a: None if past == 0 else a.reshape(batch, t_len, FOX_HEADS, FOX_HEAD_DIM)
            ev_states.append((heads(kf), heads(vf),
                              gc[:, :FOX_HEADS].reshape(batch, t_len, FOX_HEADS),
                              c_new, n_new[:, :, 0, :], m_new[:, :, 0, 0], buf_new[:, 8 - (ML_CONV - 1):, :]))
            mixes = [attn, h_ml]
        else:
            q, k, v, gg, la = _inproj_odd(xf, norm_mix[layer][None], w_odd[j], w_a2[j], b_a[j][None], tm)
            s0t = gla_s[j].transpose(0, 1, 3, 2)
            o, s_new = _gla(q, k, v, gg, la, s0t, g_gla[j][None], batch, t_len)
            odd_states.append(s_new.transpose(0, 1, 3, 2))
            mixes = [o]
        last = layer == depth - 1
        out = _post(xf, mixes, w_out[layer], norm_ffn[layer][None], w_ff1[layer], w_ff2[layer],
                    norm_final[None] if last else None, tm)
        if last:
            y = out
        else:
            xf = out
    ev = [jnp.stack([s[i] for s in ev_states]) for i in range(2 if kv_stacked else 0, 7)]
    if kv_stacked:
        ev = [a.reshape(a.shape[0], batch, FOX_HEADS, FOX_HEAD_DIM, t_len).transpose(0, 1, 4, 2, 3)
              for a in kv_stacked] + ev
    return y.reshape(batch, t_len, d), ev, jnp.stack(odd_states)


def kernel(x_prompt, x_sample, cache_fox_k, cache_fox_v, cache_fox_logf, state_mlstm_c, state_mlstm_n, state_mlstm_m, state_mlstm_conv, state_gla_s, norm_mix, norm_ffn, norm_final, w_in_even, b_fox_f, conv_w_ml, conv_b_ml, b_ml_i, b_ml_f, g_ml, w_in_odd, w_gla_a2, b_gla_a, g_gla, w_out, w_ff1, w_ff2):
    n_even, n_odd = w_in_even.shape[0], w_in_odd.shape[0]
    packed_even = [_pack_even(w_in_even[j], b_fox_f[j], b_ml_i[j], b_ml_f[j]) for j in range(n_even)]
    packed_odd = [_pack_odd(w_in_odd[j], w_gla_a2[j]) for j in range(n_odd)]
    params = (norm_mix, norm_ffn, norm_final,
              [p[0] for p in packed_even], [p[1] for p in packed_even], conv_w_ml, conv_b_ml, g_ml,
              [p[0] for p in packed_odd], [p[1] for p in packed_odd], b_gla_a, g_gla,
              w_out.astype(BF16), w_ff1.astype(BF16), w_ff2.astype(BF16))

    bp = x_prompt.shape[0]
    dt = x_prompt.dtype
    zeros = lambda *s: jnp.zeros(s, dt)
    y_p, ev_p, gla_p = _trunk(
        x_prompt,
        zeros(n_even, bp, 0, FOX_HEADS, FOX_HEAD_DIM), zeros(n_even, bp, 0, FOX_HEADS, FOX_HEAD_DIM),
        zeros(n_even, bp, 0, FOX_HEADS),
        zeros(n_even, bp, ML_HEADS, ML_HEAD_DIM, ML_HEAD_DIM), zeros(n_even, bp, ML_HEADS, ML_HEAD_DIM),
        zeros(n_even, bp, ML_HEADS), zeros(n_even, bp, ML_CONV - 1, 2 * ML_WIDTH),
        zeros(n_odd, bp, GLA_HEADS, GLA_DK, GLA_DV), params)
    y_s, ev_s, gla_s = _trunk(x_sample, cache_fox_k, cache_fox_v, cache_fox_logf, state_mlstm_c, state_mlstm_n,
                              state_mlstm_m, state_mlstm_conv, state_gla_s, params)
    return (y_p, y_s, *ev_p, gla_p, *ev_s, gla_s)
```

```python
import functools

import numpy as np
import jax
import jax.numpy as jnp
from jax import lax
from jax.experimental import pallas as pl
from jax.experimental.pallas import tpu as pltpu

F32 = jnp.float32
BF16 = jnp.bfloat16
EPS = 1e-6
NEG = -1e30
LOG2E = 1.4426950408889634

LANES = 128
VMEM_LIMIT = 56 * 1024 * 1024

D_MODEL = 1024
D_FF = 4 * D_MODEL
FOX_HEADS, FOX_HEAD_DIM = 8, 64
FOX_WIDTH = FOX_HEADS * FOX_HEAD_DIM
ML_HEADS, ML_HEAD_DIM = 4, 128
ML_WIDTH = ML_HEADS * ML_HEAD_DIM
ML_CONV = 4
GLA_HEADS, GLA_DK, GLA_DV = 4, 128, 256
GLA_KW = GLA_HEADS * GLA_DK
GLA_VW = GLA_HEADS * GLA_DV
GLA_RANK = 16
GLA_TAU = 16.0

E_Q, E_K, E_V, E_MQK, E_MV, E_MO, E_G, E_END = 0, 512, 1024, 1536, 2560, 3072, 3584, 3712
G_FOX, G_MI, G_MF = 0, 8, 12
O_Q, O_K, O_V, O_G, O_A, O_END = 0, 512, 1024, 2048, 3072, 3200

ML_CHUNK = 256
GLA_CHUNK = 128
GLA_LEVELS = (64, 32, 16, 8, 4, 2, 1)


def _cparams(sem):
    return pltpu.CompilerParams(dimension_semantics=sem, vmem_limit_bytes=VMEM_LIMIT)


def _const_spec(shape):
    nd = len(shape)
    return pl.BlockSpec(shape, lambda *_: (0,) * nd, pipeline_mode=pl.Buffered(1))


def _rms(x, g):
    return x * lax.rsqrt(jnp.mean(x * x, axis=-1, keepdims=True) + EPS) * g


def _sigmoid(x):
    return 1.0 / (1.0 + jnp.exp(-x))


def _log_sigmoid(x):
    return -(jnp.maximum(-x, 0.0) + jnp.log1p(jnp.exp(-jnp.abs(x))))


def _dot(a, b):
    return jnp.dot(a, b, preferred_element_type=F32)


def _dot_nt(a, b):
    return lax.dot_general(a, b, (((1,), (1,)), ((), ())), preferred_element_type=F32)


def _dot_tn(a, b):
    return lax.dot_general(a, b, (((0,), (0,)), ((), ())), preferred_element_type=F32)


def _split3(x):
    hi = x.astype(BF16)
    r1 = x - hi.astype(F32)
    mid = r1.astype(BF16)
    lo = (r1 - mid.astype(F32)).astype(BF16)
    return hi, mid, lo


def _split2(x):
    hi = x.astype(BF16)
    return hi, (x - hi.astype(F32)).astype(BF16)


def _mat_f32(m, parts):
    return functools.reduce(lambda a, b: a + b, [_dot(m, p) for p in parts])


def _f32_mat(parts, m):
    return functools.reduce(lambda a, b: a + b, [_dot(p, m) for p in parts])


def _tri(n, lower):
    r = lax.broadcasted_iota(jnp.int32, (n, n), 0)
    c = lax.broadcasted_iota(jnp.int32, (n, n), 1)
    keep = (c <= r) if lower else (r <= c)
    return jnp.where(keep, 1.0, 0.0).astype(BF16)


def _pad_rows(a, n):
    if a.shape[0] == n:
        return a
    return jnp.concatenate([a, jnp.zeros((n - a.shape[0], a.shape[1]), a.dtype)], axis=0)


def _conv_silu(u, prev, cw_ref, cb_ref):
    n = u.shape[0]
    ext = jnp.concatenate([prev, u], axis=0)
    y = cb_ref[...] + cw_ref[ML_CONV - 1:ML_CONV, :] * u
    for s in range(1, ML_CONV):
        y = y + cw_ref[ML_CONV - 1 - s:ML_CONV - s, :] * pltpu.roll(ext, s, axis=0)[8:8 + n]
    lane = lax.broadcasted_iota(jnp.int32, (1, 2 * ML_WIDTH), 1)
    return y * _sigmoid(y) * jnp.where(lane >= ML_WIDTH, ML_HEAD_DIM ** -0.5, 1.0), ext


FOX_Q_SCALE = FOX_HEAD_DIM ** -0.5 * LOG2E


def _even_segments(x_ref, g_ref, w_ref):
    h = _rms(x_ref[...], g_ref[...]).astype(BF16)
    return lambda a, b: _dot(h, w_ref[:, a:b])


def _even_gates(seg, bias_ref):
    gz = seg(E_G, E_END) + bias_ref[...]
    lane = lax.broadcasted_iota(jnp.int32, gz.shape, 1)
    is_log = (lane < G_MI) | (lane >= G_MF)
    return jnp.where(is_log, _log_sigmoid(gz), gz)


def _inproj_even_rows_kernel(x_ref, g_ref, w_ref, bias_ref,
                             q_ref, kf_ref, vf_ref, mqk_ref, mv_ref, mo_ref, gc_ref):
    seg = _even_segments(x_ref, g_ref, w_ref)
    q_ref[...] = (seg(E_Q, E_K) * FOX_Q_SCALE).astype(BF16)
    kf_ref[...] = seg(E_K, E_V)
    vf_ref[...] = seg(E_V, E_MQK)
    mqk_ref[...] = seg(E_MQK, E_MV)
    mv_ref[...] = seg(E_MV, E_MO).astype(BF16)
    mo_ref[...] = seg(E_MO, E_G).astype(BF16)
    gc_ref[...] = _even_gates(seg, bias_ref)


def _inproj_even_seq_kernel(*refs, tpb, aliased):
    x_ref, g_ref, w_ref, bias_ref, ind_ref, cw_ref, cb_ref, buf_ref = refs[:8]
    refs = refs[8 + (2 if aliased else 0):]
    (qt_ref, kf_ref, vf_ref, ka_ref, vt_ref, mqk_ref, mv_ref, mo_ref, gc_ref, bnd_ref, tail_ref,
     carry, conv_prev) = refs
    @pl.when(pl.program_id(0) % tpb == 0)
    def _():
        carry[...] = jnp.zeros_like(carry)
        conv_prev[...] = buf_ref[0]

    seg = _even_segments(x_ref, g_ref, w_ref)
    mqk = seg(E_MQK, E_MV)
    gates = _even_gates(seg, bias_ref)
    gc_ref[...] = gates
    q = seg(E_Q, E_K) * FOX_Q_SCALE
    act, _ = _conv_silu(mqk, conv_prev[...], cw_ref, cb_ref)
    mqk_ref[...] = act.astype(BF16)
    last_rows = mqk[mqk.shape[0] - 8:]
    conv_prev[...] = last_rows
    tail_ref[0] = last_rows
    k = seg(E_K, E_V)
    qt_ref[...] = q.T.astype(BF16)
    v = seg(E_V, E_MQK)
    kf_ref[0, 0] = k.T
    f = _append_f_terms(gates, carry, k.astype(BF16), ka_ref)
    mv_ref[...] = seg(E_MV, E_MO).astype(BF16)
    v_t = v.T
    vt_ref[...] = v_t.astype(BF16)
    vf_ref[0, 0] = v_t
    mo_ref[...] = seg(E_MO, E_G).astype(BF16)
    lane = lax.broadcasted_iota(jnp.int32, (1, LANES), 1)
    bnd_ref[...] = (jnp.where(lane < FOX_HEADS, f, 0.0) + _dot((q * q).astype(BF16), ind_ref[0])
                    + _dot((k * k).astype(BF16), ind_ref[1]))


def _inproj_even(x, g, w, bias, tm, batch, t_len, slot, n_slots, kv_prev, cw, cb, buf8):
    n = x.shape[0]
    row = lambda c: (pl.BlockSpec((tm, c), lambda i: (i, 0)), (n, c))
    col = (pl.BlockSpec((FOX_WIDTH, tm), lambda i: (0, i)), (FOX_WIDTH, n))
    common_in = [row(D_MODEL)[0], _const_spec((1, D_MODEL)), _const_spec((D_MODEL, E_END)), _const_spec((1, LANES))]
    tail = [(row(ML_WIDTH), BF16), (row(ML_WIDTH), BF16), (row(LANES), F32)]
    if t_len % tm != 0:
        outs = [(row(FOX_WIDTH), BF16), (row(FOX_WIDTH), F32), (row(FOX_WIDTH), F32), (row(2 * ML_WIDTH), F32)] + tail
        return pl.pallas_call(
            _inproj_even_rows_kernel,
            grid=(n // tm,),
            in_specs=common_in,
            out_specs=[spec for (spec, _), _ in outs],
            out_shape=[jax.ShapeDtypeStruct(shape, dt) for (_, shape), dt in outs],
            compiler_params=_cparams(("parallel",)),
            name="inproj_even_rows",
        )(x, g, w, bias)
    tpb = t_len // tm
    state = (pl.BlockSpec((1, 1, FOX_WIDTH, tm), lambda i: (slot, i // tpb, 0, i % tpb)),
             (n_slots, batch, FOX_WIDTH, t_len))
    per_seq = (pl.BlockSpec((1, 8, 2 * ML_WIDTH), lambda i: (i // tpb, 0, 0)), (batch, 8, 2 * ML_WIDTH))
    outs = ([(col, BF16), (state, F32), (state, F32), (row(2 * FOX_WIDTH), BF16), (col, BF16), (row(2 * ML_WIDTH), BF16)]
            + tail + [(row(LANES), F32), (per_seq, F32)])
    ind = np.zeros((2, FOX_WIDTH, LANES), np.float32)
    for c in range(FOX_WIDTH):
        ind[0, c, FOX_HEADS + c // FOX_HEAD_DIM] = 1.0
        ind[1, c, 2 * FOX_HEADS + c // FOX_HEAD_DIM] = 1.0
    ins = [x, g, w, bias, jnp.asarray(ind, BF16), cw, cb, buf8]
    in_specs = common_in + [_const_spec(ind.shape), _const_spec(cw.shape), _const_spec(cb.shape), per_seq[0]]
    aliases = {}
    if kv_prev is not None:
        aliases = {len(ins): 1, len(ins) + 1: 2}
        ins += list(kv_prev)
        in_specs += [pl.BlockSpec(memory_space=pl.ANY)] * 2
    return pl.pallas_call(
        functools.partial(_inproj_even_seq_kernel, tpb=tpb, aliased=kv_prev is not None),
        grid=(n // tm,),
        in_specs=in_specs,
        out_specs=[spec for (spec, _), _ in outs],
        out_shape=[jax.ShapeDtypeStruct(shape, dt) for (_, shape), dt in outs],
        scratch_shapes=[pltpu.VMEM((1, LANES), F32), pltpu.VMEM((8, 2 * ML_WIDTH), F32)],
        input_output_aliases=aliases,
        compiler_params=_cparams(("arbitrary",)),
        name="inproj_even_seq",
    )(*ins)


def _inproj_odd_kernel(x_ref, g_ref, w_ref, wa2_ref, ba_ref, q_ref, k_ref, v_ref, gg_ref, la_ref):
    h = _rms(x_ref[...], g_ref[...]).astype(BF16)

    def seg(a, b):
        return _dot(h, w_ref[:, a:b])

    q_ref[...] = (seg(O_Q, O_K) * (GLA_DK ** -0.5)).astype(BF16)
    k_ref[...] = seg(O_K, O_V).astype(BF16)
    v_ref[...] = seg(O_V, O_G).astype(BF16)
    gg_ref[...] = seg(O_G, O_A).astype(BF16)
    ga = seg(O_A, O_END).astype(BF16)
    la_ref[...] = _log_sigmoid(_dot(ga, wa2_ref[...]) + ba_ref[...]) * (1.0 / GLA_TAU)


def _inproj_odd(x, g, w, wa2, ba, tm):
    n = x.shape[0]
    row = lambda c: pl.BlockSpec((tm, c), lambda i: (i, 0))
    outs = [(GLA_KW, BF16), (GLA_KW, BF16), (GLA_VW, BF16), (GLA_VW, BF16), (GLA_KW, F32)]
    return pl.pallas_call(
        _inproj_odd_kernel,
        grid=(n // tm,),
        in_specs=[row(D_MODEL), _const_spec((1, D_MODEL)), _const_spec((D_MODEL, O_END)),
                  _const_spec((LANES, GLA_KW)), _const_spec((1, GLA_KW))],
        out_specs=[row(c) for c, _ in outs],
        out_shape=[jax.ShapeDtypeStruct((n, c), dt) for c, dt in outs],
        compiler_params=_cparams(("parallel",)),
        name="inproj_odd",
    )(x, g, w, wa2, ba)


def _post_kernel(*refs, n_mix, final):
    x_ref = refs[0]
    mix_refs = refs[1:1 + n_mix]
    wo_ref, gf_ref, w1_ref, w2_ref = refs[1 + n_mix:5 + n_mix]
    rest = refs[5 + n_mix:]
    mix = mix_refs[0][...] if n_mix == 1 else jnp.concatenate([r[...] for r in mix_refs], axis=1)
    x1 = x_ref[...] + _dot(mix, wo_ref[...])
    h = _rms(x1, gf_ref[...]).astype(BF16)
    y = x1
    for c in range(D_FF // D_MODEL):
        sl = slice(c * D_MODEL, (c + 1) * D_MODEL)
        t = jnp.maximum(_dot(h, w1_ref[:, sl]), 0.0)
        y = y + _dot((t * t).astype(BF16), w2_ref[sl, :])
    if final:
        gfin_ref, out_ref = rest
        out_ref[...] = _rms(y, gfin_ref[...])
    else:
        (out_ref,) = rest
        out_ref[...] = y


def _post(x, mixes, wo, gf, w1, w2, gfin, tm):
    n = x.shape[0]
    row = lambda c: pl.BlockSpec((tm, c), lambda i: (i, 0))
    final = gfin is not None
    ins = [x, *mixes, wo, gf, w1, w2]
    specs = [row(D_MODEL)] + [row(m.shape[1]) for m in mixes] + [
        _const_spec((D_MODEL, D_MODEL)), _const_spec((1, D_MODEL)),
        _const_spec((D_MODEL, D_FF)), _const_spec((D_FF, D_MODEL))]
    if final:
        ins.append(gfin)
        specs.append(_const_spec((1, D_MODEL)))
    return pl.pallas_call(
        functools.partial(_post_kernel, n_mix=len(mixes), final=final),
        grid=(n // tm,),
        in_specs=specs,
        out_specs=row(D_MODEL),
        out_shape=jax.ShapeDtypeStruct((n, D_MODEL), F32),
        compiler_params=_cparams(("parallel",)),
        name="post_final" if final else "post",
    )(*ins)


F_TERMS = 3


def _append_f_terms(lf, carry, k, ka_ref):
    tc = lf.shape[0]
    f = _mat_f32(_tri(tc, True), _split3(lf)) + carry[...]
    carry[...] = f[tc - 1:tc, :]
    hi, mid, lo = [p.astype(F32) for p in _split3(f * LOG2E)]
    lane = lax.broadcasted_iota(jnp.int32, (1, LANES), 1)
    cols = jnp.where(lane < FOX_HEADS, hi,
                     jnp.where(lane < 2 * FOX_HEADS, pltpu.roll(mid, FOX_HEADS, axis=1),
                               jnp.where(lane < 3 * FOX_HEADS, pltpu.roll(lo, 2 * FOX_HEADS, axis=1), 0.0)))
    cols = cols.astype(BF16)
    for hp in range(FOX_HEADS // 2):
        ka_ref[:, 2 * hp * LANES:(2 * hp + 1) * LANES] = k[:, hp * LANES:(hp + 1) * LANES]
        ka_ref[:, (2 * hp + 1) * LANES:(2 * hp + 2) * LANES] = cols
    return f


def _fox_decode_kernel(q_ref, kc_ref, vc_ref, kn_ref, vn_ref, lf_ref, o_ref, *, t_new, past):
    nkeys = past + LANES
    lane_f = lax.broadcasted_iota(jnp.int32, (1, FOX_WIDTH), 1)
    head_mask = [jnp.where((lane_f >= h * FOX_HEAD_DIM) & (lane_f < (h + 1) * FOX_HEAD_DIM), 1.0, 0.0)
                 for h in range(FOX_HEADS)]
    q = q_ref[0].astype(F32)
    qb = jnp.concatenate([q * hm for hm in head_mask], axis=0).astype(BF16)

    def keys(cache_ref, new_ref):
        return jnp.concatenate([cache_ref[0, 0].astype(BF16), new_ref[0]], axis=1)

    s = _dot(qb, keys(kc_ref, kn_ref))
    lf = lf_ref[0]
    triu = _tri(LANES, False)
    carry = jnp.zeros((lf.shape[0], 1), F32)
    blocks = []
    for c in range(nkeys // LANES):
        cs = _f32_mat(_split3(lf[:, c * LANES:(c + 1) * LANES]), triu) + carry
        carry = cs[:, LANES - 1:LANES]
        blocks.append(cs)
    f_all = jnp.concatenate(blocks, axis=1) * LOG2E
    s = s - jnp.concatenate([jnp.broadcast_to(f_all[h:h + 1, :], (t_new, nkeys)) for h in range(FOX_HEADS)], axis=0)
    kpos = lax.broadcasted_iota(jnp.int32, (1, nkeys), 1)
    qpos = past + (lax.broadcasted_iota(jnp.int32, (FOX_HEADS * t_new, 1), 0) & (t_new - 1))
    s = jnp.where(kpos <= qpos, s, NEG)
    p = jnp.exp2(s - jnp.max(s, axis=-1, keepdims=True))
    ob = _dot_nt(p.astype(BF16), keys(vc_ref, vn_ref)) / jnp.sum(p, axis=-1, keepdims=True)
    out = ob[0:t_new] * head_mask[0]
    for h in range(1, FOX_HEADS):
        out = out + ob[h * t_new:(h + 1) * t_new] * head_mask[h]
    o_ref[0] = out.astype(o_ref.dtype)


def _fox_decode(q, k_cache, v_cache, layer, k_new, v_new, lf_row):
    batch, t_new, _ = q.shape
    past = k_cache.shape[3]
    assert t_new & (t_new - 1) == 0 and t_new <= LANES and t_new % 16 == 0 and past % LANES == 0
    blk = lambda a: pl.BlockSpec((1,) + a.shape[1:], lambda b: (b, 0, 0))
    cache = pl.BlockSpec((1, 1, FOX_WIDTH, past), lambda b: (layer, b, 0, 0))
    args = (q, k_cache, v_cache, k_new, v_new, lf_row)
    return pl.pallas_call(
        functools.partial(_fox_decode_kernel, t_new=t_new, past=past),
        grid=(batch,),
        in_specs=[blk(q), cache, cache, blk(k_new), blk(v_new), blk(lf_row)],
        out_specs=blk(q),
        out_shape=jax.ShapeDtypeStruct(q.shape, BF16),
        compiler_params=_cparams(("parallel",)),
        name="fox_decode",
    )(*args)


ONES_ROWS = 16


def _fox_attn_kernel(first_ref, qt_ref, ka_ref, vt_ref, o_ref, s_scr, acc_scr, *, tq, tkc, past, n_diag):
    q_first = past + pl.program_id(2) * tq
    n_full = q_first // tkc
    drow = lax.broadcasted_iota(jnp.int32, (LANES, 1), 0)
    lane = lax.broadcasted_iota(jnp.int32, (1, 2 * tq), 1)
    lane_head = jnp.where(lane < tq, 0, 1)
    own_head = jnp.where(jnp.where(drow < FOX_HEAD_DIM, 0, 1) == lane_head, 1.0, 0.0).astype(BF16)
    qt = qt_ref[...]
    q2 = jnp.concatenate([qt, qt], axis=1) * own_head
    head = 2 * pl.program_id(1) + lane_head
    f_sel = ((drow & (FOX_HEADS - 1)) == head) & (drow < F_TERMS * FOX_HEADS)
    qa = jnp.concatenate([q2, jnp.where(f_sel, -1.0, 0.0).astype(BF16)], axis=0)
    ones = jnp.ones((ONES_ROWS, tkc), BF16)
    acc_scr[...] = jnp.zeros_like(acc_scr)

    qa_h = [qa[:, hh * tq:(hh + 1) * tq] for hh in range(2)]
    qpos = q_first + lax.broadcasted_iota(jnp.int32, (1, tq), 1)

    def produce(j, hh, buf):
        start = pl.multiple_of(j * tkc, tkc)
        s_scr[buf, hh] = _dot(ka_ref[pl.ds(start, tkc), :], qa_h[hh])

    def consume(j, hh, buf, m_prev, masked):
        start = pl.multiple_of(j * tkc, tkc)
        st = s_scr[buf, hh]
        if masked:
            kpos = start + lax.broadcasted_iota(jnp.int32, (tkc, 1), 0)
            st = jnp.where(kpos <= qpos, st, NEG)
        m_new = jnp.maximum(m_prev, jnp.max(st, axis=0, keepdims=True))
        alpha = jnp.exp2(m_prev - m_new)
        p = jnp.exp2(st - m_new).astype(BF16)
        rows = slice(hh * FOX_HEAD_DIM, (hh + 1) * FOX_HEAD_DIM)
        va = jnp.concatenate([vt_ref[rows, pl.ds(start, tkc)], ones], axis=0)
        acc_scr[hh] = alpha * acc_scr[hh] + _dot(va, p)
        return m_new

    nq = pl.num_programs(2)
    first = first_ref[(pl.program_id(0) * pl.num_programs(1) + pl.program_id(1)) * nq + pl.program_id(2)]
    for hh in range(2):
        produce(first, hh, 0)

    def body(i, ms):
        ms = list(ms)
        for step in range(2):
            for hh in range(2):
                produce(2 * i + step + 1, hh, 1 - step)
                ms[hh] = consume(2 * i + step, hh, step, ms[hh], False)
        return tuple(ms)

    m_init = jnp.full((1, tq), NEG, F32)
    ms = list(lax.fori_loop(first // 2, n_full // 2, body, (m_init, m_init)))
    for d in range(n_diag):
        for hh in range(2):
            if d + 1 < n_diag:
                produce(n_full + d + 1, hh, (d + 1) % 2)
            ms[hh] = consume(n_full + d, hh, d % 2, ms[hh], True)
    out = jnp.concatenate([acc_scr[hh, 0:FOX_HEAD_DIM] / acc_scr[hh, FOX_HEAD_DIM:FOX_HEAD_DIM + 1]
                           for hh in range(2)], axis=0)
    o_ref[...] = out.T.astype(o_ref.dtype)


SKIP_MARGIN = 40.0
NORM_SLACK = 1.02


def _fox_first_chunks(bnd, batch, t_len, tq, tkc):
    nq, nc, pairs = t_len // tq, t_len // tkc, FOX_HEADS // 2
    b3 = bnd.reshape(batch, t_len, LANES)
    f2 = b3[:, :, 0:FOX_HEADS] * LOG2E
    qn = jnp.sqrt(jnp.max(b3[:, :, FOX_HEADS:2 * FOX_HEADS].reshape(batch, nq, tq, FOX_HEADS), axis=2))
    kn = jnp.sqrt(jnp.max(b3[:, :, 2 * FOX_HEADS:3 * FOX_HEADS], axis=1))
    thr = 2.0 * NORM_SLACK * qn * kn[:, None, :] + SKIP_MARGIN
    decay = f2[:, tkc - 1::tkc, :][:, None, :, :] - f2[:, ::tq, :][:, :, None, :]
    ok = jnp.all((decay > thr[:, :, None, :]).reshape(batch, nq, nc, pairs, 2), axis=-1)
    before_tile = jnp.arange(nc)[None, :] < ((jnp.arange(nq) * tq) // tkc)[:, None]
    ok = ok & before_tile[None, :, :, None]
    lead = jnp.sum(jnp.cumprod(ok.astype(jnp.int32), axis=2), axis=2)
    return ((lead // 2) * 2).transpose(0, 2, 1).reshape(-1).astype(jnp.int32)


def _fox_attention(qt, ka, vt, bnd, batch, t_len, tq, tkc):
    past = 0
    nq = t_len // tq
    n_diag = max(1, tq // tkc)
    assert tq & (tq - 1) == 0 and tq % LANES == 0 and (tkc % tq == 0 or tq % tkc == 0) and t_len % tq == 0
    assert nq == 1 or tq % (2 * tkc) == 0
    assert t_len >= n_diag * tkc and t_len % tkc == 0
    pairs = FOX_HEADS // 2
    first = _fox_first_chunks(bnd, batch, t_len, tq, tkc)
    grid_spec = pltpu.PrefetchScalarGridSpec(
        num_scalar_prefetch=1,
        grid=(batch, pairs, nq),
        in_specs=[pl.BlockSpec((LANES, tq), lambda b, h, i, f: (h, b * nq + i)),
                  pl.BlockSpec((t_len, 2 * LANES), lambda b, h, i, f: (b, h)),
                  pl.BlockSpec((LANES, t_len), lambda b, h, i, f: (h, b))],
        out_specs=pl.BlockSpec((tq, LANES), lambda b, h, i, f: (b * nq + i, h)),
        scratch_shapes=[pltpu.VMEM((2, 2, tkc, tq), F32),
                        pltpu.VMEM((2, FOX_HEAD_DIM + ONES_ROWS, tq), F32)])
    return pl.pallas_call(
        functools.partial(_fox_attn_kernel, tq=tq, tkc=tkc, past=past, n_diag=n_diag),
        grid_spec=grid_spec,
        out_shape=jax.ShapeDtypeStruct((batch * t_len, FOX_WIDTH), BF16),
        compiler_params=_cparams(("parallel", "parallel", "arbitrary")),
        name="fox_attention",
    )(first, qt, ka, vt)


def _mlstm_kernel(mqk_ref, mv_ref, mo_ref, gc_ref, gr_ref, c0_ref, n0_ref, m0_ref, buf_ref,
                  cw_ref, cb_ref, gml_ref,
                  h_ref, c_out, n_out, m_out, buf_out,
                  c_scr, n_scr, m_scr, prev_scr, *, L, t_last, nc, preact):
    ci = pl.program_id(1)
    rows = mqk_ref.shape[0]

    @pl.when(ci == 0)
    def _():
        c_scr[...] = c0_ref[0]
        n_scr[...] = n0_ref[0]
        m_scr[...] = m0_ref[0]
        prev_scr[...] = buf_ref[0]

    if preact:
        qk = _pad_rows(mqk_ref[...], L).astype(F32)
        ext = None
    else:
        u = _pad_rows(mqk_ref[...], L)
        qk, ext = _conv_silu(u, prev_scr[...], cw_ref, cb_ref)
        prev_scr[...] = u[L - 8:L]

    gc = _pad_rows(gc_ref[...], L)
    gr = gr_ref[0]
    b_col = _mat_f32(_tri(L, True), _split3(gc))
    b_row = _f32_mat(_split3(gr), _tri(L, False))
    v_all = _pad_rows(mv_ref[...], L)
    o_all = _pad_rows(mo_ref[...], L)
    t_idx = lax.broadcasted_iota(jnp.int32, (L, 1), 0)
    causal = lax.broadcasted_iota(jnp.int32, (1, L), 1) <= t_idx

    heads = []
    for h in range(ML_HEADS):
        hs = slice(h * ML_HEAD_DIM, (h + 1) * ML_HEAD_DIM)
        q32 = qk[:, hs]
        k32 = qk[:, ML_WIDTH + h * ML_HEAD_DIM:ML_WIDTH + (h + 1) * ML_HEAD_DIM]
        qb, kb = q32.astype(BF16), k32.astype(BF16)
        c_prev = c_scr[h]
        heads.append(dict(hs=hs, q32=q32, k32=k32, qb=qb, kb=kb, vb=v_all[:, hs], c_prev=c_prev,
                          s=_dot_nt(qb, kb), qc=_dot_nt(qb, c_prev.astype(BF16))))

    for h, hd in enumerate(heads):
        bc = b_col[:, G_MF + h:G_MF + h + 1]
        ic = gc[:, G_MI + h:G_MI + h + 1]
        br = b_row[G_MF + h:G_MF + h + 1, :]
        ir = gr[G_MI + h:G_MI + h + 1, :]
        m_prev = m_scr[h][:, 0:1]
        dmat = jnp.where(causal, bc + (ir - br), NEG)
        inter = bc + m_prev
        m_t = jnp.maximum(inter, jnp.max(dmat, axis=-1, keepdims=True))
        w = jnp.exp(dmat - m_t)
        g = jnp.exp(inter - m_t)
        a = w * hd["s"]
        b_last = bc[t_last:t_last + 1, :]
        m_last = m_t[t_last:t_last + 1, :]
        w_end = jnp.where(t_idx <= t_last, jnp.exp(b_last - bc + ic - m_last), 0.0)
        hd.update(m_t=m_t, g=g, a=a, m_last=m_last, g_end=g[t_last:t_last + 1, :], w_end=w_end,
                  av=_dot(a.astype(BF16), hd["vb"]),
                  vk=_dot_tn((hd["vb"].astype(F32) * w_end).astype(BF16), hd["kb"]))

    for h, hd in enumerate(heads):
        hs, g, a, m_t = hd["hs"], hd["g"], hd["a"], hd["m_t"]
        n_prev = n_scr[h]
        num = g * hd["qc"] + hd["av"]
        den = g * jnp.sum(hd["q32"] * n_prev, axis=-1, keepdims=True) + jnp.sum(a, axis=-1, keepdims=True)
        hh = num / jnp.maximum(jnp.abs(den), jnp.exp(-m_t))
        c_scr[h] = hd["g_end"] * hd["c_prev"] + hd["vk"]
        n_scr[h] = hd["g_end"] * n_prev + jnp.sum(hd["k32"] * hd["w_end"], axis=0, keepdims=True)
        m_scr[h] = jnp.broadcast_to(hd["m_last"], (1, LANES))
        yh = hh * lax.rsqrt(jnp.mean(hh * hh, axis=-1, keepdims=True) + EPS) * gml_ref[:, hs]
        yh = yh * _sigmoid(o_all[:, hs].astype(F32))
        h_ref[:, hs] = yh[:rows].astype(h_ref.dtype)

    @pl.when(ci == nc - 1)
    def _():
        c_out[0] = c_scr[...]
        n_out[0] = n_scr[...]
        m_out[0] = m_scr[...]
        buf_out[0] = buf_ref[0] if preact else ext[t_last + 1:t_last + 9]


def _mlstm(mqk, mv, mo, gc, gr, c0, n0, m0, buf8, cw, cb, gml, batch, t_len):
    preact = mqk.dtype == BF16
    L = ML_CHUNK if t_len >= ML_CHUNK else LANES
    rows = min(L, t_len)
    nc = t_len // rows
    t_last = rows - 1
    assert t_len % rows == 0 and (nc == 1 or rows == L) and (t_last + 1) % 8 == 0
    tok = lambda c: pl.BlockSpec((rows, c), lambda b, i: (b * nc + i, 0))
    st = lambda *s: pl.BlockSpec((1,) + s, lambda b, i: (b,) + (0,) * len(s))
    shp_c = (ML_HEADS, ML_HEAD_DIM, ML_HEAD_DIM)
    shp_n = (ML_HEADS, 1, ML_HEAD_DIM)
    return pl.pallas_call(
        functools.partial(_mlstm_kernel, L=L, t_last=t_last, nc=nc, preact=preact),
        grid=(batch, nc),
        in_specs=[tok(2 * ML_WIDTH), tok(ML_WIDTH), tok(ML_WIDTH), tok(LANES),
                  pl.BlockSpec((1, 16, L), lambda b, i: (b, 0, i)),
                  st(*shp_c), st(*shp_n), st(*shp_n), st(8, 2 * ML_WIDTH),
                  _const_spec((ML_CONV, 2 * ML_WIDTH)), _const_spec((1, 2 * ML_WIDTH)), _const_spec((1, ML_WIDTH))],
        out_specs=[tok(ML_WIDTH), st(*shp_c), st(*shp_n), st(*shp_n), st(8, 2 * ML_WIDTH)],
        out_shape=[jax.ShapeDtypeStruct((batch * t_len, ML_WIDTH), BF16),
                   jax.ShapeDtypeStruct((batch,) + shp_c, F32),
                   jax.ShapeDtypeStruct((batch,) + shp_n, F32),
                   jax.ShapeDtypeStruct((batch,) + shp_n, F32),
                   jax.ShapeDtypeStruct((batch, 8, 2 * ML_WIDTH), F32)],
        scratch_shapes=[pltpu.VMEM(shp_c, F32), pltpu.VMEM(shp_n, F32), pltpu.VMEM(shp_n, F32),
                        pltpu.VMEM((8, 2 * ML_WIDTH), F32)],
        compiler_params=_cparams(("parallel", "arbitrary")),
        name="mlstm",
    )(mqk, mv, mo, gc, gr, c0, n0, m0, buf8, cw, cb, gml)


def _gla_level_matrices(L):
    mats = np.zeros((len(GLA_LEVELS), L, L), np.float32)
    for li, b in enumerate(GLA_LEVELS):
        for t in range(L):
            base = (t // (2 * b)) * 2 * b
            bound = base + b - 1
            if t > bound:
                mats[li, t, bound + 1:t + 1] = 1.0
            else:
                mats[li, t, t + 1:bound + 1] = 1.0
    return jnp.asarray(mats, BF16)


def _gla_kernel(q_ref, k_ref, v_ref, gg_ref, la_ref, s0_ref, lvl_ref, gn_ref, o_ref, s_out, s_scr, *, L, t_last, nc):
    ci = pl.program_id(1)
    nb, rows = q_ref.shape[0], q_ref.shape[1]

    @pl.when(ci == 0)
    def _():
        s_scr[...] = s0_ref[...]

    t_idx = lax.broadcasted_iota(jnp.int32, (L, 1), 0)
    s_idx = lax.broadcasted_iota(jnp.int32, (1, L), 1)
    tril = _tri(L, True)
    level_mask = []
    for b in GLA_LEVELS:
        sh = b.bit_length() - 1
        level_mask.append((jnp.right_shift(t_idx, sh + 1) == jnp.right_shift(s_idx, sh + 1))
                          & ((jnp.right_shift(t_idx, sh) & 1) == 1) & ((jnp.right_shift(s_idx, sh) & 1) == 0))

    els = []
    for bi in range(nb):
        la3 = _split2(_pad_rows(la_ref[bi], L))
        q_all = _pad_rows(q_ref[bi], L)
        k_all = _pad_rows(k_ref[bi], L)
        el = dict(la3=la3, cb=_mat_f32(tril, la3), v=_pad_rows(v_ref[bi], L), g=_pad_rows(gg_ref[bi], L),
                  q32=[], k32=[], a=[], e_next=jnp.exp(_mat_f32(lvl_ref[0], la3)), pending=None)
        for h in range(GLA_HEADS):
            ks = slice(h * GLA_DK, (h + 1) * GLA_DK)
            el["q32"].append(q_all[:, ks].astype(F32))
            el["k32"].append(k_all[:, ks].astype(F32))
            el["a"].append(jnp.where(t_idx == s_idx, _dot_nt(q_all[:, ks], k_all[:, ks]), 0.0))
        els.append(el)

    def settle(el):
        if el["pending"] is not None:
            prods, mask = el["pending"]
            el["a"] = [jnp.where(mask, prods[h], el["a"][h]) for h in range(GLA_HEADS)]

    for li in range(len(GLA_LEVELS)):
        for el in els:
            e_all = el["e_next"]
            if li + 1 < len(GLA_LEVELS):
                el["e_next"] = jnp.exp(_mat_f32(lvl_ref[li + 1], el["la3"]))
            prods = []
            for h in range(GLA_HEADS):
                e = e_all[:, h * GLA_DK:(h + 1) * GLA_DK]
                prods.append(_dot_nt((el["q32"][h] * e).astype(BF16), (el["k32"][h] * e).astype(BF16)))
            settle(el)
            el["pending"] = (prods, level_mask[li])
    for el in els:
        settle(el)

    for bi, el in enumerate(els):
        el["o"] = []
        for h in range(GLA_HEADS):
            ks = slice(h * GLA_DK, (h + 1) * GLA_DK)
            vs = slice(h * GLA_DV, (h + 1) * GLA_DV)
            cbh = el["cb"][:, ks]
            vb = el["v"][:, vs]
            s_t = s_scr[bi, h]
            o = (_dot_nt((el["q32"][h] * jnp.exp(cbh)).astype(BF16), s_t.astype(BF16))
                 + _dot(el["a"][h].astype(BF16), vb))
            cl = cbh[t_last:t_last + 1, :]
            kd = jnp.where(t_idx <= t_last, el["k32"][h] * jnp.exp(cl - cbh), 0.0).astype(BF16)
            s_scr[bi, h] = jnp.exp(cl) * s_t + _dot_tn(vb, kd)
            el["o"].append(o)
    for bi, el in enumerate(els):
        for h in range(GLA_HEADS):
            vs = slice(h * GLA_DV, (h + 1) * GLA_DV)
            o = el["o"][h]
            y = o * lax.rsqrt(jnp.mean(o * o, axis=-1, keepdims=True) + EPS) * gn_ref[:, vs]
            gate = el["g"][:, vs].astype(F32)
            y = y * (gate * _sigmoid(gate))
            o_ref[bi, :, vs] = y[:rows].astype(o_ref.dtype)

    @pl.when(ci == nc - 1)
    def _():
        s_out[...] = s_scr[...]


GLA_BATCH_PER_STEP = 4


def _gla(q, k, v, gg, la, s0t, gn, batch, t_len):
    L = GLA_CHUNK
    rows = min(L, t_len)
    nc = t_len // rows
    t_last = rows - 1
    nb = max(d for d in range(1, GLA_BATCH_PER_STEP + 1) if batch % d == 0)
    assert t_len % rows == 0 and (nc == 1 or rows == L)
    tok = lambda c: pl.BlockSpec((nb, rows, c), lambda b, i: (b, i, 0))
    shp_s = (GLA_HEADS, GLA_DV, GLA_DK)
    st = pl.BlockSpec((nb,) + shp_s, lambda b, i: (b, 0, 0, 0))
    levels = _gla_level_matrices(L)
    seq = lambda a: a.reshape(batch, t_len, a.shape[-1])
    o, s_new = pl.pallas_call(
        functools.partial(_gla_kernel, L=L, t_last=t_last, nc=nc),
        grid=(batch // nb, nc),
        in_specs=[tok(GLA_KW), tok(GLA_KW), tok(GLA_VW), tok(GLA_VW), tok(GLA_KW), st,
                  _const_spec(levels.shape), _const_spec((1, GLA_VW))],
        out_specs=[tok(GLA_VW), st],
        out_shape=[jax.ShapeDtypeStruct((batch, t_len, GLA_VW), BF16),
                   jax.ShapeDtypeStruct((batch,) + shp_s, F32)],
        scratch_shapes=[pltpu.VMEM((nb,) + shp_s, F32)],
        compiler_params=_cparams(("parallel", "arbitrary")),
        name="gla",
    )(seq(q), seq(k), seq(v), seq(gg), seq(la), s0t, levels, gn)
    return o.reshape(batch * t_len, GLA_VW), s_new


def _pack_even(w, b_fox_f, b_i, b_f):
    d = w.shape[0]
    o = np.cumsum((0, FOX_WIDTH, FOX_WIDTH, FOX_WIDTH, FOX_HEADS, 2 * ML_WIDTH, ML_WIDTH, ML_WIDTH, ML_HEADS, ML_HEADS))
    n_gate = FOX_HEADS + 2 * ML_HEADS
    wp = jnp.concatenate([w[:, o[0]:o[3]], w[:, o[4]:o[7]], w[:, o[3]:o[4]], w[:, o[7]:o[9]],
                          jnp.zeros((d, LANES - n_gate), w.dtype)], axis=1).astype(BF16)
    bias = jnp.concatenate([b_fox_f, b_i, b_f, jnp.zeros((LANES - n_gate,), F32)]).reshape(1, LANES)
    return wp, bias


def _pack_odd(w, w_a2):
    d = w.shape[0]
    wp = jnp.concatenate([w, jnp.zeros((d, LANES - GLA_RANK), w.dtype)], axis=1).astype(BF16)
    wa2 = jnp.concatenate([w_a2, jnp.zeros((LANES - GLA_RANK, w_a2.shape[1]), w_a2.dtype)], axis=0).astype(BF16)
    return wp, wa2


def _gate_rows(gc, batch, t_len, t_pad):
    g = gc.reshape(batch, t_len, LANES)[:, :, :16].transpose(0, 2, 1)
    if t_pad > t_len:
        g = jnp.pad(g, ((0, 0), (0, 0), (0, t_pad - t_len)))
    return g


def _trunk(x, fox_k, fox_v, fox_lf, ml_c, ml_n, ml_m, ml_buf, gla_s, params):
    (norm_mix, norm_ffn, norm_final, w_even, bias_even, conv_w, conv_b, g_ml,
     w_odd, w_a2, b_a, g_gla, w_out, w_ff1, w_ff2) = params
    batch, t_len, d = x.shape
    past = fox_k.shape[2]
    n = batch * t_len
    tm = 512 if n % 512 == 0 else 256 if n % 256 == 0 else n
    depth = norm_mix.shape[0]
    xf = x.reshape(n, d)
    ev_states, odd_states = [], []
    kv_stacked = None
    y = None
    for layer in range(depth):
        j = layer // 2
        if layer % 2 == 0:
            assert (t_len % tm == 0) == (past == 0)
            buf8 = jnp.pad(ml_buf[j], ((0, 0), (8 - (ML_CONV - 1), 0), (0, 0)))
            proj = _inproj_even(xf, norm_mix[layer][None], w_even[j], bias_even[j],
                                tm, batch, t_len, j, (depth + 1) // 2, kv_stacked, conv_w[j], conv_b[j][None], buf8)
            conv_tail = None
            if past == 0:
                qt, kf, vf, ka, vt, mqk, mv, mo, gc, bnd, conv_tail = proj
                kv_stacked = (kf, vf)
                attn = _fox_attention(qt, ka, vt, bnd, batch, t_len, min(512, t_len), 256)
            else:
                q, kf, vf, mqk, mv, mo, gc = proj
                lf_new = gc.reshape(batch, t_len, LANES)[:, :, :FOX_HEADS]
                lf_row = jnp.concatenate([fox_lf[j], lf_new, jnp.zeros((batch, LANES - t_len, FOX_HEADS), F32)], axis=1)
                lf_row = jnp.pad(lf_row.transpose(0, 2, 1), ((0, 0), (0, 16 - FOX_HEADS), (0, 0)))
                cache_t = lambda a: a.transpose(0, 1, 3, 4, 2).reshape(a.shape[0], batch, FOX_WIDTH, past)
                new_t = lambda a: jnp.pad(a.reshape(batch, t_len, FOX_WIDTH).transpose(0, 2, 1).astype(BF16),
                                          ((0, 0), (0, 0), (0, LANES - t_len)))
                attn = _fox_decode(q.reshape(batch, t_len, FOX_WIDTH), cache_t(fox_k), cache_t(fox_v), j,
                                   new_t(kf), new_t(vf), lf_row).reshape(n, FOX_WIDTH)
            l_ml = ML_CHUNK if t_len >= ML_CHUNK else LANES
            gr = _gate_rows(gc, batch, t_len, max(t_len, l_ml))
            c0 = ml_c[j]
            n0 = ml_n[j][:, :, None, :]
            m0 = jnp.broadcast_to(ml_m[j][:, :, None, None], (batch, ML_HEADS, 1, LANES))
            h_ml, c_new, n_new, m_new, buf_new = _mlstm(mqk, mv, mo, gc, gr, c0, n0, m0, buf8,
                                                        conv_w[j], conv_b[j][None], g_ml[j][None], batch, t_len)
            if conv_tail is not None:
                buf_new = conv_tail
            heads = lambda a: None if past == 0 else a.reshape(batch, t_len, FOX_HEADS, FOX_HEAD_DIM)
            ev_states.append((heads(kf), heads(vf),
                              gc[:, :FOX_HEADS].reshape(batch, t_len, FOX_HEADS),
                              c_new, n_new[:, :, 0, :], m_new[:, :, 0, 0], buf_new[:, 8 - (ML_CONV - 1):, :]))
            mixes = [attn, h_ml]
        else:
            q, k, v, gg, la = _inproj_odd(xf, norm_mix[layer][None], w_odd[j], w_a2[j], b_a[j][None], tm)
            s0t = gla_s[j].transpose(0, 1, 3, 2)
            o, s_new = _gla(q, k, v, gg, la, s0t, g_gla[j][None], batch, t_len)
            odd_states.append(s_new.transpose(0, 1, 3, 2))
            mixes = [o]
        last = layer == depth - 1
        out = _post(xf, mixes, w_out[layer], norm_ffn[layer][None], w_ff1[layer], w_ff2[layer],
                    norm_final[None] if last else None, tm)
        if last:
            y = out
        else:
            xf = out
    ev = [jnp.stack([s[i] for s in ev_states]) for i in range(2 if kv_stacked else 0, 7)]
    if kv_stacked:
        ev = [a.reshape(a.shape[0], batch, FOX_HEADS, FOX_HEAD_DIM, t_len).transpose(0, 1, 4, 2, 3)
              for a in kv_stacked] + ev
    return y.reshape(batch, t_len, d), ev, jnp.stack(odd_states)


def kernel(x_prompt, x_sample, cache_fox_k, cache_fox_v, cache_fox_logf, state_mlstm_c, state_mlstm_n, state_mlstm_m, state_mlstm_conv, state_gla_s, norm_mix, norm_ffn, norm_final, w_in_even, b_fox_f, conv_w_ml, conv_b_ml, b_ml_i, b_ml_f, g_ml, w_in_odd, w_gla_a2, b_gla_a, g_gla, w_out, w_ff1, w_ff2):
    n_even, n_odd = w_in_even.shape[0], w_in_odd.shape[0]
    packed_even = [_pack_even(w_in_even[j], b_fox_f[j], b_ml_i[j], b_ml_f[j]) for j in range(n_even)]
    packed_odd = [_pack_odd(w_in_odd[j], w_gla_a2[j]) for j in range(n_odd)]
    params = (norm_mix, norm_ffn, norm_final,
              [p[0] for p in packed_even], [p[1] for p in packed_even], conv_w_ml, conv_b_ml, g_ml,
              [p[0] for p in packed_odd], [p[1] for p in packed_odd], b_gla_a, g_gla,
              w_out.astype(BF16), w_ff1.astype(BF16), w_ff2.astype(BF16))

    bp = x_prompt.shape[0]
    dt = x_prompt.dtype
    zeros = lambda *s: jnp.zeros(s, dt)
    y_p, ev_p, gla_p = _trunk(
        x_prompt,
        zeros(n_even, bp, 0, FOX_HEADS, FOX_HEAD_DIM), zeros(n_even, bp, 0, FOX_HEADS, FOX_HEAD_DIM),
        zeros(n_even, bp, 0, FOX_HEADS),
        zeros(n_even, bp, ML_HEADS, ML_HEAD_DIM, ML_HEAD_DIM), zeros(n_even, bp, ML_HEADS, ML_HEAD_DIM),
        zeros(n_even, bp, ML_HEADS), zeros(n_even, bp, ML_CONV - 1, 2 * ML_WIDTH),
        zeros(n_odd, bp, GLA_HEADS, GLA_DK, GLA_DV), params)
    y_s, ev_s, gla_s = _trunk(x_sample, cache_fox_k, cache_fox_v, cache_fox_logf, state_mlstm_c, state_mlstm_n,
                              state_mlstm_m, state_mlstm_conv, state_gla_s, params)
    return (y_p, y_s, *ev_p, gla_p, *ev_s, gla_s)
```

```python
import functools

import numpy as np
import jax
import jax.numpy as jnp
from jax import lax
from jax.experimental import pallas as pl
from jax.experimental.pallas import tpu as pltpu

F32 = jnp.float32
BF16 = jnp.bfloat16
EPS = 1e-6
NEG = -1e30
LOG2E = 1.4426950408889634

LANES = 128
VMEM_LIMIT = 56 * 1024 * 1024

D_MODEL = 1024
D_FF = 4 * D_MODEL
FOX_HEADS, FOX_HEAD_DIM = 8, 64
FOX_WIDTH = FOX_HEADS * FOX_HEAD_DIM
ML_HEADS, ML_HEAD_DIM = 4, 128
ML_WIDTH = ML_HEADS * ML_HEAD_DIM
ML_CONV = 4
GLA_HEADS, GLA_DK, GLA_DV = 4, 128, 256
GLA_KW = GLA_HEADS * GLA_DK
GLA_VW = GLA_HEADS * GLA_DV
GLA_RANK = 16
GLA_TAU = 16.0

E_Q, E_K, E_V, E_MQK, E_MV, E_MO, E_G, E_END = np.cumsum(
    (0, FOX_WIDTH, FOX_WIDTH, FOX_WIDTH, 2 * ML_WIDTH, ML_WIDTH, ML_WIDTH, LANES)).tolist()
G_FOX, G_MI, G_MF = 0, FOX_HEADS, FOX_HEADS + ML_HEADS
O_Q, O_K, O_V, O_G, O_A, O_END = np.cumsum((0, GLA_KW, GLA_KW, GLA_VW, GLA_VW, LANES)).tolist()

ML_CHUNK = 256
FOX_KEY_CHUNK = 256
FOX_Q_TILE = 512
GLA_CHUNK = 128
GLA_LEVELS = (64, 32, 16, 8, 4, 2, 1)


def _cparams(sem):
    return pltpu.CompilerParams(dimension_semantics=sem, vmem_limit_bytes=VMEM_LIMIT)


def _const_spec(shape):
    nd = len(shape)
    return pl.BlockSpec(shape, lambda *_: (0,) * nd, pipeline_mode=pl.Buffered(1))


def _rms(x, g):
    return x * lax.rsqrt(jnp.mean(x * x, axis=-1, keepdims=True) + EPS) * g


def _sigmoid(x):
    return 1.0 / (1.0 + jnp.exp(-x))


def _log_sigmoid(x):
    return -(jnp.maximum(-x, 0.0) + jnp.log1p(jnp.exp(-jnp.abs(x))))


def _dot(a, b):
    return jnp.dot(a, b, preferred_element_type=F32)


def _dot_nt(a, b):
    return lax.dot_general(a, b, (((1,), (1,)), ((), ())), preferred_element_type=F32)


def _dot_tn(a, b):
    return lax.dot_general(a, b, (((0,), (0,)), ((), ())), preferred_element_type=F32)


def _split3(x):
    hi = x.astype(BF16)
    r1 = x - hi.astype(F32)
    mid = r1.astype(BF16)
    lo = (r1 - mid.astype(F32)).astype(BF16)
    return hi, mid, lo


def _split2(x):
    hi = x.astype(BF16)
    return hi, (x - hi.astype(F32)).astype(BF16)


def _mat_f32(m, parts):
    return functools.reduce(lambda a, b: a + b, [_dot(m, p) for p in parts])


def _f32_mat(parts, m):
    return functools.reduce(lambda a, b: a + b, [_dot(p, m) for p in parts])


def _tri(n, lower):
    r = lax.broadcasted_iota(jnp.int32, (n, n), 0)
    c = lax.broadcasted_iota(jnp.int32, (n, n), 1)
    keep = (c <= r) if lower else (r <= c)
    return jnp.where(keep, 1.0, 0.0).astype(BF16)


def _pad_rows(a, n):
    if a.shape[0] == n:
        return a
    return jnp.concatenate([a, jnp.zeros((n - a.shape[0], a.shape[1]), a.dtype)], axis=0)


def _conv_silu(u, prev, cw_ref, cb_ref):
    n = u.shape[0]
    ext = jnp.concatenate([prev, u], axis=0)
    y = cb_ref[...] + cw_ref[ML_CONV - 1:ML_CONV, :] * u
    for s in range(1, ML_CONV):
        y = y + cw_ref[ML_CONV - 1 - s:ML_CONV - s, :] * pltpu.roll(ext, s, axis=0)[8:8 + n]
    lane = lax.broadcasted_iota(jnp.int32, (1, 2 * ML_WIDTH), 1)
    return y * _sigmoid(y) * jnp.where(lane >= ML_WIDTH, ML_HEAD_DIM ** -0.5, 1.0), ext


FOX_Q_SCALE = FOX_HEAD_DIM ** -0.5 * LOG2E


def _even_segments(x_ref, g_ref, w_ref):
    h = _rms(x_ref[...], g_ref[...]).astype(BF16)
    return lambda a, b: _dot(h, w_ref[:, a:b])


def _even_gates(seg, bias_ref):
    gz = seg(E_G, E_END) + bias_ref[...]
    lane = lax.broadcasted_iota(jnp.int32, gz.shape, 1)
    is_log = (lane < G_MI) | (lane >= G_MF)
    return jnp.where(is_log, _log_sigmoid(gz), gz)


def _inproj_even_rows_kernel(x_ref, g_ref, w_ref, bias_ref,
                             q_ref, kf_ref, vf_ref, mqk_ref, mv_ref, mo_ref, gc_ref):
    seg = _even_segments(x_ref, g_ref, w_ref)
    q_ref[...] = (seg(E_Q, E_K) * FOX_Q_SCALE).astype(BF16)
    kf_ref[...] = seg(E_K, E_V)
    vf_ref[...] = seg(E_V, E_MQK)
    mqk_ref[...] = seg(E_MQK, E_MV)
    mv_ref[...] = seg(E_MV, E_MO).astype(BF16)
    mo_ref[...] = seg(E_MO, E_G).astype(BF16)
    gc_ref[...] = _even_gates(seg, bias_ref)


def _inproj_even_seq_kernel(*refs, tpb, aliased):
    x_ref, g_ref, w_ref, bias_ref, ind_ref, cw_ref, cb_ref, buf_ref = refs[:8]
    refs = refs[8 + (2 if aliased else 0):]
    (qt_ref, kf_ref, vf_ref, ka_ref, vt_ref, mqk_ref, mv_ref, mo_ref, gc_ref, bnd_ref, tail_ref,
     carry, conv_prev) = refs
    @pl.when(pl.program_id(0) % tpb == 0)
    def _():
        carry[...] = jnp.zeros_like(carry)
        conv_prev[...] = buf_ref[0]

    seg = _even_segments(x_ref, g_ref, w_ref)
    mqk = seg(E_MQK, E_MV)
    gates = _even_gates(seg, bias_ref)
    gc_ref[...] = gates
    q = seg(E_Q, E_K) * FOX_Q_SCALE
    k = seg(E_K, E_V)
    v = seg(E_V, E_MQK)
    qt_ref[...] = q.T.astype(BF16)
    mv = seg(E_MV, E_MO)
    kf_ref[0, 0] = k.T
    f = _append_f_terms(gates, carry, k.astype(BF16), ka_ref)
    mo = seg(E_MO, E_G)
    act, _ = _conv_silu(mqk, conv_prev[...], cw_ref, cb_ref)
    mqk_ref[...] = act.astype(BF16)
    last_rows = mqk[mqk.shape[0] - 8:]
    conv_prev[...] = last_rows
    tail_ref[0] = last_rows
    mv_ref[...] = mv.astype(BF16)
    v_t = v.T
    vt_ref[...] = v_t.astype(BF16)
    vf_ref[0, 0] = v_t
    mo_ref[...] = mo.astype(BF16)
    norms = _dot((q * q).astype(BF16), ind_ref[0]) + _dot((k * k).astype(BF16), ind_ref[1])
    bnd_ref[...] = jnp.zeros_like(bnd_ref)
    bnd_ref[0, 0:1, :] = f[0:1, :]
    for c in range(f.shape[0] // FOX_KEY_CHUNK):
        bnd_ref[0, 1 + c:2 + c, :] = f[(c + 1) * FOX_KEY_CHUNK - 1:(c + 1) * FOX_KEY_CHUNK, :]
    bnd_ref[0, 7:8, :] = jnp.max(norms, axis=0, keepdims=True)


def _inproj_even(x, g, w, bias, tm, batch, t_len, slot, n_slots, kv_prev, cw, cb, buf8):
    n = x.shape[0]
    row = lambda c: (pl.BlockSpec((tm, c), lambda i: (i, 0)), (n, c))
    col = (pl.BlockSpec((FOX_WIDTH, tm), lambda i: (0, i)), (FOX_WIDTH, n))
    common_in = [row(D_MODEL)[0], _const_spec((1, D_MODEL)), _const_spec((D_MODEL, E_END)), _const_spec((1, LANES))]
    tail = [(row(ML_WIDTH), BF16), (row(ML_WIDTH), BF16), (row(LANES), F32)]
    if t_len % tm != 0:
        outs = [(row(FOX_WIDTH), BF16), (row(FOX_WIDTH), F32), (row(FOX_WIDTH), F32), (row(2 * ML_WIDTH), F32)] + tail
        return pl.pallas_call(
            _inproj_even_rows_kernel,
            grid=(n // tm,),
            in_specs=common_in,
            out_specs=[spec for (spec, _), _ in outs],
            out_shape=[jax.ShapeDtypeStruct(shape, dt) for (_, shape), dt in outs],
            compiler_params=_cparams(("parallel",)),
            name="inproj_even_rows",
        )(x, g, w, bias)
    tpb = t_len // tm
    state = (pl.BlockSpec((1, 1, FOX_WIDTH, tm), lambda i: (slot, i // tpb, 0, i % tpb)),
             (n_slots, batch, FOX_WIDTH, t_len))
    per_seq = (pl.BlockSpec((1, 8, 2 * ML_WIDTH), lambda i: (i // tpb, 0, 0)), (batch, 8, 2 * ML_WIDTH))
    outs = ([(col, BF16), (state, F32), (state, F32), (row(2 * FOX_WIDTH), BF16), (col, BF16), (row(2 * ML_WIDTH), BF16)]
            + tail + [((pl.BlockSpec((1, 8, LANES), lambda i: (i, 0, 0)), (n // tm, 8, LANES)), F32), (per_seq, F32)])
    assert tm % FOX_KEY_CHUNK == 0 and tm // FOX_KEY_CHUNK <= 6
    ind = np.zeros((2, FOX_WIDTH, LANES), np.float32)
    for c in range(FOX_WIDTH):
        ind[0, c, FOX_HEADS + c // FOX_HEAD_DIM] = 1.0
        ind[1, c, 2 * FOX_HEADS + c // FOX_HEAD_DIM] = 1.0
    ins = [x, g, w, bias, jnp.asarray(ind, BF16), cw, cb, buf8]
    in_specs = common_in + [_const_spec(ind.shape), _const_spec(cw.shape), _const_spec(cb.shape), per_seq[0]]
    aliases = {}
    if kv_prev is not None:
        aliases = {len(ins): 1, len(ins) + 1: 2}
        ins += list(kv_prev)
        in_specs += [pl.BlockSpec(memory_space=pl.ANY)] * 2
    return pl.pallas_call(
        functools.partial(_inproj_even_seq_kernel, tpb=tpb, aliased=kv_prev is not None),
        grid=(n // tm,),
        in_specs=in_specs,
        out_specs=[spec for (spec, _), _ in outs],
        out_shape=[jax.ShapeDtypeStruct(shape, dt) for (_, shape), dt in outs],
        scratch_shapes=[pltpu.VMEM((1, LANES), F32), pltpu.VMEM((8, 2 * ML_WIDTH), F32)],
        input_output_aliases=aliases,
        compiler_params=_cparams(("arbitrary",)),
        name="inproj_even_seq",
    )(*ins)


def _inproj_odd_kernel(x_ref, g_ref, w_ref, wa2_ref, ba_ref, q_ref, k_ref, v_ref, gg_ref, la_ref):
    h = _rms(x_ref[...], g_ref[...]).astype(BF16)

    def seg(a, b):
        return _dot(h, w_ref[:, a:b])

    q_ref[...] = (seg(O_Q, O_K) * (GLA_DK ** -0.5)).astype(BF16)
    k_ref[...] = seg(O_K, O_V).astype(BF16)
    v_ref[...] = seg(O_V, O_G).astype(BF16)
    gg_ref[...] = seg(O_G, O_A).astype(BF16)
    ga = seg(O_A, O_END).astype(BF16)
    la_ref[...] = _log_sigmoid(_dot(ga, wa2_ref[...]) + ba_ref[...]) * (1.0 / GLA_TAU)


def _inproj_odd(x, g, w, wa2, ba, tm):
    n = x.shape[0]
    row = lambda c: pl.BlockSpec((tm, c), lambda i: (i, 0))
    outs = [(GLA_KW, BF16), (GLA_KW, BF16), (GLA_VW, BF16), (GLA_VW, BF16), (GLA_KW, F32)]
    return pl.pallas_call(
        _inproj_odd_kernel,
        grid=(n // tm,),
        in_specs=[row(D_MODEL), _const_spec((1, D_MODEL)), _const_spec((D_MODEL, O_END)),
                  _const_spec((LANES, GLA_KW)), _const_spec((1, GLA_KW))],
        out_specs=[row(c) for c, _ in outs],
        out_shape=[jax.ShapeDtypeStruct((n, c), dt) for c, dt in outs],
        compiler_params=_cparams(("parallel",)),
        name="inproj_odd",
    )(x, g, w, wa2, ba)


def _post_kernel(*refs, n_mix, final):
    x_ref = refs[0]
    mix_refs = refs[1:1 + n_mix]
    wo_ref, gf_ref, w1_ref, w2_ref = refs[1 + n_mix:5 + n_mix]
    rest = refs[5 + n_mix:]
    mix = mix_refs[0][...] if n_mix == 1 else jnp.concatenate([r[...] for r in mix_refs], axis=1)
    x1 = x_ref[...] + _dot(mix, wo_ref[...])
    h = _rms(x1, gf_ref[...]).astype(BF16)
    y = x1
    for c in range(D_FF // D_MODEL):
        sl = slice(c * D_MODEL, (c + 1) * D_MODEL)
        t = jnp.maximum(_dot(h, w1_ref[:, sl]), 0.0)
        y = y + _dot((t * t).astype(BF16), w2_ref[sl, :])
    if final:
        gfin_ref, out_ref = rest
        out_ref[...] = _rms(y, gfin_ref[...])
    else:
        (out_ref,) = rest
        out_ref[...] = y


def _post(x, mixes, wo, gf, w1, w2, gfin, tm):
    n = x.shape[0]
    row = lambda c: pl.BlockSpec((tm, c), lambda i: (i, 0))
    final = gfin is not None
    ins = [x, *mixes, wo, gf, w1, w2]
    specs = [row(D_MODEL)] + [row(m.shape[1]) for m in mixes] + [
        _const_spec((D_MODEL, D_MODEL)), _const_spec((1, D_MODEL)),
        _const_spec((D_MODEL, D_FF)), _const_spec((D_FF, D_MODEL))]
    if final:
        ins.append(gfin)
        specs.append(_const_spec((1, D_MODEL)))
    return pl.pallas_call(
        functools.partial(_post_kernel, n_mix=len(mixes), final=final),
        grid=(n // tm,),
        in_specs=specs,
        out_specs=row(D_MODEL),
        out_shape=jax.ShapeDtypeStruct((n, D_MODEL), F32),
        compiler_params=_cparams(("parallel",)),
        name="post_final" if final else "post",
    )(*ins)


F_TERMS = 3


def _append_f_terms(lf, carry, k, ka_ref):
    tc = lf.shape[0]
    f = _mat_f32(_tri(tc, True), _split3(lf)) + carry[...]
    carry[...] = f[tc - 1:tc, :]
    hi, mid, lo = [p.astype(F32) for p in _split3(f * LOG2E)]
    lane = lax.broadcasted_iota(jnp.int32, (1, LANES), 1)
    cols = jnp.where(lane < FOX_HEADS, hi,
                     jnp.where(lane < 2 * FOX_HEADS, pltpu.roll(mid, FOX_HEADS, axis=1),
                               jnp.where(lane < 3 * FOX_HEADS, pltpu.roll(lo, 2 * FOX_HEADS, axis=1), 0.0)))
    cols = cols.astype(BF16)
    for hp in range(FOX_HEADS // 2):
        ka_ref[:, 2 * hp * LANES:(2 * hp + 1) * LANES] = k[:, hp * LANES:(hp + 1) * LANES]
        ka_ref[:, (2 * hp + 1) * LANES:(2 * hp + 2) * LANES] = cols
    return f


def _fox_decode_kernel(q_ref, kc_ref, vc_ref, kn_ref, vn_ref, lf_ref, o_ref, *, t_new, past):
    nkeys = past + LANES
    lane_f = lax.broadcasted_iota(jnp.int32, (1, FOX_WIDTH), 1)
    head_mask = [jnp.where((lane_f >= h * FOX_HEAD_DIM) & (lane_f < (h + 1) * FOX_HEAD_DIM), 1.0, 0.0)
                 for h in range(FOX_HEADS)]
    q = q_ref[0].astype(F32)
    qb = jnp.concatenate([q * hm for hm in head_mask], axis=0).astype(BF16)

    def keys(cache_ref, new_ref):
        return jnp.concatenate([cache_ref[0, 0].astype(BF16), new_ref[0]], axis=1)

    s = _dot(qb, keys(kc_ref, kn_ref))
    lf = lf_ref[0]
    triu = _tri(LANES, False)
    carry = jnp.zeros((lf.shape[0], 1), F32)
    blocks = []
    for c in range(nkeys // LANES):
        cs = _f32_mat(_split3(lf[:, c * LANES:(c + 1) * LANES]), triu) + carry
        carry = cs[:, LANES - 1:LANES]
        blocks.append(cs)
    f_all = jnp.concatenate(blocks, axis=1) * LOG2E
    s = s - jnp.concatenate([jnp.broadcast_to(f_all[h:h + 1, :], (t_new, nkeys)) for h in range(FOX_HEADS)], axis=0)
    kpos = lax.broadcasted_iota(jnp.int32, (1, nkeys), 1)
    qpos = past + (lax.broadcasted_iota(jnp.int32, (FOX_HEADS * t_new, 1), 0) & (t_new - 1))
    s = jnp.where(kpos <= qpos, s, NEG)
    p = jnp.exp2(s - jnp.max(s, axis=-1, keepdims=True))
    ob = _dot_nt(p.astype(BF16), keys(vc_ref, vn_ref)) / jnp.sum(p, axis=-1, keepdims=True)
    out = ob[0:t_new] * head_mask[0]
    for h in range(1, FOX_HEADS):
        out = out + ob[h * t_new:(h + 1) * t_new] * head_mask[h]
    o_ref[0] = out.astype(o_ref.dtype)


def _fox_decode(q, k_cache, v_cache, layer, k_new, v_new, lf_row):
    batch, t_new, _ = q.shape
    past = k_cache.shape[3]
    assert t_new & (t_new - 1) == 0 and t_new <= LANES and t_new % 16 == 0 and past % LANES == 0
    blk = lambda a: pl.BlockSpec((1,) + a.shape[1:], lambda b: (b, 0, 0))
    cache = pl.BlockSpec((1, 1, FOX_WIDTH, past), lambda b: (layer, b, 0, 0))
    args = (q, k_cache, v_cache, k_new, v_new, lf_row)
    return pl.pallas_call(
        functools.partial(_fox_decode_kernel, t_new=t_new, past=past),
        grid=(batch,),
        in_specs=[blk(q), cache, cache, blk(k_new), blk(v_new), blk(lf_row)],
        out_specs=blk(q),
        out_shape=jax.ShapeDtypeStruct(q.shape, BF16),
        compiler_params=_cparams(("parallel",)),
        name="fox_decode",
    )(*args)


ONES_ROWS = 16


def _fox_attn_kernel(first_ref, qt_ref, ka_ref, vt_ref, o_ref, s_scr, acc_scr, *, tq, tkc, past, n_diag):
    q_first = past + pl.program_id(2) * tq
    n_full = q_first // tkc
    drow = lax.broadcasted_iota(jnp.int32, (LANES, 1), 0)
    lane = lax.broadcasted_iota(jnp.int32, (1, 2 * tq), 1)
    lane_head = jnp.where(lane < tq, 0, 1)
    own_head = jnp.where(jnp.where(drow < FOX_HEAD_DIM, 0, 1) == lane_head, 1.0, 0.0).astype(BF16)
    qt = qt_ref[...]
    q2 = jnp.concatenate([qt, qt], axis=1) * own_head
    head = 2 * pl.program_id(1) + lane_head
    f_sel = ((drow & (FOX_HEADS - 1)) == head) & (drow < F_TERMS * FOX_HEADS)
    qa = jnp.concatenate([q2, jnp.where(f_sel, -1.0, 0.0).astype(BF16)], axis=0)
    ones = jnp.ones((ONES_ROWS, tkc), BF16)
    acc_scr[...] = jnp.zeros_like(acc_scr)

    qa_h = [qa[:, hh * tq:(hh + 1) * tq] for hh in range(2)]
    qpos = q_first + lax.broadcasted_iota(jnp.int32, (1, tq), 1)

    def produce(j, hh, buf):
        start = pl.multiple_of(j * tkc, tkc)
        s_scr[buf, hh] = _dot(ka_ref[pl.ds(start, tkc), :], qa_h[hh])

    def consume(j, hh, buf, m_prev, masked):
        start = pl.multiple_of(j * tkc, tkc)
        st = s_scr[buf, hh]
        if masked:
            kpos = start + lax.broadcasted_iota(jnp.int32, (tkc, 1), 0)
            st = jnp.where(kpos <= qpos, st, NEG)
        m_new = jnp.maximum(m_prev, jnp.max(st, axis=0, keepdims=True))
        alpha = jnp.exp2(m_prev - m_new)
        p = jnp.exp2(st - m_new).astype(BF16)
        rows = slice(hh * FOX_HEAD_DIM, (hh + 1) * FOX_HEAD_DIM)
        va = jnp.concatenate([vt_ref[rows, pl.ds(start, tkc)], ones], axis=0)
        acc_scr[hh] = alpha * acc_scr[hh] + _dot(va, p)
        return m_new

    nq = pl.num_programs(2)
    first = first_ref[(pl.program_id(0) * pl.num_programs(1) + pl.program_id(1)) * nq + pl.program_id(2)]
    for hh in range(2):
        produce(first, hh, 0)

    def body(i, ms):
        ms = list(ms)
        for step in range(2):
            for hh in range(2):
                produce(2 * i + step + 1, hh, 1 - step)
                ms[hh] = consume(2 * i + step, hh, step, ms[hh], False)
        return tuple(ms)

    m_init = jnp.full((1, tq), NEG, F32)
    ms = list(lax.fori_loop(first // 2, n_full // 2, body, (m_init, m_init)))
    for d in range(n_diag):
        for hh in range(2):
            if d + 1 < n_diag:
                produce(n_full + d + 1, hh, (d + 1) % 2)
            ms[hh] = consume(n_full + d, hh, d % 2, ms[hh], True)
    out = jnp.concatenate([acc_scr[hh, 0:FOX_HEAD_DIM] / acc_scr[hh, FOX_HEAD_DIM:FOX_HEAD_DIM + 1]
                           for hh in range(2)], axis=0)
    o_ref[...] = out.T.astype(o_ref.dtype)


SKIP_MARGIN = 40.0
NORM_SLACK = 1.02


def _fox_first_chunks(bnd, batch, t_len, tq, tkc):
    nq, nc, pairs = t_len // tq, t_len // tkc, FOX_HEADS // 2
    per_tile = tq // tkc
    b4 = bnd.reshape(batch, nq, 8, LANES)
    f_start = b4[:, :, 0, 0:FOX_HEADS] * LOG2E
    f_end = b4[:, :, 1:1 + per_tile, 0:FOX_HEADS].reshape(batch, nc, FOX_HEADS) * LOG2E
    qn = jnp.sqrt(b4[:, :, 7, FOX_HEADS:2 * FOX_HEADS])
    kn = jnp.sqrt(jnp.max(b4[:, :, 7, 2 * FOX_HEADS:3 * FOX_HEADS], axis=1))
    thr = 2.0 * NORM_SLACK * qn * kn[:, None, :] + SKIP_MARGIN
    decay = f_end[:, None, :, :] - f_start[:, :, None, :]
    ok = jnp.all((decay > thr[:, :, None, :]).reshape(batch, nq, nc, pairs, 2), axis=-1)
    before_tile = jnp.arange(nc)[None, :] < ((jnp.arange(nq) * tq) // tkc)[:, None]
    ok = ok & before_tile[None, :, :, None]
    lead = jnp.sum(jnp.cumprod(ok.astype(jnp.int32), axis=2), axis=2)
    return ((lead // 2) * 2).transpose(0, 2, 1).reshape(-1).astype(jnp.int32)


def _fox_attention(qt, ka, vt, bnd, batch, t_len, tq, tkc):
    past = 0
    nq = t_len // tq
    n_diag = max(1, tq // tkc)
    assert tq & (tq - 1) == 0 and tq % LANES == 0 and (tkc % tq == 0 or tq % tkc == 0) and t_len % tq == 0
    assert nq == 1 or tq % (2 * tkc) == 0
    assert t_len >= n_diag * tkc and t_len % tkc == 0
    pairs = FOX_HEADS // 2
    first = _fox_first_chunks(bnd, batch, t_len, tq, tkc)
    grid_spec = pltpu.PrefetchScalarGridSpec(
        num_scalar_prefetch=1,
        grid=(batch, pairs, nq),
        in_specs=[pl.BlockSpec((LANES, tq), lambda b, h, i, f: (h, b * nq + i)),
                  pl.BlockSpec((t_len, 2 * LANES), lambda b, h, i, f: (b, h)),
                  pl.BlockSpec((LANES, t_len), lambda b, h, i, f: (h, b))],
        out_specs=pl.BlockSpec((tq, LANES), lambda b, h, i, f: (b * nq + i, h)),
        scratch_shapes=[pltpu.VMEM((2, 2, tkc, tq), F32),
                        pltpu.VMEM((2, FOX_HEAD_DIM + ONES_ROWS, tq), F32)])
    return pl.pallas_call(
        functools.partial(_fox_attn_kernel, tq=tq, tkc=tkc, past=past, n_diag=n_diag),
        grid_spec=grid_spec,
        out_shape=jax.ShapeDtypeStruct((batch * t_len, FOX_WIDTH), BF16),
        compiler_params=_cparams(("parallel", "parallel", "arbitrary")),
        name="fox_attention",
    )(first, qt, ka, vt)


def _mlstm_kernel(mqk_ref, mv_ref, mo_ref, gc_ref, gr_ref, c0_ref, n0_ref, m0_ref, buf_ref,
                  cw_ref, cb_ref, gml_ref,
                  h_ref, c_out, n_out, m_out, buf_out,
                  c_scr, n_scr, m_scr, prev_scr, *, L, t_last, nc, preact):
    ci = pl.program_id(1)
    rows = mqk_ref.shape[0]

    @pl.when(ci == 0)
    def _():
        c_scr[...] = c0_ref[0]
        n_scr[...] = n0_ref[0]
        m_scr[...] = m0_ref[0]
        prev_scr[...] = buf_ref[0]

    if preact:
        qk = _pad_rows(mqk_ref[...], L).astype(F32)
        ext = None
    else:
        u = _pad_rows(mqk_ref[...], L)
        qk, ext = _conv_silu(u, prev_scr[...], cw_ref, cb_ref)
        prev_scr[...] = u[L - 8:L]

    gc = _pad_rows(gc_ref[...], L)
    gr = gr_ref[0]
    b_col = _mat_f32(_tri(L, True), _split3(gc))
    b_row = _f32_mat(_split3(gr), _tri(L, False))
    v_all = _pad_rows(mv_ref[...], L)
    o_all = _pad_rows(mo_ref[...], L)
    t_idx = lax.broadcasted_iota(jnp.int32, (L, 1), 0)
    causal = lax.broadcasted_iota(jnp.int32, (1, L), 1) <= t_idx

    heads = []
    for h in range(ML_HEADS):
        hs = slice(h * ML_HEAD_DIM, (h + 1) * ML_HEAD_DIM)
        q32 = qk[:, hs]
        k32 = qk[:, ML_WIDTH + h * ML_HEAD_DIM:ML_WIDTH + (h + 1) * ML_HEAD_DIM]
        qb, kb = q32.astype(BF16), k32.astype(BF16)
        c_prev = c_scr[h]
        heads.append(dict(hs=hs, q32=q32, k32=k32, qb=qb, kb=kb, vb=v_all[:, hs], c_prev=c_prev,
                          s=_dot_nt(qb, kb), qc=_dot_nt(qb, c_prev.astype(BF16))))

    for h, hd in enumerate(heads):
        bc = b_col[:, G_MF + h:G_MF + h + 1]
        ic = gc[:, G_MI + h:G_MI + h + 1]
        br = b_row[G_MF + h:G_MF + h + 1, :]
        ir = gr[G_MI + h:G_MI + h + 1, :]
        m_prev = m_scr[h][:, 0:1]
        dmat = jnp.where(causal, bc + (ir - br), NEG)
        inter = bc + m_prev
        m_t = jnp.maximum(inter, jnp.max(dmat, axis=-1, keepdims=True))
        w = jnp.exp(dmat - m_t)
        g = jnp.exp(inter - m_t)
        a = w * hd["s"]
        b_last = bc[t_last:t_last + 1, :]
        m_last = m_t[t_last:t_last + 1, :]
        w_end = jnp.where(t_idx <= t_last, jnp.exp(b_last - bc + ic - m_last), 0.0)
        hd.update(m_t=m_t, g=g, a=a, m_last=m_last, g_end=g[t_last:t_last + 1, :], w_end=w_end,
                  av=_dot(a.astype(BF16), hd["vb"]),
                  vk=_dot_tn((hd["vb"].astype(F32) * w_end).astype(BF16), hd["kb"]))

    for h, hd in enumerate(heads):
        hs, g, a, m_t = hd["hs"], hd["g"], hd["a"], hd["m_t"]
        n_prev = n_scr[h]
        num = g * hd["qc"] + hd["av"]
        den = g * jnp.sum(hd["q32"] * n_prev, axis=-1, keepdims=True) + jnp.sum(a, axis=-1, keepdims=True)
        hh = num / jnp.maximum(jnp.abs(den), jnp.exp(-m_t))
        c_scr[h] = hd["g_end"] * hd["c_prev"] + hd["vk"]
        n_scr[h] = hd["g_end"] * n_prev + jnp.sum(hd["k32"] * hd["w_end"], axis=0, keepdims=True)
        m_scr[h] = jnp.broadcast_to(hd["m_last"], (1, LANES))
        yh = hh * lax.rsqrt(jnp.mean(hh * hh, axis=-1, keepdims=True) + EPS) * gml_ref[:, hs]
        yh = yh * _sigmoid(o_all[:, hs].astype(F32))
        h_ref[:, hs] = yh[:rows].astype(h_ref.dtype)

    @pl.when(ci == nc - 1)
    def _():
        c_out[0] = c_scr[...]
        n_out[0] = n_scr[...]
        m_out[0] = m_scr[...]
        buf_out[0] = buf_ref[0] if preact else ext[t_last + 1:t_last + 9]


def _mlstm(mqk, mv, mo, gc, gr, c0, n0, m0, buf8, cw, cb, gml, batch, t_len):
    preact = mqk.dtype == BF16
    L = ML_CHUNK if t_len >= ML_CHUNK else LANES
    rows = min(L, t_len)
    nc = t_len // rows
    t_last = rows - 1
    assert t_len % rows == 0 and (nc == 1 or rows == L) and (t_last + 1) % 8 == 0
    tok = lambda c: pl.BlockSpec((rows, c), lambda b, i: (b * nc + i, 0))
    st = lambda *s: pl.BlockSpec((1,) + s, lambda b, i: (b,) + (0,) * len(s))
    shp_c = (ML_HEADS, ML_HEAD_DIM, ML_HEAD_DIM)
    shp_n = (ML_HEADS, 1, ML_HEAD_DIM)
    return pl.pallas_call(
        functools.partial(_mlstm_kernel, L=L, t_last=t_last, nc=nc, preact=preact),
        grid=(batch, nc),
        in_specs=[tok(2 * ML_WIDTH), tok(ML_WIDTH), tok(ML_WIDTH), tok(LANES),
                  pl.BlockSpec((1, 16, L), lambda b, i: (b, 0, i)),
                  st(*shp_c), st(*shp_n), st(*shp_n), st(8, 2 * ML_WIDTH),
                  _const_spec((ML_CONV, 2 * ML_WIDTH)), _const_spec((1, 2 * ML_WIDTH)), _const_spec((1, ML_WIDTH))],
        out_specs=[tok(ML_WIDTH), st(*shp_c), st(*shp_n), st(*shp_n), st(8, 2 * ML_WIDTH)],
        out_shape=[jax.ShapeDtypeStruct((batch * t_len, ML_WIDTH), BF16),
                   jax.ShapeDtypeStruct((batch,) + shp_c, F32),
                   jax.ShapeDtypeStruct((batch,) + shp_n, F32),
                   jax.ShapeDtypeStruct((batch,) + shp_n, F32),
                   jax.ShapeDtypeStruct((batch, 8, 2 * ML_WIDTH), F32)],
        scratch_shapes=[pltpu.VMEM(shp_c, F32), pltpu.VMEM(shp_n, F32), pltpu.VMEM(shp_n, F32),
                        pltpu.VMEM((8, 2 * ML_WIDTH), F32)],
        compiler_params=_cparams(("parallel", "arbitrary")),
        name="mlstm",
    )(mqk, mv, mo, gc, gr, c0, n0, m0, buf8, cw, cb, gml)


def _gla_level_matrices(L):
    mats = np.zeros((len(GLA_LEVELS), L, L), np.float32)
    for li, b in enumerate(GLA_LEVELS):
        for t in range(L):
            base = (t // (2 * b)) * 2 * b
            bound = base + b - 1
            if t > bound:
                mats[li, t, bound + 1:t + 1] = 1.0
            else:
                mats[li, t, t + 1:bound + 1] = 1.0
    return jnp.asarray(mats, BF16)


def _gla_kernel(q_ref, k_ref, v_ref, gg_ref, la_ref, s0_ref, lvl_ref, gn_ref, o_ref, s_out, s_scr, *, L, t_last, nc):
    ci = pl.program_id(1)
    nb, rows = q_ref.shape[0], q_ref.shape[1]

    @pl.when(ci == 0)
    def _():
        s_scr[...] = s0_ref[...]

    t_idx = lax.broadcasted_iota(jnp.int32, (L, 1), 0)
    s_idx = lax.broadcasted_iota(jnp.int32, (1, L), 1)
    tril = _tri(L, True)
    level_mask = []
    for b in GLA_LEVELS:
        sh = b.bit_length() - 1
        level_mask.append((jnp.right_shift(t_idx, sh + 1) == jnp.right_shift(s_idx, sh + 1))
                          & ((jnp.right_shift(t_idx, sh) & 1) == 1) & ((jnp.right_shift(s_idx, sh) & 1) == 0))

    els = []
    for bi in range(nb):
        la3 = _split2(_pad_rows(la_ref[bi], L))
        q_all = _pad_rows(q_ref[bi], L)
        k_all = _pad_rows(k_ref[bi], L)
        el = dict(la3=la3, cb=_mat_f32(tril, la3), v=_pad_rows(v_ref[bi], L), g=_pad_rows(gg_ref[bi], L),
                  q32=[], k32=[], a=[], e_next=jnp.exp(_mat_f32(lvl_ref[0], la3)), pending=None)
        for h in range(GLA_HEADS):
            ks = slice(h * GLA_DK, (h + 1) * GLA_DK)
            el["q32"].append(q_all[:, ks].astype(F32))
            el["k32"].append(k_all[:, ks].astype(F32))
            el["a"].append(jnp.where(t_idx == s_idx, _dot_nt(q_all[:, ks], k_all[:, ks]), 0.0))
        els.append(el)

    def settle(el):
        if el["pending"] is not None:
            prods, mask = el["pending"]
            el["a"] = [jnp.where(mask, prods[h], el["a"][h]) for h in range(GLA_HEADS)]

    for li in range(len(GLA_LEVELS)):
        for el in els:
            e_all = el["e_next"]
            if li + 1 < len(GLA_LEVELS):
                el["e_next"] = jnp.exp(_mat_f32(lvl_ref[li + 1], el["la3"]))
            prods = []
            for h in range(GLA_HEADS):
                e = e_all[:, h * GLA_DK:(h + 1) * GLA_DK]
                prods.append(_dot_nt((el["q32"][h] * e).astype(BF16), (el["k32"][h] * e).astype(BF16)))
            settle(el)
            el["pending"] = (prods, level_mask[li])
    for el in els:
        settle(el)

    for bi, el in enumerate(els):
        el["o"] = []
        for h in range(GLA_HEADS):
            ks = slice(h * GLA_DK, (h + 1) * GLA_DK)
            vs = slice(h * GLA_DV, (h + 1) * GLA_DV)
            cbh = el["cb"][:, ks]
            vb = el["v"][:, vs]
            s_t = s_scr[bi, h]
            o = (_dot_nt((el["q32"][h] * jnp.exp(cbh)).astype(BF16), s_t.astype(BF16))
                 + _dot(el["a"][h].astype(BF16), vb))
            cl = cbh[t_last:t_last + 1, :]
            kd = jnp.where(t_idx <= t_last, el["k32"][h] * jnp.exp(cl - cbh), 0.0).astype(BF16)
            s_scr[bi, h] = jnp.exp(cl) * s_t + _dot_tn(vb, kd)
            el["o"].append(o)
    for bi, el in enumerate(els):
        for h in range(GLA_HEADS):
            vs = slice(h * GLA_DV, (h + 1) * GLA_DV)
            o = el["o"][h]
            y = o * lax.rsqrt(jnp.mean(o * o, axis=-1, keepdims=True) + EPS) * gn_ref[:, vs]
            gate = el["g"][:, vs].astype(F32)
            y = y * (gate * _sigmoid(gate))
            o_ref[bi, :, vs] = y[:rows].astype(o_ref.dtype)

    @pl.when(ci == nc - 1)
    def _():
        s_out[...] = s_scr[...]


GLA_BATCH_PER_STEP = 4


def _gla(q, k, v, gg, la, s0t, gn, batch, t_len):
    L = GLA_CHUNK
    rows = min(L, t_len)
    nc = t_len // rows
    t_last = rows - 1
    nb = max(d for d in range(1, GLA_BATCH_PER_STEP + 1) if batch % d == 0)
    assert t_len % rows == 0 and (nc == 1 or rows == L)
    tok = lambda c: pl.BlockSpec((nb, rows, c), lambda b, i: (b, i, 0))
    shp_s = (GLA_HEADS, GLA_DV, GLA_DK)
    st = pl.BlockSpec((nb,) + shp_s, lambda b, i: (b, 0, 0, 0))
    levels = _gla_level_matrices(L)
    seq = lambda a: a.reshape(batch, t_len, a.shape[-1])
    o, s_new = pl.pallas_call(
        functools.partial(_gla_kernel, L=L, t_last=t_last, nc=nc),
        grid=(batch // nb, nc),
        in_specs=[tok(GLA_KW), tok(GLA_KW), tok(GLA_VW), tok(GLA_VW), tok(GLA_KW), st,
                  _const_spec(levels.shape), _const_spec((1, GLA_VW))],
        out_specs=[tok(GLA_VW), st],
        out_shape=[jax.ShapeDtypeStruct((batch, t_len, GLA_VW), BF16),
                   jax.ShapeDtypeStruct((batch,) + shp_s, F32)],
        scratch_shapes=[pltpu.VMEM((nb,) + shp_s, F32)],
        compiler_params=_cparams(("parallel", "arbitrary")),
        name="gla",
    )(seq(q), seq(k), seq(v), seq(gg), seq(la), s0t, levels, gn)
    return o.reshape(batch * t_len, GLA_VW), s_new


def _pack_even(w, b_fox_f, b_i, b_f):
    d = w.shape[0]
    o = np.cumsum((0, FOX_WIDTH, FOX_WIDTH, FOX_WIDTH, FOX_HEADS, 2 * ML_WIDTH, ML_WIDTH, ML_WIDTH, ML_HEADS, ML_HEADS))
    n_gate = FOX_HEADS + 2 * ML_HEADS
    wp = jnp.concatenate([w[:, o[0]:o[3]], w[:, o[4]:o[7]], w[:, o[3]:o[4]], w[:, o[7]:o[9]],
                          jnp.zeros((d, LANES - n_gate), w.dtype)], axis=1).astype(BF16)
    bias = jnp.concatenate([b_fox_f, b_i, b_f, jnp.zeros((LANES - n_gate,), F32)]).reshape(1, LANES)
    return wp, bias


def _pack_odd(w, w_a2):
    d = w.shape[0]
    wp = jnp.concatenate([w, jnp.zeros((d, LANES - GLA_RANK), w.dtype)], axis=1).astype(BF16)
    wa2 = jnp.concatenate([w_a2, jnp.zeros((LANES - GLA_RANK, w_a2.shape[1]), w_a2.dtype)], axis=0).astype(BF16)
    return wp, wa2


def _gate_rows(gc, batch, t_len, t_pad):
    g = gc.reshape(batch, t_len, LANES)[:, :, :16].transpose(0, 2, 1)
    if t_pad > t_len:
        g = jnp.pad(g, ((0, 0), (0, 0), (0, t_pad - t_len)))
    return g


def _trunk(x, fox_k, fox_v, fox_lf, ml_c, ml_n, ml_m, ml_buf, gla_s, params):
    (norm_mix, norm_ffn, norm_final, w_even, bias_even, conv_w, conv_b, g_ml,
     w_odd, w_a2, b_a, g_gla, w_out, w_ff1, w_ff2) = params
    batch, t_len, d = x.shape
    past = fox_k.shape[2]
    n = batch * t_len
    tm = 512 if n % 512 == 0 else 256 if n % 256 == 0 else n
    depth = norm_mix.shape[0]
    xf = x.reshape(n, d)
    ev_states, odd_states = [], []
    kv_stacked = None
    y = None
    for layer in range(depth):
        j = layer // 2
        if layer % 2 == 0:
            assert (t_len % tm == 0) == (past == 0)
            buf8 = jnp.pad(ml_buf[j], ((0, 0), (8 - (ML_CONV - 1), 0), (0, 0)))
            proj = _inproj_even(xf, norm_mix[layer][None], w_even[j], bias_even[j],
                                tm, batch, t_len, j, (depth + 1) // 2, kv_stacked, conv_w[j], conv_b[j][None], buf8)
            conv_tail = None
            if past == 0:
                qt, kf, vf, ka, vt, mqk, mv, mo, gc, bnd, conv_tail = proj
                kv_stacked = (kf, vf)
                assert tm == FOX_Q_TILE
                attn = _fox_attention(qt, ka, vt, bnd, batch, t_len, FOX_Q_TILE, FOX_KEY_CHUNK)
            else:
                q, kf, vf, mqk, mv, mo, gc = proj
                lf_new = gc.reshape(batch, t_len, LANES)[:, :, :FOX_HEADS]
                lf_row = jnp.concatenate([fox_lf[j], lf_new, jnp.zeros((batch, LANES - t_len, FOX_HEADS), F32)], axis=1)
                lf_row = jnp.pad(lf_row.transpose(0, 2, 1), ((0, 0), (0, 16 - FOX_HEADS), (0, 0)))
                cache_t = lambda a: a.transpose(0, 1, 3, 4, 2).reshape(a.shape[0], batch, FOX_WIDTH, past)
                new_t = lambda a: jnp.pad(a.reshape(batch, t_len, FOX_WIDTH).transpose(0, 2, 1).astype(BF16),
                                          ((0, 0), (0, 0), (0, LANES - t_len)))
                attn = _fox_decode(q.reshape(batch, t_len, FOX_WIDTH), cache_t(fox_k), cache_t(fox_v), j,
                                   new_t(kf), new_t(vf), lf_row).reshape(n, FOX_WIDTH)
            l_ml = ML_CHUNK if t_len >= ML_CHUNK else LANES
            gr = _gate_rows(gc, batch, t_len, max(t_len, l_ml))
            c0 = ml_c[j]
            n0 = ml_n[j][:, :, None, :]
            m0 = jnp.broadcast_to(ml_m[j][:, :, None, None], (batch, ML_HEADS, 1, LANES))
            h_ml, c_new, n_new, m_new, buf_new = _mlstm(mqk, mv, mo, gc, gr, c0, n0, m0, buf8,
                                                        conv_w[j], conv_b[j][None], g_ml[j][None], batch, t_len)
            if conv_tail is not None:
                buf_new = conv_tail
            heads = lambda a: None if past == 0 else a.reshape(batch, t_len, FOX_HEADS, FOX_HEAD_DIM)
            ev_states.append((heads(kf), heads(vf),
                              gc[:, :FOX_HEADS].reshape(batch, t_len, FOX_HEADS),
                              c_new, n_new[:, :, 0, :], m_new[:, :, 0, 0], buf_new[:, 8 - (ML_CONV - 1):, :]))
            mixes = [attn, h_ml]
        else:
            q, k, v, gg, la = _inproj_odd(xf, norm_mix[layer][None], w_odd[j], w_a2[j], b_a[j][None], tm)
            s0t = gla_s[j].transpose(0, 1, 3, 2)
            o, s_new = _gla(q, k, v, gg, la, s0t, g_gla[j][None], batch, t_len)
            odd_states.append(s_new.transpose(0, 1, 3, 2))
            mixes = [o]
        last = layer == depth - 1
        out = _post(xf, mixes, w_out[layer], norm_ffn[layer][None], w_ff1[layer], w_ff2[layer],
                    norm_final[None] if last else None, tm)
        if last:
            y = out
        else:
            xf = out
    ev = [jnp.stack([s[i] for s in ev_states]) for i in range(2 if kv_stacked else 0, 7)]
    if kv_stacked:
        ev = [a.reshape(a.shape[0], batch, FOX_HEADS, FOX_HEAD_DIM, t_len).transpose(0, 1, 4, 2, 3)
              for a in kv_stacked] + ev
    return y.reshape(batch, t_len, d), ev, jnp.stack(odd_states)


def kernel(x_prompt, x_sample, cache_fox_k, cache_fox_v, cache_fox_logf, state_mlstm_c, state_mlstm_n, state_mlstm_m, state_mlstm_conv, state_gla_s, norm_mix, norm_ffn, norm_final, w_in_even, b_fox_f, conv_w_ml, conv_b_ml, b_ml_i, b_ml_f, g_ml, w_in_odd, w_gla_a2, b_gla_a, g_gla, w_out, w_ff1, w_ff2):
    n_even, n_odd = w_in_even.shape[0], w_in_odd.shape[0]
    packed_even = [_pack_even(w_in_even[j], b_fox_f[j], b_ml_i[j], b_ml_f[j]) for j in range(n_even)]
    packed_odd = [_pack_odd(w_in_odd[j], w_gla_a2[j]) for j in range(n_odd)]
    params = (norm_mix, norm_ffn, norm_final,
              [p[0] for p in packed_even], [p[1] for p in packed_even], conv_w_ml, conv_b_ml, g_ml,
              [p[0] for p in packed_odd], [p[1] for p in packed_odd], b_gla_a, g_gla,
              w_out.astype(BF16), w_ff1.astype(BF16), w_ff2.astype(BF16))

    bp = x_prompt.shape[0]
    dt = x_prompt.dtype
    zeros = lambda *s: jnp.zeros(s, dt)
    y_p, ev_p, gla_p = _trunk(
        x_prompt,
        zeros(n_even, bp, 0, FOX_HEADS, FOX_HEAD_DIM), zeros(n_even, bp, 0, FOX_HEADS, FOX_HEAD_DIM),
        zeros(n_even, bp, 0, FOX_HEADS),
        zeros(n_even, bp, ML_HEADS, ML_HEAD_DIM, ML_HEAD_DIM), zeros(n_even, bp, ML_HEADS, ML_HEAD_DIM),
        zeros(n_even, bp, ML_HEADS), zeros(n_even, bp, ML_CONV - 1, 2 * ML_WIDTH),
        zeros(n_odd, bp, GLA_HEADS, GLA_DK, GLA_DV), params)
    y_s, ev_s, gla_s = _trunk(x_sample, cache_fox_k, cache_fox_v, cache_fox_logf, state_mlstm_c, state_mlstm_n,
                              state_mlstm_m, state_mlstm_conv, state_gla_s, params)
    return (y_p, y_s, *ev_p, gla_p, *ev_s, gla_s)
```

```python
import functools

import numpy as np
import jax
import jax.numpy as jnp
from jax import lax
from jax.experimental import pallas as pl
from jax.experimental.pallas import tpu as pltpu

F32 = jnp.float32
BF16 = jnp.bfloat16
EPS = 1e-6
NEG = -1e30
LOG2E = 1.4426950408889634

LANES = 128
VMEM_LIMIT = 56 * 1024 * 1024

D_MODEL = 1024
D_FF = 4 * D_MODEL
FOX_HEADS, FOX_HEAD_DIM = 8, 64
FOX_WIDTH = FOX_HEADS * FOX_HEAD_DIM
ML_HEADS, ML_HEAD_DIM = 4, 128
ML_WIDTH = ML_HEADS * ML_HEAD_DIM
ML_CONV = 4
GLA_HEADS, GLA_DK, GLA_DV = 4, 128, 256
GLA_KW = GLA_HEADS * GLA_DK
GLA_VW = GLA_HEADS * GLA_DV
GLA_RANK = 16
GLA_TAU = 16.0

E_Q, E_K, E_V, E_MQK, E_MV, E_MO, E_G, E_END = np.cumsum(
    (0, FOX_WIDTH, FOX_WIDTH, FOX_WIDTH, 2 * ML_WIDTH, ML_WIDTH, ML_WIDTH, LANES)).tolist()
G_FOX, G_MI, G_MF = 0, FOX_HEADS, FOX_HEADS + ML_HEADS
O_Q, O_K, O_V, O_G, O_A, O_END = np.cumsum((0, GLA_KW, GLA_KW, GLA_VW, GLA_VW, LANES)).tolist()

ML_CHUNK = 256
FOX_KEY_CHUNK = 256
FOX_Q_TILE = 512
GLA_CHUNK = 128
GLA_LEVELS = (64, 32, 16, 8, 4, 2, 1)


def _cparams(sem):
    return pltpu.CompilerParams(dimension_semantics=sem, vmem_limit_bytes=VMEM_LIMIT)


def _const_spec(shape):
    nd = len(shape)
    return pl.BlockSpec(shape, lambda *_: (0,) * nd, pipeline_mode=pl.Buffered(1))


def _rms(x, g):
    return x * lax.rsqrt(jnp.mean(x * x, axis=-1, keepdims=True) + EPS) * g


def _sigmoid(x):
    return 1.0 / (1.0 + jnp.exp(-x))


def _log_sigmoid(x):
    return -(jnp.maximum(-x, 0.0) + jnp.log1p(jnp.exp(-jnp.abs(x))))


def _dot(a, b):
    return jnp.dot(a, b, preferred_element_type=F32)


def _dot_nt(a, b):
    return lax.dot_general(a, b, (((1,), (1,)), ((), ())), preferred_element_type=F32)


def _dot_tn(a, b):
    return lax.dot_general(a, b, (((0,), (0,)), ((), ())), preferred_element_type=F32)


def _split3(x):
    hi = x.astype(BF16)
    r1 = x - hi.astype(F32)
    mid = r1.astype(BF16)
    lo = (r1 - mid.astype(F32)).astype(BF16)
    return hi, mid, lo


def _split2(x):
    hi = x.astype(BF16)
    return hi, (x - hi.astype(F32)).astype(BF16)


def _mat_f32(m, parts):
    return functools.reduce(lambda a, b: a + b, [_dot(m, p) for p in parts])


def _f32_mat(parts, m):
    return functools.reduce(lambda a, b: a + b, [_dot(p, m) for p in parts])


def _tri(n, lower):
    r = lax.broadcasted_iota(jnp.int32, (n, n), 0)
    c = lax.broadcasted_iota(jnp.int32, (n, n), 1)
    keep = (c <= r) if lower else (r <= c)
    return jnp.where(keep, 1.0, 0.0).astype(BF16)


def _pad_rows(a, n):
    if a.shape[0] == n:
        return a
    return jnp.concatenate([a, jnp.zeros((n - a.shape[0], a.shape[1]), a.dtype)], axis=0)


def _conv_silu(u, prev, cw_ref, cb_ref):
    n = u.shape[0]
    ext = jnp.concatenate([prev, u], axis=0)
    y = cb_ref[...] + cw_ref[ML_CONV - 1:ML_CONV, :] * u
    for s in range(1, ML_CONV):
        y = y + cw_ref[ML_CONV - 1 - s:ML_CONV - s, :] * pltpu.roll(ext, s, axis=0)[8:8 + n]
    lane = lax.broadcasted_iota(jnp.int32, (1, 2 * ML_WIDTH), 1)
    return y * _sigmoid(y) * jnp.where(lane >= ML_WIDTH, ML_HEAD_DIM ** -0.5, 1.0), ext


FOX_Q_SCALE = FOX_HEAD_DIM ** -0.5 * LOG2E


def _even_segments(x_ref, g_ref, w_ref):
    h = _rms(x_ref[...], g_ref[...]).astype(BF16)
    return lambda a, b: _dot(h, w_ref[:, a:b])


def _even_gates(seg, bias_ref):
    gz = seg(E_G, E_END) + bias_ref[...]
    lane = lax.broadcasted_iota(jnp.int32, gz.shape, 1)
    is_log = (lane < G_MI) | (lane >= G_MF)
    return jnp.where(is_log, _log_sigmoid(gz), gz)


def _inproj_even_rows_kernel(x_ref, g_ref, w_ref, bias_ref,
                             q_ref, kf_ref, vf_ref, mqk_ref, mv_ref, mo_ref, gc_ref):
    seg = _even_segments(x_ref, g_ref, w_ref)
    q_ref[...] = (seg(E_Q, E_K) * FOX_Q_SCALE).astype(BF16)
    kf_ref[...] = seg(E_K, E_V)
    vf_ref[...] = seg(E_V, E_MQK)
    mqk_ref[...] = seg(E_MQK, E_MV)
    mv_ref[...] = seg(E_MV, E_MO).astype(BF16)
    mo_ref[...] = seg(E_MO, E_G).astype(BF16)
    gc_ref[...] = _even_gates(seg, bias_ref)


def _inproj_even_seq_kernel(*refs, tpb, aliased):
    x_ref, g_ref, w_ref, bias_ref, ind_ref, cw_ref, cb_ref, buf_ref = refs[:8]
    refs = refs[8 + (2 if aliased else 0):]
    (qt_ref, kf_ref, vf_ref, ka_ref, vt_ref, mqk_ref, mv_ref, mo_ref, gc_ref, bnd_ref, tail_ref,
     carry, conv_prev) = refs
    @pl.when(pl.program_id(0) % tpb == 0)
    def _():
        carry[...] = jnp.zeros_like(carry)
        conv_prev[...] = buf_ref[0]

    seg = _even_segments(x_ref, g_ref, w_ref)
    mqk = seg(E_MQK, E_MV)
    gates = _even_gates(seg, bias_ref)
    gc_ref[...] = gates
    q = seg(E_Q, E_K) * FOX_Q_SCALE
    k = seg(E_K, E_V)
    v = seg(E_V, E_MQK)
    qt_ref[...] = q.T.astype(BF16)
    mv = seg(E_MV, E_MO)
    kf_ref[0, 0] = k.T
    f = _append_f_terms(gates, carry, k.astype(BF16), ka_ref)
    mo = seg(E_MO, E_G)
    act, _ = _conv_silu(mqk, conv_prev[...], cw_ref, cb_ref)
    mqk_ref[...] = act.astype(BF16)
    last_rows = mqk[mqk.shape[0] - 8:]
    conv_prev[...] = last_rows
    tail_ref[0] = last_rows
    mv_ref[...] = mv.astype(BF16)
    v_t = v.T
    vt_ref[...] = v_t.astype(BF16)
    vf_ref[0, 0] = v_t
    mo_ref[...] = mo.astype(BF16)
    norms = _dot((q * q).astype(BF16), ind_ref[0]) + _dot((k * k).astype(BF16), ind_ref[1])
    bnd_ref[...] = jnp.zeros_like(bnd_ref)
    bnd_ref[0, 0:1, :] = f[0:1, :]
    for c in range(f.shape[0] // FOX_KEY_CHUNK):
        bnd_ref[0, 1 + c:2 + c, :] = f[(c + 1) * FOX_KEY_CHUNK - 1:(c + 1) * FOX_KEY_CHUNK, :]
    bnd_ref[0, 7:8, :] = jnp.max(norms, axis=0, keepdims=True)


def _inproj_even(x, g, w, bias, tm, batch, t_len, slot, n_slots, kv_prev, cw, cb, buf8):
    n = x.shape[0]
    row = lambda c: (pl.BlockSpec((tm, c), lambda i: (i, 0)), (n, c))
    col = (pl.BlockSpec((FOX_WIDTH, tm), lambda i: (0, i)), (FOX_WIDTH, n))
    common_in = [row(D_MODEL)[0], _const_spec((1, D_MODEL)), _const_spec((D_MODEL, E_END)), _const_spec((1, LANES))]
    tail = [(row(ML_WIDTH), BF16), (row(ML_WIDTH), BF16), (row(LANES), F32)]
    if t_len % tm != 0:
        outs = [(row(FOX_WIDTH), BF16), (row(FOX_WIDTH), F32), (row(FOX_WIDTH), F32), (row(2 * ML_WIDTH), F32)] + tail
        return pl.pallas_call(
            _inproj_even_rows_kernel,
            grid=(n // tm,),
            in_specs=common_in,
            out_specs=[spec for (spec, _), _ in outs],
            out_shape=[jax.ShapeDtypeStruct(shape, dt) for (_, shape), dt in outs],
            compiler_params=_cparams(("parallel",)),
            name="inproj_even_rows",
        )(x, g, w, bias)
    tpb = t_len // tm
    state = (pl.BlockSpec((1, 1, FOX_WIDTH, tm), lambda i: (slot, i // tpb, 0, i % tpb)),
             (n_slots, batch, FOX_WIDTH, t_len))
    per_seq = (pl.BlockSpec((1, 8, 2 * ML_WIDTH), lambda i: (i // tpb, 0, 0)), (batch, 8, 2 * ML_WIDTH))
    outs = ([(col, BF16), (state, F32), (state, F32), (row(2 * FOX_WIDTH), BF16), (col, BF16), (row(2 * ML_WIDTH), BF16)]
            + tail + [((pl.BlockSpec((1, 8, LANES), lambda i: (i, 0, 0)), (n // tm, 8, LANES)), F32), (per_seq, F32)])
    assert tm % FOX_KEY_CHUNK == 0 and tm // FOX_KEY_CHUNK <= 6
    ind = np.zeros((2, FOX_WIDTH, LANES), np.float32)
    for c in range(FOX_WIDTH):
        ind[0, c, FOX_HEADS + c // FOX_HEAD_DIM] = 1.0
        ind[1, c, 2 * FOX_HEADS + c // FOX_HEAD_DIM] = 1.0
    ins = [x, g, w, bias, jnp.asarray(ind, BF16), cw, cb, buf8]
    in_specs = common_in + [_const_spec(ind.shape), _const_spec(cw.shape), _const_spec(cb.shape), per_seq[0]]
    aliases = {}
    if kv_prev is not None:
        aliases = {len(ins): 1, len(ins) + 1: 2}
        ins += list(kv_prev)
        in_specs += [pl.BlockSpec(memory_space=pl.ANY)] * 2
    return pl.pallas_call(
        functools.partial(_inproj_even_seq_kernel, tpb=tpb, aliased=kv_prev is not None),
        grid=(n // tm,),
        in_specs=in_specs,
        out_specs=[spec for (spec, _), _ in outs],
        out_shape=[jax.ShapeDtypeStruct(shape, dt) for (_, shape), dt in outs],
        scratch_shapes=[pltpu.VMEM((1, LANES), F32), pltpu.VMEM((8, 2 * ML_WIDTH), F32)],
        input_output_aliases=aliases,
        compiler_params=_cparams(("arbitrary",)),
        name="inproj_even_seq",
    )(*ins)


def _inproj_odd_kernel(x_ref, g_ref, w_ref, wa2_ref, ba_ref, q_ref, k_ref, v_ref, gg_ref, la_ref):
    h = _rms(x_ref[...], g_ref[...]).astype(BF16)

    def seg(a, b):
        return _dot(h, w_ref[:, a:b])

    q_ref[...] = (seg(O_Q, O_K) * (GLA_DK ** -0.5)).astype(BF16)
    k_ref[...] = seg(O_K, O_V).astype(BF16)
    v_ref[...] = seg(O_V, O_G).astype(BF16)
    gg_ref[...] = seg(O_G, O_A).astype(BF16)
    ga = seg(O_A, O_END).astype(BF16)
    la_ref[...] = _log_sigmoid(_dot(ga, wa2_ref[...]) + ba_ref[...]) * (1.0 / GLA_TAU)


def _inproj_odd(x, g, w, wa2, ba, tm):
    n = x.shape[0]
    row = lambda c: pl.BlockSpec((tm, c), lambda i: (i, 0))
    outs = [(GLA_KW, BF16), (GLA_KW, BF16), (GLA_VW, BF16), (GLA_VW, BF16), (GLA_KW, F32)]
    return pl.pallas_call(
        _inproj_odd_kernel,
        grid=(n // tm,),
        in_specs=[row(D_MODEL), _const_spec((1, D_MODEL)), _const_spec((D_MODEL, O_END)),
                  _const_spec((LANES, GLA_KW)), _const_spec((1, GLA_KW))],
        out_specs=[row(c) for c, _ in outs],
        out_shape=[jax.ShapeDtypeStruct((n, c), dt) for c, dt in outs],
        compiler_params=_cparams(("parallel",)),
        name="inproj_odd",
    )(x, g, w, wa2, ba)


def _post_kernel(*refs, n_mix, final):
    x_ref = refs[0]
    mix_refs = refs[1:1 + n_mix]
    wo_ref, gf_ref, w1_ref, w2_ref = refs[1 + n_mix:5 + n_mix]
    rest = refs[5 + n_mix:]
    mix = mix_refs[0][...] if n_mix == 1 else jnp.concatenate([r[...] for r in mix_refs], axis=1)
    x1 = x_ref[...] + _dot(mix, wo_ref[...])
    h = _rms(x1, gf_ref[...]).astype(BF16)
    y = x1
    for c in range(D_FF // D_MODEL):
        sl = slice(c * D_MODEL, (c + 1) * D_MODEL)
        t = jnp.maximum(_dot(h, w1_ref[:, sl]), 0.0)
        y = y + _dot((t * t).astype(BF16), w2_ref[sl, :])
    if final:
        gfin_ref, out_ref = rest
        out_ref[...] = _rms(y, gfin_ref[...])
    else:
        (out_ref,) = rest
        out_ref[...] = y


def _post(x, mixes, wo, gf, w1, w2, gfin, tm):
    n = x.shape[0]
    row = lambda c: pl.BlockSpec((tm, c), lambda i: (i, 0))
    final = gfin is not None
    ins = [x, *mixes, wo, gf, w1, w2]
    specs = [row(D_MODEL)] + [row(m.shape[1]) for m in mixes] + [
        _const_spec((D_MODEL, D_MODEL)), _const_spec((1, D_MODEL)),
        _const_spec((D_MODEL, D_FF)), _const_spec((D_FF, D_MODEL))]
    if final:
        ins.append(gfin)
        specs.append(_const_spec((1, D_MODEL)))
    return pl.pallas_call(
        functools.partial(_post_kernel, n_mix=len(mixes), final=final),
        grid=(n // tm,),
        in_specs=specs,
        out_specs=row(D_MODEL),
        out_shape=jax.ShapeDtypeStruct((n, D_MODEL), F32),
        compiler_params=_cparams(("parallel",)),
        name="post_final" if final else "post",
    )(*ins)


F_TERMS = 3


def _append_f_terms(lf, carry, k, ka_ref):
    tc = lf.shape[0]
    f = _mat_f32(_tri(tc, True), _split3(lf)) + carry[...]
    carry[...] = f[tc - 1:tc, :]
    hi, mid, lo = [p.astype(F32) for p in _split3(f * LOG2E)]
    lane = lax.broadcasted_iota(jnp.int32, (1, LANES), 1)
    cols = jnp.where(lane < FOX_HEADS, hi,
                     jnp.where(lane < 2 * FOX_HEADS, pltpu.roll(mid, FOX_HEADS, axis=1),
                               jnp.where(lane < 3 * FOX_HEADS, pltpu.roll(lo, 2 * FOX_HEADS, axis=1), 0.0)))
    cols = cols.astype(BF16)
    for hp in range(FOX_HEADS // 2):
        ka_ref[:, 2 * hp * LANES:(2 * hp + 1) * LANES] = k[:, hp * LANES:(hp + 1) * LANES]
        ka_ref[:, (2 * hp + 1) * LANES:(2 * hp + 2) * LANES] = cols
    return f


def _fox_decode_kernel(q_ref, kc_ref, vc_ref, kn_ref, vn_ref, lf_ref, o_ref, *, t_new, past):
    nkeys = past + LANES
    lane_f = lax.broadcasted_iota(jnp.int32, (1, FOX_WIDTH), 1)
    head_mask = [jnp.where((lane_f >= h * FOX_HEAD_DIM) & (lane_f < (h + 1) * FOX_HEAD_DIM), 1.0, 0.0)
                 for h in range(FOX_HEADS)]
    q = q_ref[0].astype(F32)
    qb = jnp.concatenate([q * hm for hm in head_mask], axis=0).astype(BF16)

    def keys(cache_ref, new_ref):
        return jnp.concatenate([cache_ref[0, 0].astype(BF16), new_ref[0]], axis=1)

    s = _dot(qb, keys(kc_ref, kn_ref))
    lf = lf_ref[0]
    triu = _tri(LANES, False)
    carry = jnp.zeros((lf.shape[0], 1), F32)
    blocks = []
    for c in range(nkeys // LANES):
        cs = _f32_mat(_split3(lf[:, c * LANES:(c + 1) * LANES]), triu) + carry
        carry = cs[:, LANES - 1:LANES]
        blocks.append(cs)
    f_all = jnp.concatenate(blocks, axis=1) * LOG2E
    s = s - jnp.concatenate([jnp.broadcast_to(f_all[h:h + 1, :], (t_new, nkeys)) for h in range(FOX_HEADS)], axis=0)
    kpos = lax.broadcasted_iota(jnp.int32, (1, nkeys), 1)
    qpos = past + (lax.broadcasted_iota(jnp.int32, (FOX_HEADS * t_new, 1), 0) & (t_new - 1))
    s = jnp.where(kpos <= qpos, s, NEG)
    p = jnp.exp2(s - jnp.max(s, axis=-1, keepdims=True))
    ob = _dot_nt(p.astype(BF16), keys(vc_ref, vn_ref)) / jnp.sum(p, axis=-1, keepdims=True)
    out = ob[0:t_new] * head_mask[0]
    for h in range(1, FOX_HEADS):
        out = out + ob[h * t_new:(h + 1) * t_new] * head_mask[h]
    o_ref[0] = out.astype(o_ref.dtype)


def _fox_decode(q, k_cache, v_cache, layer, k_new, v_new, lf_row):
    batch, t_new, _ = q.shape
    past = k_cache.shape[3]
    assert t_new & (t_new - 1) == 0 and t_new <= LANES and t_new % 16 == 0 and past % LANES == 0
    blk = lambda a: pl.BlockSpec((1,) + a.shape[1:], lambda b: (b, 0, 0))
    cache = pl.BlockSpec((1, 1, FOX_WIDTH, past), lambda b: (layer, b, 0, 0))
    args = (q, k_cache, v_cache, k_new, v_new, lf_row)
    return pl.pallas_call(
        functools.partial(_fox_decode_kernel, t_new=t_new, past=past),
        grid=(batch,),
        in_specs=[blk(q), cache, cache, blk(k_new), blk(v_new), blk(lf_row)],
        out_specs=blk(q),
        out_shape=jax.ShapeDtypeStruct(q.shape, BF16),
        compiler_params=_cparams(("parallel",)),
        name="fox_decode",
    )(*args)


ONES_ROWS = 16


def _fox_attn_kernel(first_ref, qt_ref, ka_ref, vt_ref, o_ref, s_scr, acc_scr, *, tq, tkc, past, n_diag):
    q_first = past + pl.program_id(2) * tq
    n_full = q_first // tkc
    drow = lax.broadcasted_iota(jnp.int32, (LANES, 1), 0)
    lane = lax.broadcasted_iota(jnp.int32, (1, 2 * tq), 1)
    lane_head = jnp.where(lane < tq, 0, 1)
    own_head = jnp.where(jnp.where(drow < FOX_HEAD_DIM, 0, 1) == lane_head, 1.0, 0.0).astype(BF16)
    qt = qt_ref[...]
    q2 = jnp.concatenate([qt, qt], axis=1) * own_head
    head = 2 * pl.program_id(1) + lane_head
    f_sel = ((drow & (FOX_HEADS - 1)) == head) & (drow < F_TERMS * FOX_HEADS)
    qa = jnp.concatenate([q2, jnp.where(f_sel, -1.0, 0.0).astype(BF16)], axis=0)
    ones = jnp.ones((ONES_ROWS, tkc), BF16)
    acc_scr[...] = jnp.zeros_like(acc_scr)

    qa_h = [qa[:, hh * tq:(hh + 1) * tq] for hh in range(2)]
    qpos = q_first + lax.broadcasted_iota(jnp.int32, (1, tq), 1)

    def produce(j, hh, buf):
        start = pl.multiple_of(j * tkc, tkc)
        s_scr[buf, hh] = _dot(ka_ref[pl.ds(start, tkc), :], qa_h[hh])

    def consume(j, hh, buf, m_prev, masked):
        start = pl.multiple_of(j * tkc, tkc)
        st = s_scr[buf, hh]
        if masked:
            kpos = start + lax.broadcasted_iota(jnp.int32, (tkc, 1), 0)
            st = jnp.where(kpos <= qpos, st, NEG)
        m_new = jnp.maximum(m_prev, jnp.max(st, axis=0, keepdims=True))
        alpha = jnp.exp2(m_prev - m_new)
        p = jnp.exp2(st - m_new).astype(BF16)
        rows = slice(hh * FOX_HEAD_DIM, (hh + 1) * FOX_HEAD_DIM)
        va = jnp.concatenate([vt_ref[rows, pl.ds(start, tkc)], ones], axis=0)
        acc_scr[hh] = alpha * acc_scr[hh] + _dot(va, p)
        return m_new

    nq = pl.num_programs(2)
    slot = ((pl.program_id(0) * pl.num_programs(1) + pl.program_id(1)) * nq + pl.program_id(2)) * 2
    first = [first_ref[slot], first_ref[slot + 1]]
    joint = jnp.maximum(first[0], first[1])
    for hh in range(2):
        produce(first[hh], hh, 0)

    def pair_of_chunks(i, ms, heads):
        ms = list(ms)
        for step in range(2):
            for hh in heads:
                produce(2 * i + step + 1, hh, 1 - step)
                ms[hh] = consume(2 * i + step, hh, step, ms[hh], False)
        return tuple(ms)

    m_init = jnp.full((1, tq), NEG, F32)
    ms = (m_init, m_init)
    for hh in range(2):
        ms = lax.fori_loop(first[hh] // 2, joint // 2, functools.partial(pair_of_chunks, heads=(hh,)), ms)
    ms = list(lax.fori_loop(joint // 2, n_full // 2, functools.partial(pair_of_chunks, heads=(0, 1)), ms))
    for d in range(n_diag):
        for hh in range(2):
            if d + 1 < n_diag:
                produce(n_full + d + 1, hh, (d + 1) % 2)
            ms[hh] = consume(n_full + d, hh, d % 2, ms[hh], True)
    out = jnp.concatenate([acc_scr[hh, 0:FOX_HEAD_DIM] / acc_scr[hh, FOX_HEAD_DIM:FOX_HEAD_DIM + 1]
                           for hh in range(2)], axis=0)
    o_ref[...] = out.T.astype(o_ref.dtype)


SKIP_MARGIN = 40.0
NORM_SLACK = 1.02


def _fox_first_chunks(bnd, batch, t_len, tq, tkc):
    nq, nc, pairs = t_len // tq, t_len // tkc, FOX_HEADS // 2
    per_tile = tq // tkc
    b4 = bnd.reshape(batch, nq, 8, LANES)
    f_start = b4[:, :, 0, 0:FOX_HEADS] * LOG2E
    f_end = b4[:, :, 1:1 + per_tile, 0:FOX_HEADS].reshape(batch, nc, FOX_HEADS) * LOG2E
    qn = jnp.sqrt(b4[:, :, 7, FOX_HEADS:2 * FOX_HEADS])
    kn = jnp.sqrt(jnp.max(b4[:, :, 7, 2 * FOX_HEADS:3 * FOX_HEADS], axis=1))
    thr = 2.0 * NORM_SLACK * qn * kn[:, None, :] + SKIP_MARGIN
    decay = f_end[:, None, :, :] - f_start[:, :, None, :]
    before_tile = jnp.arange(nc)[None, :] < ((jnp.arange(nq) * tq) // tkc)[:, None]
    ok = (decay > thr[:, :, None, :]) & before_tile[None, :, :, None]
    lead = jnp.sum(jnp.cumprod(ok.astype(jnp.int32), axis=2), axis=2)
    lead = ((lead // 2) * 2).reshape(batch, nq, pairs, 2)
    return lead.transpose(0, 2, 1, 3).reshape(-1).astype(jnp.int32)


def _fox_attention(qt, ka, vt, bnd, batch, t_len, tq, tkc):
    past = 0
    nq = t_len // tq
    n_diag = max(1, tq // tkc)
    assert tq & (tq - 1) == 0 and tq % LANES == 0 and (tkc % tq == 0 or tq % tkc == 0) and t_len % tq == 0
    assert nq == 1 or tq % (2 * tkc) == 0
    assert t_len >= n_diag * tkc and t_len % tkc == 0
    pairs = FOX_HEADS // 2
    first = _fox_first_chunks(bnd, batch, t_len, tq, tkc)
    grid_spec = pltpu.PrefetchScalarGridSpec(
        num_scalar_prefetch=1,
        grid=(batch, pairs, nq),
        in_specs=[pl.BlockSpec((LANES, tq), lambda b, h, i, f: (h, b * nq + i)),
                  pl.BlockSpec((t_len, 2 * LANES), lambda b, h, i, f: (b, h)),
                  pl.BlockSpec((LANES, t_len), lambda b, h, i, f: (h, b))],
        out_specs=pl.BlockSpec((tq, LANES), lambda b, h, i, f: (b * nq + i, h)),
        scratch_shapes=[pltpu.VMEM((2, 2, tkc, tq), F32),
                        pltpu.VMEM((2, FOX_HEAD_DIM + ONES_ROWS, tq), F32)])
    return pl.pallas_call(
        functools.partial(_fox_attn_kernel, tq=tq, tkc=tkc, past=past, n_diag=n_diag),
        grid_spec=grid_spec,
        out_shape=jax.ShapeDtypeStruct((batch * t_len, FOX_WIDTH), BF16),
        compiler_params=_cparams(("parallel", "parallel", "arbitrary")),
        name="fox_attention",
    )(first, qt, ka, vt)


def _mlstm_kernel(mqk_ref, mv_ref, mo_ref, gc_ref, gr_ref, c0_ref, n0_ref, m0_ref, buf_ref,
                  cw_ref, cb_ref, gml_ref,
                  h_ref, c_out, n_out, m_out, buf_out,
                  c_scr, n_scr, m_scr, prev_scr, *, L, t_last, nc, preact):
    ci = pl.program_id(1)
    rows = mqk_ref.shape[0]

    @pl.when(ci == 0)
    def _():
        c_scr[...] = c0_ref[0]
        n_scr[...] = n0_ref[0]
        m_scr[...] = m0_ref[0]
        prev_scr[...] = buf_ref[0]

    if preact:
        qk = _pad_rows(mqk_ref[...], L).astype(F32)
        ext = None
    else:
        u = _pad_rows(mqk_ref[...], L)
        qk, ext = _conv_silu(u, prev_scr[...], cw_ref, cb_ref)
        prev_scr[...] = u[L - 8:L]

    gc = _pad_rows(gc_ref[...], L)
    gr = gr_ref[0]
    b_col = _mat_f32(_tri(L, True), _split3(gc))
    b_row = _f32_mat(_split3(gr), _tri(L, False))
    v_all = _pad_rows(mv_ref[...], L)
    o_all = _pad_rows(mo_ref[...], L)
    t_idx = lax.broadcasted_iota(jnp.int32, (L, 1), 0)
    causal = lax.broadcasted_iota(jnp.int32, (1, L), 1) <= t_idx

    heads = []
    for h in range(ML_HEADS):
        hs = slice(h * ML_HEAD_DIM, (h + 1) * ML_HEAD_DIM)
        q32 = qk[:, hs]
        k32 = qk[:, ML_WIDTH + h * ML_HEAD_DIM:ML_WIDTH + (h + 1) * ML_HEAD_DIM]
        qb, kb = q32.astype(BF16), k32.astype(BF16)
        c_prev = c_scr[h]
        heads.append(dict(hs=hs, q32=q32, k32=k32, qb=qb, kb=kb, vb=v_all[:, hs], c_prev=c_prev,
                          s=_dot_nt(qb, kb), qc=_dot_nt(qb, c_prev.astype(BF16))))

    for h, hd in enumerate(heads):
        bc = b_col[:, G_MF + h:G_MF + h + 1]
        ic = gc[:, G_MI + h:G_MI + h + 1]
        br = b_row[G_MF + h:G_MF + h + 1, :]
        ir = gr[G_MI + h:G_MI + h + 1, :]
        m_prev = m_scr[h][:, 0:1]
        dmat = jnp.where(causal, bc + (ir - br), NEG)
        inter = bc + m_prev
        m_t = jnp.maximum(inter, jnp.max(dmat, axis=-1, keepdims=True))
        w = jnp.exp(dmat - m_t)
        g = jnp.exp(inter - m_t)
        a = w * hd["s"]
        b_last = bc[t_last:t_last + 1, :]
        m_last = m_t[t_last:t_last + 1, :]
        w_end = jnp.where(t_idx <= t_last, jnp.exp(b_last - bc + ic - m_last), 0.0)
        hd.update(m_t=m_t, g=g, a=a, m_last=m_last, g_end=g[t_last:t_last + 1, :], w_end=w_end,
                  av=_dot(a.astype(BF16), hd["vb"]),
                  vk=_dot_tn((hd["vb"].astype(F32) * w_end).astype(BF16), hd["kb"]))

    for h, hd in enumerate(heads):
        hs, g, a, m_t = hd["hs"], hd["g"], hd["a"], hd["m_t"]
        n_prev = n_scr[h]
        num = g * hd["qc"] + hd["av"]
        den = g * jnp.sum(hd["q32"] * n_prev, axis=-1, keepdims=True) + jnp.sum(a, axis=-1, keepdims=True)
        hh = num / jnp.maximum(jnp.abs(den), jnp.exp(-m_t))
        c_scr[h] = hd["g_end"] * hd["c_prev"] + hd["vk"]
        n_scr[h] = hd["g_end"] * n_prev + jnp.sum(hd["k32"] * hd["w_end"], axis=0, keepdims=True)
        m_scr[h] = jnp.broadcast_to(hd["m_last"], (1, LANES))
        yh = hh * lax.rsqrt(jnp.mean(hh * hh, axis=-1, keepdims=True) + EPS) * gml_ref[:, hs]
        yh = yh * _sigmoid(o_all[:, hs].astype(F32))
        h_ref[:, hs] = yh[:rows].astype(h_ref.dtype)

    @pl.when(ci == nc - 1)
    def _():
        c_out[0] = c_scr[...]
        n_out[0] = n_scr[...]
        m_out[0] = m_scr[...]
        buf_out[0] = buf_ref[0] if preact else ext[t_last + 1:t_last + 9]


def _mlstm(mqk, mv, mo, gc, gr, c0, n0, m0, buf8, cw, cb, gml, batch, t_len):
    preact = mqk.dtype == BF16
    L = ML_CHUNK if t_len >= ML_CHUNK else LANES
    rows = min(L, t_len)
    nc = t_len // rows
    t_last = rows - 1
    assert t_len % rows == 0 and (nc == 1 or rows == L) and (t_last + 1) % 8 == 0
    tok = lambda c: pl.BlockSpec((rows, c), lambda b, i: (b * nc + i, 0))
    st = lambda *s: pl.BlockSpec((1,) + s, lambda b, i: (b,) + (0,) * len(s))
    shp_c = (ML_HEADS, ML_HEAD_DIM, ML_HEAD_DIM)
    shp_n = (ML_HEADS, 1, ML_HEAD_DIM)
    return pl.pallas_call(
        functools.partial(_mlstm_kernel, L=L, t_last=t_last, nc=nc, preact=preact),
        grid=(batch, nc),
        in_specs=[tok(2 * ML_WIDTH), tok(ML_WIDTH), tok(ML_WIDTH), tok(LANES),
                  pl.BlockSpec((1, 16, L), lambda b, i: (b, 0, i)),
                  st(*shp_c), st(*shp_n), st(*shp_n), st(8, 2 * ML_WIDTH),
                  _const_spec((ML_CONV, 2 * ML_WIDTH)), _const_spec((1, 2 * ML_WIDTH)), _const_spec((1, ML_WIDTH))],
        out_specs=[tok(ML_WIDTH), st(*shp_c), st(*shp_n), st(*shp_n), st(8, 2 * ML_WIDTH)],
        out_shape=[jax.ShapeDtypeStruct((batch * t_len, ML_WIDTH), BF16),
                   jax.ShapeDtypeStruct((batch,) + shp_c, F32),
                   jax.ShapeDtypeStruct((batch,) + shp_n, F32),
                   jax.ShapeDtypeStruct((batch,) + shp_n, F32),
                   jax.ShapeDtypeStruct((batch, 8, 2 * ML_WIDTH), F32)],
        scratch_shapes=[pltpu.VMEM(shp_c, F32), pltpu.VMEM(shp_n, F32), pltpu.VMEM(shp_n, F32),
                        pltpu.VMEM((8, 2 * ML_WIDTH), F32)],
        compiler_params=_cparams(("parallel", "arbitrary")),
        name="mlstm",
    )(mqk, mv, mo, gc, gr, c0, n0, m0, buf8, cw, cb, gml)


def _gla_level_matrices(L):
    mats = np.zeros((len(GLA_LEVELS), L, L), np.float32)
    for li, b in enumerate(GLA_LEVELS):
        for t in range(L):
            base = (t // (2 * b)) * 2 * b
            bound = base + b - 1
            if t > bound:
                mats[li, t, bound + 1:t + 1] = 1.0
            else:
                mats[li, t, t + 1:bound + 1] = 1.0
    return jnp.asarray(mats, BF16)


def _gla_kernel(q_ref, k_ref, v_ref, gg_ref, la_ref, s0_ref, lvl_ref, gn_ref, o_ref, s_out, s_scr, *, L, t_last, nc):
    ci = pl.program_id(1)
    nb, rows = q_ref.shape[0], q_ref.shape[1]

    @pl.when(ci == 0)
    def _():
        s_scr[...] = s0_ref[...]

    t_idx = lax.broadcasted_iota(jnp.int32, (L, 1), 0)
    s_idx = lax.broadcasted_iota(jnp.int32, (1, L), 1)
    tril = _tri(L, True)
    level_mask = []
    for b in GLA_LEVELS:
        sh = b.bit_length() - 1
        level_mask.append((jnp.right_shift(t_idx, sh + 1) == jnp.right_shift(s_idx, sh + 1))
                          & ((jnp.right_shift(t_idx, sh) & 1) == 1) & ((jnp.right_shift(s_idx, sh) & 1) == 0))

    els = []
    for bi in range(nb):
        la3 = _split2(_pad_rows(la_ref[bi], L))
        q_all = _pad_rows(q_ref[bi], L)
        k_all = _pad_rows(k_ref[bi], L)
        el = dict(la3=la3, cb=_mat_f32(tril, la3), v=_pad_rows(v_ref[bi], L), g=_pad_rows(gg_ref[bi], L),
                  q32=[], k32=[], a=[], e_next=jnp.exp(_mat_f32(lvl_ref[0], la3)), pending=None)
        for h in range(GLA_HEADS):
            ks = slice(h * GLA_DK, (h + 1) * GLA_DK)
            el["q32"].append(q_all[:, ks].astype(F32))
            el["k32"].append(k_all[:, ks].astype(F32))
            el["a"].append(jnp.where(t_idx == s_idx, _dot_nt(q_all[:, ks], k_all[:, ks]), 0.0))
        els.append(el)

    def settle(el):
        if el["pending"] is not None:
            prods, mask = el["pending"]
            el["a"] = [jnp.where(mask, prods[h], el["a"][h]) for h in range(GLA_HEADS)]

    for li in range(len(GLA_LEVELS)):
        for el in els:
            e_all = el["e_next"]
            if li + 1 < len(GLA_LEVELS):
                el["e_next"] = jnp.exp(_mat_f32(lvl_ref[li + 1], el["la3"]))
            prods = []
            for h in range(GLA_HEADS):
                e = e_all[:, h * GLA_DK:(h + 1) * GLA_DK]
                prods.append(_dot_nt((el["q32"][h] * e).astype(BF16), (el["k32"][h] * e).astype(BF16)))
            settle(el)
            el["pending"] = (prods, level_mask[li])
    for el in els:
        settle(el)

    for bi, el in enumerate(els):
        el["o"] = []
        for h in range(GLA_HEADS):
            ks = slice(h * GLA_DK, (h + 1) * GLA_DK)
            vs = slice(h * GLA_DV, (h + 1) * GLA_DV)
            cbh = el["cb"][:, ks]
            vb = el["v"][:, vs]
            s_t = s_scr[bi, h]
            o = (_dot_nt((el["q32"][h] * jnp.exp(cbh)).astype(BF16), s_t.astype(BF16))
                 + _dot(el["a"][h].astype(BF16), vb))
            cl = cbh[t_last:t_last + 1, :]
            kd = jnp.where(t_idx <= t_last, el["k32"][h] * jnp.exp(cl - cbh), 0.0).astype(BF16)
            s_scr[bi, h] = jnp.exp(cl) * s_t + _dot_tn(vb, kd)
            el["o"].append(o)
    for bi, el in enumerate(els):
        for h in range(GLA_HEADS):
            vs = slice(h * GLA_DV, (h + 1) * GLA_DV)
            o = el["o"][h]
            y = o * lax.rsqrt(jnp.mean(o * o, axis=-1, keepdims=True) + EPS) * gn_ref[:, vs]
            gate = el["g"][:, vs].astype(F32)
            y = y * (gate * _sigmoid(gate))
            o_ref[bi, :, vs] = y[:rows].astype(o_ref.dtype)

    @pl.when(ci == nc - 1)
    def _():
        s_out[...] = s_scr[...]


GLA_BATCH_PER_STEP = 4


def _gla(q, k, v, gg, la, s0t, gn, batch, t_len):
    L = GLA_CHUNK
    rows = min(L, t_len)
    nc = t_len // rows
    t_last = rows - 1
    nb = max(d for d in range(1, GLA_BATCH_PER_STEP + 1) if batch % d == 0)
    assert t_len % rows == 0 and (nc == 1 or rows == L)
    tok = lambda c: pl.BlockSpec((nb, rows, c), lambda b, i: (b, i, 0))
    shp_s = (GLA_HEADS, GLA_DV, GLA_DK)
    st = pl.BlockSpec((nb,) + shp_s, lambda b, i: (b, 0, 0, 0))
    levels = _gla_level_matrices(L)
    seq = lambda a: a.reshape(batch, t_len, a.shape[-1])
    o, s_new = pl.pallas_call(
        functools.partial(_gla_kernel, L=L, t_last=t_last, nc=nc),
        grid=(batch // nb, nc),
        in_specs=[tok(GLA_KW), tok(GLA_KW), tok(GLA_VW), tok(GLA_VW), tok(GLA_KW), st,
                  _const_spec(levels.shape), _const_spec((1, GLA_VW))],
        out_specs=[tok(GLA_VW), st],
        out_shape=[jax.ShapeDtypeStruct((batch, t_len, GLA_VW), BF16),
                   jax.ShapeDtypeStruct((batch,) + shp_s, F32)],
        scratch_shapes=[pltpu.VMEM((nb,) + shp_s, F32)],
        compiler_params=_cparams(("parallel", "arbitrary")),
        name="gla",
    )(seq(q), seq(k), seq(v), seq(gg), seq(la), s0t, levels, gn)
    return o.reshape(batch * t_len, GLA_VW), s_new


def _pack_even(w, b_fox_f, b_i, b_f):
    d = w.shape[0]
    o = np.cumsum((0, FOX_WIDTH, FOX_WIDTH, FOX_WIDTH, FOX_HEADS, 2 * ML_WIDTH, ML_WIDTH, ML_WIDTH, ML_HEADS, ML_HEADS))
    n_gate = FOX_HEADS + 2 * ML_HEADS
    wp = jnp.concatenate([w[:, o[0]:o[3]], w[:, o[4]:o[7]], w[:, o[3]:o[4]], w[:, o[7]:o[9]],
                          jnp.zeros((d, LANES - n_gate), w.dtype)], axis=1).astype(BF16)
    bias = jnp.concatenate([b_fox_f, b_i, b_f, jnp.zeros((LANES - n_gate,), F32)]).reshape(1, LANES)
    return wp, bias


def _pack_odd(w, w_a2):
    d = w.shape[0]
    wp = jnp.concatenate([w, jnp.zeros((d, LANES - GLA_RANK), w.dtype)], axis=1).astype(BF16)
    wa2 = jnp.concatenate([w_a2, jnp.zeros((LANES - GLA_RANK, w_a2.shape[1]), w_a2.dtype)], axis=0).astype(BF16)
    return wp, wa2


def _gate_rows(gc, batch, t_len, t_pad):
    g = gc.reshape(batch, t_len, LANES)[:, :, :16].transpose(0, 2, 1)
    if t_pad > t_len:
        g = jnp.pad(g, ((0, 0), (0, 0), (0, t_pad - t_len)))
    return g


def _trunk(x, fox_k, fox_v, fox_lf, ml_c, ml_n, ml_m, ml_buf, gla_s, params):
    (norm_mix, norm_ffn, norm_final, w_even, bias_even, conv_w, conv_b, g_ml,
     w_odd, w_a2, b_a, g_gla, w_out, w_ff1, w_ff2) = params
    batch, t_len, d = x.shape
    past = fox_k.shape[2]
    n = batch * t_len
    tm = 512 if n % 512 == 0 else 256 if n % 256 == 0 else n
    depth = norm_mix.shape[0]
    xf = x.reshape(n, d)
    ev_states, odd_states = [], []
    kv_stacked = None
    y = None
    for layer in range(depth):
        j = layer // 2
        if layer % 2 == 0:
            assert (t_len % tm == 0) == (past == 0)
            buf8 = jnp.pad(ml_buf[j], ((0, 0), (8 - (ML_CONV - 1), 0), (0, 0)))
            proj = _inproj_even(xf, norm_mix[layer][None], w_even[j], bias_even[j],
                                tm, batch, t_len, j, (depth + 1) // 2, kv_stacked, conv_w[j], conv_b[j][None], buf8)
            conv_tail = None
            if past == 0:
                qt, kf, vf, ka, vt, mqk, mv, mo, gc, bnd, conv_tail = proj
                kv_stacked = (kf, vf)
                assert tm == FOX_Q_TILE
                attn = _fox_attention(qt, ka, vt, bnd, batch, t_len, FOX_Q_TILE, FOX_KEY_CHUNK)
            else:
                q, kf, vf, mqk, mv, mo, gc = proj
                lf_new = gc.reshape(batch, t_len, LANES)[:, :, :FOX_HEADS]
                lf_row = jnp.concatenate([fox_lf[j], lf_new, jnp.zeros((batch, LANES - t_len, FOX_HEADS), F32)], axis=1)
                lf_row = jnp.pad(lf_row.transpose(0, 2, 1), ((0, 0), (0, 16 - FOX_HEADS), (0, 0)))
                cache_t = lambda a: a.transpose(0, 1, 3, 4, 2).reshape(a.shape[0], batch, FOX_WIDTH, past)
                new_t = lambda a: jnp.pad(a.reshape(batch, t_len, FOX_WIDTH).transpose(0, 2, 1).astype(BF16),
                                          ((0, 0), (0, 0), (0, LANES - t_len)))
                attn = _fox_decode(q.reshape(batch, t_len, FOX_WIDTH), cache_t(fox_k), cache_t(fox_v), j,
                                   new_t(kf), new_t(vf), lf_row).reshape(n, FOX_WIDTH)
            l_ml = ML_CHUNK if t_len >= ML_CHUNK else LANES
            gr = _gate_rows(gc, batch, t_len, max(t_len, l_ml))
            c0 = ml_c[j]
            n0 = ml_n[j][:, :, None, :]
            m0 = jnp.broadcast_to(ml_m[j][:, :, None, None], (batch, ML_HEADS, 1, LANES))
            h_ml, c_new, n_new, m_new, buf_new = _mlstm(mqk, mv, mo, gc, gr, c0, n0, m0, buf8,
                                                        conv_w[j], conv_b[j][None], g_ml[j][None], batch, t_len)
            if conv_tail is not None:
                buf_new = conv_tail
            heads = lambda a: None if past == 0 else a.reshape(batch, t_len, FOX_HEADS, FOX_HEAD_DIM)
            ev_states.append((heads(kf), heads(vf),
                              gc[:, :FOX_HEADS].reshape(batch, t_len, FOX_HEADS),
                              c_new, n_new[:, :, 0, :], m_new[:, :, 0, 0], buf_new[:, 8 - (ML_CONV - 1):, :]))
            mixes = [attn, h_ml]
        else:
            q, k, v, gg, la = _inproj_odd(xf, norm_mix[layer][None], w_odd[j], w_a2[j], b_a[j][None], tm)
            s0t = gla_s[j].transpose(0, 1, 3, 2)
            o, s_new = _gla(q, k, v, gg, la, s0t, g_gla[j][None], batch, t_len)
            odd_states.append(s_new.transpose(0, 1, 3, 2))
            mixes = [o]
        last = layer == depth - 1
        out = _post(xf, mixes, w_out[layer], norm_ffn[layer][None], w_ff1[layer], w_ff2[layer],
                    norm_final[None] if last else None, tm)
        if last:
            y = out
        else:
            xf = out
    ev = [jnp.stack([s[i] for s in ev_states]) for i in range(2 if kv_stacked else 0, 7)]
    if kv_stacked:
        ev = [a.reshape(a.shape[0], batch, FOX_HEADS, FOX_HEAD_DIM, t_len).transpose(0, 1, 4, 2, 3)
              for a in kv_stacked] + ev
    return y.reshape(batch, t_len, d), ev, jnp.stack(odd_states)


def kernel(x_prompt, x_sample, cache_fox_k, cache_fox_v, cache_fox_logf, state_mlstm_c, state_mlstm_n, state_mlstm_m, state_mlstm_conv, state_gla_s, norm_mix, norm_ffn, norm_final, w_in_even, b_fox_f, conv_w_ml, conv_b_ml, b_ml_i, b_ml_f, g_ml, w_in_odd, w_gla_a2, b_gla_a, g_gla, w_out, w_ff1, w_ff2):
    n_even, n_odd = w_in_even.shape[0], w_in_odd.shape[0]
    packed_even = [_pack_even(w_in_even[j], b_fox_f[j], b_ml_i[j], b_ml_f[j]) for j in range(n_even)]
    packed_odd = [_pack_odd(w_in_odd[j], w_gla_a2[j]) for j in range(n_odd)]
    params = (norm_mix, norm_ffn, norm_final,
              [p[0] for p in packed_even], [p[1] for p in packed_even], conv_w_ml, conv_b_ml, g_ml,
              [p[0] for p in packed_odd], [p[1] for p in packed_odd], b_gla_a, g_gla,
              w_out.astype(BF16), w_ff1.astype(BF16), w_ff2.astype(BF16))

    bp = x_prompt.shape[0]
    dt = x_prompt.dtype
    zeros = lambda *s: jnp.zeros(s, dt)
    y_p, ev_p, gla_p = _trunk(
        x_prompt,
        zeros(n_even, bp, 0, FOX_HEADS, FOX_HEAD_DIM), zeros(n_even, bp, 0, FOX_HEADS, FOX_HEAD_DIM),
        zeros(n_even, bp, 0, FOX_HEADS),
        zeros(n_even, bp, ML_HEADS, ML_HEAD_DIM, ML_HEAD_DIM), zeros(n_even, bp, ML_HEADS, ML_HEAD_DIM),
        zeros(n_even, bp, ML_HEADS), zeros(n_even, bp, ML_CONV - 1, 2 * ML_WIDTH),
        zeros(n_odd, bp, GLA_HEADS, GLA_DK, GLA_DV), params)
    y_s, ev_s, gla_s = _trunk(x_sample, cache_fox_k, cache_fox_v, cache_fox_logf, state_mlstm_c, state_mlstm_n,
                              state_mlstm_m, state_mlstm_conv, state_gla_s, params)
    return (y_p, y_s, *ev_p, gla_p, *ev_s, gla_s)
```

```python
import functools

import numpy as np
import jax
import jax.numpy as jnp
from jax import lax
from jax.experimental import pallas as pl
from jax.experimental.pallas import tpu as pltpu

F32 = jnp.float32
BF16 = jnp.bfloat16
EPS = 1e-6
NEG = -1e30
LOG2E = 1.4426950408889634

LANES = 128
VMEM_LIMIT = 56 * 1024 * 1024

D_MODEL = 1024
D_FF = 4 * D_MODEL
FOX_HEADS, FOX_HEAD_DIM = 8, 64
FOX_WIDTH = FOX_HEADS * FOX_HEAD_DIM
ML_HEADS, ML_HEAD_DIM = 4, 128
ML_WIDTH = ML_HEADS * ML_HEAD_DIM
ML_CONV = 4
GLA_HEADS, GLA_DK, GLA_DV = 4, 128, 256
GLA_KW = GLA_HEADS * GLA_DK
GLA_VW = GLA_HEADS * GLA_DV
GLA_RANK = 16
GLA_TAU = 16.0

E_Q, E_K, E_V, E_MQK, E_MV, E_MO, E_G, E_END = np.cumsum(
    (0, FOX_WIDTH, FOX_WIDTH, FOX_WIDTH, 2 * ML_WIDTH, ML_WIDTH, ML_WIDTH, LANES)).tolist()
G_FOX, G_MI, G_MF = 0, FOX_HEADS, FOX_HEADS + ML_HEADS
O_Q, O_K, O_V, O_G, O_A, O_END = np.cumsum((0, GLA_KW, GLA_KW, GLA_VW, GLA_VW, LANES)).tolist()

ML_CHUNK = 256
FOX_KEY_CHUNK = 256
FOX_Q_TILE = 512
GLA_CHUNK = 128
GLA_LEVELS = (64, 32, 16, 8, 4, 2, 1)


def _cparams(sem):
    return pltpu.CompilerParams(dimension_semantics=sem, vmem_limit_bytes=VMEM_LIMIT)


def _const_spec(shape):
    nd = len(shape)
    return pl.BlockSpec(shape, lambda *_: (0,) * nd, pipeline_mode=pl.Buffered(1))


def _rms(x, g):
    return x * lax.rsqrt(jnp.mean(x * x, axis=-1, keepdims=True) + EPS) * g


def _sigmoid(x):
    return 1.0 / (1.0 + jnp.exp(-x))


def _log_sigmoid(x):
    return -(jnp.maximum(-x, 0.0) + jnp.log1p(jnp.exp(-jnp.abs(x))))


def _dot(a, b):
    return jnp.dot(a, b, preferred_element_type=F32)


def _dot_nt(a, b):
    return lax.dot_general(a, b, (((1,), (1,)), ((), ())), preferred_element_type=F32)


def _dot_tn(a, b):
    return lax.dot_general(a, b, (((0,), (0,)), ((), ())), preferred_element_type=F32)


def _split3(x):
    hi = x.astype(BF16)
    r1 = x - hi.astype(F32)
    mid = r1.astype(BF16)
    lo = (r1 - mid.astype(F32)).astype(BF16)
    return hi, mid, lo


def _split2(x):
    hi = x.astype(BF16)
    return hi, (x - hi.astype(F32)).astype(BF16)


def _mat_f32(m, parts):
    return functools.reduce(lambda a, b: a + b, [_dot(m, p) for p in parts])


def _f32_mat(parts, m):
    return functools.reduce(lambda a, b: a + b, [_dot(p, m) for p in parts])


def _tri(n, lower):
    r = lax.broadcasted_iota(jnp.int32, (n, n), 0)
    c = lax.broadcasted_iota(jnp.int32, (n, n), 1)
    keep = (c <= r) if lower else (r <= c)
    return jnp.where(keep, 1.0, 0.0).astype(BF16)


def _pad_rows(a, n):
    if a.shape[0] == n:
        return a
    return jnp.concatenate([a, jnp.zeros((n - a.shape[0], a.shape[1]), a.dtype)], axis=0)


def _conv_silu(u, prev, cw_ref, cb_ref):
    n = u.shape[0]
    ext = jnp.concatenate([prev, u], axis=0)
    y = cb_ref[...] + cw_ref[ML_CONV - 1:ML_CONV, :] * u
    for s in range(1, ML_CONV):
        y = y + cw_ref[ML_CONV - 1 - s:ML_CONV - s, :] * pltpu.roll(ext, s, axis=0)[8:8 + n]
    lane = lax.broadcasted_iota(jnp.int32, (1, 2 * ML_WIDTH), 1)
    return y * _sigmoid(y) * jnp.where(lane >= ML_WIDTH, ML_HEAD_DIM ** -0.5, 1.0), ext


FOX_Q_SCALE = FOX_HEAD_DIM ** -0.5 * LOG2E


def _even_segments(x_ref, g_ref, w_ref):
    h = _rms(x_ref[...], g_ref[...]).astype(BF16)
    return lambda a, b: _dot(h, w_ref[:, a:b])


def _even_gates(seg, bias_ref):
    gz = seg(E_G, E_END) + bias_ref[...]
    lane = lax.broadcasted_iota(jnp.int32, gz.shape, 1)
    is_log = (lane < G_MI) | (lane >= G_MF)
    return jnp.where(is_log, _log_sigmoid(gz), gz)


def _inproj_even_rows_kernel(x_ref, g_ref, w_ref, bias_ref,
                             q_ref, kf_ref, vf_ref, mqk_ref, mv_ref, mo_ref, gc_ref):
    seg = _even_segments(x_ref, g_ref, w_ref)
    q_ref[...] = (seg(E_Q, E_K) * FOX_Q_SCALE).astype(BF16)
    kf_ref[...] = seg(E_K, E_V)
    vf_ref[...] = seg(E_V, E_MQK)
    mqk_ref[...] = seg(E_MQK, E_MV)
    mv_ref[...] = seg(E_MV, E_MO).astype(BF16)
    mo_ref[...] = seg(E_MO, E_G).astype(BF16)
    gc_ref[...] = _even_gates(seg, bias_ref)


def _inproj_even_seq_kernel(*refs, tpb, aliased):
    x_ref, g_ref, w_ref, bias_ref, ind_ref, cw_ref, cb_ref, buf_ref = refs[:8]
    refs = refs[8 + (2 if aliased else 0):]
    (qt_ref, kf_ref, vf_ref, ka_ref, vt_ref, mqk_ref, mv_ref, mo_ref, gc_ref, bnd_ref, tail_ref,
     carry, conv_prev) = refs
    @pl.when(pl.program_id(0) % tpb == 0)
    def _():
        carry[...] = jnp.zeros_like(carry)
        conv_prev[...] = buf_ref[0]

    seg = _even_segments(x_ref, g_ref, w_ref)
    mqk = seg(E_MQK, E_MV)
    gates = _even_gates(seg, bias_ref)
    gc_ref[...] = gates
    q = seg(E_Q, E_K) * FOX_Q_SCALE
    k = seg(E_K, E_V)
    v = seg(E_V, E_MQK)
    qt_ref[...] = q.T.astype(BF16)
    mv = seg(E_MV, E_MO)
    kf_ref[0, 0] = k.T
    f = _append_f_terms(gates, carry, k.astype(BF16), ka_ref)
    mo = seg(E_MO, E_G)
    act, _ = _conv_silu(mqk, conv_prev[...], cw_ref, cb_ref)
    mqk_ref[...] = act.astype(BF16)
    last_rows = mqk[mqk.shape[0] - 8:]
    conv_prev[...] = last_rows
    tail_ref[0] = last_rows
    mv_ref[...] = mv.astype(BF16)
    v_t = v.T
    vt_ref[...] = v_t.astype(BF16)
    vf_ref[0, 0] = v_t
    mo_ref[...] = mo.astype(BF16)
    norms = _dot((q * q).astype(BF16), ind_ref[0]) + _dot((k * k).astype(BF16), ind_ref[1])
    bnd_ref[...] = jnp.zeros_like(bnd_ref)
    bnd_ref[0, 0:1, :] = f[0:1, :]
    for c in range(f.shape[0] // FOX_KEY_CHUNK):
        bnd_ref[0, 1 + c:2 + c, :] = f[(c + 1) * FOX_KEY_CHUNK - 1:(c + 1) * FOX_KEY_CHUNK, :]
    bnd_ref[0, 7:8, :] = jnp.max(norms, axis=0, keepdims=True)


def _inproj_even(x, g, w, bias, tm, batch, t_len, slot, n_slots, kv_prev, cw, cb, buf8):
    n = x.shape[0]
    row = lambda c: (pl.BlockSpec((tm, c), lambda i: (i, 0)), (n, c))
    col = (pl.BlockSpec((FOX_WIDTH, tm), lambda i: (0, i)), (FOX_WIDTH, n))
    common_in = [row(D_MODEL)[0], _const_spec((1, D_MODEL)), _const_spec((D_MODEL, E_END)), _const_spec((1, LANES))]
    tail = [(row(ML_WIDTH), BF16), (row(ML_WIDTH), BF16), (row(LANES), F32)]
    if t_len % tm != 0:
        outs = [(row(FOX_WIDTH), BF16), (row(FOX_WIDTH), F32), (row(FOX_WIDTH), F32), (row(2 * ML_WIDTH), F32)] + tail
        return pl.pallas_call(
            _inproj_even_rows_kernel,
            grid=(n // tm,),
            in_specs=common_in,
            out_specs=[spec for (spec, _), _ in outs],
            out_shape=[jax.ShapeDtypeStruct(shape, dt) for (_, shape), dt in outs],
            compiler_params=_cparams(("parallel",)),
            name="inproj_even_rows",
        )(x, g, w, bias)
    tpb = t_len // tm
    state = (pl.BlockSpec((1, 1, FOX_WIDTH, tm), lambda i: (slot, i // tpb, 0, i % tpb)),
             (n_slots, batch, FOX_WIDTH, t_len))
    per_seq = (pl.BlockSpec((1, 8, 2 * ML_WIDTH), lambda i: (i // tpb, 0, 0)), (batch, 8, 2 * ML_WIDTH))
    outs = ([(col, BF16), (state, F32), (state, F32), (row(2 * FOX_WIDTH), BF16), (col, BF16), (row(2 * ML_WIDTH), BF16)]
            + tail + [((pl.BlockSpec((1, 8, LANES), lambda i: (i, 0, 0)), (n // tm, 8, LANES)), F32), (per_seq, F32)])
    assert tm % FOX_KEY_CHUNK == 0 and tm // FOX_KEY_CHUNK <= 6
    ind = np.zeros((2, FOX_WIDTH, LANES), np.float32)
    for c in range(FOX_WIDTH):
        ind[0, c, FOX_HEADS + c // FOX_HEAD_DIM] = 1.0
        ind[1, c, 2 * FOX_HEADS + c // FOX_HEAD_DIM] = 1.0
    ins = [x, g, w, bias, jnp.asarray(ind, BF16), cw, cb, buf8]
    in_specs = common_in + [_const_spec(ind.shape), _const_spec(cw.shape), _const_spec(cb.shape), per_seq[0]]
    aliases = {}
    if kv_prev is not None:
        aliases = {len(ins): 1, len(ins) + 1: 2}
        ins += list(kv_prev)
        in_specs += [pl.BlockSpec(memory_space=pl.ANY)] * 2
    return pl.pallas_call(
        functools.partial(_inproj_even_seq_kernel, tpb=tpb, aliased=kv_prev is not None),
        grid=(n // tm,),
        in_specs=in_specs,
        out_specs=[spec for (spec, _), _ in outs],
        out_shape=[jax.ShapeDtypeStruct(shape, dt) for (_, shape), dt in outs],
        scratch_shapes=[pltpu.VMEM((1, LANES), F32), pltpu.VMEM((8, 2 * ML_WIDTH), F32)],
        input_output_aliases=aliases,
        compiler_params=_cparams(("arbitrary",)),
        name="inproj_even_seq",
    )(*ins)


def _inproj_odd_kernel(x_ref, g_ref, w_ref, wa2_ref, ba_ref, q_ref, k_ref, v_ref, gg_ref, la_ref):
    h = _rms(x_ref[...], g_ref[...]).astype(BF16)

    def seg(a, b):
        return _dot(h, w_ref[:, a:b])

    q_ref[...] = (seg(O_Q, O_K) * (GLA_DK ** -0.5)).astype(BF16)
    k_ref[...] = seg(O_K, O_V).astype(BF16)
    v_ref[...] = seg(O_V, O_G).astype(BF16)
    gg_ref[...] = seg(O_G, O_A).astype(BF16)
    ga = seg(O_A, O_END).astype(BF16)
    la_ref[...] = _log_sigmoid(_dot(ga, wa2_ref[...]) + ba_ref[...]) * (1.0 / GLA_TAU)


def _inproj_odd(x, g, w, wa2, ba, tm):
    n = x.shape[0]
    row = lambda c: pl.BlockSpec((tm, c), lambda i: (i, 0))
    outs = [(GLA_KW, BF16), (GLA_KW, BF16), (GLA_VW, BF16), (GLA_VW, BF16), (GLA_KW, F32)]
    return pl.pallas_call(
        _inproj_odd_kernel,
        grid=(n // tm,),
        in_specs=[row(D_MODEL), _const_spec((1, D_MODEL)), _const_spec((D_MODEL, O_END)),
                  _const_spec((LANES, GLA_KW)), _const_spec((1, GLA_KW))],
        out_specs=[row(c) for c, _ in outs],
        out_shape=[jax.ShapeDtypeStruct((n, c), dt) for c, dt in outs],
        compiler_params=_cparams(("parallel",)),
        name="inproj_odd",
    )(x, g, w, wa2, ba)


def _post_kernel(*refs, n_mix, final):
    x_ref = refs[0]
    mix_refs = refs[1:1 + n_mix]
    wo_ref, gf_ref, w1_ref, w2_ref = refs[1 + n_mix:5 + n_mix]
    rest = refs[5 + n_mix:]
    mix = mix_refs[0][...] if n_mix == 1 else jnp.concatenate([r[...] for r in mix_refs], axis=1)
    x1 = x_ref[...] + _dot(mix, wo_ref[...])
    h = _rms(x1, gf_ref[...]).astype(BF16)
    y = x1
    for c in range(D_FF // D_MODEL):
        sl = slice(c * D_MODEL, (c + 1) * D_MODEL)
        t = jnp.maximum(_dot(h, w1_ref[:, sl]), 0.0)
        y = y + _dot((t * t).astype(BF16), w2_ref[sl, :])
    if final:
        gfin_ref, out_ref = rest
        out_ref[...] = _rms(y, gfin_ref[...])
    else:
        (out_ref,) = rest
        out_ref[...] = y


def _post(x, mixes, wo, gf, w1, w2, gfin, tm):
    n = x.shape[0]
    row = lambda c: pl.BlockSpec((tm, c), lambda i: (i, 0))
    final = gfin is not None
    ins = [x, *mixes, wo, gf, w1, w2]
    specs = [row(D_MODEL)] + [row(m.shape[1]) for m in mixes] + [
        _const_spec((D_MODEL, D_MODEL)), _const_spec((1, D_MODEL)),
        _const_spec((D_MODEL, D_FF)), _const_spec((D_FF, D_MODEL))]
    if final:
        ins.append(gfin)
        specs.append(_const_spec((1, D_MODEL)))
    return pl.pallas_call(
        functools.partial(_post_kernel, n_mix=len(mixes), final=final),
        grid=(n // tm,),
        in_specs=specs,
        out_specs=row(D_MODEL),
        out_shape=jax.ShapeDtypeStruct((n, D_MODEL), F32),
        compiler_params=_cparams(("parallel",)),
        name="post_final" if final else "post",
    )(*ins)


F_TERMS = 3


def _append_f_terms(lf, carry, k, ka_ref):
    tc = lf.shape[0]
    f = _mat_f32(_tri(tc, True), _split3(lf)) + carry[...]
    carry[...] = f[tc - 1:tc, :]
    hi, mid, lo = [p.astype(F32) for p in _split3(f * LOG2E)]
    lane = lax.broadcasted_iota(jnp.int32, (1, LANES), 1)
    cols = jnp.where(lane < FOX_HEADS, hi,
                     jnp.where(lane < 2 * FOX_HEADS, pltpu.roll(mid, FOX_HEADS, axis=1),
                               jnp.where(lane < 3 * FOX_HEADS, pltpu.roll(lo, 2 * FOX_HEADS, axis=1), 0.0)))
    cols = cols.astype(BF16)
    for hp in range(FOX_HEADS // 2):
        ka_ref[:, 2 * hp * LANES:(2 * hp + 1) * LANES] = k[:, hp * LANES:(hp + 1) * LANES]
        ka_ref[:, (2 * hp + 1) * LANES:(2 * hp + 2) * LANES] = cols
    return f


def _fox_decode_kernel(q_ref, kc_ref, vc_ref, kn_ref, vn_ref, lf_ref, o_ref, *, t_new, past):
    nkeys = past + LANES
    lane_f = lax.broadcasted_iota(jnp.int32, (1, FOX_WIDTH), 1)
    head_mask = [jnp.where((lane_f >= h * FOX_HEAD_DIM) & (lane_f < (h + 1) * FOX_HEAD_DIM), 1.0, 0.0)
                 for h in range(FOX_HEADS)]
    q = q_ref[0].astype(F32)
    qb = jnp.concatenate([q * hm for hm in head_mask], axis=0).astype(BF16)

    def keys(cache_ref, new_ref):
        return jnp.concatenate([cache_ref[0, 0].astype(BF16), new_ref[0]], axis=1)

    s = _dot(qb, keys(kc_ref, kn_ref))
    lf = lf_ref[0]
    triu = _tri(LANES, False)
    carry = jnp.zeros((lf.shape[0], 1), F32)
    blocks = []
    for c in range(nkeys // LANES):
        cs = _f32_mat(_split3(lf[:, c * LANES:(c + 1) * LANES]), triu) + carry
        carry = cs[:, LANES - 1:LANES]
        blocks.append(cs)
    f_all = jnp.concatenate(blocks, axis=1) * LOG2E
    s = s - jnp.concatenate([jnp.broadcast_to(f_all[h:h + 1, :], (t_new, nkeys)) for h in range(FOX_HEADS)], axis=0)
    kpos = lax.broadcasted_iota(jnp.int32, (1, nkeys), 1)
    qpos = past + (lax.broadcasted_iota(jnp.int32, (FOX_HEADS * t_new, 1), 0) & (t_new - 1))
    s = jnp.where(kpos <= qpos, s, NEG)
    p = jnp.exp2(s - jnp.max(s, axis=-1, keepdims=True))
    ob = _dot_nt(p.astype(BF16), keys(vc_ref, vn_ref)) / jnp.sum(p, axis=-1, keepdims=True)
    out = ob[0:t_new] * head_mask[0]
    for h in range(1, FOX_HEADS):
        out = out + ob[h * t_new:(h + 1) * t_new] * head_mask[h]
    o_ref[0] = out.astype(o_ref.dtype)


def _fox_decode(q, k_cache, v_cache, layer, k_new, v_new, lf_row):
    batch, t_new, _ = q.shape
    past = k_cache.shape[3]
    assert t_new & (t_new - 1) == 0 and t_new <= LANES and t_new % 16 == 0 and past % LANES == 0
    blk = lambda a: pl.BlockSpec((1,) + a.shape[1:], lambda b: (b, 0, 0))
    cache = pl.BlockSpec((1, 1, FOX_WIDTH, past), lambda b: (layer, b, 0, 0))
    args = (q, k_cache, v_cache, k_new, v_new, lf_row)
    return pl.pallas_call(
        functools.partial(_fox_decode_kernel, t_new=t_new, past=past),
        grid=(batch,),
        in_specs=[blk(q), cache, cache, blk(k_new), blk(v_new), blk(lf_row)],
        out_specs=blk(q),
        out_shape=jax.ShapeDtypeStruct(q.shape, BF16),
        compiler_params=_cparams(("parallel",)),
        name="fox_decode",
    )(*args)


ONES_ROWS = 16


def _fox_attn_kernel(first_ref, qt_ref, ka_ref, vt_ref, o_ref, s_scr, acc_scr, *, tq, tkc, past, n_diag):
    q_first = past + pl.program_id(2) * tq
    n_full = q_first // tkc
    drow = lax.broadcasted_iota(jnp.int32, (LANES, 1), 0)
    lane = lax.broadcasted_iota(jnp.int32, (1, 2 * tq), 1)
    lane_head = jnp.where(lane < tq, 0, 1)
    own_head = jnp.where(jnp.where(drow < FOX_HEAD_DIM, 0, 1) == lane_head, 1.0, 0.0).astype(BF16)
    qt = qt_ref[...]
    q2 = jnp.concatenate([qt, qt], axis=1) * own_head
    head = 2 * pl.program_id(1) + lane_head
    f_sel = ((drow & (FOX_HEADS - 1)) == head) & (drow < F_TERMS * FOX_HEADS)
    qa = jnp.concatenate([q2, jnp.where(f_sel, -1.0, 0.0).astype(BF16)], axis=0)
    ones = jnp.ones((ONES_ROWS, tkc), BF16)
    acc_scr[...] = jnp.zeros_like(acc_scr)

    qa_h = [qa[:, hh * tq:(hh + 1) * tq] for hh in range(2)]
    qpos = q_first + lax.broadcasted_iota(jnp.int32, (1, tq), 1)

    def produce(j, hh, buf, lo=0):
        start = pl.multiple_of(j * tkc, tkc)
        s_scr[buf, hh, :, lo:] = _dot(ka_ref[pl.ds(start, tkc), :], qa_h[hh][:, lo:])

    def consume(j, hh, buf, m_all, masked, lo=0):
        start = pl.multiple_of(j * tkc, tkc)
        st = s_scr[buf, hh, :, lo:]
        m_prev = m_all[:, lo:]
        if masked:
            kpos = start + lax.broadcasted_iota(jnp.int32, (tkc, 1), 0)
            st = jnp.where(kpos <= qpos[:, lo:], st, NEG)
        m_new = jnp.maximum(m_prev, jnp.max(st, axis=0, keepdims=True))
        alpha = jnp.exp2(m_prev - m_new)
        p = jnp.exp2(st - m_new).astype(BF16)
        rows = slice(hh * FOX_HEAD_DIM, (hh + 1) * FOX_HEAD_DIM)
        va = jnp.concatenate([vt_ref[rows, pl.ds(start, tkc)], ones], axis=0)
        acc_scr[hh, :, lo:] = alpha * acc_scr[hh, :, lo:] + _dot(va, p)
        return m_new if lo == 0 else jnp.concatenate([m_all[:, :lo], m_new], axis=1)

    nq = pl.num_programs(2)
    slot = ((pl.program_id(0) * pl.num_programs(1) + pl.program_id(1)) * nq + pl.program_id(2)) * 2
    first = [first_ref[slot], first_ref[slot + 1]]
    joint = jnp.maximum(first[0], first[1])
    for hh in range(2):
        produce(first[hh], hh, 0)

    def pair_of_chunks(i, ms, heads):
        ms = list(ms)
        for step in range(2):
            for hh in heads:
                produce(2 * i + step + 1, hh, 1 - step)
                ms[hh] = consume(2 * i + step, hh, step, ms[hh], False)
        return tuple(ms)

    m_init = jnp.full((1, tq), NEG, F32)
    ms = (m_init, m_init)
    for hh in range(2):
        ms = lax.fori_loop(first[hh] // 2, joint // 2, functools.partial(pair_of_chunks, heads=(hh,)), ms)
    ms = list(lax.fori_loop(joint // 2, n_full // 2, functools.partial(pair_of_chunks, heads=(0, 1)), ms))
    for d in range(n_diag):
        for hh in range(2):
            if d + 1 < n_diag:
                produce(n_full + d + 1, hh, (d + 1) % 2, lo=min((d + 1) * tkc, tq - LANES))
            ms[hh] = consume(n_full + d, hh, d % 2, ms[hh], True, lo=min(d * tkc, tq - LANES))
    out = jnp.concatenate([acc_scr[hh, 0:FOX_HEAD_DIM] / acc_scr[hh, FOX_HEAD_DIM:FOX_HEAD_DIM + 1]
                           for hh in range(2)], axis=0)
    o_ref[...] = out.T.astype(o_ref.dtype)


SKIP_MARGIN = 40.0
NORM_SLACK = 1.02


def _fox_first_chunks(bnd, batch, t_len, tq, tkc):
    nq, nc, pairs = t_len // tq, t_len // tkc, FOX_HEADS // 2
    per_tile = tq // tkc
    b4 = bnd.reshape(batch, nq, 8, LANES)
    f_start = b4[:, :, 0, 0:FOX_HEADS] * LOG2E
    f_end = b4[:, :, 1:1 + per_tile, 0:FOX_HEADS].reshape(batch, nc, FOX_HEADS) * LOG2E
    qn = jnp.sqrt(b4[:, :, 7, FOX_HEADS:2 * FOX_HEADS])
    kn = jnp.sqrt(jnp.max(b4[:, :, 7, 2 * FOX_HEADS:3 * FOX_HEADS], axis=1))
    thr = 2.0 * NORM_SLACK * qn * kn[:, None, :] + SKIP_MARGIN
    decay = f_end[:, None, :, :] - f_start[:, :, None, :]
    before_tile = jnp.arange(nc)[None, :] < ((jnp.arange(nq) * tq) // tkc)[:, None]
    ok = (decay > thr[:, :, None, :]) & before_tile[None, :, :, None]
    lead = jnp.sum(jnp.cumprod(ok.astype(jnp.int32), axis=2), axis=2)
    lead = ((lead // 2) * 2).reshape(batch, nq, pairs, 2)
    return lead.transpose(0, 2, 1, 3).reshape(-1).astype(jnp.int32)


def _fox_attention(qt, ka, vt, bnd, batch, t_len, tq, tkc):
    past = 0
    nq = t_len // tq
    n_diag = max(1, tq // tkc)
    assert tq & (tq - 1) == 0 and tq % LANES == 0 and (tkc % tq == 0 or tq % tkc == 0) and t_len % tq == 0
    assert nq == 1 or tq % (2 * tkc) == 0
    assert t_len >= n_diag * tkc and t_len % tkc == 0
    pairs = FOX_HEADS // 2
    first = _fox_first_chunks(bnd, batch, t_len, tq, tkc)
    grid_spec = pltpu.PrefetchScalarGridSpec(
        num_scalar_prefetch=1,
        grid=(batch, pairs, nq),
        in_specs=[pl.BlockSpec((LANES, tq), lambda b, h, i, f: (h, b * nq + i)),
                  pl.BlockSpec((t_len, 2 * LANES), lambda b, h, i, f: (b, h)),
                  pl.BlockSpec((LANES, t_len), lambda b, h, i, f: (h, b))],
        out_specs=pl.BlockSpec((tq, LANES), lambda b, h, i, f: (b * nq + i, h)),
        scratch_shapes=[pltpu.VMEM((2, 2, tkc, tq), F32),
                        pltpu.VMEM((2, FOX_HEAD_DIM + ONES_ROWS, tq), F32)])
    return pl.pallas_call(
        functools.partial(_fox_attn_kernel, tq=tq, tkc=tkc, past=past, n_diag=n_diag),
        grid_spec=grid_spec,
        out_shape=jax.ShapeDtypeStruct((batch * t_len, FOX_WIDTH), BF16),
        compiler_params=_cparams(("parallel", "parallel", "arbitrary")),
        name="fox_attention",
    )(first, qt, ka, vt)


def _mlstm_kernel(mqk_ref, mv_ref, mo_ref, gc_ref, gr_ref, c0_ref, n0_ref, m0_ref, buf_ref,
                  cw_ref, cb_ref, gml_ref,
                  h_ref, c_out, n_out, m_out, buf_out,
                  c_scr, n_scr, m_scr, prev_scr, *, L, t_last, nc, preact):
    ci = pl.program_id(1)
    rows = mqk_ref.shape[0]

    @pl.when(ci == 0)
    def _():
        c_scr[...] = c0_ref[0]
        n_scr[...] = n0_ref[0]
        m_scr[...] = m0_ref[0]
        prev_scr[...] = buf_ref[0]

    if preact:
        qk = _pad_rows(mqk_ref[...], L).astype(F32)
        ext = None
    else:
        u = _pad_rows(mqk_ref[...], L)
        qk, ext = _conv_silu(u, prev_scr[...], cw_ref, cb_ref)
        prev_scr[...] = u[L - 8:L]

    gc = _pad_rows(gc_ref[...], L)
    gr = gr_ref[0]
    b_col = _mat_f32(_tri(L, True), _split3(gc))
    b_row = _f32_mat(_split3(gr), _tri(L, False))
    v_all = _pad_rows(mv_ref[...], L)
    o_all = _pad_rows(mo_ref[...], L)
    t_idx = lax.broadcasted_iota(jnp.int32, (L, 1), 0)
    causal = lax.broadcasted_iota(jnp.int32, (1, L), 1) <= t_idx

    heads = []
    for h in range(ML_HEADS):
        hs = slice(h * ML_HEAD_DIM, (h + 1) * ML_HEAD_DIM)
        q32 = qk[:, hs]
        k32 = qk[:, ML_WIDTH + h * ML_HEAD_DIM:ML_WIDTH + (h + 1) * ML_HEAD_DIM]
        qb, kb = q32.astype(BF16), k32.astype(BF16)
        c_prev = c_scr[h]
        heads.append(dict(hs=hs, q32=q32, k32=k32, qb=qb, kb=kb, vb=v_all[:, hs], c_prev=c_prev,
                          s=_dot_nt(qb, kb), qc=_dot_nt(qb, c_prev.astype(BF16))))

    for h, hd in enumerate(heads):
        bc = b_col[:, G_MF + h:G_MF + h + 1]
        ic = gc[:, G_MI + h:G_MI + h + 1]
        br = b_row[G_MF + h:G_MF + h + 1, :]
        ir = gr[G_MI + h:G_MI + h + 1, :]
        m_prev = m_scr[h][:, 0:1]
        dmat = jnp.where(causal, bc + (ir - br), NEG)
        inter = bc + m_prev
        m_t = jnp.maximum(inter, jnp.max(dmat, axis=-1, keepdims=True))
        w = jnp.exp(dmat - m_t)
        g = jnp.exp(inter - m_t)
        a = w * hd["s"]
        b_last = bc[t_last:t_last + 1, :]
        m_last = m_t[t_last:t_last + 1, :]
        w_end = jnp.where(t_idx <= t_last, jnp.exp(b_last - bc + ic - m_last), 0.0)
        hd.update(m_t=m_t, g=g, a=a, m_last=m_last, g_end=g[t_last:t_last + 1, :], w_end=w_end,
                  av=_dot(a.astype(BF16), hd["vb"]),
                  vk=_dot_tn((hd["vb"].astype(F32) * w_end).astype(BF16), hd["kb"]))

    for h, hd in enumerate(heads):
        hs, g, a, m_t = hd["hs"], hd["g"], hd["a"], hd["m_t"]
        n_prev = n_scr[h]
        num = g * hd["qc"] + hd["av"]
        den = g * jnp.sum(hd["q32"] * n_prev, axis=-1, keepdims=True) + jnp.sum(a, axis=-1, keepdims=True)
        hh = num / jnp.maximum(jnp.abs(den), jnp.exp(-m_t))
        c_scr[h] = hd["g_end"] * hd["c_prev"] + hd["vk"]
        n_scr[h] = hd["g_end"] * n_prev + jnp.sum(hd["k32"] * hd["w_end"], axis=0, keepdims=True)
        m_scr[h] = jnp.broadcast_to(hd["m_last"], (1, LANES))
        yh = hh * lax.rsqrt(jnp.mean(hh * hh, axis=-1, keepdims=True) + EPS) * gml_ref[:, hs]
        yh = yh * _sigmoid(o_all[:, hs].astype(F32))
        h_ref[:, hs] = yh[:rows].astype(h_ref.dtype)

    @pl.when(ci == nc - 1)
    def _():
        c_out[0] = c_scr[...]
        n_out[0] = n_scr[...]
        m_out[0] = m_scr[...]
        buf_out[0] = buf_ref[0] if preact else ext[t_last + 1:t_last + 9]


def _mlstm(mqk, mv, mo, gc, gr, c0, n0, m0, buf8, cw, cb, gml, batch, t_len):
    preact = mqk.dtype == BF16
    L = ML_CHUNK if t_len >= ML_CHUNK else LANES
    rows = min(L, t_len)
    nc = t_len // rows
    t_last = rows - 1
    assert t_len % rows == 0 and (nc == 1 or rows == L) and (t_last + 1) % 8 == 0
    tok = lambda c: pl.BlockSpec((rows, c), lambda b, i: (b * nc + i, 0))
    st = lambda *s: pl.BlockSpec((1,) + s, lambda b, i: (b,) + (0,) * len(s))
    shp_c = (ML_HEADS, ML_HEAD_DIM, ML_HEAD_DIM)
    shp_n = (ML_HEADS, 1, ML_HEAD_DIM)
    return pl.pallas_call(
        functools.partial(_mlstm_kernel, L=L, t_last=t_last, nc=nc, preact=preact),
        grid=(batch, nc),
        in_specs=[tok(2 * ML_WIDTH), tok(ML_WIDTH), tok(ML_WIDTH), tok(LANES),
                  pl.BlockSpec((1, 16, L), lambda b, i: (b, 0, i)),
                  st(*shp_c), st(*shp_n), st(*shp_n), st(8, 2 * ML_WIDTH),
                  _const_spec((ML_CONV, 2 * ML_WIDTH)), _const_spec((1, 2 * ML_WIDTH)), _const_spec((1, ML_WIDTH))],
        out_specs=[tok(ML_WIDTH), st(*shp_c), st(*shp_n), st(*shp_n), st(8, 2 * ML_WIDTH)],
        out_shape=[jax.ShapeDtypeStruct((batch * t_len, ML_WIDTH), BF16),
                   jax.ShapeDtypeStruct((batch,) + shp_c, F32),
                   jax.ShapeDtypeStruct((batch,) + shp_n, F32),
                   jax.ShapeDtypeStruct((batch,) + shp_n, F32),
                   jax.ShapeDtypeStruct((batch, 8, 2 * ML_WIDTH), F32)],
        scratch_shapes=[pltpu.VMEM(shp_c, F32), pltpu.VMEM(shp_n, F32), pltpu.VMEM(shp_n, F32),
                        pltpu.VMEM((8, 2 * ML_WIDTH), F32)],
        compiler_params=_cparams(("parallel", "arbitrary")),
        name="mlstm",
    )(mqk, mv, mo, gc, gr, c0, n0, m0, buf8, cw, cb, gml)


def _gla_level_matrices(L):
    mats = np.zeros((len(GLA_LEVELS), L, L), np.float32)
    for li, b in enumerate(GLA_LEVELS):
        for t in range(L):
            base = (t // (2 * b)) * 2 * b
            bound = base + b - 1
            if t > bound:
                mats[li, t, bound + 1:t + 1] = 1.0
            else:
                mats[li, t, t + 1:bound + 1] = 1.0
    return jnp.asarray(mats, BF16)


def _gla_kernel(q_ref, k_ref, v_ref, gg_ref, la_ref, s0_ref, lvl_ref, gn_ref, o_ref, s_out, s_scr, *, L, t_last, nc):
    ci = pl.program_id(1)
    nb, rows = q_ref.shape[0], q_ref.shape[1]

    @pl.when(ci == 0)
    def _():
        s_scr[...] = s0_ref[...]

    t_idx = lax.broadcasted_iota(jnp.int32, (L, 1), 0)
    s_idx = lax.broadcasted_iota(jnp.int32, (1, L), 1)
    tril = _tri(L, True)
    level_mask = []
    for b in GLA_LEVELS:
        sh = b.bit_length() - 1
        level_mask.append((jnp.right_shift(t_idx, sh + 1) == jnp.right_shift(s_idx, sh + 1))
                          & ((jnp.right_shift(t_idx, sh) & 1) == 1) & ((jnp.right_shift(s_idx, sh) & 1) == 0))

    els = []
    for bi in range(nb):
        la3 = _split2(_pad_rows(la_ref[bi], L))
        q_all = _pad_rows(q_ref[bi], L)
        k_all = _pad_rows(k_ref[bi], L)
        el = dict(la3=la3, cb=_mat_f32(tril, la3), v=_pad_rows(v_ref[bi], L), g=_pad_rows(gg_ref[bi], L),
                  q32=[], k32=[], a=[], e_next=jnp.exp(_mat_f32(lvl_ref[0], la3)), pending=None)
        for h in range(GLA_HEADS):
            ks = slice(h * GLA_DK, (h + 1) * GLA_DK)
            el["q32"].append(q_all[:, ks].astype(F32))
            el["k32"].append(k_all[:, ks].astype(F32))
            el["a"].append(jnp.where(t_idx == s_idx, _dot_nt(q_all[:, ks], k_all[:, ks]), 0.0))
        els.append(el)

    def settle(el):
        if el["pending"] is not None:
            prods, mask = el["pending"]
            el["a"] = [jnp.where(mask, prods[h], el["a"][h]) for h in range(GLA_HEADS)]

    for li in range(len(GLA_LEVELS)):
        for el in els:
            e_all = el["e_next"]
            if li + 1 < len(GLA_LEVELS):
                el["e_next"] = jnp.exp(_mat_f32(lvl_ref[li + 1], el["la3"]))
            prods = []
            for h in range(GLA_HEADS):
                e = e_all[:, h * GLA_DK:(h + 1) * GLA_DK]
                prods.append(_dot_nt((el["q32"][h] * e).astype(BF16), (el["k32"][h] * e).astype(BF16)))
            settle(el)
            el["pending"] = (prods, level_mask[li])
    for el in els:
        settle(el)

    for bi, el in enumerate(els):
        el["o"] = []
        for h in range(GLA_HEADS):
            ks = slice(h * GLA_DK, (h + 1) * GLA_DK)
            vs = slice(h * GLA_DV, (h + 1) * GLA_DV)
            cbh = el["cb"][:, ks]
            vb = el["v"][:, vs]
            s_t = s_scr[bi, h]
            o = (_dot_nt((el["q32"][h] * jnp.exp(cbh)).astype(BF16), s_t.astype(BF16))
                 + _dot(el["a"][h].astype(BF16), vb))
            cl = cbh[t_last:t_last + 1, :]
            kd = jnp.where(t_idx <= t_last, el["k32"][h] * jnp.exp(cl - cbh), 0.0).astype(BF16)
            s_scr[bi, h] = jnp.exp(cl) * s_t + _dot_tn(vb, kd)
            el["o"].append(o)
    for bi, el in enumerate(els):
        for h in range(GLA_HEADS):
            vs = slice(h * GLA_DV, (h + 1) * GLA_DV)
            o = el["o"][h]
            y = o * lax.rsqrt(jnp.mean(o * o, axis=-1, keepdims=True) + EPS) * gn_ref[:, vs]
            gate = el["g"][:, vs].astype(F32)
            y = y * (gate * _sigmoid(gate))
            o_ref[bi, :, vs] = y[:rows].astype(o_ref.dtype)

    @pl.when(ci == nc - 1)
    def _():
        s_out[...] = s_scr[...]


GLA_BATCH_PER_STEP = 4


def _gla(q, k, v, gg, la, s0t, gn, batch, t_len):
    L = GLA_CHUNK
    rows = min(L, t_len)
    nc = t_len // rows
    t_last = rows - 1
    nb = max(d for d in range(1, GLA_BATCH_PER_STEP + 1) if batch % d == 0)
    assert t_len % rows == 0 and (nc == 1 or rows == L)
    tok = lambda c: pl.BlockSpec((nb, rows, c), lambda b, i: (b, i, 0))
    shp_s = (GLA_HEADS, GLA_DV, GLA_DK)
    st = pl.BlockSpec((nb,) + shp_s, lambda b, i: (b, 0, 0, 0))
    levels = _gla_level_matrices(L)
    seq = lambda a: a.reshape(batch, t_len, a.shape[-1])
    o, s_new = pl.pallas_call(
        functools.partial(_gla_kernel, L=L, t_last=t_last, nc=nc),
        grid=(batch // nb, nc),
        in_specs=[tok(GLA_KW), tok(GLA_KW), tok(GLA_VW), tok(GLA_VW), tok(GLA_KW), st,
                  _const_spec(levels.shape), _const_spec((1, GLA_VW))],
        out_specs=[tok(GLA_VW), st],
        out_shape=[jax.ShapeDtypeStruct((batch, t_len, GLA_VW), BF16),
                   jax.ShapeDtypeStruct((batch,) + shp_s, F32)],
        scratch_shapes=[pltpu.VMEM((nb,) + shp_s, F32)],
        compiler_params=_cparams(("parallel", "arbitrary")),
        name="gla",
    )(seq(q), seq(k), seq(v), seq(gg), seq(la), s0t, levels, gn)
    return o.reshape(batch * t_len, GLA_VW), s_new


def _pack_even(w, b_fox_f, b_i, b_f):
    d = w.shape[0]
    o = np.cumsum((0, FOX_WIDTH, FOX_WIDTH, FOX_WIDTH, FOX_HEADS, 2 * ML_WIDTH, ML_WIDTH, ML_WIDTH, ML_HEADS, ML_HEADS))
    n_gate = FOX_HEADS + 2 * ML_HEADS
    wp = jnp.concatenate([w[:, o[0]:o[3]], w[:, o[4]:o[7]], w[:, o[3]:o[4]], w[:, o[7]:o[9]],
                          jnp.zeros((d, LANES - n_gate), w.dtype)], axis=1).astype(BF16)
    bias = jnp.concatenate([b_fox_f, b_i, b_f, jnp.zeros((LANES - n_gate,), F32)]).reshape(1, LANES)
    return wp, bias


def _pack_odd(w, w_a2):
    d = w.shape[0]
    wp = jnp.concatenate([w, jnp.zeros((d, LANES - GLA_RANK), w.dtype)], axis=1).astype(BF16)
    wa2 = jnp.concatenate([w_a2, jnp.zeros((LANES - GLA_RANK, w_a2.shape[1]), w_a2.dtype)], axis=0).astype(BF16)
    return wp, wa2


def _gate_rows(gc, batch, t_len, t_pad):
    g = gc.reshape(batch, t_len, LANES)[:, :, :16].transpose(0, 2, 1)
    if t_pad > t_len:
        g = jnp.pad(g, ((0, 0), (0, 0), (0, t_pad - t_len)))
    return g


def _trunk(x, fox_k, fox_v, fox_lf, ml_c, ml_n, ml_m, ml_buf, gla_s, params):
    (norm_mix, norm_ffn, norm_final, w_even, bias_even, conv_w, conv_b, g_ml,
     w_odd, w_a2, b_a, g_gla, w_out, w_ff1, w_ff2) = params
    batch, t_len, d = x.shape
    past = fox_k.shape[2]
    n = batch * t_len
    tm = 512 if n % 512 == 0 else 256 if n % 256 == 0 else n
    depth = norm_mix.shape[0]
    xf = x.reshape(n, d)
    ev_states, odd_states = [], []
    kv_stacked = None
    y = None
    for layer in range(depth):
        j = layer // 2
        if layer % 2 == 0:
            assert (t_len % tm == 0) == (past == 0)
            buf8 = jnp.pad(ml_buf[j], ((0, 0), (8 - (ML_CONV - 1), 0), (0, 0)))
            proj = _inproj_even(xf, norm_mix[layer][None], w_even[j], bias_even[j],
                                tm, batch, t_len, j, (depth + 1) // 2, kv_stacked, conv_w[j], conv_b[j][None], buf8)
            conv_tail = None
            if past == 0:
                qt, kf, vf, ka, vt, mqk, mv, mo, gc, bnd, conv_tail = proj
                kv_stacked = (kf, vf)
                assert tm == FOX_Q_TILE
                attn = _fox_attention(qt, ka, vt, bnd, batch, t_len, FOX_Q_TILE, FOX_KEY_CHUNK)
            else:
                q, kf, vf, mqk, mv, mo, gc = proj
                lf_new = gc.reshape(batch, t_len, LANES)[:, :, :FOX_HEADS]
                lf_row = jnp.concatenate([fox_lf[j], lf_new, jnp.zeros((batch, LANES - t_len, FOX_HEADS), F32)], axis=1)
                lf_row = jnp.pad(lf_row.transpose(0, 2, 1), ((0, 0), (0, 16 - FOX_HEADS), (0, 0)))
                cache_t = lambda a: a.transpose(0, 1, 3, 4, 2).reshape(a.shape[0], batch, FOX_WIDTH, past)
                new_t = lambda a: jnp.pad(a.reshape(batch, t_len, FOX_WIDTH).transpose(0, 2, 1).astype(BF16),
                                          ((0, 0), (0, 0), (0, LANES - t_len)))
                attn = _fox_decode(q.reshape(batch, t_len, FOX_WIDTH), cache_t(fox_k), cache_t(fox_v), j,
                                   new_t(kf), new_t(vf), lf_row).reshape(n, FOX_WIDTH)
            l_ml = ML_CHUNK if t_len >= ML_CHUNK else LANES
            gr = _gate_rows(gc, batch, t_len, max(t_len, l_ml))
            c0 = ml_c[j]
            n0 = ml_n[j][:, :, None, :]
            m0 = jnp.broadcast_to(ml_m[j][:, :, None, None], (batch, ML_HEADS, 1, LANES))
            h_ml, c_new, n_new, m_new, buf_new = _mlstm(mqk, mv, mo, gc, gr, c0, n0, m0, buf8,
                                                        conv_w[j], conv_b[j][None], g_ml[j][None], batch, t_len)
            if conv_tail is not None:
                buf_new = conv_tail
            heads = lambda a: None if past == 0 else a.reshape(batch, t_len, FOX_HEADS, FOX_HEAD_DIM)
            ev_states.append((heads(kf), heads(vf),
                              gc[:, :FOX_HEADS].reshape(batch, t_len, FOX_HEADS),
                              c_new, n_new[:, :, 0, :], m_new[:, :, 0, 0], buf_new[:, 8 - (ML_CONV - 1):, :]))
            mixes = [attn, h_ml]
        else:
            q, k, v, gg, la = _inproj_odd(xf, norm_mix[layer][None], w_odd[j], w_a2[j], b_a[j][None], tm)
            s0t = gla_s[j].transpose(0, 1, 3, 2)
            o, s_new = _gla(q, k, v, gg, la, s0t, g_gla[j][None], batch, t_len)
            odd_states.append(s_new.transpose(0, 1, 3, 2))
            mixes = [o]
        last = layer == depth - 1
        out = _post(xf, mixes, w_out[layer], norm_ffn[layer][None], w_ff1[layer], w_ff2[layer],
                    norm_final[None] if last else None, tm)
        if last:
            y = out
        else:
            xf = out
    ev = [jnp.stack([s[i] for s in ev_states]) for i in range(2 if kv_stacked else 0, 7)]
    if kv_stacked:
        ev = [a.reshape(a.shape[0], batch, FOX_HEADS, FOX_HEAD_DIM, t_len).transpose(0, 1, 4, 2, 3)
              for a in kv_stacked] + ev
    return y.reshape(batch, t_len, d), ev, jnp.stack(odd_states)


def kernel(x_prompt, x_sample, cache_fox_k, cache_fox_v, cache_fox_logf, state_mlstm_c, state_mlstm_n, state_mlstm_m, state_mlstm_conv, state_gla_s, norm_mix, norm_ffn, norm_final, w_in_even, b_fox_f, conv_w_ml, conv_b_ml, b_ml_i, b_ml_f, g_ml, w_in_odd, w_gla_a2, b_gla_a, g_gla, w_out, w_ff1, w_ff2):
    n_even, n_odd = w_in_even.shape[0], w_in_odd.shape[0]
    packed_even = [_pack_even(w_in_even[j], b_fox_f[j], b_ml_i[j], b_ml_f[j]) for j in range(n_even)]
    packed_odd = [_pack_odd(w_in_odd[j], w_gla_a2[j]) for j in range(n_odd)]
    params = (norm_mix, norm_ffn, norm_final,
              [p[0] for p in packed_even], [p[1] for p in packed_even], conv_w_ml, conv_b_ml, g_ml,
              [p[0] for p in packed_odd], [p[1] for p in packed_odd], b_gla_a, g_gla,
              w_out.astype(BF16), w_ff1.astype(BF16), w_ff2.astype(BF16))

    bp = x_prompt.shape[0]
    dt = x_prompt.dtype
    zeros = lambda *s: jnp.zeros(s, dt)
    y_p, ev_p, gla_p = _trunk(
        x_prompt,
        zeros(n_even, bp, 0, FOX_HEADS, FOX_HEAD_DIM), zeros(n_even, bp, 0, FOX_HEADS, FOX_HEAD_DIM),
        zeros(n_even, bp, 0, FOX_HEADS),
        zeros(n_even, bp, ML_HEADS, ML_HEAD_DIM, ML_HEAD_DIM), zeros(n_even, bp, ML_HEADS, ML_HEAD_DIM),
        zeros(n_even, bp, ML_HEADS), zeros(n_even, bp, ML_CONV - 1, 2 * ML_WIDTH),
        zeros(n_odd, bp, GLA_HEADS, GLA_DK, GLA_DV), params)
    y_s, ev_s, gla_s = _trunk(x_sample, cache_fox_k, cache_fox_v, cache_fox_logf, state_mlstm_c, state_mlstm_n,
                              state_mlstm_m, state_mlstm_conv, state_gla_s, params)
    return (y_p, y_s, *ev_p, gla_p, *ev_s, gla_s)
```

```python
import functools

import numpy as np
import jax
import jax.numpy as jnp
from jax import lax
from jax.experimental import pallas as pl
from jax.experimental.pallas import tpu as pltpu

F32 = jnp.float32
BF16 = jnp.bfloat16
EPS = 1e-6
NEG = -1e30
LOG2E = 1.4426950408889634

LANES = 128
VMEM_LIMIT = 56 * 1024 * 1024

D_MODEL = 1024
D_FF = 4 * D_MODEL
FOX_HEADS, FOX_HEAD_DIM = 8, 64
FOX_WIDTH = FOX_HEADS * FOX_HEAD_DIM
ML_HEADS, ML_HEAD_DIM = 4, 128
ML_WIDTH = ML_HEADS * ML_HEAD_DIM
ML_CONV = 4
GLA_HEADS, GLA_DK, GLA_DV = 4, 128, 256
GLA_KW = GLA_HEADS * GLA_DK
GLA_VW = GLA_HEADS * GLA_DV
GLA_RANK = 16
GLA_TAU = 16.0

E_Q, E_K, E_V, E_MQK, E_MV, E_MO, E_G, E_END = np.cumsum(
    (0, FOX_WIDTH, FOX_WIDTH, FOX_WIDTH, 2 * ML_WIDTH, ML_WIDTH, ML_WIDTH, LANES)).tolist()
G_FOX, G_MI, G_MF = 0, FOX_HEADS, FOX_HEADS + ML_HEADS
O_Q, O_K, O_V, O_G, O_A, O_END = np.cumsum((0, GLA_KW, GLA_KW, GLA_VW, GLA_VW, LANES)).tolist()

ML_CHUNK = 256
FOX_KEY_CHUNK = 256
FOX_Q_TILE = 512
GLA_CHUNK = 128
GLA_LEVELS = (64, 32, 16, 8, 4, 2, 1)


def _cparams(sem):
    return pltpu.CompilerParams(dimension_semantics=sem, vmem_limit_bytes=VMEM_LIMIT)


def _const_spec(shape):
    nd = len(shape)
    return pl.BlockSpec(shape, lambda *_: (0,) * nd, pipeline_mode=pl.Buffered(1))


def _rms(x, g):
    return x * lax.rsqrt(jnp.mean(x * x, axis=-1, keepdims=True) + EPS) * g


def _sigmoid(x):
    return 1.0 / (1.0 + jnp.exp(-x))


def _log_sigmoid(x):
    return -(jnp.maximum(-x, 0.0) + jnp.log1p(jnp.exp(-jnp.abs(x))))


def _dot(a, b):
    return jnp.dot(a, b, preferred_element_type=F32)


def _dot_nt(a, b):
    return lax.dot_general(a, b, (((1,), (1,)), ((), ())), preferred_element_type=F32)


def _dot_tn(a, b):
    return lax.dot_general(a, b, (((0,), (0,)), ((), ())), preferred_element_type=F32)


def _split3(x):
    hi = x.astype(BF16)
    r1 = x - hi.astype(F32)
    mid = r1.astype(BF16)
    lo = (r1 - mid.astype(F32)).astype(BF16)
    return hi, mid, lo


def _split2(x):
    hi = x.astype(BF16)
    return hi, (x - hi.astype(F32)).astype(BF16)


def _mat_f32(m, parts):
    return functools.reduce(lambda a, b: a + b, [_dot(m, p) for p in parts])


def _f32_mat(parts, m):
    return functools.reduce(lambda a, b: a + b, [_dot(p, m) for p in parts])


def _tri(n, lower):
    r = lax.broadcasted_iota(jnp.int32, (n, n), 0)
    c = lax.broadcasted_iota(jnp.int32, (n, n), 1)
    keep = (c <= r) if lower else (r <= c)
    return jnp.where(keep, 1.0, 0.0).astype(BF16)


def _pad_rows(a, n):
    if a.shape[0] == n:
        return a
    return jnp.concatenate([a, jnp.zeros((n - a.shape[0], a.shape[1]), a.dtype)], axis=0)


def _conv_silu(u, prev, cw_ref, cb_ref):
    n = u.shape[0]
    ext = jnp.concatenate([prev, u], axis=0)
    y = cb_ref[...] + cw_ref[ML_CONV - 1:ML_CONV, :] * u
    for s in range(1, ML_CONV):
        y = y + cw_ref[ML_CONV - 1 - s:ML_CONV - s, :] * pltpu.roll(ext, s, axis=0)[8:8 + n]
    lane = lax.broadcasted_iota(jnp.int32, (1, 2 * ML_WIDTH), 1)
    return y * _sigmoid(y) * jnp.where(lane >= ML_WIDTH, ML_HEAD_DIM ** -0.5, 1.0), ext


FOX_Q_SCALE = FOX_HEAD_DIM ** -0.5 * LOG2E


def _even_segments(x_ref, g_ref, w_ref):
    h = _rms(x_ref[...], g_ref[...]).astype(BF16)
    return lambda a, b: _dot(h, w_ref[:, a:b])


def _even_gates(seg, bias_ref):
    gz = seg(E_G, E_END) + bias_ref[...]
    lane = lax.broadcasted_iota(jnp.int32, gz.shape, 1)
    is_log = (lane < G_MI) | (lane >= G_MF)
    return jnp.where(is_log, _log_sigmoid(gz), gz)


def _inproj_even_rows_kernel(x_ref, g_ref, w_ref, bias_ref,
                             q_ref, kf_ref, vf_ref, mqk_ref, mv_ref, mo_ref, gc_ref):
    seg = _even_segments(x_ref, g_ref, w_ref)
    q_ref[...] = (seg(E_Q, E_K) * FOX_Q_SCALE).astype(BF16)
    kf_ref[...] = seg(E_K, E_V)
    vf_ref[...] = seg(E_V, E_MQK)
    mqk_ref[...] = seg(E_MQK, E_MV)
    mv_ref[...] = seg(E_MV, E_MO).astype(BF16)
    mo_ref[...] = seg(E_MO, E_G).astype(BF16)
    gc_ref[...] = _even_gates(seg, bias_ref)


def _inproj_even_seq_kernel(*refs, tpb, aliased):
    x_ref, g_ref, w_ref, bias_ref, ind_ref, cw_ref, cb_ref, buf_ref = refs[:8]
    refs = refs[8 + (2 if aliased else 0):]
    (qt_ref, kf_ref, vf_ref, ka_ref, vt_ref, mqk_ref, mv_ref, mo_ref, gc_ref, bnd_ref, tail_ref,
     carry, conv_prev) = refs
    @pl.when(pl.program_id(0) % tpb == 0)
    def _():
        carry[...] = jnp.zeros_like(carry)
        conv_prev[...] = buf_ref[0]

    seg = _even_segments(x_ref, g_ref, w_ref)
    mqk = seg(E_MQK, E_MV)
    gates = _even_gates(seg, bias_ref)
    gc_ref[...] = gates
    q = seg(E_Q, E_K) * FOX_Q_SCALE
    k = seg(E_K, E_V)
    v = seg(E_V, E_MQK)
    qt_ref[...] = q.T.astype(BF16)
    mv = seg(E_MV, E_MO)
    kf_ref[0, 0] = k.T
    f = _append_f_terms(gates, carry, k.astype(BF16), ka_ref)
    mo = seg(E_MO, E_G)
    act, _ = _conv_silu(mqk, conv_prev[...], cw_ref, cb_ref)
    mqk_ref[...] = act.astype(BF16)
    last_rows = mqk[mqk.shape[0] - 8:]
    conv_prev[...] = last_rows
    tail_ref[0] = last_rows
    mv_ref[...] = mv.astype(BF16)
    v_t = v.T
    vt_ref[...] = v_t.astype(BF16)
    vf_ref[0, 0] = v_t
    mo_ref[...] = mo.astype(BF16)
    norms = _dot((q * q).astype(BF16), ind_ref[0]) + _dot((k * k).astype(BF16), ind_ref[1])
    bnd_ref[...] = jnp.zeros_like(bnd_ref)
    bnd_ref[0, 0:1, :] = f[0:1, :]
    for c in range(f.shape[0] // FOX_KEY_CHUNK):
        bnd_ref[0, 1 + c:2 + c, :] = f[(c + 1) * FOX_KEY_CHUNK - 1:(c + 1) * FOX_KEY_CHUNK, :]
    bnd_ref[0, 7:8, :] = jnp.max(norms, axis=0, keepdims=True)


def _inproj_even(x, g, w, bias, tm, batch, t_len, slot, n_slots, kv_prev, cw, cb, buf8):
    n = x.shape[0]
    row = lambda c: (pl.BlockSpec((tm, c), lambda i: (i, 0)), (n, c))
    col = (pl.BlockSpec((FOX_WIDTH, tm), lambda i: (0, i)), (FOX_WIDTH, n))
    common_in = [row(D_MODEL)[0], _const_spec((1, D_MODEL)), _const_spec((D_MODEL, E_END)), _const_spec((1, LANES))]
    tail = [(row(ML_WIDTH), BF16), (row(ML_WIDTH), BF16), (row(LANES), F32)]
    if t_len % tm != 0:
        outs = [(row(FOX_WIDTH), BF16), (row(FOX_WIDTH), F32), (row(FOX_WIDTH), F32), (row(2 * ML_WIDTH), F32)] + tail
        return pl.pallas_call(
            _inproj_even_rows_kernel,
            grid=(n // tm,),
            in_specs=common_in,
            out_specs=[spec for (spec, _), _ in outs],
            out_shape=[jax.ShapeDtypeStruct(shape, dt) for (_, shape), dt in outs],
            compiler_params=_cparams(("parallel",)),
            name="inproj_even_rows",
        )(x, g, w, bias)
    tpb = t_len // tm
    state = (pl.BlockSpec((1, 1, FOX_WIDTH, tm), lambda i: (slot, i // tpb, 0, i % tpb)),
             (n_slots, batch, FOX_WIDTH, t_len))
    per_seq = (pl.BlockSpec((1, 8, 2 * ML_WIDTH), lambda i: (i // tpb, 0, 0)), (batch, 8, 2 * ML_WIDTH))
    outs = ([(col, BF16), (state, F32), (state, F32), (row(2 * FOX_WIDTH), BF16), (col, BF16), (row(2 * ML_WIDTH), BF16)]
            + tail + [((pl.BlockSpec((1, 8, LANES), lambda i: (i, 0, 0)), (n // tm, 8, LANES)), F32), (per_seq, F32)])
    assert tm % FOX_KEY_CHUNK == 0 and tm // FOX_KEY_CHUNK <= 6
    ind = np.zeros((2, FOX_WIDTH, LANES), np.float32)
    for c in range(FOX_WIDTH):
        ind[0, c, FOX_HEADS + c // FOX_HEAD_DIM] = 1.0
        ind[1, c, 2 * FOX_HEADS + c // FOX_HEAD_DIM] = 1.0
    ins = [x, g, w, bias, jnp.asarray(ind, BF16), cw, cb, buf8]
    in_specs = common_in + [_const_spec(ind.shape), _const_spec(cw.shape), _const_spec(cb.shape), per_seq[0]]
    aliases = {}
    if kv_prev is not None:
        aliases = {len(ins): 1, len(ins) + 1: 2}
        ins += list(kv_prev)
        in_specs += [pl.BlockSpec(memory_space=pl.ANY)] * 2
    return pl.pallas_call(
        functools.partial(_inproj_even_seq_kernel, tpb=tpb, aliased=kv_prev is not None),
        grid=(n // tm,),
        in_specs=in_specs,
        out_specs=[spec for (spec, _), _ in outs],
        out_shape=[jax.ShapeDtypeStruct(shape, dt) for (_, shape), dt in outs],
        scratch_shapes=[pltpu.VMEM((1, LANES), F32), pltpu.VMEM((8, 2 * ML_WIDTH), F32)],
        input_output_aliases=aliases,
        compiler_params=_cparams(("arbitrary",)),
        name="inproj_even_seq",
    )(*ins)


def _inproj_odd_kernel(x_ref, g_ref, w_ref, wa2_ref, ba_ref, q_ref, k_ref, v_ref, gg_ref, la_ref):
    h = _rms(x_ref[...], g_ref[...]).astype(BF16)

    def seg(a, b):
        return _dot(h, w_ref[:, a:b])

    q_ref[...] = (seg(O_Q, O_K) * (GLA_DK ** -0.5)).astype(BF16)
    k_ref[...] = seg(O_K, O_V).astype(BF16)
    v_ref[...] = seg(O_V, O_G).astype(BF16)
    gg_ref[...] = seg(O_G, O_A).astype(BF16)
    ga = seg(O_A, O_END).astype(BF16)
    la_ref[...] = _log_sigmoid(_dot(ga, wa2_ref[...]) + ba_ref[...]) * (1.0 / GLA_TAU)


def _inproj_odd(x, g, w, wa2, ba, tm):
    n = x.shape[0]
    row = lambda c: pl.BlockSpec((tm, c), lambda i: (i, 0))
    outs = [(GLA_KW, BF16), (GLA_KW, BF16), (GLA_VW, BF16), (GLA_VW, BF16), (GLA_KW, F32)]
    return pl.pallas_call(
        _inproj_odd_kernel,
        grid=(n // tm,),
        in_specs=[row(D_MODEL), _const_spec((1, D_MODEL)), _const_spec((D_MODEL, O_END)),
                  _const_spec((LANES, GLA_KW)), _const_spec((1, GLA_KW))],
        out_specs=[row(c) for c, _ in outs],
        out_shape=[jax.ShapeDtypeStruct((n, c), dt) for c, dt in outs],
        compiler_params=_cparams(("parallel",)),
        name="inproj_odd",
    )(x, g, w, wa2, ba)


def _post_kernel(*refs, n_mix, final):
    x_ref = refs[0]
    mix_refs = refs[1:1 + n_mix]
    wo_ref, gf_ref, w1_ref, w2_ref = refs[1 + n_mix:5 + n_mix]
    rest = refs[5 + n_mix:]
    mix = mix_refs[0][...] if n_mix == 1 else jnp.concatenate([r[...] for r in mix_refs], axis=1)
    x1 = x_ref[...] + _dot(mix, wo_ref[0])
    h = _rms(x1, gf_ref[...]).astype(BF16)
    y = x1
    for c in range(D_FF // D_MODEL):
        sl = slice(c * D_MODEL, (c + 1) * D_MODEL)
        t = jnp.maximum(_dot(h, w1_ref[0, :, sl]), 0.0)
        y = y + _dot((t * t).astype(BF16), w2_ref[0, sl, :])
    if final:
        gfin_ref, out_ref = rest
        out_ref[...] = _rms(y, gfin_ref[...])
    else:
        (out_ref,) = rest
        out_ref[...] = y


def _post(x, mixes, wo, gf, w1, w2, gfin, tm, layer):
    n = x.shape[0]
    row = lambda c: pl.BlockSpec((tm, c), lambda i: (i, 0))
    of_layer = lambda a: pl.BlockSpec((1,) + a.shape[1:], lambda i: (layer, 0, 0), pipeline_mode=pl.Buffered(1))
    final = gfin is not None
    ins = [x, *mixes, wo, gf, w1, w2]
    specs = [row(D_MODEL)] + [row(m.shape[1]) for m in mixes] + [
        of_layer(wo), _const_spec((1, D_MODEL)), of_layer(w1), of_layer(w2)]
    if final:
        ins.append(gfin)
        specs.append(_const_spec((1, D_MODEL)))
    return pl.pallas_call(
        functools.partial(_post_kernel, n_mix=len(mixes), final=final),
        grid=(n // tm,),
        in_specs=specs,
        out_specs=row(D_MODEL),
        out_shape=jax.ShapeDtypeStruct((n, D_MODEL), F32),
        compiler_params=_cparams(("parallel",)),
        name="post_final" if final else "post",
    )(*ins)


F_TERMS = 3


def _append_f_terms(lf, carry, k, ka_ref):
    tc = lf.shape[0]
    f = _mat_f32(_tri(tc, True), _split3(lf)) + carry[...]
    carry[...] = f[tc - 1:tc, :]
    hi, mid, lo = [p.astype(F32) for p in _split3(f * LOG2E)]
    lane = lax.broadcasted_iota(jnp.int32, (1, LANES), 1)
    cols = jnp.where(lane < FOX_HEADS, hi,
                     jnp.where(lane < 2 * FOX_HEADS, pltpu.roll(mid, FOX_HEADS, axis=1),
                               jnp.where(lane < 3 * FOX_HEADS, pltpu.roll(lo, 2 * FOX_HEADS, axis=1), 0.0)))
    cols = cols.astype(BF16)
    for hp in range(FOX_HEADS // 2):
        ka_ref[:, 2 * hp * LANES:(2 * hp + 1) * LANES] = k[:, hp * LANES:(hp + 1) * LANES]
        ka_ref[:, (2 * hp + 1) * LANES:(2 * hp + 2) * LANES] = cols
    return f


def _fox_decode_kernel(q_ref, kc_ref, vc_ref, kn_ref, vn_ref, lf_ref, o_ref, *, t_new, past):
    nkeys = past + LANES
    lane_f = lax.broadcasted_iota(jnp.int32, (1, FOX_WIDTH), 1)
    head_mask = [jnp.where((lane_f >= h * FOX_HEAD_DIM) & (lane_f < (h + 1) * FOX_HEAD_DIM), 1.0, 0.0)
                 for h in range(FOX_HEADS)]
    q = q_ref[0].astype(F32)
    qb = jnp.concatenate([q * hm for hm in head_mask], axis=0).astype(BF16)

    def keys(cache_ref, new_ref):
        return jnp.concatenate([cache_ref[0, 0].astype(BF16), new_ref[0]], axis=1)

    s = _dot(qb, keys(kc_ref, kn_ref))
    lf = lf_ref[0]
    triu = _tri(LANES, False)
    carry = jnp.zeros((lf.shape[0], 1), F32)
    blocks = []
    for c in range(nkeys // LANES):
        cs = _f32_mat(_split3(lf[:, c * LANES:(c + 1) * LANES]), triu) + carry
        carry = cs[:, LANES - 1:LANES]
        blocks.append(cs)
    f_all = jnp.concatenate(blocks, axis=1) * LOG2E
    s = s - jnp.concatenate([jnp.broadcast_to(f_all[h:h + 1, :], (t_new, nkeys)) for h in range(FOX_HEADS)], axis=0)
    kpos = lax.broadcasted_iota(jnp.int32, (1, nkeys), 1)
    qpos = past + (lax.broadcasted_iota(jnp.int32, (FOX_HEADS * t_new, 1), 0) & (t_new - 1))
    s = jnp.where(kpos <= qpos, s, NEG)
    p = jnp.exp2(s - jnp.max(s, axis=-1, keepdims=True))
    ob = _dot_nt(p.astype(BF16), keys(vc_ref, vn_ref)) / jnp.sum(p, axis=-1, keepdims=True)
    out = ob[0:t_new] * head_mask[0]
    for h in range(1, FOX_HEADS):
        out = out + ob[h * t_new:(h + 1) * t_new] * head_mask[h]
    o_ref[0] = out.astype(o_ref.dtype)


def _fox_decode(q, k_cache, v_cache, layer, k_new, v_new, lf_row):
    batch, t_new, _ = q.shape
    past = k_cache.shape[3]
    assert t_new & (t_new - 1) == 0 and t_new <= LANES and t_new % 16 == 0 and past % LANES == 0
    blk = lambda a: pl.BlockSpec((1,) + a.shape[1:], lambda b: (b, 0, 0))
    cache = pl.BlockSpec((1, 1, FOX_WIDTH, past), lambda b: (layer, b, 0, 0))
    args = (q, k_cache, v_cache, k_new, v_new, lf_row)
    return pl.pallas_call(
        functools.partial(_fox_decode_kernel, t_new=t_new, past=past),
        grid=(batch,),
        in_specs=[blk(q), cache, cache, blk(k_new), blk(v_new), blk(lf_row)],
        out_specs=blk(q),
        out_shape=jax.ShapeDtypeStruct(q.shape, BF16),
        compiler_params=_cparams(("parallel",)),
        name="fox_decode",
    )(*args)


ONES_ROWS = 16


def _fox_attn_kernel(first_ref, qt_ref, ka_ref, vt_ref, o_ref, s_scr, acc_scr, *, tq, tkc, past, n_diag):
    q_first = past + pl.program_id(2) * tq
    n_full = q_first // tkc
    drow = lax.broadcasted_iota(jnp.int32, (LANES, 1), 0)
    lane = lax.broadcasted_iota(jnp.int32, (1, 2 * tq), 1)
    lane_head = jnp.where(lane < tq, 0, 1)
    own_head = jnp.where(jnp.where(drow < FOX_HEAD_DIM, 0, 1) == lane_head, 1.0, 0.0).astype(BF16)
    qt = qt_ref[...]
    q2 = jnp.concatenate([qt, qt], axis=1) * own_head
    head = 2 * pl.program_id(1) + lane_head
    f_sel = ((drow & (FOX_HEADS - 1)) == head) & (drow < F_TERMS * FOX_HEADS)
    qa = jnp.concatenate([q2, jnp.where(f_sel, -1.0, 0.0).astype(BF16)], axis=0)
    ones = jnp.ones((ONES_ROWS, tkc), BF16)
    acc_scr[...] = jnp.zeros_like(acc_scr)

    qa_h = [qa[:, hh * tq:(hh + 1) * tq] for hh in range(2)]
    qpos = q_first + lax.broadcasted_iota(jnp.int32, (1, tq), 1)

    def produce(j, hh, buf, lo=0):
        start = pl.multiple_of(j * tkc, tkc)
        s_scr[buf, hh, :, lo:] = _dot(ka_ref[pl.ds(start, tkc), :], qa_h[hh][:, lo:])

    def consume(j, hh, buf, m_all, masked, lo=0):
        start = pl.multiple_of(j * tkc, tkc)
        st = s_scr[buf, hh, :, lo:]
        m_prev = m_all[:, lo:]
        if masked:
            kpos = start + lax.broadcasted_iota(jnp.int32, (tkc, 1), 0)
            st = jnp.where(kpos <= qpos[:, lo:], st, NEG)
        m_new = jnp.maximum(m_prev, jnp.max(st, axis=0, keepdims=True))
        alpha = jnp.exp2(m_prev - m_new)
        p = jnp.exp2(st - m_new).astype(BF16)
        rows = slice(hh * FOX_HEAD_DIM, (hh + 1) * FOX_HEAD_DIM)
        va = jnp.concatenate([vt_ref[rows, pl.ds(start, tkc)], ones], axis=0)
        acc_scr[hh, :, lo:] = alpha * acc_scr[hh, :, lo:] + _dot(va, p)
        return m_new if lo == 0 else jnp.concatenate([m_all[:, :lo], m_new], axis=1)

    nq = pl.num_programs(2)
    slot = ((pl.program_id(0) * pl.num_programs(1) + pl.program_id(1)) * nq + pl.program_id(2)) * 2
    first = [first_ref[slot], first_ref[slot + 1]]
    joint = jnp.maximum(first[0], first[1])
    for hh in range(2):
        produce(first[hh], hh, 0)

    def pair_of_chunks(i, ms, heads):
        ms = list(ms)
        for step in range(2):
            for hh in heads:
                produce(2 * i + step + 1, hh, 1 - step)
                ms[hh] = consume(2 * i + step, hh, step, ms[hh], False)
        return tuple(ms)

    m_init = jnp.full((1, tq), NEG, F32)
    ms = (m_init, m_init)
    for hh in range(2):
        ms = lax.fori_loop(first[hh] // 2, joint // 2, functools.partial(pair_of_chunks, heads=(hh,)), ms)
    ms = list(lax.fori_loop(joint // 2, n_full // 2, functools.partial(pair_of_chunks, heads=(0, 1)), ms))
    for d in range(n_diag):
        for hh in range(2):
            if d + 1 < n_diag:
                produce(n_full + d + 1, hh, (d + 1) % 2, lo=min((d + 1) * tkc, tq - LANES))
            ms[hh] = consume(n_full + d, hh, d % 2, ms[hh], True, lo=min(d * tkc, tq - LANES))
    out = jnp.concatenate([acc_scr[hh, 0:FOX_HEAD_DIM] / acc_scr[hh, FOX_HEAD_DIM:FOX_HEAD_DIM + 1]
                           for hh in range(2)], axis=0)
    o_ref[...] = out.T.astype(o_ref.dtype)


SKIP_MARGIN = 40.0
NORM_SLACK = 1.02


def _fox_first_chunks(bnd, batch, t_len, tq, tkc):
    nq, nc, pairs = t_len // tq, t_len // tkc, FOX_HEADS // 2
    per_tile = tq // tkc
    b4 = bnd.reshape(batch, nq, 8, LANES)
    f_start = b4[:, :, 0, 0:FOX_HEADS] * LOG2E
    f_end = b4[:, :, 1:1 + per_tile, 0:FOX_HEADS].reshape(batch, nc, FOX_HEADS) * LOG2E
    qn = jnp.sqrt(b4[:, :, 7, FOX_HEADS:2 * FOX_HEADS])
    kn = jnp.sqrt(jnp.max(b4[:, :, 7, 2 * FOX_HEADS:3 * FOX_HEADS], axis=1))
    thr = 2.0 * NORM_SLACK * qn * kn[:, None, :] + SKIP_MARGIN
    decay = f_end[:, None, :, :] - f_start[:, :, None, :]
    before_tile = jnp.arange(nc)[None, :] < ((jnp.arange(nq) * tq) // tkc)[:, None]
    ok = (decay > thr[:, :, None, :]) & before_tile[None, :, :, None]
    lead = jnp.sum(jnp.cumprod(ok.astype(jnp.int32), axis=2), axis=2)
    lead = ((lead // 2) * 2).reshape(batch, nq, pairs, 2)
    return lead.transpose(0, 2, 1, 3).reshape(-1).astype(jnp.int32)


def _fox_attention(qt, ka, vt, bnd, batch, t_len, tq, tkc):
    past = 0
    nq = t_len // tq
    n_diag = max(1, tq // tkc)
    assert tq & (tq - 1) == 0 and tq % LANES == 0 and (tkc % tq == 0 or tq % tkc == 0) and t_len % tq == 0
    assert nq == 1 or tq % (2 * tkc) == 0
    assert t_len >= n_diag * tkc and t_len % tkc == 0
    pairs = FOX_HEADS // 2
    first = _fox_first_chunks(bnd, batch, t_len, tq, tkc)
    grid_spec = pltpu.PrefetchScalarGridSpec(
        num_scalar_prefetch=1,
        grid=(batch, pairs, nq),
        in_specs=[pl.BlockSpec((LANES, tq), lambda b, h, i, f: (h, b * nq + i)),
                  pl.BlockSpec((t_len, 2 * LANES), lambda b, h, i, f: (b, h)),
                  pl.BlockSpec((LANES, t_len), lambda b, h, i, f: (h, b))],
        out_specs=pl.BlockSpec((tq, LANES), lambda b, h, i, f: (b * nq + i, h)),
        scratch_shapes=[pltpu.VMEM((2, 2, tkc, tq), F32),
                        pltpu.VMEM((2, FOX_HEAD_DIM + ONES_ROWS, tq), F32)])
    return pl.pallas_call(
        functools.partial(_fox_attn_kernel, tq=tq, tkc=tkc, past=past, n_diag=n_diag),
        grid_spec=grid_spec,
        out_shape=jax.ShapeDtypeStruct((batch * t_len, FOX_WIDTH), BF16),
        compiler_params=_cparams(("parallel", "parallel", "arbitrary")),
        name="fox_attention",
    )(first, qt, ka, vt)


def _mlstm_kernel(mqk_ref, mv_ref, mo_ref, gc_ref, gr_ref, c0_ref, n0_ref, m0_ref, buf_ref,
                  cw_ref, cb_ref, gml_ref,
                  h_ref, c_out, n_out, m_out, buf_out,
                  c_scr, n_scr, m_scr, prev_scr, *, L, t_last, nc, preact):
    ci = pl.program_id(1)
    rows = mqk_ref.shape[0]

    @pl.when(ci == 0)
    def _():
        c_scr[...] = c0_ref[0]
        n_scr[...] = n0_ref[0]
        m_scr[...] = m0_ref[0]
        prev_scr[...] = buf_ref[0]

    if preact:
        qk = _pad_rows(mqk_ref[...], L).astype(F32)
        ext = None
    else:
        u = _pad_rows(mqk_ref[...], L)
        qk, ext = _conv_silu(u, prev_scr[...], cw_ref, cb_ref)
        prev_scr[...] = u[L - 8:L]

    gc = _pad_rows(gc_ref[...], L)
    gr = gr_ref[0]
    b_col = _mat_f32(_tri(L, True), _split3(gc))
    b_row = _f32_mat(_split3(gr), _tri(L, False))
    v_all = _pad_rows(mv_ref[...], L)
    o_all = _pad_rows(mo_ref[...], L)
    t_idx = lax.broadcasted_iota(jnp.int32, (L, 1), 0)
    causal = lax.broadcasted_iota(jnp.int32, (1, L), 1) <= t_idx

    heads = []
    for h in range(ML_HEADS):
        hs = slice(h * ML_HEAD_DIM, (h + 1) * ML_HEAD_DIM)
        q32 = qk[:, hs]
        k32 = qk[:, ML_WIDTH + h * ML_HEAD_DIM:ML_WIDTH + (h + 1) * ML_HEAD_DIM]
        qb, kb = q32.astype(BF16), k32.astype(BF16)
        c_prev = c_scr[h]
        heads.append(dict(hs=hs, q32=q32, k32=k32, qb=qb, kb=kb, vb=v_all[:, hs], c_prev=c_prev,
                          s=_dot_nt(qb, kb), qc=_dot_nt(qb, c_prev.astype(BF16))))

    for h, hd in enumerate(heads):
        bc = b_col[:, G_MF + h:G_MF + h + 1]
        ic = gc[:, G_MI + h:G_MI + h + 1]
        br = b_row[G_MF + h:G_MF + h + 1, :]
        ir = gr[G_MI + h:G_MI + h + 1, :]
        m_prev = m_scr[h][:, 0:1]
        dmat = jnp.where(causal, bc + (ir - br), NEG)
        inter = bc + m_prev
        m_t = jnp.maximum(inter, jnp.max(dmat, axis=-1, keepdims=True))
        w = jnp.exp(dmat - m_t)
        g = jnp.exp(inter - m_t)
        a = w * hd["s"]
        b_last = bc[t_last:t_last + 1, :]
        m_last = m_t[t_last:t_last + 1, :]
        w_end = jnp.where(t_idx <= t_last, jnp.exp(b_last - bc + ic - m_last), 0.0)
        hd.update(m_t=m_t, g=g, a=a, m_last=m_last, g_end=g[t_last:t_last + 1, :], w_end=w_end,
                  av=_dot(a.astype(BF16), hd["vb"]),
                  vk=_dot_tn((hd["vb"].astype(F32) * w_end).astype(BF16), hd["kb"]))

    for h, hd in enumerate(heads):
        hs, g, a, m_t = hd["hs"], hd["g"], hd["a"], hd["m_t"]
        n_prev = n_scr[h]
        num = g * hd["qc"] + hd["av"]
        den = g * jnp.sum(hd["q32"] * n_prev, axis=-1, keepdims=True) + jnp.sum(a, axis=-1, keepdims=True)
        hh = num / jnp.maximum(jnp.abs(den), jnp.exp(-m_t))
        c_scr[h] = hd["g_end"] * hd["c_prev"] + hd["vk"]
        n_scr[h] = hd["g_end"] * n_prev + jnp.sum(hd["k32"] * hd["w_end"], axis=0, keepdims=True)
        m_scr[h] = jnp.broadcast_to(hd["m_last"], (1, LANES))
        yh = hh * lax.rsqrt(jnp.mean(hh * hh, axis=-1, keepdims=True) + EPS) * gml_ref[:, hs]
        yh = yh * _sigmoid(o_all[:, hs].astype(F32))
        h_ref[:, hs] = yh[:rows].astype(h_ref.dtype)

    @pl.when(ci == nc - 1)
    def _():
        c_out[0] = c_scr[...]
        n_out[0] = n_scr[...]
        m_out[0] = m_scr[...]
        buf_out[0] = buf_ref[0] if preact else ext[t_last + 1:t_last + 9]


def _mlstm(mqk, mv, mo, gc, gr, c0, n0, m0, buf8, cw, cb, gml, batch, t_len):
    preact = mqk.dtype == BF16
    L = ML_CHUNK if t_len >= ML_CHUNK else LANES
    rows = min(L, t_len)
    nc = t_len // rows
    t_last = rows - 1
    assert t_len % rows == 0 and (nc == 1 or rows == L) and (t_last + 1) % 8 == 0
    tok = lambda c: pl.BlockSpec((rows, c), lambda b, i: (b * nc + i, 0))
    st = lambda *s: pl.BlockSpec((1,) + s, lambda b, i: (b,) + (0,) * len(s))
    shp_c = (ML_HEADS, ML_HEAD_DIM, ML_HEAD_DIM)
    shp_n = (ML_HEADS, 1, ML_HEAD_DIM)
    return pl.pallas_call(
        functools.partial(_mlstm_kernel, L=L, t_last=t_last, nc=nc, preact=preact),
        grid=(batch, nc),
        in_specs=[tok(2 * ML_WIDTH), tok(ML_WIDTH), tok(ML_WIDTH), tok(LANES),
                  pl.BlockSpec((1, 16, L), lambda b, i: (b, 0, i)),
                  st(*shp_c), st(*shp_n), st(*shp_n), st(8, 2 * ML_WIDTH),
                  _const_spec((ML_CONV, 2 * ML_WIDTH)), _const_spec((1, 2 * ML_WIDTH)), _const_spec((1, ML_WIDTH))],
        out_specs=[tok(ML_WIDTH), st(*shp_c), st(*shp_n), st(*shp_n), st(8, 2 * ML_WIDTH)],
        out_shape=[jax.ShapeDtypeStruct((batch * t_len, ML_WIDTH), BF16),
                   jax.ShapeDtypeStruct((batch,) + shp_c, F32),
                   jax.ShapeDtypeStruct((batch,) + shp_n, F32),
                   jax.ShapeDtypeStruct((batch,) + shp_n, F32),
                   jax.ShapeDtypeStruct((batch, 8, 2 * ML_WIDTH), F32)],
        scratch_shapes=[pltpu.VMEM(shp_c, F32), pltpu.VMEM(shp_n, F32), pltpu.VMEM(shp_n, F32),
                        pltpu.VMEM((8, 2 * ML_WIDTH), F32)],
        compiler_params=_cparams(("parallel", "arbitrary")),
        name="mlstm",
    )(mqk, mv, mo, gc, gr, c0, n0, m0, buf8, cw, cb, gml)


def _gla_level_matrices(L):
    mats = np.zeros((len(GLA_LEVELS), L, L), np.float32)
    for li, b in enumerate(GLA_LEVELS):
        for t in range(L):
            base = (t // (2 * b)) * 2 * b
            bound = base + b - 1
            if t > bound:
                mats[li, t, bound + 1:t + 1] = 1.0
            else:
                mats[li, t, t + 1:bound + 1] = 1.0
    return jnp.asarray(mats, BF16)


def _gla_kernel(q_ref, k_ref, v_ref, gg_ref, la_ref, s0_ref, lvl_ref, gn_ref, o_ref, s_out, s_scr, *, L, t_last, nc):
    ci = pl.program_id(1)
    nb, rows = q_ref.shape[0], q_ref.shape[1]

    @pl.when(ci == 0)
    def _():
        s_scr[...] = s0_ref[...]

    t_idx = lax.broadcasted_iota(jnp.int32, (L, 1), 0)
    s_idx = lax.broadcasted_iota(jnp.int32, (1, L), 1)
    tril = _tri(L, True)
    level_mask = []
    for b in GLA_LEVELS:
        sh = b.bit_length() - 1
        level_mask.append((jnp.right_shift(t_idx, sh + 1) == jnp.right_shift(s_idx, sh + 1))
                          & ((jnp.right_shift(t_idx, sh) & 1) == 1) & ((jnp.right_shift(s_idx, sh) & 1) == 0))

    els = []
    for bi in range(nb):
        la3 = _split2(_pad_rows(la_ref[bi], L))
        q_all = _pad_rows(q_ref[bi], L)
        k_all = _pad_rows(k_ref[bi], L)
        el = dict(la3=la3, cb=_mat_f32(tril, la3), v=_pad_rows(v_ref[bi], L), g=_pad_rows(gg_ref[bi], L),
                  q32=[], k32=[], a=[], e_next=jnp.exp(_mat_f32(lvl_ref[0], la3)), pending=None)
        for h in range(GLA_HEADS):
            ks = slice(h * GLA_DK, (h + 1) * GLA_DK)
            el["q32"].append(q_all[:, ks].astype(F32))
            el["k32"].append(k_all[:, ks].astype(F32))
            el["a"].append(jnp.where(t_idx == s_idx, _dot_nt(q_all[:, ks], k_all[:, ks]), 0.0))
        els.append(el)

    def settle(el):
        if el["pending"] is not None:
            prods, mask = el["pending"]
            el["a"] = [jnp.where(mask, prods[h], el["a"][h]) for h in range(GLA_HEADS)]

    for li in range(len(GLA_LEVELS)):
        for el in els:
            e_all = el["e_next"]
            if li + 1 < len(GLA_LEVELS):
                el["e_next"] = jnp.exp(_mat_f32(lvl_ref[li + 1], el["la3"]))
            prods = []
            for h in range(GLA_HEADS):
                e = e_all[:, h * GLA_DK:(h + 1) * GLA_DK]
                prods.append(_dot_nt((el["q32"][h] * e).astype(BF16), (el["k32"][h] * e).astype(BF16)))
            settle(el)
            el["pending"] = (prods, level_mask[li])
    for el in els:
        settle(el)

    for bi, el in enumerate(els):
        el["o"] = []
        for h in range(GLA_HEADS):
            ks = slice(h * GLA_DK, (h + 1) * GLA_DK)
            vs = slice(h * GLA_DV, (h + 1) * GLA_DV)
            cbh = el["cb"][:, ks]
            vb = el["v"][:, vs]
            s_t = s_scr[bi, h]
            o = (_dot_nt((el["q32"][h] * jnp.exp(cbh)).astype(BF16), s_t.astype(BF16))
                 + _dot(el["a"][h].astype(BF16), vb))
            cl = cbh[t_last:t_last + 1, :]
            kd = jnp.where(t_idx <= t_last, el["k32"][h] * jnp.exp(cl - cbh), 0.0).astype(BF16)
            s_scr[bi, h] = jnp.exp(cl) * s_t + _dot_tn(vb, kd)
            el["o"].append(o)
    for bi, el in enumerate(els):
        for h in range(GLA_HEADS):
            vs = slice(h * GLA_DV, (h + 1) * GLA_DV)
            o = el["o"][h]
            y = o * lax.rsqrt(jnp.mean(o * o, axis=-1, keepdims=True) + EPS) * gn_ref[:, vs]
            gate = el["g"][:, vs].astype(F32)
            y = y * (gate * _sigmoid(gate))
            o_ref[bi, :, vs] = y[:rows].astype(o_ref.dtype)

    @pl.when(ci == nc - 1)
    def _():
        s_out[...] = s_scr[...]


GLA_BATCH_PER_STEP = 4


def _gla(q, k, v, gg, la, s0t, gn, batch, t_len):
    L = GLA_CHUNK
    rows = min(L, t_len)
    nc = t_len // rows
    t_last = rows - 1
    nb = max(d for d in range(1, GLA_BATCH_PER_STEP + 1) if batch % d == 0)
    assert t_len % rows == 0 and (nc == 1 or rows == L)
    tok = lambda c: pl.BlockSpec((nb, rows, c), lambda b, i: (b, i, 0))
    shp_s = (GLA_HEADS, GLA_DV, GLA_DK)
    st = pl.BlockSpec((nb,) + shp_s, lambda b, i: (b, 0, 0, 0))
    levels = _gla_level_matrices(L)
    seq = lambda a: a.reshape(batch, t_len, a.shape[-1])
    o, s_new = pl.pallas_call(
        functools.partial(_gla_kernel, L=L, t_last=t_last, nc=nc),
        grid=(batch // nb, nc),
        in_specs=[tok(GLA_KW), tok(GLA_KW), tok(GLA_VW), tok(GLA_VW), tok(GLA_KW), st,
                  _const_spec(levels.shape), _const_spec((1, GLA_VW))],
        out_specs=[tok(GLA_VW), st],
        out_shape=[jax.ShapeDtypeStruct((batch, t_len, GLA_VW), BF16),
                   jax.ShapeDtypeStruct((batch,) + shp_s, F32)],
        scratch_shapes=[pltpu.VMEM((nb,) + shp_s, F32)],
        compiler_params=_cparams(("parallel", "arbitrary")),
        name="gla",
    )(seq(q), seq(k), seq(v), seq(gg), seq(la), s0t, levels, gn)
    return o.reshape(batch * t_len, GLA_VW), s_new


def _pack_even(w, b_fox_f, b_i, b_f):
    d = w.shape[0]
    o = np.cumsum((0, FOX_WIDTH, FOX_WIDTH, FOX_WIDTH, FOX_HEADS, 2 * ML_WIDTH, ML_WIDTH, ML_WIDTH, ML_HEADS, ML_HEADS))
    n_gate = FOX_HEADS + 2 * ML_HEADS
    wp = jnp.concatenate([w[:, o[0]:o[3]], w[:, o[4]:o[7]], w[:, o[3]:o[4]], w[:, o[7]:o[9]],
                          jnp.zeros((d, LANES - n_gate), w.dtype)], axis=1).astype(BF16)
    bias = jnp.concatenate([b_fox_f, b_i, b_f, jnp.zeros((LANES - n_gate,), F32)]).reshape(1, LANES)
    return wp, bias


def _pack_odd(w, w_a2):
    d = w.shape[0]
    wp = jnp.concatenate([w, jnp.zeros((d, LANES - GLA_RANK), w.dtype)], axis=1).astype(BF16)
    wa2 = jnp.concatenate([w_a2, jnp.zeros((LANES - GLA_RANK, w_a2.shape[1]), w_a2.dtype)], axis=0).astype(BF16)
    return wp, wa2


def _gate_rows(gc, batch, t_len, t_pad):
    g = gc.reshape(batch, t_len, LANES)[:, :, :16].transpose(0, 2, 1)
    if t_pad > t_len:
        g = jnp.pad(g, ((0, 0), (0, 0), (0, t_pad - t_len)))
    return g


def _trunk(x, fox_k, fox_v, fox_lf, ml_c, ml_n, ml_m, ml_buf, gla_s, params):
    (norm_mix, norm_ffn, norm_final, w_even, bias_even, conv_w, conv_b, g_ml,
     w_odd, w_a2, b_a, g_gla, w_out, w_ff1, w_ff2) = params
    batch, t_len, d = x.shape
    past = fox_k.shape[2]
    n = batch * t_len
    tm = 512 if n % 512 == 0 else 256 if n % 256 == 0 else n
    depth = norm_mix.shape[0]
    xf = x.reshape(n, d)
    ev_states, odd_states = [], []
    kv_stacked = None
    y = None
    for layer in range(depth):
        j = layer // 2
        if layer % 2 == 0:
            assert (t_len % tm == 0) == (past == 0)
            buf8 = jnp.pad(ml_buf[j], ((0, 0), (8 - (ML_CONV - 1), 0), (0, 0)))
            proj = _inproj_even(xf, norm_mix[layer][None], w_even[j], bias_even[j],
                                tm, batch, t_len, j, (depth + 1) // 2, kv_stacked, conv_w[j], conv_b[j][None], buf8)
            conv_tail = None
            if past == 0:
                qt, kf, vf, ka, vt, mqk, mv, mo, gc, bnd, conv_tail = proj
                kv_stacked = (kf, vf)
                assert tm == FOX_Q_TILE
                attn = _fox_attention(qt, ka, vt, bnd, batch, t_len, FOX_Q_TILE, FOX_KEY_CHUNK)
            else:
                q, kf, vf, mqk, mv, mo, gc = proj
                lf_new = gc.reshape(batch, t_len, LANES)[:, :, :FOX_HEADS]
                lf_row = jnp.concatenate([fox_lf[j], lf_new, jnp.zeros((batch, LANES - t_len, FOX_HEADS), F32)], axis=1)
                lf_row = jnp.pad(lf_row.transpose(0, 2, 1), ((0, 0), (0, 16 - FOX_HEADS), (0, 0)))
                cache_t = lambda a: a.transpose(0, 1, 3, 4, 2).reshape(a.shape[0], batch, FOX_WIDTH, past)
                new_t = lambda a: jnp.pad(a.reshape(batch, t_len, FOX_WIDTH).transpose(0, 2, 1).astype(BF16),
                                          ((0, 0), (0, 0), (0, LANES - t_len)))
                attn = _fox_decode(q.reshape(batch, t_len, FOX_WIDTH), cache_t(fox_k), cache_t(fox_v), j,
                                   new_t(kf), new_t(vf), lf_row).reshape(n, FOX_WIDTH)
            l_ml = ML_CHUNK if t_len >= ML_CHUNK else LANES
            gr = _gate_rows(gc, batch, t_len, max(t_len, l_ml))
            c0 = ml_c[j]
            n0 = ml_n[j][:, :, None, :]
            m0 = jnp.broadcast_to(ml_m[j][:, :, None, None], (batch, ML_HEADS, 1, LANES))
            h_ml, c_new, n_new, m_new, buf_new = _mlstm(mqk, mv, mo, gc, gr, c0, n0, m0, buf8,
                                                        conv_w[j], conv_b[j][None], g_ml[j][None], batch, t_len)
            if conv_tail is not None:
                buf_new = conv_tail
            heads = lambda a: None if past == 0 else a.reshape(batch, t_len, FOX_HEADS, FOX_HEAD_DIM)
            ev_states.append((heads(kf), heads(vf),
                              gc[:, :FOX_HEADS].reshape(batch, t_len, FOX_HEADS),
                              c_new, n_new[:, :, 0, :], m_new[:, :, 0, 0], buf_new[:, 8 - (ML_CONV - 1):, :]))
            mixes = [attn, h_ml]
        else:
            q, k, v, gg, la = _inproj_odd(xf, norm_mix[layer][None], w_odd[j], w_a2[j], b_a[j][None], tm)
            s0t = gla_s[j].transpose(0, 1, 3, 2)
            o, s_new = _gla(q, k, v, gg, la, s0t, g_gla[j][None], batch, t_len)
            odd_states.append(s_new.transpose(0, 1, 3, 2))
            mixes = [o]
        last = layer == depth - 1
        out = _post(xf, mixes, w_out, norm_ffn[layer][None], w_ff1, w_ff2,
                    norm_final[None] if last else None, tm, layer)
        if last:
            y = out
        else:
            xf = out
    ev = [jnp.stack([s[i] for s in ev_states]) for i in range(2 if kv_stacked else 0, 7)]
    if kv_stacked:
        ev = [a.reshape(a.shape[0], batch, FOX_HEADS, FOX_HEAD_DIM, t_len).transpose(0, 1, 4, 2, 3)
              for a in kv_stacked] + ev
    return y.reshape(batch, t_len, d), ev, jnp.stack(odd_states)


def kernel(x_prompt, x_sample, cache_fox_k, cache_fox_v, cache_fox_logf, state_mlstm_c, state_mlstm_n, state_mlstm_m, state_mlstm_conv, state_gla_s, norm_mix, norm_ffn, norm_final, w_in_even, b_fox_f, conv_w_ml, conv_b_ml, b_ml_i, b_ml_f, g_ml, w_in_odd, w_gla_a2, b_gla_a, g_gla, w_out, w_ff1, w_ff2):
    n_even, n_odd = w_in_even.shape[0], w_in_odd.shape[0]
    packed_even = [_pack_even(w_in_even[j], b_fox_f[j], b_ml_i[j], b_ml_f[j]) for j in range(n_even)]
    packed_odd = [_pack_odd(w_in_odd[j], w_gla_a2[j]) for j in range(n_odd)]
    params = (norm_mix, norm_ffn, norm_final,
              [p[0] for p in packed_even], [p[1] for p in packed_even], conv_w_ml, conv_b_ml, g_ml,
              [p[0] for p in packed_odd], [p[1] for p in packed_odd], b_gla_a, g_gla,
              w_out.astype(BF16), w_ff1.astype(BF16), w_ff2.astype(BF16))

    bp = x_prompt.shape[0]
    dt = x_prompt.dtype
    zeros = lambda *s: jnp.zeros(s, dt)
    y_p, ev_p, gla_p = _trunk(
        x_prompt,
        zeros(n_even, bp, 0, FOX_HEADS, FOX_HEAD_DIM), zeros(n_even, bp, 0, FOX_HEADS, FOX_HEAD_DIM),
        zeros(n_even, bp, 0, FOX_HEADS),
        zeros(n_even, bp, ML_HEADS, ML_HEAD_DIM, ML_HEAD_DIM), zeros(n_even, bp, ML_HEADS, ML_HEAD_DIM),
        zeros(n_even, bp, ML_HEADS), zeros(n_even, bp, ML_CONV - 1, 2 * ML_WIDTH),
        zeros(n_odd, bp, GLA_HEADS, GLA_DK, GLA_DV), params)
    y_s, ev_s, gla_s = _trunk(x_sample, cache_fox_k, cache_fox_v, cache_fox_logf, state_mlstm_c, state_mlstm_n,
                              state_mlstm_m, state_mlstm_conv, state_gla_s, params)
    return (y_p, y_s, *ev_p, gla_p, *ev_s, gla_s)
```

```python
import functools

import numpy as np
import jax
import jax.numpy as jnp
from jax import lax
from jax.experimental import pallas as pl
from jax.experimental.pallas import tpu as pltpu

F32 = jnp.float32
BF16 = jnp.bfloat16
EPS = 1e-6
NEG = -1e30
LOG2E = 1.4426950408889634

LANES = 128
VMEM_LIMIT = 56 * 1024 * 1024

D_MODEL = 1024
D_FF = 4 * D_MODEL
FOX_HEADS, FOX_HEAD_DIM = 8, 64
FOX_WIDTH = FOX_HEADS * FOX_HEAD_DIM
ML_HEADS, ML_HEAD_DIM = 4, 128
ML_WIDTH = ML_HEADS * ML_HEAD_DIM
ML_CONV = 4
GLA_HEADS, GLA_DK, GLA_DV = 4, 128, 256
GLA_KW = GLA_HEADS * GLA_DK
GLA_VW = GLA_HEADS * GLA_DV
GLA_RANK = 16
GLA_TAU = 16.0

E_Q, E_K, E_V, E_MQK, E_MV, E_MO, E_G, E_END = np.cumsum(
    (0, FOX_WIDTH, FOX_WIDTH, FOX_WIDTH, 2 * ML_WIDTH, ML_WIDTH, ML_WIDTH, LANES)).tolist()
G_FOX, G_MI, G_MF = 0, FOX_HEADS, FOX_HEADS + ML_HEADS
O_Q, O_K, O_V, O_G, O_A, O_END = np.cumsum((0, GLA_KW, GLA_KW, GLA_VW, GLA_VW, LANES)).tolist()

ML_CHUNK = 256
FOX_KEY_CHUNK = 256
FOX_Q_TILE = 512
GLA_CHUNK = 128
GLA_LEVELS = (64, 32, 16, 8, 4, 2, 1)


def _cparams(sem):
    return pltpu.CompilerParams(dimension_semantics=sem, vmem_limit_bytes=VMEM_LIMIT)


def _const_spec(shape):
    nd = len(shape)
    return pl.BlockSpec(shape, lambda *_: (0,) * nd, pipeline_mode=pl.Buffered(1))


def _rms(x, g):
    return x * lax.rsqrt(jnp.mean(x * x, axis=-1, keepdims=True) + EPS) * g


def _sigmoid(x):
    return 1.0 / (1.0 + jnp.exp(-x))


def _log_sigmoid(x):
    return -(jnp.maximum(-x, 0.0) + jnp.log1p(jnp.exp(-jnp.abs(x))))


def _dot(a, b):
    return jnp.dot(a, b, preferred_element_type=F32)


def _dot_nt(a, b):
    return lax.dot_general(a, b, (((1,), (1,)), ((), ())), preferred_element_type=F32)


def _dot_tn(a, b):
    return lax.dot_general(a, b, (((0,), (0,)), ((), ())), preferred_element_type=F32)


def _split3(x):
    hi = x.astype(BF16)
    r1 = x - hi.astype(F32)
    mid = r1.astype(BF16)
    lo = (r1 - mid.astype(F32)).astype(BF16)
    return hi, mid, lo


def _split2(x):
    hi = x.astype(BF16)
    return hi, (x - hi.astype(F32)).astype(BF16)


def _mat_f32(m, parts):
    return functools.reduce(lambda a, b: a + b, [_dot(m, p) for p in parts])


def _f32_mat(parts, m):
    return functools.reduce(lambda a, b: a + b, [_dot(p, m) for p in parts])


def _tri(n, lower):
    r = lax.broadcasted_iota(jnp.int32, (n, n), 0)
    c = lax.broadcasted_iota(jnp.int32, (n, n), 1)
    keep = (c <= r) if lower else (r <= c)
    return jnp.where(keep, 1.0, 0.0).astype(BF16)


def _pad_rows(a, n):
    if a.shape[0] == n:
        return a
    return jnp.concatenate([a, jnp.zeros((n - a.shape[0], a.shape[1]), a.dtype)], axis=0)


def _conv_silu(u, prev, cw_ref, cb_ref):
    n = u.shape[0]
    ext = jnp.concatenate([prev, u], axis=0)
    y = cb_ref[...] + cw_ref[ML_CONV - 1:ML_CONV, :] * u
    for s in range(1, ML_CONV):
        y = y + cw_ref[ML_CONV - 1 - s:ML_CONV - s, :] * pltpu.roll(ext, s, axis=0)[8:8 + n]
    lane = lax.broadcasted_iota(jnp.int32, (1, 2 * ML_WIDTH), 1)
    return y * _sigmoid(y) * jnp.where(lane >= ML_WIDTH, ML_HEAD_DIM ** -0.5, 1.0), ext


FOX_Q_SCALE = FOX_HEAD_DIM ** -0.5 * LOG2E


def _even_segments(x_ref, g_ref, w_ref):
    h = _rms(x_ref[...], g_ref[...]).astype(BF16)
    return lambda a, b: _dot(h, w_ref[:, a:b])


def _even_gates(seg, bias_ref):
    gz = seg(E_G, E_END) + bias_ref[...]
    lane = lax.broadcasted_iota(jnp.int32, gz.shape, 1)
    is_log = (lane < G_MI) | (lane >= G_MF)
    return jnp.where(is_log, _log_sigmoid(gz), gz)


def _inproj_even_rows_kernel(x_ref, g_ref, w_ref, bias_ref,
                             q_ref, kf_ref, vf_ref, mqk_ref, mv_ref, mo_ref, gc_ref):
    seg = _even_segments(x_ref, g_ref, w_ref)
    q_ref[...] = (seg(E_Q, E_K) * FOX_Q_SCALE).astype(BF16)
    kf_ref[...] = seg(E_K, E_V)
    vf_ref[...] = seg(E_V, E_MQK)
    mqk_ref[...] = seg(E_MQK, E_MV)
    mv_ref[...] = seg(E_MV, E_MO).astype(BF16)
    mo_ref[...] = seg(E_MO, E_G).astype(BF16)
    gc_ref[...] = _even_gates(seg, bias_ref)


def _inproj_even_seq_kernel(*refs, tpb, aliased):
    x_ref, g_ref, w_ref, bias_ref, ind_ref, cw_ref, cb_ref, buf_ref = refs[:8]
    refs = refs[8 + (2 if aliased else 0):]
    (qt_ref, kf_ref, vf_ref, ka_ref, vt_ref, mqk_ref, mv_ref, mo_ref, gc_ref, bnd_ref, tail_ref,
     carry, conv_prev) = refs
    @pl.when(pl.program_id(0) % tpb == 0)
    def _():
        carry[...] = jnp.zeros_like(carry)
        conv_prev[...] = buf_ref[0]

    seg = _even_segments(x_ref, g_ref, w_ref)
    mqk = seg(E_MQK, E_MV)
    gates = _even_gates(seg, bias_ref)
    gc_ref[...] = gates
    q = seg(E_Q, E_K) * FOX_Q_SCALE
    k = seg(E_K, E_V)
    v = seg(E_V, E_MQK)
    qt_ref[...] = q.T.astype(BF16)
    mv = seg(E_MV, E_MO)
    kf_ref[0, 0] = k.T
    f = _append_f_terms(gates, carry, k.astype(BF16), ka_ref)
    mo = seg(E_MO, E_G)
    act, _ = _conv_silu(mqk, conv_prev[...], cw_ref, cb_ref)
    mqk_ref[...] = act.astype(BF16)
    last_rows = mqk[mqk.shape[0] - 8:]
    conv_prev[...] = last_rows
    tail_ref[0] = last_rows
    mv_ref[...] = mv.astype(BF16)
    v_t = v.T
    vt_ref[...] = v_t.astype(BF16)
    vf_ref[0, 0] = v_t
    mo_ref[...] = mo.astype(BF16)
    norms = _dot((q * q).astype(BF16), ind_ref[0]) + _dot((k * k).astype(BF16), ind_ref[1])
    bnd_ref[...] = jnp.zeros_like(bnd_ref)
    bnd_ref[0, 0:1, :] = f[0:1, :]
    for c in range(f.shape[0] // FOX_KEY_CHUNK):
        bnd_ref[0, 1 + c:2 + c, :] = f[(c + 1) * FOX_KEY_CHUNK - 1:(c + 1) * FOX_KEY_CHUNK, :]
    bnd_ref[0, 7:8, :] = jnp.max(norms, axis=0, keepdims=True)


def _inproj_even(x, g, w, bias, tm, batch, t_len, slot, n_slots, kv_prev, cw, cb, buf8):
    n = x.shape[0]
    row = lambda c: (pl.BlockSpec((tm, c), lambda i: (i, 0)), (n, c))
    col = (pl.BlockSpec((FOX_WIDTH, tm), lambda i: (0, i)), (FOX_WIDTH, n))
    common_in = [row(D_MODEL)[0], _const_spec((1, D_MODEL)), _const_spec((D_MODEL, E_END)), _const_spec((1, LANES))]
    tail = [(row(ML_WIDTH), BF16), (row(ML_WIDTH), BF16), (row(LANES), F32)]
    if t_len % tm != 0:
        outs = [(row(FOX_WIDTH), BF16), (row(FOX_WIDTH), F32), (row(FOX_WIDTH), F32), (row(2 * ML_WIDTH), F32)] + tail
        return pl.pallas_call(
            _inproj_even_rows_kernel,
            grid=(n // tm,),
            in_specs=common_in,
            out_specs=[spec for (spec, _), _ in outs],
            out_shape=[jax.ShapeDtypeStruct(shape, dt) for (_, shape), dt in outs],
            compiler_params=_cparams(("parallel",)),
            name="inproj_even_rows",
        )(x, g, w, bias)
    tpb = t_len // tm
    state = (pl.BlockSpec((1, 1, FOX_WIDTH, tm), lambda i: (slot, i // tpb, 0, i % tpb)),
             (n_slots, batch, FOX_WIDTH, t_len))
    per_seq = (pl.BlockSpec((1, 8, 2 * ML_WIDTH), lambda i: (i // tpb, 0, 0)), (batch, 8, 2 * ML_WIDTH))
    outs = ([(col, BF16), (state, F32), (state, F32), (row(2 * FOX_WIDTH), BF16), (col, BF16), (row(2 * ML_WIDTH), BF16)]
            + tail + [((pl.BlockSpec((1, 8, LANES), lambda i: (i, 0, 0)), (n // tm, 8, LANES)), F32), (per_seq, F32)])
    assert tm % FOX_KEY_CHUNK == 0 and tm // FOX_KEY_CHUNK <= 6
    ind = np.zeros((2, FOX_WIDTH, LANES), np.float32)
    for c in range(FOX_WIDTH):
        ind[0, c, FOX_HEADS + c // FOX_HEAD_DIM] = 1.0
        ind[1, c, 2 * FOX_HEADS + c // FOX_HEAD_DIM] = 1.0
    ins = [x, g, w, bias, jnp.asarray(ind, BF16), cw, cb, buf8]
    in_specs = common_in + [_const_spec(ind.shape), _const_spec(cw.shape), _const_spec(cb.shape), per_seq[0]]
    aliases = {}
    if kv_prev is not None:
        aliases = {len(ins): 1, len(ins) + 1: 2}
        ins += list(kv_prev)
        in_specs += [pl.BlockSpec(memory_space=pl.ANY)] * 2
    return pl.pallas_call(
        functools.partial(_inproj_even_seq_kernel, tpb=tpb, aliased=kv_prev is not None),
        grid=(n // tm,),
        in_specs=in_specs,
        out_specs=[spec for (spec, _), _ in outs],
        out_shape=[jax.ShapeDtypeStruct(shape, dt) for (_, shape), dt in outs],
        scratch_shapes=[pltpu.VMEM((1, LANES), F32), pltpu.VMEM((8, 2 * ML_WIDTH), F32)],
        input_output_aliases=aliases,
        compiler_params=_cparams(("arbitrary",)),
        name="inproj_even_seq",
    )(*ins)


def _inproj_odd_kernel(x_ref, g_ref, w_ref, wa2_ref, ba_ref, q_ref, k_ref, v_ref, gg_ref, la_ref):
    h = _rms(x_ref[...], g_ref[...]).astype(BF16)

    def seg(a, b):
        return _dot(h, w_ref[:, a:b])

    q_ref[...] = (seg(O_Q, O_K) * (GLA_DK ** -0.5)).astype(BF16)
    k_ref[...] = seg(O_K, O_V).astype(BF16)
    v_ref[...] = seg(O_V, O_G).astype(BF16)
    gg_ref[...] = seg(O_G, O_A).astype(BF16)
    ga = seg(O_A, O_END).astype(BF16)
    la_ref[...] = _log_sigmoid(_dot(ga, wa2_ref[...]) + ba_ref[...]) * (1.0 / GLA_TAU)


def _inproj_odd(x, g, w, wa2, ba, tm):
    n = x.shape[0]
    row = lambda c: pl.BlockSpec((tm, c), lambda i: (i, 0))
    outs = [(GLA_KW, BF16), (GLA_KW, BF16), (GLA_VW, BF16), (GLA_VW, BF16), (GLA_KW, F32)]
    return pl.pallas_call(
        _inproj_odd_kernel,
        grid=(n // tm,),
        in_specs=[row(D_MODEL), _const_spec((1, D_MODEL)), _const_spec((D_MODEL, O_END)),
                  _const_spec((LANES, GLA_KW)), _const_spec((1, GLA_KW))],
        out_specs=[row(c) for c, _ in outs],
        out_shape=[jax.ShapeDtypeStruct((n, c), dt) for c, dt in outs],
        compiler_params=_cparams(("parallel",)),
        name="inproj_odd",
    )(x, g, w, wa2, ba)


def _post_kernel(*refs, n_mix, final):
    x_ref = refs[0]
    mix_refs = refs[1:1 + n_mix]
    wo_ref, gf_ref, w1_ref, w2_ref = refs[1 + n_mix:5 + n_mix]
    rest = refs[5 + n_mix:]
    mix = mix_refs[0][...] if n_mix == 1 else jnp.concatenate([r[...] for r in mix_refs], axis=1)
    x1 = x_ref[...] + _dot(mix, wo_ref[0])
    h = _rms(x1, gf_ref[...]).astype(BF16)
    y = x1
    for c in range(D_FF // D_MODEL):
        sl = slice(c * D_MODEL, (c + 1) * D_MODEL)
        t = jnp.maximum(_dot(h, w1_ref[0, :, sl]), 0.0)
        y = y + _dot((t * t).astype(BF16), w2_ref[0, sl, :])
    if final:
        gfin_ref, out_ref = rest
        out_ref[...] = _rms(y, gfin_ref[...])
    else:
        (out_ref,) = rest
        out_ref[...] = y


def _post(x, mixes, wo, gf, w1, w2, gfin, tm, layer):
    n = x.shape[0]
    row = lambda c: pl.BlockSpec((tm, c), lambda i: (i, 0))
    of_layer = lambda a: pl.BlockSpec((1,) + a.shape[1:], lambda i: (layer, 0, 0), pipeline_mode=pl.Buffered(1))
    final = gfin is not None
    ins = [x, *mixes, wo, gf, w1, w2]
    specs = [row(D_MODEL)] + [row(m.shape[1]) for m in mixes] + [
        of_layer(wo), _const_spec((1, D_MODEL)), of_layer(w1), of_layer(w2)]
    if final:
        ins.append(gfin)
        specs.append(_const_spec((1, D_MODEL)))
    return pl.pallas_call(
        functools.partial(_post_kernel, n_mix=len(mixes), final=final),
        grid=(n // tm,),
        in_specs=specs,
        out_specs=row(D_MODEL),
        out_shape=jax.ShapeDtypeStruct((n, D_MODEL), F32),
        compiler_params=_cparams(("parallel",)),
        name="post_final" if final else "post",
    )(*ins)


F_TERMS = 3


def _append_f_terms(lf, carry, k, ka_ref):
    tc = lf.shape[0]
    f = _mat_f32(_tri(tc, True), _split3(lf)) + carry[...]
    carry[...] = f[tc - 1:tc, :]
    hi, mid, lo = [p.astype(F32) for p in _split3(f * LOG2E)]
    lane = lax.broadcasted_iota(jnp.int32, (1, LANES), 1)
    cols = jnp.where(lane < FOX_HEADS, hi,
                     jnp.where(lane < 2 * FOX_HEADS, pltpu.roll(mid, FOX_HEADS, axis=1),
                               jnp.where(lane < 3 * FOX_HEADS, pltpu.roll(lo, 2 * FOX_HEADS, axis=1), 0.0)))
    cols = cols.astype(BF16)
    for hp in range(FOX_HEADS // 2):
        ka_ref[:, 2 * hp * LANES:(2 * hp + 1) * LANES] = k[:, hp * LANES:(hp + 1) * LANES]
        ka_ref[:, (2 * hp + 1) * LANES:(2 * hp + 2) * LANES] = cols
    return f


def _fox_decode_kernel(q_ref, kc_ref, vc_ref, kn_ref, vn_ref, lf_ref, o_ref, *, t_new, past):
    nkeys = past + LANES
    lane_f = lax.broadcasted_iota(jnp.int32, (1, FOX_WIDTH), 1)
    head_mask = [jnp.where((lane_f >= h * FOX_HEAD_DIM) & (lane_f < (h + 1) * FOX_HEAD_DIM), 1.0, 0.0)
                 for h in range(FOX_HEADS)]
    q = q_ref[0].astype(F32)
    qb = jnp.concatenate([q * hm for hm in head_mask], axis=0).astype(BF16)

    def keys(cache_ref, new_ref):
        return jnp.concatenate([cache_ref[0, 0].astype(BF16), new_ref[0]], axis=1)

    s = _dot(qb, keys(kc_ref, kn_ref))
    lf = lf_ref[0]
    triu = _tri(LANES, False)
    carry = jnp.zeros((lf.shape[0], 1), F32)
    blocks = []
    for c in range(nkeys // LANES):
        cs = _f32_mat(_split3(lf[:, c * LANES:(c + 1) * LANES]), triu) + carry
        carry = cs[:, LANES - 1:LANES]
        blocks.append(cs)
    f_all = jnp.concatenate(blocks, axis=1) * LOG2E
    s = s - jnp.concatenate([jnp.broadcast_to(f_all[h:h + 1, :], (t_new, nkeys)) for h in range(FOX_HEADS)], axis=0)
    kpos = lax.broadcasted_iota(jnp.int32, (1, nkeys), 1)
    qpos = past + (lax.broadcasted_iota(jnp.int32, (FOX_HEADS * t_new, 1), 0) & (t_new - 1))
    s = jnp.where(kpos <= qpos, s, NEG)
    p = jnp.exp2(s - jnp.max(s, axis=-1, keepdims=True))
    ob = _dot_nt(p.astype(BF16), keys(vc_ref, vn_ref)) / jnp.sum(p, axis=-1, keepdims=True)
    out = ob[0:t_new] * head_mask[0]
    for h in range(1, FOX_HEADS):
        out = out + ob[h * t_new:(h + 1) * t_new] * head_mask[h]
    o_ref[0] = out.astype(o_ref.dtype)


def _fox_decode(q, k_cache, v_cache, layer, k_new, v_new, lf_row):
    batch, t_new, _ = q.shape
    past = k_cache.shape[3]
    assert t_new & (t_new - 1) == 0 and t_new <= LANES and t_new % 16 == 0 and past % LANES == 0
    blk = lambda a: pl.BlockSpec((1,) + a.shape[1:], lambda b: (b, 0, 0))
    cache = pl.BlockSpec((1, 1, FOX_WIDTH, past), lambda b: (layer, b, 0, 0))
    args = (q, k_cache, v_cache, k_new, v_new, lf_row)
    return pl.pallas_call(
        functools.partial(_fox_decode_kernel, t_new=t_new, past=past),
        grid=(batch,),
        in_specs=[blk(q), cache, cache, blk(k_new), blk(v_new), blk(lf_row)],
        out_specs=blk(q),
        out_shape=jax.ShapeDtypeStruct(q.shape, BF16),
        compiler_params=_cparams(("parallel",)),
        name="fox_decode",
    )(*args)


ONES_ROWS = 16


def _fox_attn_kernel(first_ref, qt_ref, ka_ref, vt_ref, o_ref, s_scr, acc_scr, *, tq, tkc, past, n_diag):
    q_first = past + pl.program_id(2) * tq
    n_full = q_first // tkc
    drow = lax.broadcasted_iota(jnp.int32, (LANES, 1), 0)
    lane = lax.broadcasted_iota(jnp.int32, (1, 2 * tq), 1)
    lane_head = jnp.where(lane < tq, 0, 1)
    own_head = jnp.where(jnp.where(drow < FOX_HEAD_DIM, 0, 1) == lane_head, 1.0, 0.0).astype(BF16)
    qt = qt_ref[...]
    q2 = jnp.concatenate([qt, qt], axis=1) * own_head
    head = 2 * pl.program_id(1) + lane_head
    f_sel = ((drow & (FOX_HEADS - 1)) == head) & (drow < F_TERMS * FOX_HEADS)
    qa = jnp.concatenate([q2, jnp.where(f_sel, -1.0, 0.0).astype(BF16)], axis=0)
    ones = jnp.ones((ONES_ROWS, tkc), BF16)
    acc_scr[...] = jnp.zeros_like(acc_scr)

    qa_h = [qa[:, hh * tq:(hh + 1) * tq] for hh in range(2)]
    qpos = q_first + lax.broadcasted_iota(jnp.int32, (1, tq), 1)

    def produce(j, hh, buf, lo=0):
        start = pl.multiple_of(j * tkc, tkc)
        s_scr[buf, hh, :, lo:] = _dot(ka_ref[pl.ds(start, tkc), :], qa_h[hh][:, lo:])

    def consume(j, hh, buf, m_all, masked, lo=0):
        start = pl.multiple_of(j * tkc, tkc)
        st = s_scr[buf, hh, :, lo:]
        m_prev = m_all[:, lo:]
        if masked:
            kpos = start + lax.broadcasted_iota(jnp.int32, (tkc, 1), 0)
            st = jnp.where(kpos <= qpos[:, lo:], st, NEG)
        m_new = jnp.maximum(m_prev, jnp.max(st, axis=0, keepdims=True))
        alpha = jnp.exp2(m_prev - m_new)
        p = jnp.exp2(st - m_new).astype(BF16)
        rows = slice(hh * FOX_HEAD_DIM, (hh + 1) * FOX_HEAD_DIM)
        va = jnp.concatenate([vt_ref[rows, pl.ds(start, tkc)], ones], axis=0)
        acc_scr[hh, :, lo:] = alpha * acc_scr[hh, :, lo:] + _dot(va, p)
        return m_new if lo == 0 else jnp.concatenate([m_all[:, :lo], m_new], axis=1)

    nq = pl.num_programs(2)
    slot = ((pl.program_id(0) * pl.num_programs(1) + pl.program_id(1)) * nq + pl.program_id(2)) * 2
    first = [first_ref[slot], first_ref[slot + 1]]
    joint = jnp.maximum(first[0], first[1])
    for hh in range(2):
        produce(first[hh], hh, 0)

    def pair_of_chunks(i, ms, heads):
        ms = list(ms)
        for step in range(2):
            for hh in heads:
                produce(2 * i + step + 1, hh, 1 - step)
                ms[hh] = consume(2 * i + step, hh, step, ms[hh], False)
        return tuple(ms)

    m_init = jnp.full((1, tq), NEG, F32)
    ms = (m_init, m_init)
    for hh in range(2):
        ms = lax.fori_loop(first[hh] // 2, joint // 2, functools.partial(pair_of_chunks, heads=(hh,)), ms)
    ms = list(lax.fori_loop(joint // 2, n_full // 2, functools.partial(pair_of_chunks, heads=(0, 1)), ms))
    for d in range(n_diag):
        for hh in range(2):
            if d + 1 < n_diag:
                produce(n_full + d + 1, hh, (d + 1) % 2, lo=min((d + 1) * tkc, tq - LANES))
            ms[hh] = consume(n_full + d, hh, d % 2, ms[hh], True, lo=min(d * tkc, tq - LANES))
    out = jnp.concatenate([acc_scr[hh, 0:FOX_HEAD_DIM] / acc_scr[hh, FOX_HEAD_DIM:FOX_HEAD_DIM + 1]
                           for hh in range(2)], axis=0)
    o_ref[...] = out.T.astype(o_ref.dtype)


SKIP_MARGIN = 40.0
NORM_SLACK = 1.02


def _fox_first_chunks(bnd, batch, t_len, tq, tkc):
    nq, nc, pairs = t_len // tq, t_len // tkc, FOX_HEADS // 2
    per_tile = tq // tkc
    b4 = bnd.reshape(batch, nq, 8, LANES)
    f_start = b4[:, :, 0, 0:FOX_HEADS] * LOG2E
    f_end = b4[:, :, 1:1 + per_tile, 0:FOX_HEADS].reshape(batch, nc, FOX_HEADS) * LOG2E
    qn = jnp.sqrt(b4[:, :, 7, FOX_HEADS:2 * FOX_HEADS])
    kn = jnp.sqrt(jnp.max(b4[:, :, 7, 2 * FOX_HEADS:3 * FOX_HEADS], axis=1))
    thr = 2.0 * NORM_SLACK * qn * kn[:, None, :] + SKIP_MARGIN
    decay = f_end[:, None, :, :] - f_start[:, :, None, :]
    before_tile = jnp.arange(nc)[None, :] < ((jnp.arange(nq) * tq) // tkc)[:, None]
    ok = (decay > thr[:, :, None, :]) & before_tile[None, :, :, None]
    lead = jnp.sum(jnp.cumprod(ok.astype(jnp.int32), axis=2), axis=2)
    lead = ((lead // 2) * 2).reshape(batch, nq, pairs, 2)
    return lead.transpose(0, 2, 1, 3).reshape(-1).astype(jnp.int32)


def _fox_attention(qt, ka, vt, bnd, batch, t_len, tq, tkc):
    past = 0
    nq = t_len // tq
    n_diag = max(1, tq // tkc)
    assert tq & (tq - 1) == 0 and tq % LANES == 0 and (tkc % tq == 0 or tq % tkc == 0) and t_len % tq == 0
    assert nq == 1 or tq % (2 * tkc) == 0
    assert t_len >= n_diag * tkc and t_len % tkc == 0
    pairs = FOX_HEADS // 2
    first = _fox_first_chunks(bnd, batch, t_len, tq, tkc)
    grid_spec = pltpu.PrefetchScalarGridSpec(
        num_scalar_prefetch=1,
        grid=(batch, pairs, nq),
        in_specs=[pl.BlockSpec((LANES, tq), lambda b, h, i, f: (h, b * nq + i)),
                  pl.BlockSpec((t_len, 2 * LANES), lambda b, h, i, f: (b, h)),
                  pl.BlockSpec((LANES, t_len), lambda b, h, i, f: (h, b))],
        out_specs=pl.BlockSpec((tq, LANES), lambda b, h, i, f: (b * nq + i, h)),
        scratch_shapes=[pltpu.VMEM((2, 2, tkc, tq), F32),
                        pltpu.VMEM((2, FOX_HEAD_DIM + ONES_ROWS, tq), F32)])
    return pl.pallas_call(
        functools.partial(_fox_attn_kernel, tq=tq, tkc=tkc, past=past, n_diag=n_diag),
        grid_spec=grid_spec,
        out_shape=jax.ShapeDtypeStruct((batch * t_len, FOX_WIDTH), BF16),
        compiler_params=_cparams(("parallel", "parallel", "arbitrary")),
        name="fox_attention",
    )(first, qt, ka, vt)


def _mlstm_kernel(mqk_ref, mv_ref, mo_ref, gc_ref, gr_ref, c0_ref, n0_ref, m0_ref, buf_ref,
                  cw_ref, cb_ref, gml_ref,
                  h_ref, c_out, n_out, m_out, buf_out,
                  c_scr, n_scr, m_scr, prev_scr, *, L, t_last, nc, preact):
    ci = pl.program_id(1)
    rows = mqk_ref.shape[0]

    @pl.when(ci == 0)
    def _():
        c_scr[...] = c0_ref[0]
        n_scr[...] = n0_ref[0]
        m_scr[...] = m0_ref[0]
        prev_scr[...] = buf_ref[0]

    if preact:
        qk = _pad_rows(mqk_ref[...], L).astype(F32)
        ext = None
    else:
        u = _pad_rows(mqk_ref[...], L)
        qk, ext = _conv_silu(u, prev_scr[...], cw_ref, cb_ref)
        prev_scr[...] = u[L - 8:L]

    gc = _pad_rows(gc_ref[...], L)
    gr = gr_ref[0]
    b_col = _mat_f32(_tri(L, True), _split3(gc))
    b_row = _f32_mat(_split3(gr), _tri(L, False))
    v_all = _pad_rows(mv_ref[...], L)
    o_all = _pad_rows(mo_ref[...], L)
    t_idx = lax.broadcasted_iota(jnp.int32, (L, 1), 0)
    causal = lax.broadcasted_iota(jnp.int32, (1, L), 1) <= t_idx

    heads = []
    for h in range(ML_HEADS):
        hs = slice(h * ML_HEAD_DIM, (h + 1) * ML_HEAD_DIM)
        q32 = qk[:, hs]
        k32 = qk[:, ML_WIDTH + h * ML_HEAD_DIM:ML_WIDTH + (h + 1) * ML_HEAD_DIM]
        qb, kb = q32.astype(BF16), k32.astype(BF16)
        c_prev = c_scr[h]
        heads.append(dict(hs=hs, q32=q32, k32=k32, qb=qb, kb=kb, vb=v_all[:, hs], c_prev=c_prev,
                          s=_dot_nt(qb, kb), qc=_dot_nt(qb, c_prev.astype(BF16))))

    for h, hd in enumerate(heads):
        bc = b_col[:, G_MF + h:G_MF + h + 1]
        ic = gc[:, G_MI + h:G_MI + h + 1]
        br = b_row[G_MF + h:G_MF + h + 1, :]
        ir = gr[G_MI + h:G_MI + h + 1, :]
        m_prev = m_scr[h][:, 0:1]
        dmat = jnp.where(causal, bc + (ir - br), NEG)
        inter = bc + m_prev
        m_t = jnp.maximum(inter, jnp.max(dmat, axis=-1, keepdims=True))
        w = jnp.exp(dmat - m_t)
        g = jnp.exp(inter - m_t)
        a = w * hd["s"]
        b_last = bc[t_last:t_last + 1, :]
        m_last = m_t[t_last:t_last + 1, :]
        w_end = jnp.where(t_idx <= t_last, jnp.exp(b_last - bc + ic - m_last), 0.0)
        hd.update(m_t=m_t, g=g, a=a, m_last=m_last, g_end=g[t_last:t_last + 1, :], w_end=w_end,
                  av=_dot(a.astype(BF16), hd["vb"]),
                  vk=_dot_tn((hd["vb"].astype(F32) * w_end).astype(BF16), hd["kb"]))

    for h, hd in enumerate(heads):
        hs, g, a, m_t = hd["hs"], hd["g"], hd["a"], hd["m_t"]
        n_prev = n_scr[h]
        num = g * hd["qc"] + hd["av"]
        den = g * jnp.sum(hd["q32"] * n_prev, axis=-1, keepdims=True) + jnp.sum(a, axis=-1, keepdims=True)
        hh = num / jnp.maximum(jnp.abs(den), jnp.exp(-m_t))
        c_scr[h] = hd["g_end"] * hd["c_prev"] + hd["vk"]
        n_scr[h] = hd["g_end"] * n_prev + jnp.sum(hd["k32"] * hd["w_end"], axis=0, keepdims=True)
        m_scr[h] = jnp.broadcast_to(hd["m_last"], (1, LANES))
        yh = hh * lax.rsqrt(jnp.mean(hh * hh, axis=-1, keepdims=True) + EPS) * gml_ref[:, hs]
        yh = yh * _sigmoid(o_all[:, hs].astype(F32))
        h_ref[:, hs] = yh[:rows].astype(h_ref.dtype)

    @pl.when(ci == nc - 1)
    def _():
        c_out[0] = c_scr[...]
        n_out[0] = n_scr[...]
        m_out[0] = m_scr[...]
        buf_out[0] = buf_ref[0] if preact else ext[t_last + 1:t_last + 9]


def _mlstm(mqk, mv, mo, gc, gr, c0, n0, m0, buf8, cw, cb, gml, batch, t_len):
    preact = mqk.dtype == BF16
    L = ML_CHUNK if t_len >= ML_CHUNK else LANES
    rows = min(L, t_len)
    nc = t_len // rows
    t_last = rows - 1
    assert t_len % rows == 0 and (nc == 1 or rows == L) and (t_last + 1) % 8 == 0
    tok = lambda c: pl.BlockSpec((rows, c), lambda b, i: (b * nc + i, 0))
    st = lambda *s: pl.BlockSpec((1,) + s, lambda b, i: (b,) + (0,) * len(s))
    shp_c = (ML_HEADS, ML_HEAD_DIM, ML_HEAD_DIM)
    shp_n = (ML_HEADS, 1, ML_HEAD_DIM)
    return pl.pallas_call(
        functools.partial(_mlstm_kernel, L=L, t_last=t_last, nc=nc, preact=preact),
        grid=(batch, nc),
        in_specs=[tok(2 * ML_WIDTH), tok(ML_WIDTH), tok(ML_WIDTH), tok(LANES),
                  pl.BlockSpec((1, 16, L), lambda b, i: (b, 0, i)),
                  st(*shp_c), st(*shp_n), st(*shp_n), st(8, 2 * ML_WIDTH),
                  _const_spec((ML_CONV, 2 * ML_WIDTH)), _const_spec((1, 2 * ML_WIDTH)), _const_spec((1, ML_WIDTH))],
        out_specs=[tok(ML_WIDTH), st(*shp_c), st(*shp_n), st(*shp_n), st(8, 2 * ML_WIDTH)],
        out_shape=[jax.ShapeDtypeStruct((batch * t_len, ML_WIDTH), BF16),
                   jax.ShapeDtypeStruct((batch,) + shp_c, F32),
                   jax.ShapeDtypeStruct((batch,) + shp_n, F32),
                   jax.ShapeDtypeStruct((batch,) + shp_n, F32),
                   jax.ShapeDtypeStruct((batch, 8, 2 * ML_WIDTH), F32)],
        scratch_shapes=[pltpu.VMEM(shp_c, F32), pltpu.VMEM(shp_n, F32), pltpu.VMEM(shp_n, F32),
                        pltpu.VMEM((8, 2 * ML_WIDTH), F32)],
        compiler_params=_cparams(("parallel", "arbitrary")),
        name="mlstm",
    )(mqk, mv, mo, gc, gr, c0, n0, m0, buf8, cw, cb, gml)


def _gla_level_matrices(L):
    mats = np.zeros((len(GLA_LEVELS), L, L), np.float32)
    for li, b in enumerate(GLA_LEVELS):
        for t in range(L):
            base = (t // (2 * b)) * 2 * b
            bound = base + b - 1
            if t > bound:
                mats[li, t, bound + 1:t + 1] = 1.0
            else:
                mats[li, t, t + 1:bound + 1] = 1.0
    return jnp.asarray(mats, BF16)


def _gla_kernel(q_ref, k_ref, v_ref, gg_ref, la_ref, s0_ref, lvl_ref, gn_ref, o_ref, s_out, s_scr, *, L, t_last, nc):
    ci = pl.program_id(1)
    nb, rows = q_ref.shape[0], q_ref.shape[1]

    @pl.when(ci == 0)
    def _():
        for bi in range(nb):
            for h in range(GLA_HEADS):
                s_scr[bi, h] = s0_ref[0, bi, h].T

    t_idx = lax.broadcasted_iota(jnp.int32, (L, 1), 0)
    s_idx = lax.broadcasted_iota(jnp.int32, (1, L), 1)
    tril = _tri(L, True)
    level_mask = []
    for b in GLA_LEVELS:
        sh = b.bit_length() - 1
        level_mask.append((jnp.right_shift(t_idx, sh + 1) == jnp.right_shift(s_idx, sh + 1))
                          & ((jnp.right_shift(t_idx, sh) & 1) == 1) & ((jnp.right_shift(s_idx, sh) & 1) == 0))

    els = []
    for bi in range(nb):
        la3 = _split2(_pad_rows(la_ref[bi], L))
        q_all = _pad_rows(q_ref[bi], L)
        k_all = _pad_rows(k_ref[bi], L)
        el = dict(la3=la3, cb=_mat_f32(tril, la3), v=_pad_rows(v_ref[bi], L), g=_pad_rows(gg_ref[bi], L),
                  q32=[], k32=[], a=[], e_next=jnp.exp(_mat_f32(lvl_ref[0], la3)), pending=None)
        for h in range(GLA_HEADS):
            ks = slice(h * GLA_DK, (h + 1) * GLA_DK)
            el["q32"].append(q_all[:, ks].astype(F32))
            el["k32"].append(k_all[:, ks].astype(F32))
            el["a"].append(jnp.where(t_idx == s_idx, _dot_nt(q_all[:, ks], k_all[:, ks]), 0.0))
        els.append(el)

    def settle(el):
        if el["pending"] is not None:
            prods, mask = el["pending"]
            el["a"] = [jnp.where(mask, prods[h], el["a"][h]) for h in range(GLA_HEADS)]

    for li in range(len(GLA_LEVELS)):
        for el in els:
            e_all = el["e_next"]
            if li + 1 < len(GLA_LEVELS):
                el["e_next"] = jnp.exp(_mat_f32(lvl_ref[li + 1], el["la3"]))
            prods = []
            for h in range(GLA_HEADS):
                e = e_all[:, h * GLA_DK:(h + 1) * GLA_DK]
                prods.append(_dot_nt((el["q32"][h] * e).astype(BF16), (el["k32"][h] * e).astype(BF16)))
            settle(el)
            el["pending"] = (prods, level_mask[li])
    for el in els:
        settle(el)

    for bi, el in enumerate(els):
        el["o"] = []
        for h in range(GLA_HEADS):
            ks = slice(h * GLA_DK, (h + 1) * GLA_DK)
            vs = slice(h * GLA_DV, (h + 1) * GLA_DV)
            cbh = el["cb"][:, ks]
            vb = el["v"][:, vs]
            s_t = s_scr[bi, h]
            o = (_dot_nt((el["q32"][h] * jnp.exp(cbh)).astype(BF16), s_t.astype(BF16))
                 + _dot(el["a"][h].astype(BF16), vb))
            cl = cbh[t_last:t_last + 1, :]
            kd = jnp.where(t_idx <= t_last, el["k32"][h] * jnp.exp(cl - cbh), 0.0).astype(BF16)
            s_scr[bi, h] = jnp.exp(cl) * s_t + _dot_tn(vb, kd)
            el["o"].append(o)
    for bi, el in enumerate(els):
        for h in range(GLA_HEADS):
            vs = slice(h * GLA_DV, (h + 1) * GLA_DV)
            o = el["o"][h]
            y = o * lax.rsqrt(jnp.mean(o * o, axis=-1, keepdims=True) + EPS) * gn_ref[:, vs]
            gate = el["g"][:, vs].astype(F32)
            y = y * (gate * _sigmoid(gate))
            o_ref[bi, :, vs] = y[:rows].astype(o_ref.dtype)

    @pl.when(ci == nc - 1)
    def _():
        for bi in range(nb):
            for h in range(GLA_HEADS):
                s_out[bi, h] = s_scr[bi, h].T


GLA_BATCH_PER_STEP = 4


def _gla(q, k, v, gg, la, s0_all, layer, gn, batch, t_len):
    L = GLA_CHUNK
    rows = min(L, t_len)
    nc = t_len // rows
    t_last = rows - 1
    nb = max(d for d in range(1, GLA_BATCH_PER_STEP + 1) if batch % d == 0)
    assert t_len % rows == 0 and (nc == 1 or rows == L)
    tok = lambda c: pl.BlockSpec((nb, rows, c), lambda b, i: (b, i, 0))
    shp_s = (GLA_HEADS, GLA_DK, GLA_DV)
    levels = _gla_level_matrices(L)
    seq = lambda a: a.reshape(batch, t_len, a.shape[-1])
    o, s_new = pl.pallas_call(
        functools.partial(_gla_kernel, L=L, t_last=t_last, nc=nc),
        grid=(batch // nb, nc),
        in_specs=[tok(GLA_KW), tok(GLA_KW), tok(GLA_VW), tok(GLA_VW), tok(GLA_KW),
                  pl.BlockSpec((1, nb) + shp_s, lambda b, i: (layer, b, 0, 0, 0)),
                  _const_spec(levels.shape), _const_spec((1, GLA_VW))],
        out_specs=[tok(GLA_VW), pl.BlockSpec((nb,) + shp_s, lambda b, i: (b, 0, 0, 0))],
        out_shape=[jax.ShapeDtypeStruct((batch, t_len, GLA_VW), BF16),
                   jax.ShapeDtypeStruct((batch,) + shp_s, F32)],
        scratch_shapes=[pltpu.VMEM((nb, GLA_HEADS, GLA_DV, GLA_DK), F32)],
        compiler_params=_cparams(("parallel", "arbitrary")),
        name="gla",
    )(seq(q), seq(k), seq(v), seq(gg), seq(la), s0_all, levels, gn)
    return o.reshape(batch * t_len, GLA_VW), s_new


def _pack_even(w, b_fox_f, b_i, b_f):
    d = w.shape[0]
    o = np.cumsum((0, FOX_WIDTH, FOX_WIDTH, FOX_WIDTH, FOX_HEADS, 2 * ML_WIDTH, ML_WIDTH, ML_WIDTH, ML_HEADS, ML_HEADS))
    n_gate = FOX_HEADS + 2 * ML_HEADS
    wp = jnp.concatenate([w[:, o[0]:o[3]], w[:, o[4]:o[7]], w[:, o[3]:o[4]], w[:, o[7]:o[9]],
                          jnp.zeros((d, LANES - n_gate), w.dtype)], axis=1).astype(BF16)
    bias = jnp.concatenate([b_fox_f, b_i, b_f, jnp.zeros((LANES - n_gate,), F32)]).reshape(1, LANES)
    return wp, bias


def _pack_odd(w, w_a2):
    d = w.shape[0]
    wp = jnp.concatenate([w, jnp.zeros((d, LANES - GLA_RANK), w.dtype)], axis=1).astype(BF16)
    wa2 = jnp.concatenate([w_a2, jnp.zeros((LANES - GLA_RANK, w_a2.shape[1]), w_a2.dtype)], axis=0).astype(BF16)
    return wp, wa2


def _gate_rows(gc, batch, t_len, t_pad):
    g = gc.reshape(batch, t_len, LANES)[:, :, :16].transpose(0, 2, 1)
    if t_pad > t_len:
        g = jnp.pad(g, ((0, 0), (0, 0), (0, t_pad - t_len)))
    return g


def _trunk(x, fox_k, fox_v, fox_lf, ml_c, ml_n, ml_m, ml_buf, gla_s, params):
    (norm_mix, norm_ffn, norm_final, w_even, bias_even, conv_w, conv_b, g_ml,
     w_odd, w_a2, b_a, g_gla, w_out, w_ff1, w_ff2) = params
    batch, t_len, d = x.shape
    past = fox_k.shape[2]
    n = batch * t_len
    tm = 512 if n % 512 == 0 else 256 if n % 256 == 0 else n
    depth = norm_mix.shape[0]
    xf = x.reshape(n, d)
    ev_states, odd_states = [], []
    kv_stacked = None
    y = None
    for layer in range(depth):
        j = layer // 2
        if layer % 2 == 0:
            assert (t_len % tm == 0) == (past == 0)
            buf8 = jnp.pad(ml_buf[j], ((0, 0), (8 - (ML_CONV - 1), 0), (0, 0)))
            proj = _inproj_even(xf, norm_mix[layer][None], w_even[j], bias_even[j],
                                tm, batch, t_len, j, (depth + 1) // 2, kv_stacked, conv_w[j], conv_b[j][None], buf8)
            conv_tail = None
            if past == 0:
                qt, kf, vf, ka, vt, mqk, mv, mo, gc, bnd, conv_tail = proj
                kv_stacked = (kf, vf)
                assert tm == FOX_Q_TILE
                attn = _fox_attention(qt, ka, vt, bnd, batch, t_len, FOX_Q_TILE, FOX_KEY_CHUNK)
            else:
                q, kf, vf, mqk, mv, mo, gc = proj
                lf_new = gc.reshape(batch, t_len, LANES)[:, :, :FOX_HEADS]
                lf_row = jnp.concatenate([fox_lf[j], lf_new, jnp.zeros((batch, LANES - t_len, FOX_HEADS), F32)], axis=1)
                lf_row = jnp.pad(lf_row.transpose(0, 2, 1), ((0, 0), (0, 16 - FOX_HEADS), (0, 0)))
                cache_t = lambda a: a.transpose(0, 1, 3, 4, 2).reshape(a.shape[0], batch, FOX_WIDTH, past)
                new_t = lambda a: jnp.pad(a.reshape(batch, t_len, FOX_WIDTH).transpose(0, 2, 1).astype(BF16),
                                          ((0, 0), (0, 0), (0, LANES - t_len)))
                attn = _fox_decode(q.reshape(batch, t_len, FOX_WIDTH), cache_t(fox_k), cache_t(fox_v), j,
                                   new_t(kf), new_t(vf), lf_row).reshape(n, FOX_WIDTH)
            l_ml = ML_CHUNK if t_len >= ML_CHUNK else LANES
            gr = _gate_rows(gc, batch, t_len, max(t_len, l_ml))
            c0 = ml_c[j]
            n0 = ml_n[j][:, :, None, :]
            m0 = jnp.broadcast_to(ml_m[j][:, :, None, None], (batch, ML_HEADS, 1, LANES))
            h_ml, c_new, n_new, m_new, buf_new = _mlstm(mqk, mv, mo, gc, gr, c0, n0, m0, buf8,
                                                        conv_w[j], conv_b[j][None], g_ml[j][None], batch, t_len)
            if conv_tail is not None:
                buf_new = conv_tail
            heads = lambda a: None if past == 0 else a.reshape(batch, t_len, FOX_HEADS, FOX_HEAD_DIM)
            ev_states.append((heads(kf), heads(vf),
                              gc[:, :FOX_HEADS].reshape(batch, t_len, FOX_HEADS),
                              c_new, n_new[:, :, 0, :], m_new[:, :, 0, 0], buf_new[:, 8 - (ML_CONV - 1):, :]))
            mixes = [attn, h_ml]
        else:
            q, k, v, gg, la = _inproj_odd(xf, norm_mix[layer][None], w_odd[j], w_a2[j], b_a[j][None], tm)
            o, s_new = _gla(q, k, v, gg, la, gla_s, j, g_gla[j][None], batch, t_len)
            odd_states.append(s_new)
            mixes = [o]
        last = layer == depth - 1
        out = _post(xf, mixes, w_out, norm_ffn[layer][None], w_ff1, w_ff2,
                    norm_final[None] if last else None, tm, layer)
        if last:
            y = out
        else:
            xf = out
    ev = [jnp.stack([s[i] for s in ev_states]) for i in range(2 if kv_stacked else 0, 7)]
    if kv_stacked:
        ev = [a.reshape(a.shape[0], batch, FOX_HEADS, FOX_HEAD_DIM, t_len).transpose(0, 1, 4, 2, 3)
              for a in kv_stacked] + ev
    return y.reshape(batch, t_len, d), ev, jnp.stack(odd_states)


def kernel(x_prompt, x_sample, cache_fox_k, cache_fox_v, cache_fox_logf, state_mlstm_c, state_mlstm_n, state_mlstm_m, state_mlstm_conv, state_gla_s, norm_mix, norm_ffn, norm_final, w_in_even, b_fox_f, conv_w_ml, conv_b_ml, b_ml_i, b_ml_f, g_ml, w_in_odd, w_gla_a2, b_gla_a, g_gla, w_out, w_ff1, w_ff2):
    n_even, n_odd = w_in_even.shape[0], w_in_odd.shape[0]
    packed_even = [_pack_even(w_in_even[j], b_fox_f[j], b_ml_i[j], b_ml_f[j]) for j in range(n_even)]
    packed_odd = [_pack_odd(w_in_odd[j], w_gla_a2[j]) for j in range(n_odd)]
    params = (norm_mix, norm_ffn, norm_final,
              [p[0] for p in packed_even], [p[1] for p in packed_even], conv_w_ml, conv_b_ml, g_ml,
              [p[0] for p in packed_odd], [p[1] for p in packed_odd], b_gla_a, g_gla,
              w_out.astype(BF16), w_ff1.astype(BF16), w_ff2.astype(BF16))

    bp = x_prompt.shape[0]
    dt = x_prompt.dtype
    zeros = lambda *s: jnp.zeros(s, dt)
    y_p, ev_p, gla_p = _trunk(
        x_prompt,
        zeros(n_even, bp, 0, FOX_HEADS, FOX_HEAD_DIM), zeros(n_even, bp, 0, FOX_HEADS, FOX_HEAD_DIM),
        zeros(n_even, bp, 0, FOX_HEADS),
        zeros(n_even, bp, ML_HEADS, ML_HEAD_DIM, ML_HEAD_DIM), zeros(n_even, bp, ML_HEADS, ML_HEAD_DIM),
        zeros(n_even, bp, ML_HEADS), zeros(n_even, bp, ML_CONV - 1, 2 * ML_WIDTH),
        zeros(n_odd, bp, GLA_HEADS, GLA_DK, GLA_DV), params)
    y_s, ev_s, gla_s = _trunk(x_sample, cache_fox_k, cache_fox_v, cache_fox_logf, state_mlstm_c, state_mlstm_n,
                              state_mlstm_m, state_mlstm_conv, state_gla_s, params)
    return (y_p, y_s, *ev_p, gla_p, *ev_s, gla_s)
```

```python
import functools

import numpy as np
import jax
import jax.numpy as jnp
from jax import lax
from jax.experimental import pallas as pl
from jax.experimental.pallas import tpu as pltpu

F32 = jnp.float32
BF16 = jnp.bfloat16
EPS = 1e-6
NEG = -1e30
LOG2E = 1.4426950408889634

LANES = 128
VMEM_LIMIT = 56 * 1024 * 1024

D_MODEL = 1024
D_FF = 4 * D_MODEL
FOX_HEADS, FOX_HEAD_DIM = 8, 64
FOX_WIDTH = FOX_HEADS * FOX_HEAD_DIM
ML_HEADS, ML_HEAD_DIM = 4, 128
ML_WIDTH = ML_HEADS * ML_HEAD_DIM
ML_CONV = 4
GLA_HEADS, GLA_DK, GLA_DV = 4, 128, 256
GLA_KW = GLA_HEADS * GLA_DK
GLA_VW = GLA_HEADS * GLA_DV
GLA_RANK = 16
GLA_TAU = 16.0

E_Q, E_K, E_V, E_MQK, E_MV, E_MO, E_G, E_END = np.cumsum(
    (0, FOX_WIDTH, FOX_WIDTH, FOX_WIDTH, 2 * ML_WIDTH, ML_WIDTH, ML_WIDTH, LANES)).tolist()
G_FOX, G_MI, G_MF = 0, FOX_HEADS, FOX_HEADS + ML_HEADS
O_Q, O_K, O_V, O_G, O_A, O_END = np.cumsum((0, GLA_KW, GLA_KW, GLA_VW, GLA_VW, LANES)).tolist()

ML_CHUNK = 256
FOX_KEY_CHUNK = 256
FOX_Q_TILE = 512
GLA_CHUNK = 128
GLA_LEVELS = (64, 32, 16, 8, 4, 2, 1)


def _cparams(sem):
    return pltpu.CompilerParams(dimension_semantics=sem, vmem_limit_bytes=VMEM_LIMIT)


def _const_spec(shape):
    nd = len(shape)
    return pl.BlockSpec(shape, lambda *_: (0,) * nd, pipeline_mode=pl.Buffered(1))


def _rms(x, g):
    return x * lax.rsqrt(jnp.mean(x * x, axis=-1, keepdims=True) + EPS) * g


def _sigmoid(x):
    return 1.0 / (1.0 + jnp.exp(-x))


def _log_sigmoid(x):
    return -(jnp.maximum(-x, 0.0) + jnp.log1p(jnp.exp(-jnp.abs(x))))


def _dot(a, b):
    return jnp.dot(a, b, preferred_element_type=F32)


def _dot_nt(a, b):
    return lax.dot_general(a, b, (((1,), (1,)), ((), ())), preferred_element_type=F32)


def _dot_tn(a, b):
    return lax.dot_general(a, b, (((0,), (0,)), ((), ())), preferred_element_type=F32)


def _split3(x):
    hi = x.astype(BF16)
    r1 = x - hi.astype(F32)
    mid = r1.astype(BF16)
    lo = (r1 - mid.astype(F32)).astype(BF16)
    return hi, mid, lo


def _split2(x):
    hi = x.astype(BF16)
    return hi, (x - hi.astype(F32)).astype(BF16)


def _mat_f32(m, parts):
    return functools.reduce(lambda a, b: a + b, [_dot(m, p) for p in parts])


def _f32_mat(parts, m):
    return functools.reduce(lambda a, b: a + b, [_dot(p, m) for p in parts])


def _tri(n, lower):
    r = lax.broadcasted_iota(jnp.int32, (n, n), 0)
    c = lax.broadcasted_iota(jnp.int32, (n, n), 1)
    keep = (c <= r) if lower else (r <= c)
    return jnp.where(keep, 1.0, 0.0).astype(BF16)


def _pad_rows(a, n):
    if a.shape[0] == n:
        return a
    return jnp.concatenate([a, jnp.zeros((n - a.shape[0], a.shape[1]), a.dtype)], axis=0)


def _conv_silu(u, prev, cw_ref, cb_ref):
    n = u.shape[0]
    ext = jnp.concatenate([prev, u], axis=0)
    y = cb_ref[...] + cw_ref[ML_CONV - 1:ML_CONV, :] * u
    for s in range(1, ML_CONV):
        y = y + cw_ref[ML_CONV - 1 - s:ML_CONV - s, :] * pltpu.roll(ext, s, axis=0)[8:8 + n]
    lane = lax.broadcasted_iota(jnp.int32, (1, 2 * ML_WIDTH), 1)
    return y * _sigmoid(y) * jnp.where(lane >= ML_WIDTH, ML_HEAD_DIM ** -0.5, 1.0), ext


FOX_Q_SCALE = FOX_HEAD_DIM ** -0.5 * LOG2E


def _even_segments(x_ref, g_ref, w_ref):
    h = _rms(x_ref[...], g_ref[...]).astype(BF16)
    return lambda a, b: _dot(h, w_ref[0, :, a:b])


def _even_gates(seg, bias_ref):
    gz = seg(E_G, E_END) + bias_ref[0]
    lane = lax.broadcasted_iota(jnp.int32, gz.shape, 1)
    is_log = (lane < G_MI) | (lane >= G_MF)
    return jnp.where(is_log, _log_sigmoid(gz), gz)


def _inproj_even_rows_kernel(x_ref, g_ref, w_ref, bias_ref,
                             q_ref, kf_ref, vf_ref, mqk_ref, mv_ref, mo_ref, gc_ref):
    seg = _even_segments(x_ref, g_ref, w_ref)
    q_ref[...] = (seg(E_Q, E_K) * FOX_Q_SCALE).astype(BF16)
    kf_ref[...] = seg(E_K, E_V)
    vf_ref[...] = seg(E_V, E_MQK)
    mqk_ref[...] = seg(E_MQK, E_MV)
    mv_ref[...] = seg(E_MV, E_MO).astype(BF16)
    mo_ref[...] = seg(E_MO, E_G).astype(BF16)
    gc_ref[...] = _even_gates(seg, bias_ref)


def _inproj_even_seq_kernel(*refs, tpb, aliased):
    x_ref, g_ref, w_ref, bias_ref, ind_ref, cw_ref, cb_ref, buf_ref = refs[:8]
    refs = refs[8 + (2 if aliased else 0):]
    (qt_ref, kf_ref, vf_ref, ka_ref, vt_ref, mqk_ref, mv_ref, mo_ref, gc_ref, bnd_ref, tail_ref,
     carry, conv_prev) = refs
    @pl.when(pl.program_id(0) % tpb == 0)
    def _():
        carry[...] = jnp.zeros_like(carry)
        conv_prev[...] = buf_ref[0]

    seg = _even_segments(x_ref, g_ref, w_ref)
    mqk = seg(E_MQK, E_MV)
    gates = _even_gates(seg, bias_ref)
    gc_ref[...] = gates
    q = seg(E_Q, E_K) * FOX_Q_SCALE
    k = seg(E_K, E_V)
    v = seg(E_V, E_MQK)
    qt_ref[...] = q.T.astype(BF16)
    mv = seg(E_MV, E_MO)
    kf_ref[0, 0] = k.T
    f = _append_f_terms(gates, carry, k.astype(BF16), ka_ref)
    mo = seg(E_MO, E_G)
    act, _ = _conv_silu(mqk, conv_prev[...], cw_ref, cb_ref)
    mqk_ref[...] = act.astype(BF16)
    last_rows = mqk[mqk.shape[0] - 8:]
    conv_prev[...] = last_rows
    tail_ref[0] = last_rows
    mv_ref[...] = mv.astype(BF16)
    v_t = v.T
    vt_ref[...] = v_t.astype(BF16)
    vf_ref[0, 0] = v_t
    mo_ref[...] = mo.astype(BF16)
    norms = _dot((q * q).astype(BF16), ind_ref[0]) + _dot((k * k).astype(BF16), ind_ref[1])
    bnd_ref[...] = jnp.zeros_like(bnd_ref)
    bnd_ref[0, 0:1, :] = f[0:1, :]
    for c in range(f.shape[0] // FOX_KEY_CHUNK):
        bnd_ref[0, 1 + c:2 + c, :] = f[(c + 1) * FOX_KEY_CHUNK - 1:(c + 1) * FOX_KEY_CHUNK, :]
    bnd_ref[0, 7:8, :] = jnp.max(norms, axis=0, keepdims=True)


def _inproj_even(x, g, w, bias, tm, batch, t_len, slot, n_slots, kv_prev, cw, cb, buf8):
    n = x.shape[0]
    row = lambda c: (pl.BlockSpec((tm, c), lambda i: (i, 0)), (n, c))
    col = (pl.BlockSpec((FOX_WIDTH, tm), lambda i: (0, i)), (FOX_WIDTH, n))
    of_layer = lambda a: pl.BlockSpec((1,) + a.shape[1:], lambda i: (slot, 0, 0), pipeline_mode=pl.Buffered(1))
    common_in = [row(D_MODEL)[0], _const_spec((1, D_MODEL)), of_layer(w), of_layer(bias)]
    tail = [(row(ML_WIDTH), BF16), (row(ML_WIDTH), BF16), (row(LANES), F32)]
    if t_len % tm != 0:
        outs = [(row(FOX_WIDTH), BF16), (row(FOX_WIDTH), F32), (row(FOX_WIDTH), F32), (row(2 * ML_WIDTH), F32)] + tail
        return pl.pallas_call(
            _inproj_even_rows_kernel,
            grid=(n // tm,),
            in_specs=common_in,
            out_specs=[spec for (spec, _), _ in outs],
            out_shape=[jax.ShapeDtypeStruct(shape, dt) for (_, shape), dt in outs],
            compiler_params=_cparams(("parallel",)),
            name="inproj_even_rows",
        )(x, g, w, bias)
    tpb = t_len // tm
    state = (pl.BlockSpec((1, 1, FOX_WIDTH, tm), lambda i: (slot, i // tpb, 0, i % tpb)),
             (n_slots, batch, FOX_WIDTH, t_len))
    per_seq = (pl.BlockSpec((1, 8, 2 * ML_WIDTH), lambda i: (i // tpb, 0, 0)), (batch, 8, 2 * ML_WIDTH))
    outs = ([(col, BF16), (state, F32), (state, F32), (row(2 * FOX_WIDTH), BF16), (col, BF16), (row(2 * ML_WIDTH), BF16)]
            + tail + [((pl.BlockSpec((1, 8, LANES), lambda i: (i, 0, 0)), (n // tm, 8, LANES)), F32), (per_seq, F32)])
    assert tm % FOX_KEY_CHUNK == 0 and tm // FOX_KEY_CHUNK <= 6
    ind = np.zeros((2, FOX_WIDTH, LANES), np.float32)
    for c in range(FOX_WIDTH):
        ind[0, c, FOX_HEADS + c // FOX_HEAD_DIM] = 1.0
        ind[1, c, 2 * FOX_HEADS + c // FOX_HEAD_DIM] = 1.0
    ins = [x, g, w, bias, jnp.asarray(ind, BF16), cw, cb, buf8]
    in_specs = common_in + [_const_spec(ind.shape), _const_spec(cw.shape), _const_spec(cb.shape), per_seq[0]]
    aliases = {}
    if kv_prev is not None:
        aliases = {len(ins): 1, len(ins) + 1: 2}
        ins += list(kv_prev)
        in_specs += [pl.BlockSpec(memory_space=pl.ANY)] * 2
    return pl.pallas_call(
        functools.partial(_inproj_even_seq_kernel, tpb=tpb, aliased=kv_prev is not None),
        grid=(n // tm,),
        in_specs=in_specs,
        out_specs=[spec for (spec, _), _ in outs],
        out_shape=[jax.ShapeDtypeStruct(shape, dt) for (_, shape), dt in outs],
        scratch_shapes=[pltpu.VMEM((1, LANES), F32), pltpu.VMEM((8, 2 * ML_WIDTH), F32)],
        input_output_aliases=aliases,
        compiler_params=_cparams(("arbitrary",)),
        name="inproj_even_seq",
    )(*ins)


def _inproj_odd_kernel(x_ref, g_ref, w_ref, wa2_ref, ba_ref, q_ref, k_ref, v_ref, gg_ref, la_ref):
    h = _rms(x_ref[...], g_ref[...]).astype(BF16)

    def seg(a, b):
        return _dot(h, w_ref[0, :, a:b])

    q_ref[...] = (seg(O_Q, O_K) * (GLA_DK ** -0.5)).astype(BF16)
    k_ref[...] = seg(O_K, O_V).astype(BF16)
    v_ref[...] = seg(O_V, O_G).astype(BF16)
    gg_ref[...] = seg(O_G, O_A).astype(BF16)
    ga = seg(O_A, O_END).astype(BF16)
    la_ref[...] = _log_sigmoid(_dot(ga, wa2_ref[0]) + ba_ref[...]) * (1.0 / GLA_TAU)


def _inproj_odd(x, g, w, wa2, ba, tm, slot):
    n = x.shape[0]
    row = lambda c: pl.BlockSpec((tm, c), lambda i: (i, 0))
    of_layer = lambda a: pl.BlockSpec((1,) + a.shape[1:], lambda i: (slot, 0, 0), pipeline_mode=pl.Buffered(1))
    outs = [(GLA_KW, BF16), (GLA_KW, BF16), (GLA_VW, BF16), (GLA_VW, BF16), (GLA_KW, F32)]
    return pl.pallas_call(
        _inproj_odd_kernel,
        grid=(n // tm,),
        in_specs=[row(D_MODEL), _const_spec((1, D_MODEL)), of_layer(w), of_layer(wa2), _const_spec((1, GLA_KW))],
        out_specs=[row(c) for c, _ in outs],
        out_shape=[jax.ShapeDtypeStruct((n, c), dt) for c, dt in outs],
        compiler_params=_cparams(("parallel",)),
        name="inproj_odd",
    )(x, g, w, wa2, ba)


def _post_kernel(*refs, n_mix, final):
    x_ref = refs[0]
    mix_refs = refs[1:1 + n_mix]
    wo_ref, gf_ref, w1_ref, w2_ref = refs[1 + n_mix:5 + n_mix]
    rest = refs[5 + n_mix:]
    mix = mix_refs[0][...] if n_mix == 1 else jnp.concatenate([r[...] for r in mix_refs], axis=1)
    x1 = x_ref[...] + _dot(mix, wo_ref[0])
    h = _rms(x1, gf_ref[...]).astype(BF16)
    y = x1
    for c in range(D_FF // D_MODEL):
        sl = slice(c * D_MODEL, (c + 1) * D_MODEL)
        t = jnp.maximum(_dot(h, w1_ref[0, :, sl]), 0.0)
        y = y + _dot((t * t).astype(BF16), w2_ref[0, sl, :])
    if final:
        gfin_ref, out_ref = rest
        out_ref[...] = _rms(y, gfin_ref[...])
    else:
        (out_ref,) = rest
        out_ref[...] = y


def _post(x, mixes, wo, gf, w1, w2, gfin, tm, layer):
    n = x.shape[0]
    row = lambda c: pl.BlockSpec((tm, c), lambda i: (i, 0))
    of_layer = lambda a: pl.BlockSpec((1,) + a.shape[1:], lambda i: (layer, 0, 0), pipeline_mode=pl.Buffered(1))
    final = gfin is not None
    ins = [x, *mixes, wo, gf, w1, w2]
    specs = [row(D_MODEL)] + [row(m.shape[1]) for m in mixes] + [
        of_layer(wo), _const_spec((1, D_MODEL)), of_layer(w1), of_layer(w2)]
    if final:
        ins.append(gfin)
        specs.append(_const_spec((1, D_MODEL)))
    return pl.pallas_call(
        functools.partial(_post_kernel, n_mix=len(mixes), final=final),
        grid=(n // tm,),
        in_specs=specs,
        out_specs=row(D_MODEL),
        out_shape=jax.ShapeDtypeStruct((n, D_MODEL), F32),
        compiler_params=_cparams(("parallel",)),
        name="post_final" if final else "post",
    )(*ins)


F_TERMS = 3


def _append_f_terms(lf, carry, k, ka_ref):
    tc = lf.shape[0]
    f = _mat_f32(_tri(tc, True), _split3(lf)) + carry[...]
    carry[...] = f[tc - 1:tc, :]
    hi, mid, lo = [p.astype(F32) for p in _split3(f * LOG2E)]
    lane = lax.broadcasted_iota(jnp.int32, (1, LANES), 1)
    cols = jnp.where(lane < FOX_HEADS, hi,
                     jnp.where(lane < 2 * FOX_HEADS, pltpu.roll(mid, FOX_HEADS, axis=1),
                               jnp.where(lane < 3 * FOX_HEADS, pltpu.roll(lo, 2 * FOX_HEADS, axis=1), 0.0)))
    cols = cols.astype(BF16)
    for hp in range(FOX_HEADS // 2):
        ka_ref[:, 2 * hp * LANES:(2 * hp + 1) * LANES] = k[:, hp * LANES:(hp + 1) * LANES]
        ka_ref[:, (2 * hp + 1) * LANES:(2 * hp + 2) * LANES] = cols
    return f


def _fox_decode_kernel(q_ref, kc_ref, vc_ref, kn_ref, vn_ref, lf_ref, o_ref, *, t_new, past):
    nkeys = past + LANES
    lane_f = lax.broadcasted_iota(jnp.int32, (1, FOX_WIDTH), 1)
    head_mask = [jnp.where((lane_f >= h * FOX_HEAD_DIM) & (lane_f < (h + 1) * FOX_HEAD_DIM), 1.0, 0.0)
                 for h in range(FOX_HEADS)]
    q = q_ref[0].astype(F32)
    qb = jnp.concatenate([q * hm for hm in head_mask], axis=0).astype(BF16)

    def keys(cache_ref, new_ref):
        return jnp.concatenate([cache_ref[0, 0].astype(BF16), new_ref[0]], axis=1)

    s = _dot(qb, keys(kc_ref, kn_ref))
    lf = lf_ref[0]
    triu = _tri(LANES, False)
    carry = jnp.zeros((lf.shape[0], 1), F32)
    blocks = []
    for c in range(nkeys // LANES):
        cs = _f32_mat(_split3(lf[:, c * LANES:(c + 1) * LANES]), triu) + carry
        carry = cs[:, LANES - 1:LANES]
        blocks.append(cs)
    f_all = jnp.concatenate(blocks, axis=1) * LOG2E
    s = s - jnp.concatenate([jnp.broadcast_to(f_all[h:h + 1, :], (t_new, nkeys)) for h in range(FOX_HEADS)], axis=0)
    kpos = lax.broadcasted_iota(jnp.int32, (1, nkeys), 1)
    qpos = past + (lax.broadcasted_iota(jnp.int32, (FOX_HEADS * t_new, 1), 0) & (t_new - 1))
    s = jnp.where(kpos <= qpos, s, NEG)
    p = jnp.exp2(s - jnp.max(s, axis=-1, keepdims=True))
    ob = _dot_nt(p.astype(BF16), keys(vc_ref, vn_ref)) / jnp.sum(p, axis=-1, keepdims=True)
    out = ob[0:t_new] * head_mask[0]
    for h in range(1, FOX_HEADS):
        out = out + ob[h * t_new:(h + 1) * t_new] * head_mask[h]
    o_ref[0] = out.astype(o_ref.dtype)


def _fox_decode(q, k_cache, v_cache, layer, k_new, v_new, lf_row):
    batch, t_new, _ = q.shape
    past = k_cache.shape[3]
    assert t_new & (t_new - 1) == 0 and t_new <= LANES and t_new % 16 == 0 and past % LANES == 0
    blk = lambda a: pl.BlockSpec((1,) + a.shape[1:], lambda b: (b, 0, 0))
    cache = pl.BlockSpec((1, 1, FOX_WIDTH, past), lambda b: (layer, b, 0, 0))
    args = (q, k_cache, v_cache, k_new, v_new, lf_row)
    return pl.pallas_call(
        functools.partial(_fox_decode_kernel, t_new=t_new, past=past),
        grid=(batch,),
        in_specs=[blk(q), cache, cache, blk(k_new), blk(v_new), blk(lf_row)],
        out_specs=blk(q),
        out_shape=jax.ShapeDtypeStruct(q.shape, BF16),
        compiler_params=_cparams(("parallel",)),
        name="fox_decode",
    )(*args)


ONES_ROWS = 16


def _fox_attn_kernel(first_ref, qt_ref, ka_ref, vt_ref, o_ref, s_scr, acc_scr, *, tq, tkc, past, n_diag):
    q_first = past + pl.program_id(2) * tq
    n_full = q_first // tkc
    drow = lax.broadcasted_iota(jnp.int32, (LANES, 1), 0)
    lane = lax.broadcasted_iota(jnp.int32, (1, 2 * tq), 1)
    lane_head = jnp.where(lane < tq, 0, 1)
    own_head = jnp.where(jnp.where(drow < FOX_HEAD_DIM, 0, 1) == lane_head, 1.0, 0.0).astype(BF16)
    qt = qt_ref[...]
    q2 = jnp.concatenate([qt, qt], axis=1) * own_head
    head = 2 * pl.program_id(1) + lane_head
    f_sel = ((drow & (FOX_HEADS - 1)) == head) & (drow < F_TERMS * FOX_HEADS)
    qa = jnp.concatenate([q2, jnp.where(f_sel, -1.0, 0.0).astype(BF16)], axis=0)
    ones = jnp.ones((ONES_ROWS, tkc), BF16)
    acc_scr[...] = jnp.zeros_like(acc_scr)

    qa_h = [qa[:, hh * tq:(hh + 1) * tq] for hh in range(2)]
    qpos = q_first + lax.broadcasted_iota(jnp.int32, (1, tq), 1)

    def produce(j, hh, buf, lo=0):
        start = pl.multiple_of(j * tkc, tkc)
        s_scr[buf, hh, :, lo:] = _dot(ka_ref[pl.ds(start, tkc), :], qa_h[hh][:, lo:])

    def consume(j, hh, buf, m_all, masked, lo=0):
        start = pl.multiple_of(j * tkc, tkc)
        st = s_scr[buf, hh, :, lo:]
        m_prev = m_all[:, lo:]
        if masked:
            kpos = start + lax.broadcasted_iota(jnp.int32, (tkc, 1), 0)
            st = jnp.where(kpos <= qpos[:, lo:], st, NEG)
        m_new = jnp.maximum(m_prev, jnp.max(st, axis=0, keepdims=True))
        alpha = jnp.exp2(m_prev - m_new)
        p = jnp.exp2(st - m_new).astype(BF16)
        rows = slice(hh * FOX_HEAD_DIM, (hh + 1) * FOX_HEAD_DIM)
        va = jnp.concatenate([vt_ref[rows, pl.ds(start, tkc)], ones], axis=0)
        acc_scr[hh, :, lo:] = alpha * acc_scr[hh, :, lo:] + _dot(va, p)
        return m_new if lo == 0 else jnp.concatenate([m_all[:, :lo], m_new], axis=1)

    nq = pl.num_programs(2)
    slot = ((pl.program_id(0) * pl.num_programs(1) + pl.program_id(1)) * nq + pl.program_id(2)) * 2
    first = [first_ref[slot], first_ref[slot + 1]]
    joint = jnp.maximum(first[0], first[1])
    for hh in range(2):
        produce(first[hh], hh, 0)

    def pair_of_chunks(i, ms, heads):
        ms = list(ms)
        for step in range(2):
            for hh in heads:
                produce(2 * i + step + 1, hh, 1 - step)
                ms[hh] = consume(2 * i + step, hh, step, ms[hh], False)
        return tuple(ms)

    m_init = jnp.full((1, tq), NEG, F32)
    ms = (m_init, m_init)
    for hh in range(2):
        ms = lax.fori_loop(first[hh] // 2, joint // 2, functools.partial(pair_of_chunks, heads=(hh,)), ms)
    ms = list(lax.fori_loop(joint // 2, n_full // 2, functools.partial(pair_of_chunks, heads=(0, 1)), ms))
    for d in range(n_diag):
        for hh in range(2):
            if d + 1 < n_diag:
                produce(n_full + d + 1, hh, (d + 1) % 2, lo=min((d + 1) * tkc, tq - LANES))
            ms[hh] = consume(n_full + d, hh, d % 2, ms[hh], True, lo=min(d * tkc, tq - LANES))
    out = jnp.concatenate([acc_scr[hh, 0:FOX_HEAD_DIM] / acc_scr[hh, FOX_HEAD_DIM:FOX_HEAD_DIM + 1]
                           for hh in range(2)], axis=0)
    o_ref[...] = out.T.astype(o_ref.dtype)


SKIP_MARGIN = 40.0
NORM_SLACK = 1.02


def _fox_first_chunks(bnd, batch, t_len, tq, tkc):
    nq, nc, pairs = t_len // tq, t_len // tkc, FOX_HEADS // 2
    per_tile = tq // tkc
    b4 = bnd.reshape(batch, nq, 8, LANES)
    f_start = b4[:, :, 0, 0:FOX_HEADS] * LOG2E
    f_end = b4[:, :, 1:1 + per_tile, 0:FOX_HEADS].reshape(batch, nc, FOX_HEADS) * LOG2E
    qn = jnp.sqrt(b4[:, :, 7, FOX_HEADS:2 * FOX_HEADS])
    kn = jnp.sqrt(jnp.max(b4[:, :, 7, 2 * FOX_HEADS:3 * FOX_HEADS], axis=1))
    thr = 2.0 * NORM_SLACK * qn * kn[:, None, :] + SKIP_MARGIN
    decay = f_end[:, None, :, :] - f_start[:, :, None, :]
    before_tile = jnp.arange(nc)[None, :] < ((jnp.arange(nq) * tq) // tkc)[:, None]
    ok = (decay > thr[:, :, None, :]) & before_tile[None, :, :, None]
    lead = jnp.sum(jnp.cumprod(ok.astype(jnp.int32), axis=2), axis=2)
    lead = ((lead // 2) * 2).reshape(batch, nq, pairs, 2)
    return lead.transpose(0, 2, 1, 3).reshape(-1).astype(jnp.int32)


def _fox_attention(qt, ka, vt, bnd, batch, t_len, tq, tkc):
    past = 0
    nq = t_len // tq
    n_diag = max(1, tq // tkc)
    assert tq & (tq - 1) == 0 and tq % LANES == 0 and (tkc % tq == 0 or tq % tkc == 0) and t_len % tq == 0
    assert nq == 1 or tq % (2 * tkc) == 0
    assert t_len >= n_diag * tkc and t_len % tkc == 0
    pairs = FOX_HEADS // 2
    first = _fox_first_chunks(bnd, batch, t_len, tq, tkc)
    grid_spec = pltpu.PrefetchScalarGridSpec(
        num_scalar_prefetch=1,
        grid=(batch, pairs, nq),
        in_specs=[pl.BlockSpec((LANES, tq), lambda b, h, i, f: (h, b * nq + i)),
                  pl.BlockSpec((t_len, 2 * LANES), lambda b, h, i, f: (b, h)),
                  pl.BlockSpec((LANES, t_len), lambda b, h, i, f: (h, b))],
        out_specs=pl.BlockSpec((tq, LANES), lambda b, h, i, f: (b * nq + i, h)),
        scratch_shapes=[pltpu.VMEM((2, 2, tkc, tq), F32),
                        pltpu.VMEM((2, FOX_HEAD_DIM + ONES_ROWS, tq), F32)])
    return pl.pallas_call(
        functools.partial(_fox_attn_kernel, tq=tq, tkc=tkc, past=past, n_diag=n_diag),
        grid_spec=grid_spec,
        out_shape=jax.ShapeDtypeStruct((batch * t_len, FOX_WIDTH), BF16),
        compiler_params=_cparams(("parallel", "parallel", "arbitrary")),
        name="fox_attention",
    )(first, qt, ka, vt)


def _mlstm_kernel(mqk_ref, mv_ref, mo_ref, gc_ref, gr_ref, c0_ref, n0_ref, m0_ref, buf_ref,
                  cw_ref, cb_ref, gml_ref,
                  h_ref, c_out, n_out, m_out, buf_out,
                  c_scr, n_scr, m_scr, prev_scr, *, L, t_last, nc, preact):
    ci = pl.program_id(1)
    rows = mqk_ref.shape[0]

    @pl.when(ci == 0)
    def _():
        c_scr[...] = c0_ref[0]
        n_scr[...] = n0_ref[0]
        m_scr[...] = m0_ref[0]
        prev_scr[...] = buf_ref[0]

    if preact:
        qk = _pad_rows(mqk_ref[...], L).astype(F32)
        ext = None
    else:
        u = _pad_rows(mqk_ref[...], L)
        qk, ext = _conv_silu(u, prev_scr[...], cw_ref, cb_ref)
        prev_scr[...] = u[L - 8:L]

    gc = _pad_rows(gc_ref[...], L)
    gr = gr_ref[0]
    b_col = _mat_f32(_tri(L, True), _split3(gc))
    b_row = _f32_mat(_split3(gr), _tri(L, False))
    v_all = _pad_rows(mv_ref[...], L)
    o_all = _pad_rows(mo_ref[...], L)
    t_idx = lax.broadcasted_iota(jnp.int32, (L, 1), 0)
    causal = lax.broadcasted_iota(jnp.int32, (1, L), 1) <= t_idx

    heads = []
    for h in range(ML_HEADS):
        hs = slice(h * ML_HEAD_DIM, (h + 1) * ML_HEAD_DIM)
        q32 = qk[:, hs]
        k32 = qk[:, ML_WIDTH + h * ML_HEAD_DIM:ML_WIDTH + (h + 1) * ML_HEAD_DIM]
        qb, kb = q32.astype(BF16), k32.astype(BF16)
        c_prev = c_scr[h]
        heads.append(dict(hs=hs, q32=q32, k32=k32, qb=qb, kb=kb, vb=v_all[:, hs], c_prev=c_prev,
                          s=_dot_nt(qb, kb), qc=_dot_nt(qb, c_prev.astype(BF16))))

    for h, hd in enumerate(heads):
        bc = b_col[:, G_MF + h:G_MF + h + 1]
        ic = gc[:, G_MI + h:G_MI + h + 1]
        br = b_row[G_MF + h:G_MF + h + 1, :]
        ir = gr[G_MI + h:G_MI + h + 1, :]
        m_prev = m_scr[h][:, 0:1]
        dmat = jnp.where(causal, bc + (ir - br), NEG)
        inter = bc + m_prev
        m_t = jnp.maximum(inter, jnp.max(dmat, axis=-1, keepdims=True))
        w = jnp.exp(dmat - m_t)
        g = jnp.exp(inter - m_t)
        a = w * hd["s"]
        b_last = bc[t_last:t_last + 1, :]
        m_last = m_t[t_last:t_last + 1, :]
        w_end = jnp.where(t_idx <= t_last, jnp.exp(b_last - bc + ic - m_last), 0.0)
        hd.update(m_t=m_t, g=g, a=a, m_last=m_last, g_end=g[t_last:t_last + 1, :], w_end=w_end,
                  av=_dot(a.astype(BF16), hd["vb"]),
                  vk=_dot_tn((hd["vb"].astype(F32) * w_end).astype(BF16), hd["kb"]))

    for h, hd in enumerate(heads):
        hs, g, a, m_t = hd["hs"], hd["g"], hd["a"], hd["m_t"]
        n_prev = n_scr[h]
        num = g * hd["qc"] + hd["av"]
        den = g * jnp.sum(hd["q32"] * n_prev, axis=-1, keepdims=True) + jnp.sum(a, axis=-1, keepdims=True)
        hh = num / jnp.maximum(jnp.abs(den), jnp.exp(-m_t))
        c_scr[h] = hd["g_end"] * hd["c_prev"] + hd["vk"]
        n_scr[h] = hd["g_end"] * n_prev + jnp.sum(hd["k32"] * hd["w_end"], axis=0, keepdims=True)
        m_scr[h] = jnp.broadcast_to(hd["m_last"], (1, LANES))
        yh = hh * lax.rsqrt(jnp.mean(hh * hh, axis=-1, keepdims=True) + EPS) * gml_ref[:, hs]
        yh = yh * _sigmoid(o_all[:, hs].astype(F32))
        h_ref[:, hs] = yh[:rows].astype(h_ref.dtype)

    @pl.when(ci == nc - 1)
    def _():
        c_out[0] = c_scr[...]
        n_out[0] = n_scr[...]
        m_out[0] = m_scr[...]
        buf_out[0] = buf_ref[0] if preact else ext[t_last + 1:t_last + 9]


def _mlstm(mqk, mv, mo, gc, gr, c0, n0, m0, buf8, cw, cb, gml, batch, t_len):
    preact = mqk.dtype == BF16
    L = ML_CHUNK if t_len >= ML_CHUNK else LANES
    rows = min(L, t_len)
    nc = t_len // rows
    t_last = rows - 1
    assert t_len % rows == 0 and (nc == 1 or rows == L) and (t_last + 1) % 8 == 0
    tok = lambda c: pl.BlockSpec((rows, c), lambda b, i: (b * nc + i, 0))
    st = lambda *s: pl.BlockSpec((1,) + s, lambda b, i: (b,) + (0,) * len(s))
    shp_c = (ML_HEADS, ML_HEAD_DIM, ML_HEAD_DIM)
    shp_n = (ML_HEADS, 1, ML_HEAD_DIM)
    return pl.pallas_call(
        functools.partial(_mlstm_kernel, L=L, t_last=t_last, nc=nc, preact=preact),
        grid=(batch, nc),
        in_specs=[tok(2 * ML_WIDTH), tok(ML_WIDTH), tok(ML_WIDTH), tok(LANES),
                  pl.BlockSpec((1, 16, L), lambda b, i: (b, 0, i)),
                  st(*shp_c), st(*shp_n), st(*shp_n), st(8, 2 * ML_WIDTH),
                  _const_spec((ML_CONV, 2 * ML_WIDTH)), _const_spec((1, 2 * ML_WIDTH)), _const_spec((1, ML_WIDTH))],
        out_specs=[tok(ML_WIDTH), st(*shp_c), st(*shp_n), st(*shp_n), st(8, 2 * ML_WIDTH)],
        out_shape=[jax.ShapeDtypeStruct((batch * t_len, ML_WIDTH), BF16),
                   jax.ShapeDtypeStruct((batch,) + shp_c, F32),
                   jax.ShapeDtypeStruct((batch,) + shp_n, F32),
                   jax.ShapeDtypeStruct((batch,) + shp_n, F32),
                   jax.ShapeDtypeStruct((batch, 8, 2 * ML_WIDTH), F32)],
        scratch_shapes=[pltpu.VMEM(shp_c, F32), pltpu.VMEM(shp_n, F32), pltpu.VMEM(shp_n, F32),
                        pltpu.VMEM((8, 2 * ML_WIDTH), F32)],
        compiler_params=_cparams(("parallel", "arbitrary")),
        name="mlstm",
    )(mqk, mv, mo, gc, gr, c0, n0, m0, buf8, cw, cb, gml)


def _gla_level_matrices(L):
    mats = np.zeros((len(GLA_LEVELS), L, L), np.float32)
    for li, b in enumerate(GLA_LEVELS):
        for t in range(L):
            base = (t // (2 * b)) * 2 * b
            bound = base + b - 1
            if t > bound:
                mats[li, t, bound + 1:t + 1] = 1.0
            else:
                mats[li, t, t + 1:bound + 1] = 1.0
    return jnp.asarray(mats, BF16)


def _gla_kernel(q_ref, k_ref, v_ref, gg_ref, la_ref, s0_ref, lvl_ref, gn_ref, o_ref, s_out, s_scr, *, L, t_last, nc):
    ci = pl.program_id(1)
    nb, rows = q_ref.shape[0], q_ref.shape[1]

    @pl.when(ci == 0)
    def _():
        for bi in range(nb):
            for h in range(GLA_HEADS):
                s_scr[bi, h] = s0_ref[0, bi, h].T

    t_idx = lax.broadcasted_iota(jnp.int32, (L, 1), 0)
    s_idx = lax.broadcasted_iota(jnp.int32, (1, L), 1)
    tril = _tri(L, True)
    level_mask = []
    for b in GLA_LEVELS:
        sh = b.bit_length() - 1
        level_mask.append((jnp.right_shift(t_idx, sh + 1) == jnp.right_shift(s_idx, sh + 1))
                          & ((jnp.right_shift(t_idx, sh) & 1) == 1) & ((jnp.right_shift(s_idx, sh) & 1) == 0))

    els = []
    for bi in range(nb):
        la3 = _split2(_pad_rows(la_ref[bi], L))
        q_all = _pad_rows(q_ref[bi], L)
        k_all = _pad_rows(k_ref[bi], L)
        el = dict(la3=la3, cb=_mat_f32(tril, la3), v=_pad_rows(v_ref[bi], L), g=_pad_rows(gg_ref[bi], L),
                  q32=[], k32=[], a=[], e_next=jnp.exp(_mat_f32(lvl_ref[0], la3)), pending=None)
        for h in range(GLA_HEADS):
            ks = slice(h * GLA_DK, (h + 1) * GLA_DK)
            el["q32"].append(q_all[:, ks].astype(F32))
            el["k32"].append(k_all[:, ks].astype(F32))
            el["a"].append(jnp.where(t_idx == s_idx, _dot_nt(q_all[:, ks], k_all[:, ks]), 0.0))
        els.append(el)

    def settle(el):
        if el["pending"] is not None:
            prods, mask = el["pending"]
            el["a"] = [jnp.where(mask, prods[h], el["a"][h]) for h in range(GLA_HEADS)]

    for li in range(len(GLA_LEVELS)):
        for el in els:
            e_all = el["e_next"]
            if li + 1 < len(GLA_LEVELS):
                el["e_next"] = jnp.exp(_mat_f32(lvl_ref[li + 1], el["la3"]))
            prods = []
            for h in range(GLA_HEADS):
                e = e_all[:, h * GLA_DK:(h + 1) * GLA_DK]
                prods.append(_dot_nt((el["q32"][h] * e).astype(BF16), (el["k32"][h] * e).astype(BF16)))
            settle(el)
            el["pending"] = (prods, level_mask[li])
    for el in els:
        settle(el)

    for bi, el in enumerate(els):
        el["o"] = []
        for h in range(GLA_HEADS):
            ks = slice(h * GLA_DK, (h + 1) * GLA_DK)
            vs = slice(h * GLA_DV, (h + 1) * GLA_DV)
            cbh = el["cb"][:, ks]
            vb = el["v"][:, vs]
            s_t = s_scr[bi, h]
            o = (_dot_nt((el["q32"][h] * jnp.exp(cbh)).astype(BF16), s_t.astype(BF16))
                 + _dot(el["a"][h].astype(BF16), vb))
            cl = cbh[t_last:t_last + 1, :]
            kd = jnp.where(t_idx <= t_last, el["k32"][h] * jnp.exp(cl - cbh), 0.0).astype(BF16)
            s_scr[bi, h] = jnp.exp(cl) * s_t + _dot_tn(vb, kd)
            el["o"].append(o)
    for bi, el in enumerate(els):
        for h in range(GLA_HEADS):
            vs = slice(h * GLA_DV, (h + 1) * GLA_DV)
            o = el["o"][h]
            y = o * lax.rsqrt(jnp.mean(o * o, axis=-1, keepdims=True) + EPS) * gn_ref[:, vs]
            gate = el["g"][:, vs].astype(F32)
            y = y * (gate * _sigmoid(gate))
            o_ref[bi, :, vs] = y[:rows].astype(o_ref.dtype)

    @pl.when(ci == nc - 1)
    def _():
        for bi in range(nb):
            for h in range(GLA_HEADS):
                s_out[bi, h] = s_scr[bi, h].T


GLA_BATCH_PER_STEP = 4


def _gla(q, k, v, gg, la, s0_all, layer, gn, batch, t_len):
    L = GLA_CHUNK
    rows = min(L, t_len)
    nc = t_len // rows
    t_last = rows - 1
    nb = max(d for d in range(1, GLA_BATCH_PER_STEP + 1) if batch % d == 0)
    assert t_len % rows == 0 and (nc == 1 or rows == L)
    tok = lambda c: pl.BlockSpec((nb, rows, c), lambda b, i: (b, i, 0))
    shp_s = (GLA_HEADS, GLA_DK, GLA_DV)
    levels = _gla_level_matrices(L)
    seq = lambda a: a.reshape(batch, t_len, a.shape[-1])
    o, s_new = pl.pallas_call(
        functools.partial(_gla_kernel, L=L, t_last=t_last, nc=nc),
        grid=(batch // nb, nc),
        in_specs=[tok(GLA_KW), tok(GLA_KW), tok(GLA_VW), tok(GLA_VW), tok(GLA_KW),
                  pl.BlockSpec((1, nb) + shp_s, lambda b, i: (layer, b, 0, 0, 0)),
                  _const_spec(levels.shape), _const_spec((1, GLA_VW))],
        out_specs=[tok(GLA_VW), pl.BlockSpec((nb,) + shp_s, lambda b, i: (b, 0, 0, 0))],
        out_shape=[jax.ShapeDtypeStruct((batch, t_len, GLA_VW), BF16),
                   jax.ShapeDtypeStruct((batch,) + shp_s, F32)],
        scratch_shapes=[pltpu.VMEM((nb, GLA_HEADS, GLA_DV, GLA_DK), F32)],
        compiler_params=_cparams(("parallel", "arbitrary")),
        name="gla",
    )(seq(q), seq(k), seq(v), seq(gg), seq(la), s0_all, levels, gn)
    return o.reshape(batch * t_len, GLA_VW), s_new


def _pack_even(w, b_fox_f, b_i, b_f):
    nl, d, _ = w.shape
    o = np.cumsum((0, FOX_WIDTH, FOX_WIDTH, FOX_WIDTH, FOX_HEADS, 2 * ML_WIDTH, ML_WIDTH, ML_WIDTH, ML_HEADS, ML_HEADS))
    n_gate = FOX_HEADS + 2 * ML_HEADS
    wp = jnp.concatenate([w[:, :, o[0]:o[3]], w[:, :, o[4]:o[7]], w[:, :, o[3]:o[4]], w[:, :, o[7]:o[9]],
                          jnp.zeros((nl, d, LANES - n_gate), w.dtype)], axis=2).astype(BF16)
    bias = jnp.concatenate([b_fox_f, b_i, b_f, jnp.zeros((nl, LANES - n_gate), F32)], axis=1).reshape(nl, 1, LANES)
    return wp, bias


def _pack_odd(w, w_a2):
    nl, d, _ = w.shape
    wp = jnp.concatenate([w, jnp.zeros((nl, d, LANES - GLA_RANK), w.dtype)], axis=2).astype(BF16)
    wa2 = jnp.concatenate([w_a2, jnp.zeros((nl, LANES - GLA_RANK, w_a2.shape[2]), w_a2.dtype)], axis=1).astype(BF16)
    return wp, wa2


def _gate_rows(gc, batch, t_len, t_pad):
    g = gc.reshape(batch, t_len, LANES)[:, :, :16].transpose(0, 2, 1)
    if t_pad > t_len:
        g = jnp.pad(g, ((0, 0), (0, 0), (0, t_pad - t_len)))
    return g


def _trunk(x, fox_k, fox_v, fox_lf, ml_c, ml_n, ml_m, ml_buf, gla_s, params):
    (norm_mix, norm_ffn, norm_final, w_even, bias_even, conv_w, conv_b, g_ml,
     w_odd, w_a2, b_a, g_gla, w_out, w_ff1, w_ff2) = params
    batch, t_len, d = x.shape
    past = fox_k.shape[2]
    n = batch * t_len
    tm = 512 if n % 512 == 0 else 256 if n % 256 == 0 else n
    depth = norm_mix.shape[0]
    xf = x.reshape(n, d)
    ev_states, odd_states = [], []
    kv_stacked = None
    y = None
    for layer in range(depth):
        j = layer // 2
        if layer % 2 == 0:
            assert (t_len % tm == 0) == (past == 0)
            buf8 = jnp.pad(ml_buf[j], ((0, 0), (8 - (ML_CONV - 1), 0), (0, 0)))
            proj = _inproj_even(xf, norm_mix[layer][None], w_even, bias_even,
                                tm, batch, t_len, j, (depth + 1) // 2, kv_stacked, conv_w[j], conv_b[j][None], buf8)
            conv_tail = None
            if past == 0:
                qt, kf, vf, ka, vt, mqk, mv, mo, gc, bnd, conv_tail = proj
                kv_stacked = (kf, vf)
                assert tm == FOX_Q_TILE
                attn = _fox_attention(qt, ka, vt, bnd, batch, t_len, FOX_Q_TILE, FOX_KEY_CHUNK)
            else:
                q, kf, vf, mqk, mv, mo, gc = proj
                lf_new = gc.reshape(batch, t_len, LANES)[:, :, :FOX_HEADS]
                lf_row = jnp.concatenate([fox_lf[j], lf_new, jnp.zeros((batch, LANES - t_len, FOX_HEADS), F32)], axis=1)
                lf_row = jnp.pad(lf_row.transpose(0, 2, 1), ((0, 0), (0, 16 - FOX_HEADS), (0, 0)))
                cache_t = lambda a: a.transpose(0, 1, 3, 4, 2).reshape(a.shape[0], batch, FOX_WIDTH, past)
                new_t = lambda a: jnp.pad(a.reshape(batch, t_len, FOX_WIDTH).transpose(0, 2, 1).astype(BF16),
                                          ((0, 0), (0, 0), (0, LANES - t_len)))
                attn = _fox_decode(q.reshape(batch, t_len, FOX_WIDTH), cache_t(fox_k), cache_t(fox_v), j,
                                   new_t(kf), new_t(vf), lf_row).reshape(n, FOX_WIDTH)
            l_ml = ML_CHUNK if t_len >= ML_CHUNK else LANES
            gr = _gate_rows(gc, batch, t_len, max(t_len, l_ml))
            c0 = ml_c[j]
            n0 = ml_n[j][:, :, None, :]
            m0 = jnp.broadcast_to(ml_m[j][:, :, None, None], (batch, ML_HEADS, 1, LANES))
            h_ml, c_new, n_new, m_new, buf_new = _mlstm(mqk, mv, mo, gc, gr, c0, n0, m0, buf8,
                                                        conv_w[j], conv_b[j][None], g_ml[j][None], batch, t_len)
            if conv_tail is not None:
                buf_new = conv_tail
            heads = lambda a: None if past == 0 else a.reshape(batch, t_len, FOX_HEADS, FOX_HEAD_DIM)
            ev_states.append((heads(kf), heads(vf),
                              gc[:, :FOX_HEADS].reshape(batch, t_len, FOX_HEADS),
                              c_new, n_new[:, :, 0, :], m_new[:, :, 0, 0], buf_new[:, 8 - (ML_CONV - 1):, :]))
            mixes = [attn, h_ml]
        else:
            q, k, v, gg, la = _inproj_odd(xf, norm_mix[layer][None], w_odd, w_a2, b_a[j][None], tm, j)
            o, s_new = _gla(q, k, v, gg, la, gla_s, j, g_gla[j][None], batch, t_len)
            odd_states.append(s_new)
            mixes = [o]
        last = layer == depth - 1
        out = _post(xf, mixes, w_out, norm_ffn[layer][None], w_ff1, w_ff2,
                    norm_final[None] if last else None, tm, layer)
        if last:
            y = out
        else:
            xf = out
    ev = [jnp.stack([s[i] for s in ev_states]) for i in range(2 if kv_stacked else 0, 7)]
    if kv_stacked:
        ev = [a.reshape(a.shape[0], batch, FOX_HEADS, FOX_HEAD_DIM, t_len).transpose(0, 1, 4, 2, 3)
              for a in kv_stacked] + ev
    return y.reshape(batch, t_len, d), ev, jnp.stack(odd_states)


def kernel(x_prompt, x_sample, cache_fox_k, cache_fox_v, cache_fox_logf, state_mlstm_c, state_mlstm_n, state_mlstm_m, state_mlstm_conv, state_gla_s, norm_mix, norm_ffn, norm_final, w_in_even, b_fox_f, conv_w_ml, conv_b_ml, b_ml_i, b_ml_f, g_ml, w_in_odd, w_gla_a2, b_gla_a, g_gla, w_out, w_ff1, w_ff2):
    n_even, n_odd = w_in_even.shape[0], w_in_odd.shape[0]
    packed_even = _pack_even(w_in_even, b_fox_f, b_ml_i, b_ml_f)
    packed_odd = _pack_odd(w_in_odd, w_gla_a2)
    params = (norm_mix, norm_ffn, norm_final,
              packed_even[0], packed_even[1], conv_w_ml, conv_b_ml, g_ml,
              packed_odd[0], packed_odd[1], b_gla_a, g_gla,
              w_out.astype(BF16), w_ff1.astype(BF16), w_ff2.astype(BF16))

    bp = x_prompt.shape[0]
    dt = x_prompt.dtype
    zeros = lambda *s: jnp.zeros(s, dt)
    y_p, ev_p, gla_p = _trunk(
        x_prompt,
        zeros(n_even, bp, 0, FOX_HEADS, FOX_HEAD_DIM), zeros(n_even, bp, 0, FOX_HEADS, FOX_HEAD_DIM),
        zeros(n_even, bp, 0, FOX_HEADS),
        zeros(n_even, bp, ML_HEADS, ML_HEAD_DIM, ML_HEAD_DIM), zeros(n_even, bp, ML_HEADS, ML_HEAD_DIM),
        zeros(n_even, bp, ML_HEADS), zeros(n_even, bp, ML_CONV - 1, 2 * ML_WIDTH),
        zeros(n_odd, bp, GLA_HEADS, GLA_DK, GLA_DV), params)
    y_s, ev_s, gla_s = _trunk(x_sample, cache_fox_k, cache_fox_v, cache_fox_logf, state_mlstm_c, state_mlstm_n,
                              state_mlstm_m, state_mlstm_conv, state_gla_s, params)
    return (y_p, y_s, *ev_p, gla_p, *ev_s, gla_s)
```
